```python
import math
import jax, jax.numpy as jnp
from jax import lax
import numpy as np

D_MODEL = 1024
BATCH = 32
SEQ = 256
DEPTH = 2
DEC_BATCH = 4
DEC_SEQ = 1024
PAST_LEN = 256

GRID_W = 64
N_EVEN = (DEPTH + 1) // 2
N_ODD = DEPTH // 2
W_A = D_MODEL
SSM_GROUP = 16
G_A = W_A // SSM_GROUP
P_A = 64
DT_MIN = 1e-3
DT_MAX = 1e-1
DH = 64
H_B = D_MODEL // (2 * DH)
W_B = H_B * 2 * DH
QB = 128
ROPE_BASE = 10000.0
ROT_HALF = DH // 2
ROT_FREQS = DH // 4
E_IN = 2 * W_A + 4 * W_B
W_C = 2 * D_MODEL
NG_C = 8
GC_C = W_C // NG_C
ALPHA = (2 * DEPTH) ** 0.25
BETA = (8 * DEPTH) ** -0.25
LN_EPS = 1e-5

kernel_name = 'hybrid_s5_diffattn_fnet_prefix_diffusion_step'

F32 = jnp.float32


def _ln(x):
    xf = x.astype(F32)
    mu = jnp.mean(xf, axis=-1, keepdims=True)
    var = jnp.mean(jnp.square(xf - mu), axis=-1, keepdims=True)
    return (xf - mu) * lax.rsqrt(var + LN_EPS)


def _modulate_input(x, cond, w_mod_l, b_mod_l):
    m = jax.nn.silu(cond.astype(F32)) @ w_mod_l.astype(F32) + b_mod_l.astype(F32)
    shift, scale, gate = jnp.split(m[:, None, :], 3, axis=-1)
    h = (_ln(x) * (1.0 + scale) + shift).astype(x.dtype)
    return h, gate


def _post_norm(x, gate, out, ln_g_l, ln_b_l):
    z = ALPHA * x.astype(F32) + gate * out.astype(F32)
    return (_ln(z) * ln_g_l.astype(F32) + ln_b_l.astype(F32)).astype(x.dtype)


def _axial_rope(L):
    rows = L // GRID_W
    row = jnp.repeat(jnp.arange(rows), GRID_W).astype(F32)
    col = jnp.tile(jnp.arange(GRID_W), rows).astype(F32)
    freqs = ROPE_BASE ** (-jnp.arange(ROT_FREQS, dtype=F32) / ROT_FREQS)
    ang = jnp.concatenate([row[:, None] * freqs, col[:, None] * freqs], axis=-1)
    return jnp.cos(ang), jnp.sin(ang)


def _apply_rope(x, cos, sin):
    xf = x.astype(F32)
    x1, x2 = xf[..., :ROT_HALF], xf[..., ROT_HALF:]
    c = cos[None, :, None, None, :]
    s = sin[None, :, None, None, :]
    return jnp.concatenate([x1 * c - x2 * s, x1 * s + x2 * c], axis=-1).astype(x.dtype)


def _diff_attn(q, k, v, lam):
    bsz, lq = q.shape[0], q.shape[1]
    qb = min(QB, lq)
    nb = lq // qb
    qs = jnp.moveaxis(q.reshape(bsz, nb, qb, H_B, 2, DH), 1, 0)
    kf = k.astype(F32)
    vf = v.astype(F32)

    def block(qi):
        s = jnp.einsum('bqhmd,bkhmd->bhmqk', qi.astype(F32), kf) * (DH ** -0.5)
        p = jax.nn.softmax(s, axis=-1)
        a = p[:, :, 0] - lam * p[:, :, 1]
        return jnp.einsum('bhqk,bkhe->bqhe', a, vf)

    o = lax.map(block, qs)
    return jnp.moveaxis(o, 0, 1).reshape(bsz, lq, H_B, 2 * DH)


def _ssm_combine(e1, e2):
    a1r, a1i, b1r, b1i = e1
    a2r, a2i, b2r, b2i = e2
    return (a2r * a1r - a2i * a1i,
            a2r * a1i + a2i * a1r,
            a2r * b1r - a2i * b1i + b2r,
            a2r * b1i + a2i * b1r + b2i)


def _s5_bidir(u, lam_re, lam_im, log_dt, b_re, b_im, c_re, c_im, d_skip, h0_re, h0_im):
    bsz, L = u.shape[0], u.shape[1]
    uf = u.astype(F32).reshape(bsz, L, G_A, SSM_GROUP)
    y = uf * d_skip.astype(F32).reshape(G_A, SSM_GROUP)
    fin_re, fin_im = [], []
    for d in range(2):
        lr = lam_re[d].astype(F32)
        li = lam_im[d].astype(F32)
        dt = jnp.exp(log_dt[d].astype(F32))[:, None]
        mag = jnp.exp(lr * dt)
        ar = mag * jnp.cos(li * dt)
        ai = mag * jnp.sin(li * dt)
        den = lr * lr + li * li
        fr = ((ar - 1.0) * lr + ai * li) / den
        fi = (ai * lr - (ar - 1.0) * li) / den
        br = b_re[d].astype(F32)
        bi = b_im[d].astype(F32)
        bbr = fr[..., None] * br - fi[..., None] * bi
        bbi = fr[..., None] * bi + fi[..., None] * br
        xr = jnp.einsum('blgn,gpn->blgp', uf, bbr)
        xi = jnp.einsum('blgn,gpn->blgp', uf, bbi)
        t0 = 0 if d == 0 else L - 1
        h0r = h0_re[:, d].astype(F32)
        h0i = h0_im[:, d].astype(F32)
        xr = xr.at[:, t0].add(ar * h0r - ai * h0i)
        xi = xi.at[:, t0].add(ar * h0i + ai * h0r)
        arb = jnp.broadcast_to(ar, xr.shape)
        aib = jnp.broadcast_to(ai, xr.shape)
        _, _, hr, hi = lax.associative_scan(_ssm_combine, (arb, aib, xr, xi), reverse=(d == 1), axis=1)
        y = y + jnp.einsum('blgp,gnp->blgn', hr, c_re[d].astype(F32)) \
              - jnp.einsum('blgp,gnp->blgn', hi, c_im[d].astype(F32))
        tf = L - 1 if d == 0 else 0
        fin_re.append(hr[:, tf])
        fin_im.append(hi[:, tf])
    return y.reshape(bsz, L, W_A), jnp.stack(fin_re, axis=1), jnp.stack(fin_im, axis=1)


def _even_layer(x, cond, w_mod_l, b_mod_l, ln_g_l, ln_b_l, pe, lam_init, rope, ctx_k, ctx_v, h0_re, h0_im):
    bsz, L = x.shape[0], x.shape[1]
    h, gate = _modulate_input(x, cond, w_mod_l, b_mod_l)
    proj = h @ pe['w_in']
    u_a, z_a, q, k, v, z_b = jnp.split(
        proj, [W_A, 2 * W_A, 2 * W_A + W_B, 2 * W_A + 2 * W_B, 2 * W_A + 3 * W_B], axis=-1)
    y_a, hf_re, hf_im = _s5_bidir(u_a, pe['lam_re'], pe['lam_im'], pe['log_dt'], pe['b_re'], pe['b_im'],
                                  pe['c_re'], pe['c_im'], pe['d'], h0_re, h0_im)
    g_a = jax.nn.gelu(y_a.astype(x.dtype))
    y_a = g_a * jax.nn.sigmoid(g_a @ pe['w_glu'] + pe['b_glu'])
    y_a = y_a * jax.nn.silu(z_a)
    q = q.reshape(bsz, L, H_B, 2, DH)
    k = k.reshape(bsz, L, H_B, 2, DH)
    v = v.reshape(bsz, L, H_B, 2 * DH)
    if rope is None:
        q_r, k_all, v_all = q, k, v
    else:
        cos, sin = rope
        q_r = _apply_rope(q, cos, sin)
        k_all = jnp.concatenate([ctx_k.astype(k.dtype), _apply_rope(k, cos, sin)], axis=1)
        v_all = jnp.concatenate([ctx_v.astype(v.dtype), v], axis=1)
    lam = (jnp.exp(jnp.sum(pe['lq1'].astype(F32) * pe['lk1'].astype(F32)))
           - jnp.exp(jnp.sum(pe['lq2'].astype(F32) * pe['lk2'].astype(F32))) + lam_init)
    o = _diff_attn(q_r, k_all, v_all, lam)
    o = o * lax.rsqrt(jnp.mean(jnp.square(o), axis=-1, keepdims=True) + LN_EPS)
    o = o * pe['subln_g'].astype(F32) * (1.0 - lam_init)
    y_b = o.reshape(bsz, L, W_B).astype(x.dtype) * jax.nn.silu(z_b)
    out = jnp.concatenate([y_a.astype(x.dtype), y_b], axis=-1) @ pe['w_out']
    return _post_norm(x, gate, out, ln_g_l, ln_b_l), k, v, hf_re, hf_im


def _odd_layer(x, cond, w_mod_l, b_mod_l, ln_g_l, ln_b_l, w_in, w_fno, b_fno, w_out):
    bsz, L = x.shape[0], x.shape[1]
    h, gate = _modulate_input(x, cond, w_mod_l, b_mod_l)
    u, z = jnp.split(h @ w_in, 2, axis=-1)
    uf = u.astype(F32).reshape(bsz, L, NG_C, GC_C)
    mixed = jnp.real(jnp.fft.fft2(uf, axes=(1, 3), norm='ortho')).reshape(bsz, L, W_C).astype(x.dtype)
    y = (mixed @ w_fno + b_fno) * jax.nn.silu(z)
    return _post_norm(x, gate, y @ w_out, ln_g_l, ln_b_l)


def setup_inputs(seed: int = 0) -> dict:
    key = jax.random.key(seed)
    ks = jax.random.split(key, 40)

    def nrm(k, shape, s):
        return jax.random.normal(k, shape, F32) * s

    n_idx = jnp.arange(P_A, dtype=F32)
    return {
        'x_prompt': nrm(ks[0], (BATCH, SEQ, D_MODEL), 1.0),
        'x_sample': nrm(ks[1], (DEC_BATCH, DEC_SEQ, D_MODEL), 1.0),
        'cache_k': nrm(ks[2], (DEC_BATCH, N_EVEN, PAST_LEN, H_B, 2, DH), 1.0),
        'cache_v': nrm(ks[3], (DEC_BATCH, N_EVEN, PAST_LEN, H_B, 2 * DH), 1.0),
        'state_ssm_re': nrm(ks[4], (DEC_BATCH, N_EVEN, 2, G_A, P_A), 0.5),
        'state_ssm_im': nrm(ks[5], (DEC_BATCH, N_EVEN, 2, G_A, P_A), 0.5),
        'c': nrm(ks[6], (DEC_BATCH, D_MODEL), 1.0),
        'c_ctx': nrm(ks[7], (D_MODEL,), 1.0),
        'w_mod': nrm(ks[8], (DEPTH, D_MODEL, 3 * D_MODEL), D_MODEL ** -0.5),
        'b_mod': nrm(ks[9], (DEPTH, 3 * D_MODEL), 0.02),
        'ln_g': 1.0 + nrm(ks[10], (DEPTH, D_MODEL), 0.02),
        'ln_b': nrm(ks[11], (DEPTH, D_MODEL), 0.02),
        'w_in_e': nrm(ks[12], (N_EVEN, D_MODEL, E_IN), D_MODEL ** -0.5),
        'ssm_lam_re': -0.5 + nrm(ks[13], (N_EVEN, 2, G_A, P_A), 0.01),
        'ssm_lam_im': math.pi * n_idx + nrm(ks[14], (N_EVEN, 2, G_A, P_A), 0.01),
        'ssm_log_dt': jax.random.uniform(ks[15], (N_EVEN, 2, G_A), F32, math.log(DT_MIN), math.log(DT_MAX)),
        'ssm_b_re': nrm(ks[16], (N_EVEN, 2, G_A, P_A, SSM_GROUP), (2 * SSM_GROUP) ** -0.5),
        'ssm_b_im': nrm(ks[17], (N_EVEN, 2, G_A, P_A, SSM_GROUP), (2 * SSM_GROUP) ** -0.5),
        'ssm_c_re': nrm(ks[18], (N_EVEN, 2, G_A, SSM_GROUP, P_A), P_A ** -0.5),
        'ssm_c_im': nrm(ks[19], (N_EVEN, 2, G_A, SSM_GROUP, P_A), P_A ** -0.5),
        'ssm_d': nrm(ks[20], (N_EVEN, W_A), 1.0),
        'w_glu': nrm(ks[21], (N_EVEN, W_A, W_A), W_A ** -0.5),
        'b_glu': nrm(ks[22], (N_EVEN, W_A), 0.02),
        'lam_q1': nrm(ks[23], (N_EVEN, DH), 0.1),
        'lam_k1': nrm(ks[24], (N_EVEN, DH), 0.1),
        'lam_q2': nrm(ks[25], (N_EVEN, DH), 0.1),
        'lam_k2': nrm(ks[26], (N_EVEN, DH), 0.1),
        'subln_g': 1.0 + nrm(ks[27], (N_EVEN, 2 * DH), 0.02),
        'w_out_e': nrm(ks[28], (N_EVEN, W_A + W_B, D_MODEL), (W_A + W_B) ** -0.5 * BETA),
        'w_in_o': nrm(ks[29], (N_ODD, D_MODEL, 2 * W_C), D_MODEL ** -0.5),
        'w_fno': nrm(ks[30], (N_ODD, W_C, W_C), W_C ** -0.5),
        'b_fno': nrm(ks[31], (N_ODD, W_C), 0.02),
        'w_out_o': nrm(ks[32], (N_ODD, W_C, D_MODEL), W_C ** -0.5 * BETA),
    }


def reference(x_prompt, x_sample, cache_k, cache_v, state_ssm_re, state_ssm_im, c, c_ctx,
              w_mod, b_mod, ln_g, ln_b, w_in_e, ssm_lam_re, ssm_lam_im, ssm_log_dt,
              ssm_b_re, ssm_b_im, ssm_c_re, ssm_c_im, ssm_d, w_glu, b_glu,
              lam_q1, lam_k1, lam_q2, lam_k2, subln_g, w_out_e, w_in_o, w_fno, b_fno, w_out_o):
    rope = _axial_rope(x_sample.shape[1])
    cond_ctx = c_ctx[None, :]
    bp = x_prompt.shape[0]
    xp, xs = x_prompt, x_sample
    new_k, new_v, new_sr, new_si = [], [], [], []
    for layer in range(DEPTH):
        wm, bm, lg, lb = w_mod[layer], b_mod[layer], ln_g[layer], ln_b[layer]
        if layer % 2 == 0:
            e = layer // 2
            lam_init = 0.8 - 0.6 * math.exp(-0.3 * layer)
            pe = dict(w_in=w_in_e[e], lam_re=ssm_lam_re[e], lam_im=ssm_lam_im[e], log_dt=ssm_log_dt[e],
                      b_re=ssm_b_re[e], b_im=ssm_b_im[e], c_re=ssm_c_re[e], c_im=ssm_c_im[e], d=ssm_d[e],
                      w_glu=w_glu[e], b_glu=b_glu[e], lq1=lam_q1[e], lk1=lam_k1[e], lq2=lam_q2[e],
                      lk2=lam_k2[e], subln_g=subln_g[e], w_out=w_out_e[e])
            zeros = jnp.zeros((bp, 2, G_A, P_A), F32)
            xp, k_ctx, v_ctx, s_re, s_im = _even_layer(xp, cond_ctx, wm, bm, lg, lb, pe, lam_init,
                                                       None, None, None, zeros, zeros)
            new_k.append(k_ctx)
            new_v.append(v_ctx)
            new_sr.append(s_re)
            new_si.append(s_im)
            xs = _even_layer(xs, c, wm, bm, lg, lb, pe, lam_init, rope, cache_k[:, e], cache_v[:, e],
                             state_ssm_re[:, e], state_ssm_im[:, e])[0]
        else:
            o = layer // 2
            xp = _odd_layer(xp, cond_ctx, wm, bm, lg, lb, w_in_o[o], w_fno[o], b_fno[o], w_out_o[o])
            xs = _odd_layer(xs, c, wm, bm, lg, lb, w_in_o[o], w_fno[o], b_fno[o], w_out_o[o])
    return (xp, xs, jnp.stack(new_k, axis=1), jnp.stack(new_v, axis=1),
            jnp.stack(new_sr, axis=1), jnp.stack(new_si, axis=1))
```

```python
import functools
import math

import jax
import jax.numpy as jnp
from jax import lax
from jax.experimental import pallas as pl
from jax.experimental.pallas import tpu as pltpu

F32 = jnp.float32
BF16 = jnp.bfloat16

D_MODEL = 1024
GRID_W = 64
SSM_GROUP = 16
G_A = D_MODEL // SSM_GROUP
P_A = 64
DH = 64
H_B = D_MODEL // (2 * DH)
ROPE_BASE = 10000.0
ROT_HALF = DH // 2
ROT_FREQS = DH // 4
NG_C = 8
GC_C = 2 * D_MODEL // NG_C
W_C = 2 * D_MODEL
LN_EPS = 1e-5
CHUNK = 16
SSM_TILE = CHUNK * SSM_GROUP
S5_GROUPS_PER_STEP = 4
SUBLANES = 8
MOD_ROWS = 8
VMEM_LIMIT = 56 * 1024 * 1024


def _cparams(n_axes):
    return pltpu.CompilerParams(dimension_semantics=("arbitrary",) * n_axes, vmem_limit_bytes=VMEM_LIMIT)


def _ln_rows(x):
    mu = jnp.mean(x, axis=-1, keepdims=True)
    xc = x - mu
    var = jnp.mean(xc * xc, axis=-1, keepdims=True)
    return xc * lax.rsqrt(var + LN_EPS)


def _silu(z):
    return z * jax.nn.sigmoid(z)


def _dot(a, b):
    return jnp.dot(a, b, preferred_element_type=F32)


def _dot_nt(a, b):
    return lax.dot_general(a, b, (((1,), (1,)), ((), ())), preferred_element_type=F32)


def _post_norm(x, gate, out, g, b, alpha):
    return _ln_rows(alpha * x + gate * out) * g + b


def _mod_kernel(c_ref, w_ref, b_ref, o_ref):
    c = _silu(c_ref[...]).astype(BF16)
    o_ref[...] = _dot(c, w_ref[...].astype(BF16)) + b_ref[...]


def _modulation(cond8, w_mod, b_mod):
    depth, d, n3 = w_mod.shape
    tn = 1024
    return pl.pallas_call(
        _mod_kernel,
        grid=(depth, n3 // tn),
        in_specs=[
            pl.BlockSpec((MOD_ROWS, d), lambda l, j: (0, 0)),
            pl.BlockSpec((None, d, tn), lambda l, j: (l, 0, j)),
            pl.BlockSpec((None, 1, tn), lambda l, j: (l, 0, j)),
        ],
        out_specs=pl.BlockSpec((None, MOD_ROWS, tn), lambda l, j: (l, 0, j)),
        out_shape=jax.ShapeDtypeStruct((depth, MOD_ROWS, n3), F32),
        compiler_params=_cparams(2),
        name="modulation",
    )(cond8, w_mod, b_mod.reshape(depth, 1, n3))


def _mod_spec(layer, part, cond_of):
    return pl.BlockSpec((None, 1, D_MODEL), lambda *idx: ((layer * MOD_ROWS + cond_of(*idx)) * 3 + part, 0, 0))


def _rope(x, cos, sin_signed, first_half):
    blocks = []
    for hh in range(x.shape[1] // 128):
        b = x[:, hh * 128:(hh + 1) * 128]
        partner = jnp.where(first_half, pltpu.roll(b, 128 - ROT_HALF, 1), pltpu.roll(b, ROT_HALF, 1))
        blocks.append(b * cos + partner * sin_signed)
    return jnp.concatenate(blocks, axis=1)


def _even_in_kernel(*refs, rope):
    if rope:
        x_ref, shift_ref, scale_ref, w_ref, cos_ref, sin_ref, u_ref, sza_ref, q_ref, kb_ref, vb_ref, szb_ref = refs
    else:
        (x_ref, shift_ref, scale_ref, w_ref, u_ref, sza_ref, q_ref, kb_ref, vb_ref, szb_ref,
         kf_ref, vf_ref) = refs
    w = D_MODEL
    h = (_ln_rows(x_ref[...]) * (1.0 + scale_ref[...]) + shift_ref[...]).astype(BF16)

    def proj(j):
        return _dot(h, w_ref[:, j * w:(j + 1) * w])

    u_ref[...] = proj(0).astype(BF16)
    sza_ref[...] = _silu(proj(1)).astype(BF16)
    q = proj(2) * (DH ** -0.5)
    k = proj(3)
    v = proj(4)
    if rope:
        lane = lax.broadcasted_iota(jnp.int32, (1, 128), 1)
        first_half = (lane % DH) < ROT_HALF
        q = _rope(q, cos_ref[...], sin_ref[...], first_half)
        kb_ref[...] = _rope(k, cos_ref[...], sin_ref[...], first_half).astype(BF16)
    else:
        kf_ref[...] = k
        vf_ref[...] = v
        kb_ref[...] = k.astype(BF16)
    q_ref[...] = q.astype(BF16)
    vb_ref[...] = v.astype(BF16)
    szb_ref[...] = _silu(proj(5)).astype(BF16)


def _even_in_proj(x2, mod3, w_in_bf, layer, cond_of, tm, rope_tabs, seq_len):
    t, d = x2.shape
    n = w_in_bf.shape[1]
    row = pl.BlockSpec((tm, d), lambda i: (i, 0))
    in_specs = [row, _mod_spec(layer, 0, cond_of), _mod_spec(layer, 1, cond_of),
                pl.BlockSpec((d, n), lambda i: (0, 0))]
    args = [x2, mod3, mod3, w_in_bf]
    bf = jax.ShapeDtypeStruct((t, d), BF16)
    out_shape = [bf] * 6
    out_specs = [row] * 6
    if rope_tabs is not None:
        tiles_per_seq = seq_len // tm
        tab = pl.BlockSpec((tm, 128), lambda i: (i % tiles_per_seq, 0))
        in_specs += [tab, tab]
        args += list(rope_tabs)
    else:
        out_shape += [jax.ShapeDtypeStruct((t, d), F32)] * 2
        out_specs += [row] * 2
    return pl.pallas_call(
        functools.partial(_even_in_kernel, rope=rope_tabs is not None),
        grid=(t // tm,),
        in_specs=in_specs,
        out_specs=out_specs,
        out_shape=out_shape,
        compiler_params=_cparams(1),
        name="even_in_proj",
    )(*args)


def _lag_kernel(p_ref, c_ref, o_ref):
    o_ref[...] = jnp.dot(p_ref[...], c_ref[...], preferred_element_type=F32, precision=lax.Precision.HIGHEST)


def _s5_operators(lam_re, lam_im, log_dt, b_re, b_im, c_re, c_im, d_skip):
    lr, li = lam_re.astype(F32), lam_im.astype(F32)
    dt = jnp.exp(log_dt.astype(F32))[..., None]
    mag = jnp.exp(lr * dt)
    ar = mag * jnp.cos(li * dt)
    ai = mag * jnp.sin(li * dt)
    den = lr * lr + li * li
    fr = ((ar - 1.0) * lr + ai * li) / den
    fi = (ai * lr - (ar - 1.0) * li) / den
    br, bi = b_re.astype(F32), b_im.astype(F32)
    bbr = fr[..., None] * br - fi[..., None] * bi
    bbi = fr[..., None] * bi + fi[..., None] * br
    cr, ci = c_re.astype(F32), c_im.astype(F32)
    pr, pi = [jnp.ones_like(ar)], [jnp.zeros_like(ar)]
    for _ in range(CHUNK):
        pr, pi = pr + [pr[-1] * ar - pi[-1] * ai], pi + [pr[-1] * ai + pi[-1] * ar]
    pr, pi = jnp.stack(pr), jnp.stack(pi)

    abr = pr[:CHUNK, ..., None] * bbr - pi[:CHUNK, ..., None] * bbi
    abi = pr[:CHUNK, ..., None] * bbi + pi[:CHUNK, ..., None] * bbr

    lhs = jnp.concatenate([abr, abi], axis=3)
    lhs = lhs.transpose(1, 2, 0, 4, 3).reshape(2 * G_A, SSM_TILE, 2 * P_A)
    rhs = jnp.concatenate([cr, -ci], axis=3).transpose(0, 1, 3, 2).reshape(2 * G_A, 2 * P_A, SSM_GROUP)
    klag = pl.pallas_call(
        _lag_kernel,
        grid=(2 * G_A,),
        in_specs=[pl.BlockSpec((None, SSM_TILE, 2 * P_A), lambda i: (i, 0, 0)),
                  pl.BlockSpec((None, 2 * P_A, SSM_GROUP), lambda i: (i, 0, 0))],
        out_specs=pl.BlockSpec((None, SSM_TILE, SSM_GROUP), lambda i: (i, 0, 0)),
        out_shape=jax.ShapeDtypeStruct((2 * G_A, SSM_TILE, SSM_GROUP), F32),
        compiler_params=_cparams(1),
        name="s5_lag_kernels",
    )(lhs, rhs).reshape(2, G_A, CHUNK, SSM_GROUP, SSM_GROUP)

    s_idx = jnp.arange(CHUNK)[:, None]
    t_idx = jnp.arange(CHUNK)[None, :]
    kf = jnp.where((t_idx >= s_idx)[None, :, :, None, None], klag[0][:, jnp.clip(t_idx - s_idx, 0, CHUNK - 1)], 0.0)
    kb = jnp.where((s_idx >= t_idx)[None, :, :, None, None], klag[1][:, jnp.clip(s_idx - t_idx, 0, CHUNK - 1)], 0.0)
    eye_t = jnp.eye(CHUNK, dtype=F32)[None, :, :, None, None]
    eye_n = jnp.eye(SSM_GROUP, dtype=F32)[None, None, None, :, :]
    dsk = d_skip.astype(F32).reshape(G_A, 1, 1, 1, SSM_GROUP)
    ktoep = (kf + kb + eye_t * eye_n * dsk).transpose(0, 1, 3, 2, 4).reshape(G_A, SSM_TILE, SSM_TILE)

    def st(part, d, rev):
        x = part[:, d]
        x = x[::-1] if rev else x
        return x.transpose(1, 0, 3, 2)
    wst = jnp.concatenate([st(abr, 0, True), st(abr, 1, False), st(abi, 0, True), st(abi, 1, False)], axis=-1)
    wst = wst.reshape(G_A, SSM_TILE, 4 * P_A)

    def out_rows(d, rev):
        er = pr[1:, d][::-1] if rev else pr[1:, d]
        ei = pi[1:, d][::-1] if rev else pi[1:, d]
        re = cr[d][None] * er[:, :, None, :] - ci[d][None] * ei[:, :, None, :]
        im = cr[d][None] * ei[:, :, None, :] + ci[d][None] * er[:, :, None, :]
        return re.transpose(1, 3, 0, 2), -im.transpose(1, 3, 0, 2)
    f_re, f_im = out_rows(0, False)
    b_re_, b_im_ = out_rows(1, True)
    wout = jnp.concatenate([f_re, b_re_, f_im, b_im_], axis=1).reshape(G_A, 4 * P_A, SSM_TILE)

    a16r = jnp.concatenate([pr[CHUNK, 0], pr[CHUNK, 1]], axis=-1)
    a16i = jnp.concatenate([pi[CHUNK, 0], pi[CHUNK, 1]], axis=-1)
    gp = S5_GROUPS_PER_STEP
    a16r = a16r.reshape(G_A // gp, 1, gp * 2 * P_A)
    a16i = a16i.reshape(G_A // gp, 1, gp * 2 * P_A)
    return wst.astype(BF16), ktoep.astype(BF16), wout.astype(BF16), a16r, a16i


def _s5_kernel(u_ref, wst_ref, kt_ref, wo_ref, a16r_ref, a16i_ref, h0r_ref, h0i_ref,
               y_ref, fr_ref, fi_ref, s_re, s_im, hf_re, hf_im, hb_re, hb_im, *, nc, bp, gp):
    sw = 2 * P_A
    for j in range(gp):
        s = _dot(u_ref[j], wst_ref[j])
        s_re[:, j * sw:(j + 1) * sw] = s[:, :sw]
        s_im[:, j * sw:(j + 1) * sw] = s[:, sw:]
    lane = lax.broadcasted_iota(jnp.int32, (1, gp * sw), 1)
    fwd = (lane % sw) < P_A
    a_r = a16r_ref[...]
    a_i = a16i_ref[...]
    st_r = jnp.concatenate([h0r_ref[j] for j in range(gp)], axis=1)
    st_i = jnp.concatenate([h0i_ref[j] for j in range(gp)], axis=1)

    def step(k, carry):
        sr, si = carry
        rf = pl.multiple_of(k * bp, bp)
        rb = pl.multiple_of((nc - 1 - k) * bp, bp)
        hf_re[pl.ds(rf, bp), :] = sr
        hf_im[pl.ds(rf, bp), :] = si
        hb_re[pl.ds(rb, bp), :] = sr
        hb_im[pl.ds(rb, bp), :] = si
        xr = jnp.where(fwd, s_re[pl.ds(rf, bp), :], s_re[pl.ds(rb, bp), :])
        xi = jnp.where(fwd, s_im[pl.ds(rf, bp), :], s_im[pl.ds(rb, bp), :])
        return a_r * sr - a_i * si + xr, a_r * si + a_i * sr + xi

    st_r, st_i = lax.fori_loop(0, nc, step, (st_r, st_i))
    hin_r = jnp.where(fwd, hf_re[...], hb_re[...]).astype(BF16)
    hin_i = jnp.where(fwd, hf_im[...], hb_im[...]).astype(BF16)
    for j in range(gp):
        fr_ref[j] = st_r[:, j * sw:(j + 1) * sw]
        fi_ref[j] = st_i[:, j * sw:(j + 1) * sw]
        hin = jnp.concatenate([hin_r[:, j * sw:(j + 1) * sw], hin_i[:, j * sw:(j + 1) * sw]], axis=1)
        y_ref[j] = (_dot(u_ref[j], kt_ref[j]) + _dot(hin, wo_ref[j])).astype(BF16)


def _s5_mix(u, ops, h0_re, h0_im, bsz, seq_len):
    wst, ktoep, wout, a16r, a16i = ops
    nc = seq_len // CHUNK
    bp = -(-bsz // SUBLANES) * SUBLANES
    r = nc * bp
    gp = S5_GROUPS_PER_STEP
    u2 = u.reshape(bsz, nc, CHUNK, G_A, SSM_GROUP).transpose(3, 1, 0, 2, 4)
    u2 = jnp.pad(u2, ((0, 0), (0, 0), (0, bp - bsz), (0, 0), (0, 0))).reshape(G_A, r, SSM_TILE)

    def h0_layout(h0):
        x = h0.astype(F32).transpose(2, 0, 1, 3).reshape(G_A, bsz, 2 * P_A)
        return jnp.pad(x, ((0, 0), (0, bp - bsz), (0, 0)))

    grp = lambda i: (i, 0, 0)
    slab = pl.BlockSpec((gp, r, SSM_TILE), grp)
    wspec = pl.BlockSpec((gp, SSM_TILE, SSM_TILE), grp)
    aspec = pl.BlockSpec((None, 1, gp * 2 * P_A), grp)
    hspec = pl.BlockSpec((gp, bp, 2 * P_A), grp)
    state = jax.ShapeDtypeStruct((G_A, bp, 2 * P_A), F32)
    scratch = [pltpu.VMEM((r, gp * 2 * P_A), F32)] * 6
    y2, f_re, f_im = pl.pallas_call(
        functools.partial(_s5_kernel, nc=nc, bp=bp, gp=gp),
        grid=(G_A // gp,),
        in_specs=[slab, wspec, wspec, wspec, aspec, aspec, hspec, hspec],
        out_specs=[slab, hspec, hspec],
        out_shape=[jax.ShapeDtypeStruct((G_A, r, SSM_TILE), BF16), state, state],
        scratch_shapes=scratch,
        compiler_params=_cparams(1),
        name="s5_core",
    )(u2, wst, ktoep, wout, a16r, a16i, h0_layout(h0_re), h0_layout(h0_im))
    y = y2.reshape(G_A, nc, bp, CHUNK, SSM_GROUP)[:, :, :bsz].transpose(2, 1, 3, 0, 4)
    y = y.reshape(bsz * seq_len, G_A * SSM_GROUP)

    def fin(f):
        return f[:, :bsz].reshape(G_A, bsz, 2, P_A).transpose(1, 2, 0, 3)
    return y, fin(f_re), fin(f_im)


def _diff_lambda(lq1, lk1, lq2, lk2, lam_init):
    return (jnp.exp(jnp.sum(lq1[...] * lk1[...], axis=-1, keepdims=True))
            - jnp.exp(jnp.sum(lq2[...] * lk2[...], axis=-1, keepdims=True)) + lam_init)


def _attn_kernel(*refs, lam_init, cached):
    if cached:
        q_ref, kc_ref, kn_ref, vc_ref, vn_ref, szb_ref, lq1, lk1, lq2, lk2, g_ref, o_ref = refs
    else:
        q_ref, kn_ref, vn_ref, szb_ref, lq1, lk1, lq2, lk2, g_ref, o_ref = refs
    lam = _diff_lambda(lq1, lk1, lq2, lk2, lam_init)
    lq = q_ref.shape[0]
    hw = 2 * DH
    low = lax.broadcasted_iota(jnp.int32, (1, hw), 1) < DH
    zero = jnp.zeros((), BF16)
    for h in range(H_B):
        cols = slice(h * hw, (h + 1) * hw)
        qh = q_ref[:, cols]
        qs = jnp.concatenate([jnp.where(low, qh, zero), jnp.where(low, zero, qh)], axis=0)
        s = _dot_nt(qs, kn_ref[:, cols])
        if cached:
            s = jnp.concatenate([_dot_nt(qs, kc_ref[:, cols].astype(BF16)), s], axis=1)
        e = jnp.exp(s - jnp.max(s, axis=-1, keepdims=True))
        p = e * (1.0 / jnp.sum(e, axis=-1, keepdims=True))
        a = (p[:lq] - lam * p[lq:]).astype(BF16)
        if cached:
            lc = kc_ref.shape[0]
            o = _dot(a[:, :lc], vc_ref[:, cols].astype(BF16)) + _dot(a[:, lc:], vn_ref[:, cols])
        else:
            o = _dot(a, vn_ref[:, cols])
        o = o * lax.rsqrt(jnp.mean(o * o, axis=-1, keepdims=True) + LN_EPS)
        o = o * g_ref[...] * (1.0 - lam_init)
        o_ref[:, cols] = (o * szb_ref[:, cols].astype(F32)).astype(BF16)


def _lam_specs(n_axes):
    zero = lambda *idx: (0, 0)
    return [pl.BlockSpec((1, DH), zero)] * 4 + [pl.BlockSpec((1, 2 * DH), zero)]


def _attention_prompt(q, k, v, szb, lam_vecs, subln, lam_init, bsz, seq_len):
    blk = pl.BlockSpec((seq_len, D_MODEL), lambda b: (b, 0))
    return pl.pallas_call(
        functools.partial(_attn_kernel, lam_init=lam_init, cached=False),
        grid=(bsz,),
        in_specs=[blk] * 4 + _lam_specs(1),
        out_specs=blk,
        out_shape=jax.ShapeDtypeStruct(q.shape, BF16),
        compiler_params=_cparams(1),
        name="diff_attention_prompt",
    )(q, k, v, szb, *lam_vecs, subln)


def _attention_sample(q, kc, kn, vc, vn, szb, lam_vecs, subln, lam_init, bsz, seq_len, tq):
    nq = seq_len // tq
    past = kc.shape[1]
    qblk = pl.BlockSpec((tq, D_MODEL), lambda b, i: (b * nq + i, 0))
    cblk = pl.BlockSpec((None, past, D_MODEL), lambda b, i: (b, 0, 0))
    nblk = pl.BlockSpec((seq_len, D_MODEL), lambda b, i: (b, 0))
    return pl.pallas_call(
        functools.partial(_attn_kernel, lam_init=lam_init, cached=True),
        grid=(bsz, nq),
        in_specs=[qblk, cblk, nblk, cblk, nblk, qblk] + _lam_specs(2),
        out_specs=qblk,
        out_shape=jax.ShapeDtypeStruct(q.shape, BF16),
        compiler_params=_cparams(2),
        name="diff_attention_sample",
    )(q, kc, kn, vc, vn, szb, *lam_vecs, subln)


def _even_out_kernel(ys_ref, sza_ref, yb_ref, x_ref, gate_ref, wglu_ref, bglu_ref, wout_ref, g_ref, b_ref,
                     o_ref, *, alpha):
    ga = jax.nn.gelu(ys_ref[...].astype(F32))
    glu = jax.nn.sigmoid(_dot(ga.astype(BF16), wglu_ref[...]) + bglu_ref[...])
    ya = (ga * glu * sza_ref[...].astype(F32)).astype(BF16)
    wa = ys_ref.shape[1]
    out = _dot(ya, wout_ref[:wa, :]) + _dot(yb_ref[...], wout_ref[wa:, :])
    o_ref[...] = _post_norm(x_ref[...], gate_ref[...], out, g_ref[...], b_ref[...], alpha)


def _even_out_proj(ys, sza, yb, x2, mod3, w_glu_bf, b_glu, w_out_bf, ln_g, ln_b, layer, cond_of, tm, alpha):
    t, d = x2.shape
    row = pl.BlockSpec((tm, d), lambda i: (i, 0))
    full = lambda shape: pl.BlockSpec(shape, lambda i: (0, 0))
    return pl.pallas_call(
        functools.partial(_even_out_kernel, alpha=alpha),
        grid=(t // tm,),
        in_specs=[row, row, row, row, _mod_spec(layer, 2, cond_of), full(w_glu_bf.shape), full((1, d)),
                  full(w_out_bf.shape), full((1, d)), full((1, d))],
        out_specs=row,
        out_shape=jax.ShapeDtypeStruct((t, d), F32),
        compiler_params=_cparams(1),
        name="even_out_proj",
    )(ys, sza, yb, x2, mod3, w_glu_bf, b_glu.reshape(1, d), w_out_bf, ln_g.reshape(1, d), ln_b.reshape(1, d))


def _dft_mats(n, scale):
    k = lax.broadcasted_iota(jnp.int32, (n, n), 0)
    j = lax.broadcasted_iota(jnp.int32, (n, n), 1)
    ang = ((k * j) % n).astype(F32) * (2.0 * math.pi / n)
    return (jnp.cos(ang) * scale).astype(BF16), (jnp.sin(ang) * scale).astype(BF16)


def _odd_in_kernel(x_ref, shift_ref, scale_ref, w_ref, cc_ref, sc_ref, uc_ref, us_ref, sz_ref):
    h = (_ln_rows(x_ref[...]) * (1.0 + scale_ref[...]) + shift_ref[...]).astype(BF16)
    u = _dot(h, w_ref[:, :W_C]).astype(BF16)
    sz_ref[...] = _silu(_dot(h, w_ref[:, W_C:])).astype(BF16)
    for g in range(NG_C):
        cols = slice(g * GC_C, (g + 1) * GC_C)
        uc_ref[:, cols] = _dot(u[:, cols], cc_ref[...]).astype(BF16)
        us_ref[:, cols] = _dot(u[:, cols], sc_ref[...]).astype(BF16)


def _odd_in_proj(x2, mod3, w_in_bf, cc, sc, layer, cond_of, tm):
    t, d = x2.shape
    row = pl.BlockSpec((tm, d), lambda i: (i, 0))
    wide = pl.BlockSpec((tm, W_C), lambda i: (i, 0))
    full = lambda shape: pl.BlockSpec(shape, lambda i: (0, 0))
    wide_bf = jax.ShapeDtypeStruct((t, W_C), BF16)
    return pl.pallas_call(
        _odd_in_kernel,
        grid=(t // tm,),
        in_specs=[row, _mod_spec(layer, 0, cond_of), _mod_spec(layer, 1, cond_of), full(w_in_bf.shape),
                  full(cc.shape), full(sc.shape)],
        out_specs=[wide] * 3,
        out_shape=[wide_bf] * 3,
        compiler_params=_cparams(1),
        name="odd_in_proj",
    )(x2, mod3, mod3, w_in_bf, cc, sc)


def _odd_out_kernel(cl_ref, sl_ref, uc_ref, us_ref, sz_ref, x_ref, gate_ref, wf_ref, bf_ref, wo_ref, g_ref, b_ref,
                    o_ref, *, alpha):
    mixed = (_dot(cl_ref[...], uc_ref[...]) - _dot(sl_ref[...], us_ref[...])).astype(BF16)
    y = ((_dot(mixed, wf_ref[...]) + bf_ref[...]) * sz_ref[...].astype(F32)).astype(BF16)
    o_ref[...] = _post_norm(x_ref[...], gate_ref[...], _dot(y, wo_ref[...]), g_ref[...], b_ref[...], alpha)


def _odd_out_proj(uc, us, sz, x2, mod3, cl, sl, w_fno_bf, b_fno, w_out_bf, ln_g, ln_b, layer, cond_of,
                  bsz, seq_len, tl, alpha):
    t, d = x2.shape
    nl = seq_len // tl
    full = lambda shape: pl.BlockSpec(shape, lambda b, i: (0, 0))
    dft = pl.BlockSpec((tl, seq_len), lambda b, i: (i, 0))
    seq = pl.BlockSpec((seq_len, W_C), lambda b, i: (b, 0))
    wide = pl.BlockSpec((tl, W_C), lambda b, i: (b * nl + i, 0))
    row = pl.BlockSpec((tl, d), lambda b, i: (b * nl + i, 0))
    return pl.pallas_call(
        functools.partial(_odd_out_kernel, alpha=alpha),
        grid=(bsz, nl),
        in_specs=[dft, dft, seq, seq, wide, row, _mod_spec(layer, 2, cond_of), full(w_fno_bf.shape),
                  full((1, W_C)), full(w_out_bf.shape), full((1, d)), full((1, d))],
        out_specs=row,
        out_shape=jax.ShapeDtypeStruct((t, d), F32),
        compiler_params=_cparams(2),
        name="odd_out_proj",
    )(cl, sl, uc, us, sz, x2, mod3, w_fno_bf, b_fno.reshape(1, W_C), w_out_bf, ln_g.reshape(1, d),
      ln_b.reshape(1, d))


def _rope_tables(seq_len):
    rows = seq_len // GRID_W
    row = jnp.repeat(jnp.arange(rows), GRID_W).astype(F32)
    col = jnp.tile(jnp.arange(GRID_W), rows).astype(F32)
    freqs = ROPE_BASE ** (-jnp.arange(ROT_FREQS, dtype=F32) / ROT_FREQS)
    ang = jnp.concatenate([row[:, None] * freqs, col[:, None] * freqs], axis=-1)
    cos, sin = jnp.cos(ang), jnp.sin(ang)
    cos128 = jnp.tile(cos, (1, 128 // ROT_HALF))
    sin128 = jnp.tile(jnp.concatenate([-sin, sin], axis=-1), (1, 128 // DH))
    return cos128, sin128


def kernel(x_prompt, x_sample, cache_k, cache_v, state_ssm_re, state_ssm_im, c, c_ctx, w_mod, b_mod, ln_g, ln_b, w_in_e, ssm_lam_re, ssm_lam_im, ssm_log_dt, ssm_b_re, ssm_b_im, ssm_c_re, ssm_c_im, ssm_d, w_glu, b_glu, lam_q1, lam_k1, lam_q2, lam_k2, subln_g, w_out_e, w_in_o, w_fno, b_fno, w_out_o):
    depth = w_mod.shape[0]
    bp_, lp, d = x_prompt.shape
    bs_, ls, _ = x_sample.shape
    past = cache_k.shape[2]
    alpha = (2 * depth) ** 0.25
    assert bs_ + 1 <= MOD_ROWS and d == D_MODEL

    cond8 = jnp.concatenate([c_ctx[None, :], c, jnp.zeros((MOD_ROWS - 1 - bs_, d), F32)], axis=0).astype(F32)
    mod3 = _modulation(cond8, w_mod, b_mod).reshape(depth * MOD_ROWS * 3, 1, d)

    tm = 256
    cond_p = lambda *idx: 0
    cond_s_row = lambda i: 1 + i // (ls // tm)
    cond_s_grid = lambda b, i: 1 + b
    rope_tabs = _rope_tables(ls)
    xp = x_prompt.reshape(bp_ * lp, d)
    xs = x_sample.reshape(bs_ * ls, d)
    new_k, new_v, new_sr, new_si = [], [], [], []
    zeros_h0 = jnp.zeros((bp_, 2, G_A, P_A), F32)

    for layer in range(depth):
        if layer % 2 == 0:
            e = layer // 2
            lam_init = 0.8 - 0.6 * math.exp(-0.3 * layer)
            w_in_bf = w_in_e[e].astype(BF16)
            w_glu_bf = w_glu[e].astype(BF16)
            w_out_bf = w_out_e[e].astype(BF16)
            ops = _s5_operators(ssm_lam_re[e], ssm_lam_im[e], ssm_log_dt[e], ssm_b_re[e], ssm_b_im[e],
                                ssm_c_re[e], ssm_c_im[e], ssm_d[e])
            lam_vecs = [v[e].reshape(1, DH).astype(F32) for v in (lam_q1, lam_k1, lam_q2, lam_k2)]
            subln = subln_g[e].reshape(1, 2 * DH).astype(F32)

            u, sza, q, kb, vb, szb, kf, vf = _even_in_proj(xp, mod3, w_in_bf, layer, cond_p, tm, None, lp)
            new_k.append(kf.reshape(bp_, lp, H_B, 2, DH))
            new_v.append(vf.reshape(bp_, lp, H_B, 2 * DH))
            ys, s_re, s_im = _s5_mix(u, ops, zeros_h0, zeros_h0, bp_, lp)
            new_sr.append(s_re)
            new_si.append(s_im)
            yb = _attention_prompt(q, kb, vb, szb, lam_vecs, subln, lam_init, bp_, lp)
            xp = _even_out_proj(ys, sza, yb, xp, mod3, w_glu_bf, b_glu[e], w_out_bf, ln_g[layer], ln_b[layer],
                                layer, cond_p, tm, alpha)

            u, sza, q, kb, vb, szb = _even_in_proj(xs, mod3, w_in_bf, layer, cond_s_row, tm, rope_tabs, ls)
            ys, _, _ = _s5_mix(u, ops, state_ssm_re[:, e], state_ssm_im[:, e], bs_, ls)
            kc = cache_k[:, e].reshape(bs_, past, d)
            vc = cache_v[:, e].reshape(bs_, past, d)
            yb = _attention_sample(q, kc, kb, vc, vb, szb, lam_vecs, subln, lam_init, bs_, ls, 256)
            xs = _even_out_proj(ys, sza, yb, xs, mod3, w_glu_bf, b_glu[e], w_out_bf, ln_g[layer], ln_b[layer],
                                layer, cond_s_row, tm, alpha)
        else:
            o = layer // 2
            w_in_bf = w_in_o[o].astype(BF16)
            w_fno_bf = w_fno[o].astype(BF16)
            w_out_bf = w_out_o[o].astype(BF16)
            cc, sc = _dft_mats(GC_C, GC_C ** -0.5)
            for which in ("prompt", "sample"):
                if which == "prompt":
                    x2, cond_row, cond_grid, bsz, seq = xp, cond_p, cond_p, bp_, lp
                else:
                    x2, cond_row, cond_grid, bsz, seq = xs, cond_s_row, cond_s_grid, bs_, ls
                cl, sl = _dft_mats(seq, seq ** -0.5)
                uc, us, sz = _odd_in_proj(x2, mod3, w_in_bf, cc, sc, layer, cond_row, tm)
                x2 = _odd_out_proj(uc, us, sz, x2, mod3, cl, sl, w_fno_bf, b_fno[o], w_out_bf, ln_g[layer],
                                   ln_b[layer], layer, cond_grid, bsz, seq, min(seq, 256), alpha)
                if which == "prompt":
                    xp = x2
                else:
                    xs = x2

    return (xp.reshape(bp_, lp, d), xs.reshape(bs_, ls, d), jnp.stack(new_k, axis=1), jnp.stack(new_v, axis=1),
            jnp.stack(new_sr, axis=1), jnp.stack(new_si, axis=1))
```

```python
import functools
import math

import jax
import jax.numpy as jnp
from jax import lax
from jax.experimental import pallas as pl
from jax.experimental.pallas import tpu as pltpu

F32 = jnp.float32
BF16 = jnp.bfloat16

D_MODEL = 1024
GRID_W = 64
SSM_GROUP = 16
G_A = D_MODEL // SSM_GROUP
P_A = 64
DH = 64
H_B = D_MODEL // (2 * DH)
ROPE_BASE = 10000.0
ROT_HALF = DH // 2
ROT_FREQS = DH // 4
NG_C = 8
GC_C = 2 * D_MODEL // NG_C
W_C = 2 * D_MODEL
LN_EPS = 1e-5
CHUNK = 16
SSM_TILE = CHUNK * SSM_GROUP
LANES = 128
S5_GROUPS_PER_STEP = 16
S5_TILE_ELEMS = 2048 * 1024
MOD_ROWS = 8
VMEM_LIMIT = 56 * 1024 * 1024


def _cparams(n_axes):
    return pltpu.CompilerParams(dimension_semantics=("arbitrary",) * n_axes, vmem_limit_bytes=VMEM_LIMIT)


def _ln_rows(x):
    mu = jnp.mean(x, axis=-1, keepdims=True)
    xc = x - mu
    var = jnp.mean(xc * xc, axis=-1, keepdims=True)
    return xc * lax.rsqrt(var + LN_EPS)


def _silu(z):
    return z * jax.nn.sigmoid(z)


def _dot(a, b):
    return jnp.dot(a, b, preferred_element_type=F32)


def _dot_nt(a, b):
    return lax.dot_general(a, b, (((1,), (1,)), ((), ())), preferred_element_type=F32)


def _post_norm(x, gate, out, g, b, alpha):
    return _ln_rows(alpha * x + gate * out) * g + b


def _mod_kernel(c_ref, w_ref, b_ref, o_ref):
    c = _silu(c_ref[...]).astype(BF16)
    o_ref[...] = _dot(c, w_ref[...].astype(BF16)) + b_ref[...]


def _modulation(cond8, w_mod, b_mod):
    depth, d, n3 = w_mod.shape
    tn = 1024
    return pl.pallas_call(
        _mod_kernel,
        grid=(depth, n3 // tn),
        in_specs=[
            pl.BlockSpec((MOD_ROWS, d), lambda l, j: (0, 0)),
            pl.BlockSpec((None, d, tn), lambda l, j: (l, 0, j)),
            pl.BlockSpec((None, 1, tn), lambda l, j: (l, 0, j)),
        ],
        out_specs=pl.BlockSpec((None, MOD_ROWS, tn), lambda l, j: (l, 0, j)),
        out_shape=jax.ShapeDtypeStruct((depth, MOD_ROWS, n3), F32),
        compiler_params=_cparams(2),
        name="modulation",
    )(cond8, w_mod, b_mod.reshape(depth, 1, n3))


def _mod_spec(layer, part, cond_of):
    return pl.BlockSpec((None, 1, D_MODEL), lambda *idx: ((layer * MOD_ROWS + cond_of(*idx)) * 3 + part, 0, 0))


def _rope(x, cos, sin_signed, first_half):
    blocks = []
    for hh in range(x.shape[1] // 128):
        b = x[:, hh * 128:(hh + 1) * 128]
        partner = jnp.where(first_half, pltpu.roll(b, 128 - ROT_HALF, 1), pltpu.roll(b, ROT_HALF, 1))
        blocks.append(b * cos + partner * sin_signed)
    return jnp.concatenate(blocks, axis=1)


def _even_in_kernel(*refs, rope):
    if rope:
        x_ref, shift_ref, scale_ref, w_ref, cos_ref, sin_ref, u_ref, sza_ref, q_ref, kb_ref, vb_ref, szb_ref = refs
    else:
        (x_ref, shift_ref, scale_ref, w_ref, u_ref, sza_ref, q_ref, kb_ref, vb_ref, szb_ref,
         kf_ref, vf_ref) = refs
    w = D_MODEL
    h = (_ln_rows(x_ref[...]) * (1.0 + scale_ref[...]) + shift_ref[...]).astype(BF16)

    def proj(j):
        return _dot(h, w_ref[:, j * w:(j + 1) * w])

    u_ref[...] = proj(0).astype(BF16)
    sza_ref[...] = _silu(proj(1)).astype(BF16)
    q = proj(2) * (DH ** -0.5)
    k = proj(3)
    v = proj(4)
    if rope:
        lane = lax.broadcasted_iota(jnp.int32, (1, 128), 1)
        first_half = (lane % DH) < ROT_HALF
        q = _rope(q, cos_ref[...], sin_ref[...], first_half)
        kb_ref[...] = _rope(k, cos_ref[...], sin_ref[...], first_half).astype(BF16)
    else:
        kf_ref[...] = k
        vf_ref[...] = v
        kb_ref[...] = k.astype(BF16)
    q_ref[...] = q.astype(BF16)
    vb_ref[...] = v.astype(BF16)
    szb_ref[...] = _silu(proj(5)).astype(BF16)


def _even_in_proj(x2, mod3, w_in_bf, layer, cond_of, tm, rope_tabs, seq_len):
    t, d = x2.shape
    n = w_in_bf.shape[1]
    row = pl.BlockSpec((tm, d), lambda i: (i, 0))
    in_specs = [row, _mod_spec(layer, 0, cond_of), _mod_spec(layer, 1, cond_of),
                pl.BlockSpec((d, n), lambda i: (0, 0))]
    args = [x2, mod3, mod3, w_in_bf]
    bf = jax.ShapeDtypeStruct((t, d), BF16)
    out_shape = [bf] * 6
    out_specs = [row] * 6
    if rope_tabs is not None:
        tiles_per_seq = seq_len // tm
        tab = pl.BlockSpec((tm, 128), lambda i: (i % tiles_per_seq, 0))
        in_specs += [tab, tab]
        args += list(rope_tabs)
    else:
        out_shape += [jax.ShapeDtypeStruct((t, d), F32)] * 2
        out_specs += [row] * 2
    return pl.pallas_call(
        functools.partial(_even_in_kernel, rope=rope_tabs is not None),
        grid=(t // tm,),
        in_specs=in_specs,
        out_specs=out_specs,
        out_shape=out_shape,
        compiler_params=_cparams(1),
        name="even_in_proj",
    )(*args)


LAG_BATCH = 16


def _lag_kernel(p_ref, c_ref, o_ref):
    for i in range(LAG_BATCH):
        o_ref[i] = jnp.dot(p_ref[i], c_ref[i], preferred_element_type=F32, precision=lax.Precision.HIGHEST)


def _s5_operators(lam_re, lam_im, log_dt, b_re, b_im, c_re, c_im, d_skip):
    lr, li = lam_re.astype(F32), lam_im.astype(F32)
    dt = jnp.exp(log_dt.astype(F32))[..., None]
    mag = jnp.exp(lr * dt)
    ar = mag * jnp.cos(li * dt)
    ai = mag * jnp.sin(li * dt)
    den = lr * lr + li * li
    fr = ((ar - 1.0) * lr + ai * li) / den
    fi = (ai * lr - (ar - 1.0) * li) / den
    br, bi = b_re.astype(F32), b_im.astype(F32)
    bbr = fr[..., None] * br - fi[..., None] * bi
    bbi = fr[..., None] * bi + fi[..., None] * br
    cr, ci = c_re.astype(F32), c_im.astype(F32)
    pr, pi = [jnp.ones_like(ar)], [jnp.zeros_like(ar)]
    for _ in range(CHUNK):
        pr, pi = pr + [pr[-1] * ar - pi[-1] * ai], pi + [pr[-1] * ai + pi[-1] * ar]
    pr, pi = jnp.stack(pr), jnp.stack(pi)

    abr = pr[:CHUNK, ..., None] * bbr - pi[:CHUNK, ..., None] * bbi
    abi = pr[:CHUNK, ..., None] * bbi + pi[:CHUNK, ..., None] * bbr

    lhs = jnp.concatenate([abr, abi], axis=3)
    lhs = lhs.transpose(1, 2, 0, 4, 3).reshape(2 * G_A, SSM_TILE, 2 * P_A)
    rhs = jnp.concatenate([cr, -ci], axis=3).transpose(0, 1, 3, 2).reshape(2 * G_A, 2 * P_A, SSM_GROUP)
    klag = pl.pallas_call(
        _lag_kernel,
        grid=(2 * G_A // LAG_BATCH,),
        in_specs=[pl.BlockSpec((LAG_BATCH, SSM_TILE, 2 * P_A), lambda i: (i, 0, 0)),
                  pl.BlockSpec((LAG_BATCH, 2 * P_A, SSM_GROUP), lambda i: (i, 0, 0))],
        out_specs=pl.BlockSpec((LAG_BATCH, SSM_TILE, SSM_GROUP), lambda i: (i, 0, 0)),
        out_shape=jax.ShapeDtypeStruct((2 * G_A, SSM_TILE, SSM_GROUP), F32),
        compiler_params=_cparams(1),
        name="s5_lag_kernels",
    )(lhs, rhs).reshape(2, G_A, CHUNK, SSM_GROUP, SSM_GROUP)

    s_idx = jnp.arange(CHUNK)[:, None]
    t_idx = jnp.arange(CHUNK)[None, :]
    kf = jnp.where((t_idx >= s_idx)[None, :, :, None, None], klag[0][:, jnp.clip(t_idx - s_idx, 0, CHUNK - 1)], 0.0)
    kb = jnp.where((s_idx >= t_idx)[None, :, :, None, None], klag[1][:, jnp.clip(s_idx - t_idx, 0, CHUNK - 1)], 0.0)
    eye_t = jnp.eye(CHUNK, dtype=F32)[None, :, :, None, None]
    eye_n = jnp.eye(SSM_GROUP, dtype=F32)[None, None, None, :, :]
    dsk = d_skip.astype(F32).reshape(G_A, 1, 1, 1, SSM_GROUP)
    ktoep = (kf + kb + eye_t * eye_n * dsk).transpose(0, 1, 3, 2, 4).reshape(G_A, SSM_TILE, SSM_TILE)

    def st(part, d, rev):
        x = part[:, d]
        x = x[::-1] if rev else x
        return x.transpose(1, 0, 3, 2)
    wst = jnp.concatenate([st(abr, 0, True), st(abr, 1, False), st(abi, 0, True), st(abi, 1, False)], axis=-1)
    wst = wst.reshape(G_A, SSM_TILE, 4 * P_A)

    def out_rows(d, rev):
        er = pr[1:, d][::-1] if rev else pr[1:, d]
        ei = pi[1:, d][::-1] if rev else pi[1:, d]
        re = cr[d][None] * er[:, :, None, :] - ci[d][None] * ei[:, :, None, :]
        im = cr[d][None] * ei[:, :, None, :] + ci[d][None] * er[:, :, None, :]
        return re.transpose(1, 3, 0, 2), -im.transpose(1, 3, 0, 2)
    f_re, f_im = out_rows(0, False)
    b_re_, b_im_ = out_rows(1, True)
    wout = jnp.concatenate([f_re, b_re_, f_im, b_im_], axis=1).reshape(G_A, 4 * P_A, SSM_TILE)

    a16r = jnp.concatenate([pr[CHUNK, 0], pr[CHUNK, 1]], axis=-1)[:, None, :]
    a16i = jnp.concatenate([pi[CHUNK, 0], pi[CHUNK, 1]], axis=-1)[:, None, :]
    tr = lambda w: w.transpose(0, 2, 1).astype(BF16)
    return tr(wst), tr(ktoep), tr(wout), a16r, a16i


def _s5_kernel(u_ref, wst_ref, kt_ref, wo_ref, a16r_ref, a16i_ref, h0r_ref, h0i_ref,
               y_ref, fr_ref, fi_ref, uf, ut, yt, s_sc, hf_sc, hb_sc, *, nb, nc, gpb):
    gb = pl.program_id(2)
    seq = nc * CHUNK
    r = nb * nc
    sw = 2 * P_A

    n_cb = u_ref.shape[1] // LANES

    @pl.when(gb == 0)
    def _():
        for cb in range(n_cb):
            uf[cb] = u_ref[:, cb * LANES:(cb + 1) * LANES].astype(F32)
        for s in range(CHUNK):
            for cb in range(n_cb):
                rows = jnp.concatenate([uf[cb, pl.ds(b * seq + s, nc, stride=CHUNK), :] for b in range(nb)], axis=0)
                ut[cb * LANES:(cb + 1) * LANES, s * r:(s + 1) * r] = rows.T.astype(BF16)

    fwd = (lax.broadcasted_iota(jnp.int32, (1, sw), 1)) < P_A

    def group(j, carry):
        grow = pl.multiple_of((gb * gpb + j) * SSM_GROUP, SSM_GROUP)
        d = jnp.concatenate([ut[pl.ds(grow, SSM_GROUP), s * r:(s + 1) * r] for s in range(CHUNK)], axis=0)
        s_t = _dot(wst_ref[j], d)
        s_sc[0] = s_t[:sw].T
        s_sc[1] = s_t[sw:].T
        a_r, a_i = a16r_ref[j], a16i_ref[j]
        re, im = h0r_ref[j], h0i_ref[j]
        for k in range(nc):
            rf = pl.ds(k, nb, stride=nc)
            rb = pl.ds(nc - 1 - k, nb, stride=nc)
            hf_sc[0, rf, :] = re
            hf_sc[1, rf, :] = im
            hb_sc[0, rb, :] = re
            hb_sc[1, rb, :] = im
            xr = jnp.where(fwd, s_sc[0, rf, :], s_sc[0, rb, :])
            xi = jnp.where(fwd, s_sc[1, rf, :], s_sc[1, rb, :])
            re, im = a_r * re - a_i * im + xr, a_r * im + a_i * re + xi
        fr_ref[j] = re
        fi_ref[j] = im
        hin_t = jnp.concatenate([jnp.where(fwd, hf_sc[0], hb_sc[0]).T, jnp.where(fwd, hf_sc[1], hb_sc[1]).T], axis=0)
        y_t = (_dot(kt_ref[j], d) + _dot(wo_ref[j], hin_t.astype(BF16))).astype(BF16)
        for t in range(CHUNK):
            yt[pl.ds(grow, SSM_GROUP), t * r:(t + 1) * r] = y_t[t * SSM_GROUP:(t + 1) * SSM_GROUP, :]
        return carry

    lax.fori_loop(0, gpb, group, 0)

    @pl.when(gb == pl.num_programs(2) - 1)
    def _():
        for cb in range(n_cb):
            for t in range(CHUNK):
                blk = yt[cb * LANES:(cb + 1) * LANES, t * r:(t + 1) * r].astype(F32).T
                for b in range(nb):
                    uf[cb, pl.ds(b * seq + t, nc, stride=CHUNK), :] = blk[b * nc:(b + 1) * nc, :]
            y_ref[:, cb * LANES:(cb + 1) * LANES] = uf[cb].astype(BF16)


def _s5_mix(u, ops, h0_re, h0_im, bsz, seq_len, nb, cw):
    wst_t, kt_t, wo_t, a16r, a16i = ops
    t, w = u.shape
    nc = seq_len // CHUNK
    r = nb * nc
    gpb = S5_GROUPS_PER_STEP
    n_row_tiles = bsz // nb
    n_col_tiles = w // cw
    gb_per_tile = cw // SSM_GROUP // gpb
    assert r % 128 == 0 and cw % (SSM_GROUP * gpb) == 0 and bsz % nb == 0

    def h0_layout(h0):
        return h0.astype(F32).transpose(2, 0, 1, 3).reshape(G_A, bsz, 2 * P_A)

    tile = pl.BlockSpec((nb * seq_len, cw), lambda i, c, g: (i, c))
    grp = lambda i, c, g: (c * gb_per_tile + g, 0, 0)
    wspec = pl.BlockSpec((gpb, SSM_TILE, SSM_TILE), grp)
    aspec = pl.BlockSpec((gpb, 1, 2 * P_A), grp)
    hspec = pl.BlockSpec((gpb, nb, 2 * P_A), lambda i, c, g: (c * gb_per_tile + g, i, 0))
    state = jax.ShapeDtypeStruct((G_A, bsz, 2 * P_A), F32)
    scratch = [pltpu.VMEM((cw // LANES, nb * seq_len, LANES), F32), pltpu.VMEM((cw, CHUNK * r), BF16),
               pltpu.VMEM((cw, CHUNK * r), BF16)] + [pltpu.VMEM((2, r, 2 * P_A), F32)] * 3

    y, f_re, f_im = pl.pallas_call(
        functools.partial(_s5_kernel, nb=nb, nc=nc, gpb=gpb),
        grid=(n_row_tiles, n_col_tiles, gb_per_tile),
        in_specs=[tile, wspec, wspec, wspec, aspec, aspec, hspec, hspec],
        out_specs=[tile, hspec, hspec],
        out_shape=[jax.ShapeDtypeStruct((t, w), BF16), state, state],
        scratch_shapes=scratch,
        compiler_params=_cparams(3),
        name="s5_core",
    )(u, wst_t, kt_t, wo_t, a16r, a16i, h0_layout(h0_re), h0_layout(h0_im))

    def fin(f):
        return f.reshape(G_A, bsz, 2, P_A).transpose(1, 2, 0, 3)
    return y, fin(f_re), fin(f_im)


def _diff_lambda(lq1, lk1, lq2, lk2, lam_init):
    return (jnp.exp(jnp.sum(lq1[...] * lk1[...], axis=-1, keepdims=True))
            - jnp.exp(jnp.sum(lq2[...] * lk2[...], axis=-1, keepdims=True)) + lam_init)


def _attn_kernel(*refs, lam_init, cached):
    if cached:
        q_ref, kc_ref, kn_ref, vc_ref, vn_ref, szb_ref, lq1, lk1, lq2, lk2, g_ref, o_ref = refs
    else:
        q_ref, kn_ref, vn_ref, szb_ref, lq1, lk1, lq2, lk2, g_ref, o_ref = refs
    lam = _diff_lambda(lq1, lk1, lq2, lk2, lam_init)
    lq = q_ref.shape[0]
    hw = 2 * DH
    low = lax.broadcasted_iota(jnp.int32, (1, hw), 1) < DH
    zero = jnp.zeros((), BF16)
    for h in range(H_B):
        cols = slice(h * hw, (h + 1) * hw)
        qh = q_ref[:, cols]
        qs = jnp.concatenate([jnp.where(low, qh, zero), jnp.where(low, zero, qh)], axis=0)
        s = _dot_nt(qs, kn_ref[:, cols])
        if cached:
            s = jnp.concatenate([_dot_nt(qs, kc_ref[:, cols].astype(BF16)), s], axis=1)
        e = jnp.exp(s - jnp.max(s, axis=-1, keepdims=True))
        p = e * (1.0 / jnp.sum(e, axis=-1, keepdims=True))
        a = (p[:lq] - lam * p[lq:]).astype(BF16)
        if cached:
            lc = kc_ref.shape[0]
            o = _dot(a[:, :lc], vc_ref[:, cols].astype(BF16)) + _dot(a[:, lc:], vn_ref[:, cols])
        else:
            o = _dot(a, vn_ref[:, cols])
        o = o * lax.rsqrt(jnp.mean(o * o, axis=-1, keepdims=True) + LN_EPS)
        o = o * g_ref[...] * (1.0 - lam_init)
        o_ref[:, cols] = (o * szb_ref[:, cols].astype(F32)).astype(BF16)


def _lam_specs(n_axes):
    zero = lambda *idx: (0, 0)
    return [pl.BlockSpec((1, DH), zero)] * 4 + [pl.BlockSpec((1, 2 * DH), zero)]


def _attention_prompt(q, k, v, szb, lam_vecs, subln, lam_init, bsz, seq_len):
    blk = pl.BlockSpec((seq_len, D_MODEL), lambda b: (b, 0))
    return pl.pallas_call(
        functools.partial(_attn_kernel, lam_init=lam_init, cached=False),
        grid=(bsz,),
        in_specs=[blk] * 4 + _lam_specs(1),
        out_specs=blk,
        out_shape=jax.ShapeDtypeStruct(q.shape, BF16),
        compiler_params=_cparams(1),
        name="diff_attention_prompt",
    )(q, k, v, szb, *lam_vecs, subln)


def _attention_sample(q, kc, kn, vc, vn, szb, lam_vecs, subln, lam_init, bsz, seq_len, tq):
    nq = seq_len // tq
    past = kc.shape[1]
    qblk = pl.BlockSpec((tq, D_MODEL), lambda b, i: (b * nq + i, 0))
    cblk = pl.BlockSpec((None, past, D_MODEL), lambda b, i: (b, 0, 0))
    nblk = pl.BlockSpec((seq_len, D_MODEL), lambda b, i: (b, 0))
    return pl.pallas_call(
        functools.partial(_attn_kernel, lam_init=lam_init, cached=True),
        grid=(bsz, nq),
        in_specs=[qblk, cblk, nblk, cblk, nblk, qblk] + _lam_specs(2),
        out_specs=qblk,
        out_shape=jax.ShapeDtypeStruct(q.shape, BF16),
        compiler_params=_cparams(2),
        name="diff_attention_sample",
    )(q, kc, kn, vc, vn, szb, *lam_vecs, subln)


def _even_out_kernel(ys_ref, sza_ref, yb_ref, x_ref, gate_ref, wglu_ref, bglu_ref, wout_ref, g_ref, b_ref,
                     o_ref, *, alpha):
    ga = jax.nn.gelu(ys_ref[...].astype(F32))
    glu = jax.nn.sigmoid(_dot(ga.astype(BF16), wglu_ref[...]) + bglu_ref[...])
    ya = (ga * glu * sza_ref[...].astype(F32)).astype(BF16)
    wa = ys_ref.shape[1]
    out = _dot(ya, wout_ref[:wa, :]) + _dot(yb_ref[...], wout_ref[wa:, :])
    o_ref[...] = _post_norm(x_ref[...], gate_ref[...], out, g_ref[...], b_ref[...], alpha)


def _even_out_proj(ys, sza, yb, x2, mod3, w_glu_bf, b_glu, w_out_bf, ln_g, ln_b, layer, cond_of, tm, alpha):
    t, d = x2.shape
    row = pl.BlockSpec((tm, d), lambda i: (i, 0))
    full = lambda shape: pl.BlockSpec(shape, lambda i: (0, 0))
    return pl.pallas_call(
        functools.partial(_even_out_kernel, alpha=alpha),
        grid=(t // tm,),
        in_specs=[row, row, row, row, _mod_spec(layer, 2, cond_of), full(w_glu_bf.shape), full((1, d)),
                  full(w_out_bf.shape), full((1, d)), full((1, d))],
        out_specs=row,
        out_shape=jax.ShapeDtypeStruct((t, d), F32),
        compiler_params=_cparams(1),
        name="even_out_proj",
    )(ys, sza, yb, x2, mod3, w_glu_bf, b_glu.reshape(1, d), w_out_bf, ln_g.reshape(1, d), ln_b.reshape(1, d))


def _dft_mats(n, scale):
    k = lax.broadcasted_iota(jnp.int32, (n, n), 0)
    j = lax.broadcasted_iota(jnp.int32, (n, n), 1)
    ang = ((k * j) % n).astype(F32) * (2.0 * math.pi / n)
    return (jnp.cos(ang) * scale).astype(BF16), (jnp.sin(ang) * scale).astype(BF16)


def _odd_in_kernel(x_ref, shift_ref, scale_ref, w_ref, cc_ref, sc_ref, uc_ref, us_ref, sz_ref):
    h = (_ln_rows(x_ref[...]) * (1.0 + scale_ref[...]) + shift_ref[...]).astype(BF16)
    u = _dot(h, w_ref[:, :W_C]).astype(BF16)
    sz_ref[...] = _silu(_dot(h, w_ref[:, W_C:])).astype(BF16)
    for g in range(NG_C):
        cols = slice(g * GC_C, (g + 1) * GC_C)
        uc_ref[:, cols] = _dot(u[:, cols], cc_ref[...]).astype(BF16)
        us_ref[:, cols] = _dot(u[:, cols], sc_ref[...]).astype(BF16)


def _odd_in_proj(x2, mod3, w_in_bf, cc, sc, layer, cond_of, tm):
    t, d = x2.shape
    row = pl.BlockSpec((tm, d), lambda i: (i, 0))
    wide = pl.BlockSpec((tm, W_C), lambda i: (i, 0))
    full = lambda shape: pl.BlockSpec(shape, lambda i: (0, 0))
    wide_bf = jax.ShapeDtypeStruct((t, W_C), BF16)
    return pl.pallas_call(
        _odd_in_kernel,
        grid=(t // tm,),
        in_specs=[row, _mod_spec(layer, 0, cond_of), _mod_spec(layer, 1, cond_of), full(w_in_bf.shape),
                  full(cc.shape), full(sc.shape)],
        out_specs=[wide] * 3,
        out_shape=[wide_bf] * 3,
        compiler_params=_cparams(1),
        name="odd_in_proj",
    )(x2, mod3, mod3, w_in_bf, cc, sc)


def _odd_out_kernel(cl_ref, sl_ref, uc_ref, us_ref, sz_ref, x_ref, gate_ref, wf_ref, bf_ref, wo_ref, g_ref, b_ref,
                    o_ref, *, alpha):
    mixed = (_dot(cl_ref[...], uc_ref[...]) - _dot(sl_ref[...], us_ref[...])).astype(BF16)
    y = ((_dot(mixed, wf_ref[...]) + bf_ref[...]) * sz_ref[...].astype(F32)).astype(BF16)
    o_ref[...] = _post_norm(x_ref[...], gate_ref[...], _dot(y, wo_ref[...]), g_ref[...], b_ref[...], alpha)


def _odd_out_proj(uc, us, sz, x2, mod3, cl, sl, w_fno_bf, b_fno, w_out_bf, ln_g, ln_b, layer, cond_of,
                  bsz, seq_len, tl, alpha):
    t, d = x2.shape
    nl = seq_len // tl
    full = lambda shape: pl.BlockSpec(shape, lambda b, i: (0, 0))
    dft = pl.BlockSpec((tl, seq_len), lambda b, i: (i, 0))
    seq = pl.BlockSpec((seq_len, W_C), lambda b, i: (b, 0))
    wide = pl.BlockSpec((tl, W_C), lambda b, i: (b * nl + i, 0))
    row = pl.BlockSpec((tl, d), lambda b, i: (b * nl + i, 0))
    return pl.pallas_call(
        functools.partial(_odd_out_kernel, alpha=alpha),
        grid=(bsz, nl),
        in_specs=[dft, dft, seq, seq, wide, row, _mod_spec(layer, 2, cond_of), full(w_fno_bf.shape),
                  full((1, W_C)), full(w_out_bf.shape), full((1, d)), full((1, d))],
        out_specs=row,
        out_shape=jax.ShapeDtypeStruct((t, d), F32),
        compiler_params=_cparams(2),
        name="odd_out_proj",
    )(cl, sl, uc, us, sz, x2, mod3, w_fno_bf, b_fno.reshape(1, W_C), w_out_bf, ln_g.reshape(1, d),
      ln_b.reshape(1, d))


def _rope_tables(seq_len):
    rows = seq_len // GRID_W
    row = jnp.repeat(jnp.arange(rows), GRID_W).astype(F32)
    col = jnp.tile(jnp.arange(GRID_W), rows).astype(F32)
    freqs = ROPE_BASE ** (-jnp.arange(ROT_FREQS, dtype=F32) / ROT_FREQS)
    ang = jnp.concatenate([row[:, None] * freqs, col[:, None] * freqs], axis=-1)
    cos, sin = jnp.cos(ang), jnp.sin(ang)
    cos128 = jnp.tile(cos, (1, 128 // ROT_HALF))
    sin128 = jnp.tile(jnp.concatenate([-sin, sin], axis=-1), (1, 128 // DH))
    return cos128, sin128


def kernel(x_prompt, x_sample, cache_k, cache_v, state_ssm_re, state_ssm_im, c, c_ctx, w_mod, b_mod, ln_g, ln_b, w_in_e, ssm_lam_re, ssm_lam_im, ssm_log_dt, ssm_b_re, ssm_b_im, ssm_c_re, ssm_c_im, ssm_d, w_glu, b_glu, lam_q1, lam_k1, lam_q2, lam_k2, subln_g, w_out_e, w_in_o, w_fno, b_fno, w_out_o):
    depth = w_mod.shape[0]
    bp_, lp, d = x_prompt.shape
    bs_, ls, _ = x_sample.shape
    past = cache_k.shape[2]
    alpha = (2 * depth) ** 0.25
    assert bs_ + 1 <= MOD_ROWS and d == D_MODEL

    cond8 = jnp.concatenate([c_ctx[None, :], c, jnp.zeros((MOD_ROWS - 1 - bs_, d), F32)], axis=0).astype(F32)
    mod3 = _modulation(cond8, w_mod, b_mod).reshape(depth * MOD_ROWS * 3, 1, d)

    tm = 256
    cond_p = lambda *idx: 0
    cond_s_row = lambda i: 1 + i // (ls // tm)
    cond_s_grid = lambda b, i: 1 + b
    rope_tabs = _rope_tables(ls)
    xp = x_prompt.reshape(bp_ * lp, d)
    xs = x_sample.reshape(bs_ * ls, d)
    new_k, new_v, new_sr, new_si = [], [], [], []
    zeros_h0 = jnp.zeros((bp_, 2, G_A, P_A), F32)

    for layer in range(depth):
        if layer % 2 == 0:
            e = layer // 2
            lam_init = 0.8 - 0.6 * math.exp(-0.3 * layer)
            w_in_bf = w_in_e[e].astype(BF16)
            w_glu_bf = w_glu[e].astype(BF16)
            w_out_bf = w_out_e[e].astype(BF16)
            ops = _s5_operators(ssm_lam_re[e], ssm_lam_im[e], ssm_log_dt[e], ssm_b_re[e], ssm_b_im[e],
                                ssm_c_re[e], ssm_c_im[e], ssm_d[e])
            lam_vecs = [v[e].reshape(1, DH).astype(F32) for v in (lam_q1, lam_k1, lam_q2, lam_k2)]
            subln = subln_g[e].reshape(1, 2 * DH).astype(F32)

            u, sza, q, kb, vb, szb, kf, vf = _even_in_proj(xp, mod3, w_in_bf, layer, cond_p, tm, None, lp)
            new_k.append(kf.reshape(bp_, lp, H_B, 2, DH))
            new_v.append(vf.reshape(bp_, lp, H_B, 2 * DH))
            ys, s_re, s_im = _s5_mix(u, ops, zeros_h0, zeros_h0, bp_, lp, S5_TILE_ELEMS // (lp * d), d)
            new_sr.append(s_re)
            new_si.append(s_im)
            yb = _attention_prompt(q, kb, vb, szb, lam_vecs, subln, lam_init, bp_, lp)
            xp = _even_out_proj(ys, sza, yb, xp, mod3, w_glu_bf, b_glu[e], w_out_bf, ln_g[layer], ln_b[layer],
                                layer, cond_p, tm, alpha)

            u, sza, q, kb, vb, szb = _even_in_proj(xs, mod3, w_in_bf, layer, cond_s_row, tm, rope_tabs, ls)
            ys, _, _ = _s5_mix(u, ops, state_ssm_re[:, e], state_ssm_im[:, e], bs_, ls, bs_,
                               S5_TILE_ELEMS // (bs_ * ls))
            kc = cache_k[:, e].reshape(bs_, past, d)
            vc = cache_v[:, e].reshape(bs_, past, d)
            yb = _attention_sample(q, kc, kb, vc, vb, szb, lam_vecs, subln, lam_init, bs_, ls, 256)
            xs = _even_out_proj(ys, sza, yb, xs, mod3, w_glu_bf, b_glu[e], w_out_bf, ln_g[layer], ln_b[layer],
                                layer, cond_s_row, tm, alpha)
        else:
            o = layer // 2
            w_in_bf = w_in_o[o].astype(BF16)
            w_fno_bf = w_fno[o].astype(BF16)
            w_out_bf = w_out_o[o].astype(BF16)
            cc, sc = _dft_mats(GC_C, GC_C ** -0.5)
            for which in ("prompt", "sample"):
                if which == "prompt":
                    x2, cond_row, cond_grid, bsz, seq = xp, cond_p, cond_p, bp_, lp
                else:
                    x2, cond_row, cond_grid, bsz, seq = xs, cond_s_row, cond_s_grid, bs_, ls
                cl, sl = _dft_mats(seq, seq ** -0.5)
                uc, us, sz = _odd_in_proj(x2, mod3, w_in_bf, cc, sc, layer, cond_row, tm)
                x2 = _odd_out_proj(uc, us, sz, x2, mod3, cl, sl, w_fno_bf, b_fno[o], w_out_bf, ln_g[layer],
                                   ln_b[layer], layer, cond_grid, bsz, seq, min(seq, 256), alpha)
                if which == "prompt":
                    xp = x2
                else:
                    xs = x2

    return (xp.reshape(bp_, lp, d), xs.reshape(bs_, ls, d), jnp.stack(new_k, axis=1), jnp.stack(new_v, axis=1),
            jnp.stack(new_sr, axis=1), jnp.stack(new_si, axis=1))
```

```python
import functools
import math

import jax
import jax.numpy as jnp
from jax import lax
from jax.experimental import pallas as pl
from jax.experimental.pallas import tpu as pltpu

F32 = jnp.float32
BF16 = jnp.bfloat16

D_MODEL = 1024
GRID_W = 64
SSM_GROUP = 16
G_A = D_MODEL // SSM_GROUP
P_A = 64
DH = 64
H_B = D_MODEL // (2 * DH)
ROPE_BASE = 10000.0
ROT_HALF = DH // 2
ROT_FREQS = DH // 4
NG_C = 8
GC_C = 2 * D_MODEL // NG_C
W_C = 2 * D_MODEL
LN_EPS = 1e-5
CHUNK = 16
SSM_TILE = CHUNK * SSM_GROUP
LANES = 128
PERM_TILE = CHUNK * CHUNK
S5_GROUPS_PER_STEP = 16
S5_INTERLEAVE = 4
S5_TILE_ELEMS = 2048 * 1024
MOD_ROWS = 8
VMEM_LIMIT = 56 * 1024 * 1024


def _cparams(n_axes):
    return pltpu.CompilerParams(dimension_semantics=("arbitrary",) * n_axes, vmem_limit_bytes=VMEM_LIMIT)


def _ln_rows(x):
    mu = jnp.mean(x, axis=-1, keepdims=True)
    xc = x - mu
    var = jnp.mean(xc * xc, axis=-1, keepdims=True)
    return xc * lax.rsqrt(var + LN_EPS)


def _silu(z):
    return z * jax.nn.sigmoid(z)


def _dot(a, b):
    return jnp.dot(a, b, preferred_element_type=F32)


def _dot_nt(a, b):
    return lax.dot_general(a, b, (((1,), (1,)), ((), ())), preferred_element_type=F32)


def _post_norm(x, gate, out, g, b, alpha):
    return _ln_rows(alpha * x + gate * out) * g + b


def _mod_kernel(c_ref, w_ref, b_ref, o_ref):
    c = _silu(c_ref[...]).astype(BF16)
    o_ref[...] = _dot(c, w_ref[...].astype(BF16)) + b_ref[...]


def _modulation(cond8, w_mod, b_mod):
    depth, d, n3 = w_mod.shape
    tn = 1024
    return pl.pallas_call(
        _mod_kernel,
        grid=(depth, n3 // tn),
        in_specs=[
            pl.BlockSpec((MOD_ROWS, d), lambda l, j: (0, 0)),
            pl.BlockSpec((None, d, tn), lambda l, j: (l, 0, j)),
            pl.BlockSpec((None, 1, tn), lambda l, j: (l, 0, j)),
        ],
        out_specs=pl.BlockSpec((None, MOD_ROWS, tn), lambda l, j: (l, 0, j)),
        out_shape=jax.ShapeDtypeStruct((depth, MOD_ROWS, n3), F32),
        compiler_params=_cparams(2),
        name="modulation",
    )(cond8, w_mod, b_mod.reshape(depth, 1, n3))


def _mod_spec(layer, part, cond_of):
    return pl.BlockSpec((None, 1, D_MODEL), lambda *idx: ((layer * MOD_ROWS + cond_of(*idx)) * 3 + part, 0, 0))


def _rope(x, cos, sin_signed, first_half):
    blocks = []
    for hh in range(x.shape[1] // 128):
        b = x[:, hh * 128:(hh + 1) * 128]
        partner = jnp.where(first_half, pltpu.roll(b, 128 - ROT_HALF, 1), pltpu.roll(b, ROT_HALF, 1))
        blocks.append(b * cos + partner * sin_signed)
    return jnp.concatenate(blocks, axis=1)


def _chunk_transpose_matrix():
    i = lax.broadcasted_iota(jnp.int32, (PERM_TILE, PERM_TILE), 0)
    j = lax.broadcasted_iota(jnp.int32, (PERM_TILE, PERM_TILE), 1)
    return (j == (i % CHUNK) * CHUNK + i // CHUNK).astype(BF16)


def _even_in_kernel(*refs, rope):
    if rope:
        (x_ref, shift_ref, scale_ref, w_ref, perm_ref, cos_ref, sin_ref,
         u_ref, sza_ref, q_ref, kb_ref, vb_ref, szb_ref) = refs
    else:
        (x_ref, shift_ref, scale_ref, w_ref, perm_ref, u_ref, sza_ref, q_ref, kb_ref, vb_ref, szb_ref,
         kf_ref, vf_ref) = refs
    w = D_MODEL
    h = (_ln_rows(x_ref[...]) * (1.0 + scale_ref[...]) + shift_ref[...]).astype(BF16)

    def proj(j):
        return _dot(h, w_ref[:, j * w:(j + 1) * w])

    u_ref[...] = _dot(perm_ref[...], proj(0).astype(BF16)).astype(BF16)
    sza_ref[...] = _silu(proj(1)).astype(BF16)
    q = proj(2) * (DH ** -0.5)
    k = proj(3)
    v = proj(4)
    if rope:
        lane = lax.broadcasted_iota(jnp.int32, (1, 128), 1)
        first_half = (lane % DH) < ROT_HALF
        q = _rope(q, cos_ref[...], sin_ref[...], first_half)
        kb_ref[...] = _rope(k, cos_ref[...], sin_ref[...], first_half).astype(BF16)
    else:
        kf_ref[...] = k
        vf_ref[...] = v
        kb_ref[...] = k.astype(BF16)
    q_ref[...] = q.astype(BF16)
    vb_ref[...] = v.astype(BF16)
    szb_ref[...] = _silu(proj(5)).astype(BF16)


def _even_in_proj(x2, mod3, w_in_bf, layer, cond_of, tm, rope_tabs, seq_len):
    t, d = x2.shape
    n = w_in_bf.shape[1]
    assert tm == PERM_TILE
    row = pl.BlockSpec((tm, d), lambda i: (i, 0))
    in_specs = [row, _mod_spec(layer, 0, cond_of), _mod_spec(layer, 1, cond_of),
                pl.BlockSpec((d, n), lambda i: (0, 0)), pl.BlockSpec((tm, tm), lambda i: (0, 0))]
    args = [x2, mod3, mod3, w_in_bf, _chunk_transpose_matrix()]
    bf = jax.ShapeDtypeStruct((t, d), BF16)
    out_shape = [bf] * 6
    out_specs = [row] * 6
    if rope_tabs is not None:
        tiles_per_seq = seq_len // tm
        tab = pl.BlockSpec((tm, 128), lambda i: (i % tiles_per_seq, 0))
        in_specs += [tab, tab]
        args += list(rope_tabs)
    else:
        out_shape += [jax.ShapeDtypeStruct((t, d), F32)] * 2
        out_specs += [row] * 2
    return pl.pallas_call(
        functools.partial(_even_in_kernel, rope=rope_tabs is not None),
        grid=(t // tm,),
        in_specs=in_specs,
        out_specs=out_specs,
        out_shape=out_shape,
        compiler_params=_cparams(1),
        name="even_in_proj",
    )(*args)


LAG_BATCH = 16


def _lag_kernel(p_ref, c_ref, o_ref):
    for i in range(LAG_BATCH):
        o_ref[i] = jnp.dot(p_ref[i], c_ref[i], preferred_element_type=F32, precision=lax.Precision.HIGHEST)


def _s5_operators(lam_re, lam_im, log_dt, b_re, b_im, c_re, c_im, d_skip):
    lr, li = lam_re.astype(F32), lam_im.astype(F32)
    dt = jnp.exp(log_dt.astype(F32))[..., None]
    mag = jnp.exp(lr * dt)
    ar = mag * jnp.cos(li * dt)
    ai = mag * jnp.sin(li * dt)
    den = lr * lr + li * li
    fr = ((ar - 1.0) * lr + ai * li) / den
    fi = (ai * lr - (ar - 1.0) * li) / den
    br, bi = b_re.astype(F32), b_im.astype(F32)
    bbr = fr[..., None] * br - fi[..., None] * bi
    bbi = fr[..., None] * bi + fi[..., None] * br
    cr, ci = c_re.astype(F32), c_im.astype(F32)
    pr, pi = [jnp.ones_like(ar)], [jnp.zeros_like(ar)]
    for _ in range(CHUNK):
        pr, pi = pr + [pr[-1] * ar - pi[-1] * ai], pi + [pr[-1] * ai + pi[-1] * ar]
    pr, pi = jnp.stack(pr), jnp.stack(pi)

    abr = pr[:CHUNK, ..., None] * bbr - pi[:CHUNK, ..., None] * bbi
    abi = pr[:CHUNK, ..., None] * bbi + pi[:CHUNK, ..., None] * bbr

    lhs = jnp.concatenate([abr, abi], axis=3)
    lhs = lhs.transpose(1, 2, 0, 4, 3).reshape(2 * G_A, SSM_TILE, 2 * P_A)
    rhs = jnp.concatenate([cr, -ci], axis=3).transpose(0, 1, 3, 2).reshape(2 * G_A, 2 * P_A, SSM_GROUP)
    klag = pl.pallas_call(
        _lag_kernel,
        grid=(2 * G_A // LAG_BATCH,),
        in_specs=[pl.BlockSpec((LAG_BATCH, SSM_TILE, 2 * P_A), lambda i: (i, 0, 0)),
                  pl.BlockSpec((LAG_BATCH, 2 * P_A, SSM_GROUP), lambda i: (i, 0, 0))],
        out_specs=pl.BlockSpec((LAG_BATCH, SSM_TILE, SSM_GROUP), lambda i: (i, 0, 0)),
        out_shape=jax.ShapeDtypeStruct((2 * G_A, SSM_TILE, SSM_GROUP), F32),
        compiler_params=_cparams(1),
        name="s5_lag_kernels",
    )(lhs, rhs).reshape(2, G_A, CHUNK, SSM_GROUP, SSM_GROUP)

    s_idx = jnp.arange(CHUNK)[:, None]
    t_idx = jnp.arange(CHUNK)[None, :]
    kf = jnp.where((t_idx >= s_idx)[None, :, :, None, None], klag[0][:, jnp.clip(t_idx - s_idx, 0, CHUNK - 1)], 0.0)
    kb = jnp.where((s_idx >= t_idx)[None, :, :, None, None], klag[1][:, jnp.clip(s_idx - t_idx, 0, CHUNK - 1)], 0.0)
    eye_t = jnp.eye(CHUNK, dtype=F32)[None, :, :, None, None]
    eye_n = jnp.eye(SSM_GROUP, dtype=F32)[None, None, None, :, :]
    dsk = d_skip.astype(F32).reshape(G_A, 1, 1, 1, SSM_GROUP)
    ktoep = (kf + kb + eye_t * eye_n * dsk).transpose(0, 1, 3, 2, 4).reshape(G_A, SSM_TILE, SSM_TILE)

    def st(part, d, rev):
        x = part[:, d]
        x = x[::-1] if rev else x
        return x.transpose(1, 0, 3, 2)
    wst = jnp.concatenate([st(abr, 0, True), st(abr, 1, False), st(abi, 0, True), st(abi, 1, False)], axis=-1)
    wst = wst.reshape(G_A, SSM_TILE, 4 * P_A)

    def out_rows(d, rev):
        er = pr[1:, d][::-1] if rev else pr[1:, d]
        ei = pi[1:, d][::-1] if rev else pi[1:, d]
        re = cr[d][None] * er[:, :, None, :] - ci[d][None] * ei[:, :, None, :]
        im = cr[d][None] * ei[:, :, None, :] + ci[d][None] * er[:, :, None, :]
        return re.transpose(1, 3, 0, 2), -im.transpose(1, 3, 0, 2)
    f_re, f_im = out_rows(0, False)
    b_re_, b_im_ = out_rows(1, True)
    wout = jnp.concatenate([f_re, b_re_, f_im, b_im_], axis=1).reshape(G_A, 4 * P_A, SSM_TILE)

    a16r = jnp.concatenate([pr[CHUNK, 0], pr[CHUNK, 1]], axis=-1)[:, None, :]
    a16i = jnp.concatenate([pi[CHUNK, 0], pi[CHUNK, 1]], axis=-1)[:, None, :]
    tr = lambda w: w.transpose(0, 2, 1).astype(BF16)
    return tr(wst), tr(ktoep), tr(wout), a16r, a16i


def _s5_kernel(u_ref, wst_ref, kt_ref, wo_ref, a16r_ref, a16i_ref, h0r_ref, h0i_ref,
               pin_ref, pout_ref, y_ref, fr_ref, fi_ref, ut, yt, s_sc, hf_sc, hb_sc, *, nb, nc, gpb):
    gb = pl.program_id(2)
    seq = nc * CHUNK
    r = nb * nc
    sw = 2 * P_A
    n_cb = u_ref.shape[1] // LANES
    pieces = [b * seq + q * PERM_TILE for b in range(nb) for q in range(seq // PERM_TILE)]

    @pl.when(gb == 0)
    def _():
        for s in range(CHUNK):
            cols = []
            for cb in range(n_cb):
                rows = jnp.concatenate([u_ref[o + s * CHUNK:o + (s + 1) * CHUNK, cb * LANES:(cb + 1) * LANES]
                                        for o in pieces], axis=0)
                cols.append(rows.astype(F32).T.astype(BF16))
            ut[:, s * r:(s + 1) * r] = _dot(jnp.concatenate(cols, axis=0), pin_ref[...]).astype(BF16)

    fwd = (lax.broadcasted_iota(jnp.int32, (1, sw), 1)) < P_A

    def groups(it, carry):
        js = [it * S5_INTERLEAVE + i for i in range(S5_INTERLEAVE)]
        grows = [pl.multiple_of((gb * gpb + j) * SSM_GROUP, SSM_GROUP) for j in js]
        ds = [jnp.concatenate([ut[pl.ds(g, SSM_GROUP), s * r:(s + 1) * r] for s in range(CHUNK)], axis=0)
              for g in grows]
        for i, j in enumerate(js):
            s_t = _dot(wst_ref[j], ds[i])
            s_sc[i, 0] = s_t[:sw].T
            s_sc[i, 1] = s_t[sw:].T
        a_r = [a16r_ref[j] for j in js]
        a_i = [a16i_ref[j] for j in js]
        re = [h0r_ref[j] for j in js]
        im = [h0i_ref[j] for j in js]
        for k in range(nc):
            rf = pl.ds(k * nb, nb)
            rb = pl.ds((nc - 1 - k) * nb, nb)
            for i in range(S5_INTERLEAVE):
                hf_sc[i, 0, rf, :] = re[i]
                hf_sc[i, 1, rf, :] = im[i]
                hb_sc[i, 0, rb, :] = re[i]
                hb_sc[i, 1, rb, :] = im[i]
                xr = jnp.where(fwd, s_sc[i, 0, rf, :], s_sc[i, 0, rb, :])
                xi = jnp.where(fwd, s_sc[i, 1, rf, :], s_sc[i, 1, rb, :])
                re[i], im[i] = a_r[i] * re[i] - a_i[i] * im[i] + xr, a_r[i] * im[i] + a_i[i] * re[i] + xi
        for i, j in enumerate(js):
            fr_ref[j] = re[i]
            fi_ref[j] = im[i]
            hin_t = jnp.concatenate([jnp.where(fwd, hf_sc[i, 0], hb_sc[i, 0]).T,
                                     jnp.where(fwd, hf_sc[i, 1], hb_sc[i, 1]).T], axis=0)
            y_t = (_dot(kt_ref[j], ds[i]) + _dot(wo_ref[j], hin_t.astype(BF16))).astype(BF16)
            for t in range(CHUNK):
                yt[pl.ds(grows[i], SSM_GROUP), t * r:(t + 1) * r] = y_t[t * SSM_GROUP:(t + 1) * SSM_GROUP, :]
        return carry

    lax.fori_loop(0, gpb // S5_INTERLEAVE, groups, 0)

    @pl.when(gb == pl.num_programs(2) - 1)
    def _():
        for t in range(CHUNK):
            full = _dot(yt[:, t * r:(t + 1) * r], pout_ref[...])
            for cb in range(n_cb):
                rows = full[cb * LANES:(cb + 1) * LANES, :].T.astype(BF16)
                for idx, o in enumerate(pieces):
                    y_ref[o + t * CHUNK:o + (t + 1) * CHUNK, cb * LANES:(cb + 1) * LANES] = (
                        rows[idx * CHUNK:(idx + 1) * CHUNK, :])


def _s5_mix(u, ops, h0_re, h0_im, bsz, seq_len, nb, cw):
    wst_t, kt_t, wo_t, a16r, a16i = ops
    t, w = u.shape
    nc = seq_len // CHUNK
    r = nb * nc
    gpb = S5_GROUPS_PER_STEP
    n_row_tiles = bsz // nb
    n_col_tiles = w // cw
    gb_per_tile = cw // SSM_GROUP // gpb
    assert r % 128 == 0 and cw % (SSM_GROUP * gpb) == 0 and bsz % nb == 0

    def h0_layout(h0):
        return h0.astype(F32).transpose(2, 0, 1, 3).reshape(G_A, bsz, 2 * P_A)

    tile = pl.BlockSpec((nb * seq_len, cw), lambda i, c, g: (i, c))
    grp = lambda i, c, g: (c * gb_per_tile + g, 0, 0)
    wspec = pl.BlockSpec((gpb, SSM_TILE, SSM_TILE), grp)
    aspec = pl.BlockSpec((gpb, 1, 2 * P_A), grp)
    hspec = pl.BlockSpec((gpb, nb, 2 * P_A), lambda i, c, g: (c * gb_per_tile + g, i, 0))
    state = jax.ShapeDtypeStruct((G_A, bsz, 2 * P_A), F32)
    scratch = [pltpu.VMEM((cw, CHUNK * r), BF16)] * 2 + [pltpu.VMEM((S5_INTERLEAVE, 2, r, 2 * P_A), F32)] * 3
    src = lax.broadcasted_iota(jnp.int32, (r, r), 0)
    dst = lax.broadcasted_iota(jnp.int32, (r, r), 1)
    pin = (dst == (src % nc) * nb + src // nc).astype(BF16)
    pspec = pl.BlockSpec((r, r), lambda i, c, g: (0, 0))

    y, f_re, f_im = pl.pallas_call(
        functools.partial(_s5_kernel, nb=nb, nc=nc, gpb=gpb),
        grid=(n_row_tiles, n_col_tiles, gb_per_tile),
        in_specs=[tile, wspec, wspec, wspec, aspec, aspec, hspec, hspec, pspec, pspec],
        out_specs=[tile, hspec, hspec],
        out_shape=[jax.ShapeDtypeStruct((t, w), BF16), state, state],
        scratch_shapes=scratch,
        compiler_params=_cparams(3),
        name="s5_core",
    )(u, wst_t, kt_t, wo_t, a16r, a16i, h0_layout(h0_re), h0_layout(h0_im), pin, pin.T)

    def fin(f):
        return f.reshape(G_A, bsz, 2, P_A).transpose(1, 2, 0, 3)
    return y, fin(f_re), fin(f_im)


def _diff_lambda(lq1, lk1, lq2, lk2, lam_init):
    return (jnp.exp(jnp.sum(lq1[...] * lk1[...], axis=-1, keepdims=True))
            - jnp.exp(jnp.sum(lq2[...] * lk2[...], axis=-1, keepdims=True)) + lam_init)


def _attn_kernel(*refs, lam_init, cached):
    if cached:
        q_ref, kc_ref, kn_ref, vc_ref, vn_ref, szb_ref, lq1, lk1, lq2, lk2, g_ref, o_ref = refs
    else:
        q_ref, kn_ref, vn_ref, szb_ref, lq1, lk1, lq2, lk2, g_ref, o_ref = refs
    lam = _diff_lambda(lq1, lk1, lq2, lk2, lam_init)
    lq = q_ref.shape[0]
    hw = 2 * DH
    low = lax.broadcasted_iota(jnp.int32, (1, hw), 1) < DH
    zero = jnp.zeros((), BF16)
    for h in range(H_B):
        cols = slice(h * hw, (h + 1) * hw)
        qh = q_ref[:, cols]
        qs = jnp.concatenate([jnp.where(low, qh, zero), jnp.where(low, zero, qh)], axis=0)
        s = _dot_nt(qs, kn_ref[:, cols])
        if cached:
            s = jnp.concatenate([_dot_nt(qs, kc_ref[:, cols].astype(BF16)), s], axis=1)
        e = jnp.exp(s - jnp.max(s, axis=-1, keepdims=True))
        p = e * (1.0 / jnp.sum(e, axis=-1, keepdims=True))
        a = (p[:lq] - lam * p[lq:]).astype(BF16)
        if cached:
            lc = kc_ref.shape[0]
            o = _dot(a[:, :lc], vc_ref[:, cols].astype(BF16)) + _dot(a[:, lc:], vn_ref[:, cols])
        else:
            o = _dot(a, vn_ref[:, cols])
        o = o * lax.rsqrt(jnp.mean(o * o, axis=-1, keepdims=True) + LN_EPS)
        o = o * g_ref[...] * (1.0 - lam_init)
        o_ref[:, cols] = (o * szb_ref[:, cols].astype(F32)).astype(BF16)


def _lam_specs(n_axes):
    zero = lambda *idx: (0, 0)
    return [pl.BlockSpec((1, DH), zero)] * 4 + [pl.BlockSpec((1, 2 * DH), zero)]


def _attention_prompt(q, k, v, szb, lam_vecs, subln, lam_init, bsz, seq_len):
    blk = pl.BlockSpec((seq_len, D_MODEL), lambda b: (b, 0))
    return pl.pallas_call(
        functools.partial(_attn_kernel, lam_init=lam_init, cached=False),
        grid=(bsz,),
        in_specs=[blk] * 4 + _lam_specs(1),
        out_specs=blk,
        out_shape=jax.ShapeDtypeStruct(q.shape, BF16),
        compiler_params=_cparams(1),
        name="diff_attention_prompt",
    )(q, k, v, szb, *lam_vecs, subln)


def _attention_sample(q, kc, kn, vc, vn, szb, lam_vecs, subln, lam_init, bsz, seq_len, tq):
    nq = seq_len // tq
    past = kc.shape[1]
    qblk = pl.BlockSpec((tq, D_MODEL), lambda b, i: (b * nq + i, 0))
    cblk = pl.BlockSpec((None, past, D_MODEL), lambda b, i: (b, 0, 0))
    nblk = pl.BlockSpec((seq_len, D_MODEL), lambda b, i: (b, 0))
    return pl.pallas_call(
        functools.partial(_attn_kernel, lam_init=lam_init, cached=True),
        grid=(bsz, nq),
        in_specs=[qblk, cblk, nblk, cblk, nblk, qblk] + _lam_specs(2),
        out_specs=qblk,
        out_shape=jax.ShapeDtypeStruct(q.shape, BF16),
        compiler_params=_cparams(2),
        name="diff_attention_sample",
    )(q, kc, kn, vc, vn, szb, *lam_vecs, subln)


def _even_out_kernel(ys_ref, sza_ref, yb_ref, x_ref, gate_ref, wglu_ref, bglu_ref, wout_ref, g_ref, b_ref,
                     perm_ref, o_ref, *, alpha):
    ga = jax.nn.gelu(_dot(perm_ref[...], ys_ref[...]))
    glu = jax.nn.sigmoid(_dot(ga.astype(BF16), wglu_ref[...]) + bglu_ref[...])
    ya = (ga * glu * sza_ref[...].astype(F32)).astype(BF16)
    wa = ys_ref.shape[1]
    out = _dot(ya, wout_ref[:wa, :]) + _dot(yb_ref[...], wout_ref[wa:, :])
    o_ref[...] = _post_norm(x_ref[...], gate_ref[...], out, g_ref[...], b_ref[...], alpha)


def _even_out_proj(ys, sza, yb, x2, mod3, w_glu_bf, b_glu, w_out_bf, ln_g, ln_b, layer, cond_of, tm, alpha):
    t, d = x2.shape
    assert tm == PERM_TILE
    row = pl.BlockSpec((tm, d), lambda i: (i, 0))
    full = lambda shape: pl.BlockSpec(shape, lambda i: (0, 0))
    return pl.pallas_call(
        functools.partial(_even_out_kernel, alpha=alpha),
        grid=(t // tm,),
        in_specs=[row, row, row, row, _mod_spec(layer, 2, cond_of), full(w_glu_bf.shape), full((1, d)),
                  full(w_out_bf.shape), full((1, d)), full((1, d)), full((tm, tm))],
        out_specs=row,
        out_shape=jax.ShapeDtypeStruct((t, d), F32),
        compiler_params=_cparams(1),
        name="even_out_proj",
    )(ys, sza, yb, x2, mod3, w_glu_bf, b_glu.reshape(1, d), w_out_bf, ln_g.reshape(1, d), ln_b.reshape(1, d),
      _chunk_transpose_matrix())


def _dft_mats(n, scale):
    k = lax.broadcasted_iota(jnp.int32, (n, n), 0)
    j = lax.broadcasted_iota(jnp.int32, (n, n), 1)
    ang = ((k * j) % n).astype(F32) * (2.0 * math.pi / n)
    return (jnp.cos(ang) * scale).astype(BF16), (jnp.sin(ang) * scale).astype(BF16)


def _odd_in_kernel(x_ref, shift_ref, scale_ref, w_ref, cc_ref, sc_ref, uc_ref, us_ref, sz_ref):
    h = (_ln_rows(x_ref[...]) * (1.0 + scale_ref[...]) + shift_ref[...]).astype(BF16)
    u = _dot(h, w_ref[:, :W_C]).astype(BF16)
    sz_ref[...] = _silu(_dot(h, w_ref[:, W_C:])).astype(BF16)
    for g in range(NG_C):
        cols = slice(g * GC_C, (g + 1) * GC_C)
        uc_ref[:, cols] = _dot(u[:, cols], cc_ref[...]).astype(BF16)
        us_ref[:, cols] = _dot(u[:, cols], sc_ref[...]).astype(BF16)


def _odd_in_proj(x2, mod3, w_in_bf, cc, sc, layer, cond_of, tm):
    t, d = x2.shape
    row = pl.BlockSpec((tm, d), lambda i: (i, 0))
    wide = pl.BlockSpec((tm, W_C), lambda i: (i, 0))
    full = lambda shape: pl.BlockSpec(shape, lambda i: (0, 0))
    wide_bf = jax.ShapeDtypeStruct((t, W_C), BF16)
    return pl.pallas_call(
        _odd_in_kernel,
        grid=(t // tm,),
        in_specs=[row, _mod_spec(layer, 0, cond_of), _mod_spec(layer, 1, cond_of), full(w_in_bf.shape),
                  full(cc.shape), full(sc.shape)],
        out_specs=[wide] * 3,
        out_shape=[wide_bf] * 3,
        compiler_params=_cparams(1),
        name="odd_in_proj",
    )(x2, mod3, mod3, w_in_bf, cc, sc)


def _odd_out_kernel(cl_ref, sl_ref, uc_ref, us_ref, sz_ref, x_ref, gate_ref, wf_ref, bf_ref, wo_ref, g_ref, b_ref,
                    o_ref, *, alpha):
    mixed = (_dot(cl_ref[...], uc_ref[...]) - _dot(sl_ref[...], us_ref[...])).astype(BF16)
    y = ((_dot(mixed, wf_ref[...]) + bf_ref[...]) * sz_ref[...].astype(F32)).astype(BF16)
    o_ref[...] = _post_norm(x_ref[...], gate_ref[...], _dot(y, wo_ref[...]), g_ref[...], b_ref[...], alpha)


def _odd_out_proj(uc, us, sz, x2, mod3, cl, sl, w_fno_bf, b_fno, w_out_bf, ln_g, ln_b, layer, cond_of,
                  bsz, seq_len, tl, alpha):
    t, d = x2.shape
    nl = seq_len // tl
    full = lambda shape: pl.BlockSpec(shape, lambda b, i: (0, 0))
    dft = pl.BlockSpec((tl, seq_len), lambda b, i: (i, 0))
    seq = pl.BlockSpec((seq_len, W_C), lambda b, i: (b, 0))
    wide = pl.BlockSpec((tl, W_C), lambda b, i: (b * nl + i, 0))
    row = pl.BlockSpec((tl, d), lambda b, i: (b * nl + i, 0))
    return pl.pallas_call(
        functools.partial(_odd_out_kernel, alpha=alpha),
        grid=(bsz, nl),
        in_specs=[dft, dft, seq, seq, wide, row, _mod_spec(layer, 2, cond_of), full(w_fno_bf.shape),
                  full((1, W_C)), full(w_out_bf.shape), full((1, d)), full((1, d))],
        out_specs=row,
        out_shape=jax.ShapeDtypeStruct((t, d), F32),
        compiler_params=_cparams(2),
        name="odd_out_proj",
    )(cl, sl, uc, us, sz, x2, mod3, w_fno_bf, b_fno.reshape(1, W_C), w_out_bf, ln_g.reshape(1, d),
      ln_b.reshape(1, d))


def _rope_tables(seq_len):
    rows = seq_len // GRID_W
    row = jnp.repeat(jnp.arange(rows), GRID_W).astype(F32)
    col = jnp.tile(jnp.arange(GRID_W), rows).astype(F32)
    freqs = ROPE_BASE ** (-jnp.arange(ROT_FREQS, dtype=F32) / ROT_FREQS)
    ang = jnp.concatenate([row[:, None] * freqs, col[:, None] * freqs], axis=-1)
    cos, sin = jnp.cos(ang), jnp.sin(ang)
    cos128 = jnp.tile(cos, (1, 128 // ROT_HALF))
    sin128 = jnp.tile(jnp.concatenate([-sin, sin], axis=-1), (1, 128 // DH))
    return cos128, sin128


def kernel(x_prompt, x_sample, cache_k, cache_v, state_ssm_re, state_ssm_im, c, c_ctx, w_mod, b_mod, ln_g, ln_b, w_in_e, ssm_lam_re, ssm_lam_im, ssm_log_dt, ssm_b_re, ssm_b_im, ssm_c_re, ssm_c_im, ssm_d, w_glu, b_glu, lam_q1, lam_k1, lam_q2, lam_k2, subln_g, w_out_e, w_in_o, w_fno, b_fno, w_out_o):
    depth = w_mod.shape[0]
    bp_, lp, d = x_prompt.shape
    bs_, ls, _ = x_sample.shape
    past = cache_k.shape[2]
    alpha = (2 * depth) ** 0.25
    assert bs_ + 1 <= MOD_ROWS and d == D_MODEL

    cond8 = jnp.concatenate([c_ctx[None, :], c, jnp.zeros((MOD_ROWS - 1 - bs_, d), F32)], axis=0).astype(F32)
    mod3 = _modulation(cond8, w_mod, b_mod).reshape(depth * MOD_ROWS * 3, 1, d)

    tm = 256
    cond_p = lambda *idx: 0
    cond_s_row = lambda i: 1 + i // (ls // tm)
    cond_s_grid = lambda b, i: 1 + b
    rope_tabs = _rope_tables(ls)
    xp = x_prompt.reshape(bp_ * lp, d)
    xs = x_sample.reshape(bs_ * ls, d)
    new_k, new_v, new_sr, new_si = [], [], [], []
    zeros_h0 = jnp.zeros((bp_, 2, G_A, P_A), F32)

    for layer in range(depth):
        if layer % 2 == 0:
            e = layer // 2
            lam_init = 0.8 - 0.6 * math.exp(-0.3 * layer)
            w_in_bf = w_in_e[e].astype(BF16)
            w_glu_bf = w_glu[e].astype(BF16)
            w_out_bf = w_out_e[e].astype(BF16)
            ops = _s5_operators(ssm_lam_re[e], ssm_lam_im[e], ssm_log_dt[e], ssm_b_re[e], ssm_b_im[e],
                                ssm_c_re[e], ssm_c_im[e], ssm_d[e])
            lam_vecs = [v[e].reshape(1, DH).astype(F32) for v in (lam_q1, lam_k1, lam_q2, lam_k2)]
            subln = subln_g[e].reshape(1, 2 * DH).astype(F32)

            u, sza, q, kb, vb, szb, kf, vf = _even_in_proj(xp, mod3, w_in_bf, layer, cond_p, tm, None, lp)
            new_k.append(kf.reshape(bp_, lp, H_B, 2, DH))
            new_v.append(vf.reshape(bp_, lp, H_B, 2 * DH))
            ys, s_re, s_im = _s5_mix(u, ops, zeros_h0, zeros_h0, bp_, lp, S5_TILE_ELEMS // (lp * d), d)
            new_sr.append(s_re)
            new_si.append(s_im)
            yb = _attention_prompt(q, kb, vb, szb, lam_vecs, subln, lam_init, bp_, lp)
            xp = _even_out_proj(ys, sza, yb, xp, mod3, w_glu_bf, b_glu[e], w_out_bf, ln_g[layer], ln_b[layer],
                                layer, cond_p, tm, alpha)

            u, sza, q, kb, vb, szb = _even_in_proj(xs, mod3, w_in_bf, layer, cond_s_row, tm, rope_tabs, ls)
            ys, _, _ = _s5_mix(u, ops, state_ssm_re[:, e], state_ssm_im[:, e], bs_, ls, bs_,
                               S5_TILE_ELEMS // (bs_ * ls))
            kc = cache_k[:, e].reshape(bs_, past, d)
            vc = cache_v[:, e].reshape(bs_, past, d)
            yb = _attention_sample(q, kc, kb, vc, vb, szb, lam_vecs, subln, lam_init, bs_, ls, 256)
            xs = _even_out_proj(ys, sza, yb, xs, mod3, w_glu_bf, b_glu[e], w_out_bf, ln_g[layer], ln_b[layer],
                                layer, cond_s_row, tm, alpha)
        else:
            o = layer // 2
            w_in_bf = w_in_o[o].astype(BF16)
            w_fno_bf = w_fno[o].astype(BF16)
            w_out_bf = w_out_o[o].astype(BF16)
            cc, sc = _dft_mats(GC_C, GC_C ** -0.5)
            for which in ("prompt", "sample"):
                if which == "prompt":
                    x2, cond_row, cond_grid, bsz, seq = xp, cond_p, cond_p, bp_, lp
                else:
                    x2, cond_row, cond_grid, bsz, seq = xs, cond_s_row, cond_s_grid, bs_, ls
                cl, sl = _dft_mats(seq, seq ** -0.5)
                uc, us, sz = _odd_in_proj(x2, mod3, w_in_bf, cc, sc, layer, cond_row, tm)
                x2 = _odd_out_proj(uc, us, sz, x2, mod3, cl, sl, w_fno_bf, b_fno[o], w_out_bf, ln_g[layer],
                                   ln_b[layer], layer, cond_grid, bsz, seq, min(seq, 256), alpha)
                if which == "prompt":
                    xp = x2
                else:
                    xs = x2

    return (xp.reshape(bp_, lp, d), xs.reshape(bs_, ls, d), jnp.stack(new_k, axis=1), jnp.stack(new_v, axis=1),
            jnp.stack(new_sr, axis=1), jnp.stack(new_si, axis=1))
```

```python
import functools
import math

import jax
import jax.numpy as jnp
import numpy as np
from jax import lax
from jax.experimental import pallas as pl
from jax.experimental.pallas import tpu as pltpu

F32 = jnp.float32
BF16 = jnp.bfloat16

D_MODEL = 1024
GRID_W = 64
SSM_GROUP = 16
G_A = D_MODEL // SSM_GROUP
P_A = 64
DH = 64
H_B = D_MODEL // (2 * DH)
ROPE_BASE = 10000.0
ROT_HALF = DH // 2
ROT_FREQS = DH // 4
NG_C = 8
GC_C = 2 * D_MODEL // NG_C
W_C = 2 * D_MODEL
LN_EPS = 1e-5
LOG2_E = 1.4426950408889634
CHUNK = 16
SSM_TILE = CHUNK * SSM_GROUP
LANES = 128
PERM_TILE = CHUNK * CHUNK
S5_GROUPS_PER_STEP = 16
S5_INTERLEAVE = 4
S5_TILE_ELEMS = 2048 * 1024
MOD_ROWS = 8
VMEM_LIMIT = 56 * 1024 * 1024


def _cparams(n_axes):
    return pltpu.CompilerParams(dimension_semantics=("arbitrary",) * n_axes, vmem_limit_bytes=VMEM_LIMIT)


def _ln_rows(x):
    mu = jnp.mean(x, axis=-1, keepdims=True)
    xc = x - mu
    var = jnp.mean(xc * xc, axis=-1, keepdims=True)
    return xc * lax.rsqrt(var + LN_EPS)


def _silu(z):
    return z * jax.nn.sigmoid(z)


def _dot(a, b):
    return jnp.dot(a, b, preferred_element_type=F32)


def _dot_nt(a, b):
    return lax.dot_general(a, b, (((1,), (1,)), ((), ())), preferred_element_type=F32)


def _post_norm(x, gate, out, g, b, alpha):
    return _ln_rows(alpha * x + gate * out) * g + b


def _mod_kernel(c_ref, w_ref, b_ref, o_ref):
    c = _silu(c_ref[...]).astype(BF16)
    o_ref[...] = _dot(c, w_ref[...].astype(BF16)) + b_ref[...]


def _modulation(cond8, w_mod, b_mod):
    depth, d, n3 = w_mod.shape
    tn = 1024
    return pl.pallas_call(
        _mod_kernel,
        grid=(depth, n3 // tn),
        in_specs=[
            pl.BlockSpec((MOD_ROWS, d), lambda l, j: (0, 0)),
            pl.BlockSpec((None, d, tn), lambda l, j: (l, 0, j)),
            pl.BlockSpec((None, 1, tn), lambda l, j: (l, 0, j)),
        ],
        out_specs=pl.BlockSpec((None, MOD_ROWS, tn), lambda l, j: (l, 0, j)),
        out_shape=jax.ShapeDtypeStruct((depth, MOD_ROWS, n3), F32),
        compiler_params=_cparams(2),
        name="modulation",
    )(cond8, w_mod, b_mod.reshape(depth, 1, n3))


def _mod_spec(layer, part, cond_of):
    return pl.BlockSpec((None, 1, D_MODEL), lambda *idx: ((layer * MOD_ROWS + cond_of(*idx)) * 3 + part, 0, 0))


def _rope(x, cos, sin_signed, first_half):
    blocks = []
    for hh in range(x.shape[1] // 128):
        b = x[:, hh * 128:(hh + 1) * 128]
        partner = jnp.where(first_half, pltpu.roll(b, 128 - ROT_HALF, 1), pltpu.roll(b, ROT_HALF, 1))
        blocks.append(b * cos + partner * sin_signed)
    return jnp.concatenate(blocks, axis=1)


def _chunk_transpose_matrix():
    i = lax.broadcasted_iota(jnp.int32, (PERM_TILE, PERM_TILE), 0)
    j = lax.broadcasted_iota(jnp.int32, (PERM_TILE, PERM_TILE), 1)
    return (j == (i % CHUNK) * CHUNK + i // CHUNK).astype(BF16)


def _even_in_kernel(*refs, rope):
    if rope:
        (x_ref, shift_ref, scale_ref, w_ref, perm_ref, cos_ref, sin_ref,
         u_ref, sza_ref, q_ref, kb_ref, vb_ref, szb_ref) = refs
    else:
        (x_ref, shift_ref, scale_ref, w_ref, perm_ref, u_ref, sza_ref, q_ref, kb_ref, vb_ref, szb_ref,
         kf_ref, vf_ref) = refs
    w = D_MODEL
    h = (_ln_rows(x_ref[...]) * (1.0 + scale_ref[...]) + shift_ref[...]).astype(BF16)

    def proj(j):
        return _dot(h, w_ref[:, j * w:(j + 1) * w])

    u_ref[...] = _dot(perm_ref[...], proj(0).astype(BF16)).astype(BF16)
    sza_ref[...] = _silu(proj(1)).astype(BF16)
    q = proj(2) * (DH ** -0.5 * LOG2_E)
    k = proj(3)
    v = proj(4)
    if rope:
        lane = lax.broadcasted_iota(jnp.int32, (1, 128), 1)
        first_half = (lane % DH) < ROT_HALF
        q = _rope(q, cos_ref[...], sin_ref[...], first_half)
        kb_ref[...] = _rope(k, cos_ref[...], sin_ref[...], first_half).astype(BF16)
    else:
        kf_ref[...] = k
        vf_ref[...] = v
        kb_ref[...] = k.astype(BF16)
    q_ref[...] = q.astype(BF16)
    vb_ref[...] = v.astype(BF16)
    szb_ref[...] = _silu(proj(5)).astype(BF16)


def _even_in_proj(x2, mod3, w_in_bf, layer, cond_of, tm, rope_tabs, seq_len):
    t, d = x2.shape
    n = w_in_bf.shape[1]
    assert tm == PERM_TILE
    row = pl.BlockSpec((tm, d), lambda i: (i, 0))
    in_specs = [row, _mod_spec(layer, 0, cond_of), _mod_spec(layer, 1, cond_of),
                pl.BlockSpec((d, n), lambda i: (0, 0)), pl.BlockSpec((tm, tm), lambda i: (0, 0))]
    args = [x2, mod3, mod3, w_in_bf, _chunk_transpose_matrix()]
    bf = jax.ShapeDtypeStruct((t, d), BF16)
    out_shape = [bf] * 6
    out_specs = [row] * 6
    if rope_tabs is not None:
        tiles_per_seq = seq_len // tm
        tab = pl.BlockSpec((tm, 128), lambda i: (i % tiles_per_seq, 0))
        in_specs += [tab, tab]
        args += list(rope_tabs)
    else:
        out_shape += [jax.ShapeDtypeStruct((t, d), F32)] * 2
        out_specs += [row] * 2
    return pl.pallas_call(
        functools.partial(_even_in_kernel, rope=rope_tabs is not None),
        grid=(t // tm,),
        in_specs=in_specs,
        out_specs=out_specs,
        out_shape=out_shape,
        compiler_params=_cparams(1),
        name="even_in_proj",
    )(*args)


LAG_BATCH = 16


def _lag_kernel(p_ref, c_ref, o_ref):
    for i in range(LAG_BATCH):
        o_ref[i] = jnp.dot(p_ref[i], c_ref[i], preferred_element_type=F32, precision=lax.Precision.HIGHEST)


def _s5_operators(lam_re, lam_im, log_dt, b_re, b_im, c_re, c_im, d_skip):
    lr, li = lam_re.astype(F32), lam_im.astype(F32)
    dt = jnp.exp(log_dt.astype(F32))[..., None]
    mag = jnp.exp(lr * dt)
    ar = mag * jnp.cos(li * dt)
    ai = mag * jnp.sin(li * dt)
    den = lr * lr + li * li
    fr = ((ar - 1.0) * lr + ai * li) / den
    fi = (ai * lr - (ar - 1.0) * li) / den
    br, bi = b_re.astype(F32), b_im.astype(F32)
    bbr = fr[..., None] * br - fi[..., None] * bi
    bbi = fr[..., None] * bi + fi[..., None] * br
    cr, ci = c_re.astype(F32), c_im.astype(F32)
    e_pow = jnp.arange(CHUNK + 1, dtype=F32)[:, None, None, None]
    pmag = jnp.exp(e_pow * (lr * dt))
    pr = pmag * jnp.cos(e_pow * (li * dt))
    pi = pmag * jnp.sin(e_pow * (li * dt))

    abr = pr[:CHUNK, ..., None] * bbr - pi[:CHUNK, ..., None] * bbi
    abi = pr[:CHUNK, ..., None] * bbi + pi[:CHUNK, ..., None] * bbr

    lhs = jnp.concatenate([abr, abi], axis=3)
    lhs = lhs.transpose(1, 2, 0, 4, 3).reshape(2 * G_A, SSM_TILE, 2 * P_A)
    rhs = jnp.concatenate([cr, -ci], axis=3).transpose(0, 1, 3, 2).reshape(2 * G_A, 2 * P_A, SSM_GROUP)
    klag = pl.pallas_call(
        _lag_kernel,
        grid=(2 * G_A // LAG_BATCH,),
        in_specs=[pl.BlockSpec((LAG_BATCH, SSM_TILE, 2 * P_A), lambda i: (i, 0, 0)),
                  pl.BlockSpec((LAG_BATCH, 2 * P_A, SSM_GROUP), lambda i: (i, 0, 0))],
        out_specs=pl.BlockSpec((LAG_BATCH, SSM_TILE, SSM_GROUP), lambda i: (i, 0, 0)),
        out_shape=jax.ShapeDtypeStruct((2 * G_A, SSM_TILE, SSM_GROUP), F32),
        compiler_params=_cparams(1),
        name="s5_lag_kernels",
    )(lhs, rhs).reshape(2, G_A, CHUNK, SSM_GROUP, SSM_GROUP)

    s_idx = jnp.arange(CHUNK)[:, None]
    t_idx = jnp.arange(CHUNK)[None, :]
    kf = jnp.where((t_idx >= s_idx)[None, :, :, None, None], klag[0][:, jnp.clip(t_idx - s_idx, 0, CHUNK - 1)], 0.0)
    kb = jnp.where((s_idx >= t_idx)[None, :, :, None, None], klag[1][:, jnp.clip(s_idx - t_idx, 0, CHUNK - 1)], 0.0)
    eye_t = jnp.eye(CHUNK, dtype=F32)[None, :, :, None, None]
    eye_n = jnp.eye(SSM_GROUP, dtype=F32)[None, None, None, :, :]
    dsk = d_skip.astype(F32).reshape(G_A, 1, 1, 1, SSM_GROUP)
    ktoep = (kf + kb + eye_t * eye_n * dsk).transpose(0, 1, 3, 2, 4).reshape(G_A, SSM_TILE, SSM_TILE)

    def st(part, d, rev):
        x = part[:, d]
        x = x[::-1] if rev else x
        return x.transpose(1, 0, 3, 2)
    wst = jnp.concatenate([st(abr, 0, True), st(abr, 1, False), st(abi, 0, True), st(abi, 1, False)], axis=-1)
    wst = wst.reshape(G_A, SSM_TILE, 4 * P_A)

    def out_rows(d, rev):
        er = pr[1:, d][::-1] if rev else pr[1:, d]
        ei = pi[1:, d][::-1] if rev else pi[1:, d]
        re = cr[d][None] * er[:, :, None, :] - ci[d][None] * ei[:, :, None, :]
        im = cr[d][None] * ei[:, :, None, :] + ci[d][None] * er[:, :, None, :]
        return re.transpose(1, 3, 0, 2), -im.transpose(1, 3, 0, 2)
    f_re, f_im = out_rows(0, False)
    b_re_, b_im_ = out_rows(1, True)
    wout = jnp.concatenate([f_re, b_re_, f_im, b_im_], axis=1).reshape(G_A, 4 * P_A, SSM_TILE)

    a16r = jnp.concatenate([pr[CHUNK, 0], pr[CHUNK, 1]], axis=-1)[:, None, :]
    a16i = jnp.concatenate([pi[CHUNK, 0], pi[CHUNK, 1]], axis=-1)[:, None, :]
    tr = lambda w: w.transpose(0, 2, 1).astype(BF16)
    return tr(wst), tr(ktoep), tr(wout), a16r, a16i


def _s5_kernel(u_ref, wst_ref, kt_ref, wo_ref, a16r_ref, a16i_ref, h0r_ref, h0i_ref,
               pin_ref, pout_ref, y_ref, fr_ref, fi_ref, ut, yt, s_sc, hf_sc, hb_sc, *, nb, nc, gpb):
    gb = pl.program_id(2)
    seq = nc * CHUNK
    r = nb * nc
    sw = 2 * P_A
    n_cb = u_ref.shape[1] // LANES
    pieces = [b * seq + q * PERM_TILE for b in range(nb) for q in range(seq // PERM_TILE)]

    @pl.when(gb == 0)
    def _():
        for s in range(CHUNK):
            cols = []
            for cb in range(n_cb):
                rows = jnp.concatenate([u_ref[o + s * CHUNK:o + (s + 1) * CHUNK, cb * LANES:(cb + 1) * LANES]
                                        for o in pieces], axis=0)
                cols.append(rows.astype(F32).T.astype(BF16))
            ut[:, s * r:(s + 1) * r] = _dot(jnp.concatenate(cols, axis=0), pin_ref[...]).astype(BF16)

    fwd = (lax.broadcasted_iota(jnp.int32, (1, sw), 1)) < P_A

    def groups(it, carry):
        js = [it * S5_INTERLEAVE + i for i in range(S5_INTERLEAVE)]
        grows = [pl.multiple_of((gb * gpb + j) * SSM_GROUP, SSM_GROUP) for j in js]
        ds = [jnp.concatenate([ut[pl.ds(g, SSM_GROUP), s * r:(s + 1) * r] for s in range(CHUNK)], axis=0)
              for g in grows]
        for i, j in enumerate(js):
            s_t = _dot(wst_ref[j], ds[i])
            s_sc[i, 0] = s_t[:sw].T
            s_sc[i, 1] = s_t[sw:].T
        a_r = [a16r_ref[j] for j in js]
        a_i = [a16i_ref[j] for j in js]
        re = [h0r_ref[j] for j in js]
        im = [h0i_ref[j] for j in js]
        for k in range(nc):
            rf = pl.ds(k * nb, nb)
            rb = pl.ds((nc - 1 - k) * nb, nb)
            for i in range(S5_INTERLEAVE):
                hf_sc[i, 0, rf, :] = re[i]
                hf_sc[i, 1, rf, :] = im[i]
                hb_sc[i, 0, rb, :] = re[i]
                hb_sc[i, 1, rb, :] = im[i]
                xr = jnp.where(fwd, s_sc[i, 0, rf, :], s_sc[i, 0, rb, :])
                xi = jnp.where(fwd, s_sc[i, 1, rf, :], s_sc[i, 1, rb, :])
                re[i], im[i] = a_r[i] * re[i] - a_i[i] * im[i] + xr, a_r[i] * im[i] + a_i[i] * re[i] + xi
        for i, j in enumerate(js):
            fr_ref[j] = re[i]
            fi_ref[j] = im[i]
            hin_t = jnp.concatenate([jnp.where(fwd, hf_sc[i, 0], hb_sc[i, 0]).T,
                                     jnp.where(fwd, hf_sc[i, 1], hb_sc[i, 1]).T], axis=0)
            y_t = (_dot(kt_ref[j], ds[i]) + _dot(wo_ref[j], hin_t.astype(BF16))).astype(BF16)
            for t in range(CHUNK):
                yt[pl.ds(grows[i], SSM_GROUP), t * r:(t + 1) * r] = y_t[t * SSM_GROUP:(t + 1) * SSM_GROUP, :]
        return carry

    lax.fori_loop(0, gpb // S5_INTERLEAVE, groups, 0)

    @pl.when(gb == pl.num_programs(2) - 1)
    def _():
        for t in range(CHUNK):
            full = _dot(yt[:, t * r:(t + 1) * r], pout_ref[...])
            for cb in range(n_cb):
                rows = full[cb * LANES:(cb + 1) * LANES, :].T.astype(BF16)
                for idx, o in enumerate(pieces):
                    y_ref[o + t * CHUNK:o + (t + 1) * CHUNK, cb * LANES:(cb + 1) * LANES] = (
                        rows[idx * CHUNK:(idx + 1) * CHUNK, :])


def _s5_mix(u, ops, h0_re, h0_im, bsz, seq_len, nb, cw):
    wst_t, kt_t, wo_t, a16r, a16i = ops
    t, w = u.shape
    nc = seq_len // CHUNK
    r = nb * nc
    gpb = S5_GROUPS_PER_STEP
    n_row_tiles = bsz // nb
    n_col_tiles = w // cw
    gb_per_tile = cw // SSM_GROUP // gpb
    assert r % 128 == 0 and cw % (SSM_GROUP * gpb) == 0 and bsz % nb == 0

    def h0_layout(h0):
        return h0.astype(F32).transpose(2, 0, 1, 3).reshape(G_A, bsz, 2 * P_A)

    tile = pl.BlockSpec((nb * seq_len, cw), lambda i, c, g: (i, c))
    grp = lambda i, c, g: (c * gb_per_tile + g, 0, 0)
    wspec = pl.BlockSpec((gpb, SSM_TILE, SSM_TILE), grp)
    aspec = pl.BlockSpec((gpb, 1, 2 * P_A), grp)
    hspec = pl.BlockSpec((gpb, nb, 2 * P_A), lambda i, c, g: (c * gb_per_tile + g, i, 0))
    state = jax.ShapeDtypeStruct((G_A, bsz, 2 * P_A), F32)
    scratch = [pltpu.VMEM((cw, CHUNK * r), BF16)] * 2 + [pltpu.VMEM((S5_INTERLEAVE, 2, r, 2 * P_A), F32)] * 3
    src = lax.broadcasted_iota(jnp.int32, (r, r), 0)
    dst = lax.broadcasted_iota(jnp.int32, (r, r), 1)
    pin = (dst == (src % nc) * nb + src // nc).astype(BF16)
    pspec = pl.BlockSpec((r, r), lambda i, c, g: (0, 0))

    y, f_re, f_im = pl.pallas_call(
        functools.partial(_s5_kernel, nb=nb, nc=nc, gpb=gpb),
        grid=(n_row_tiles, n_col_tiles, gb_per_tile),
        in_specs=[tile, wspec, wspec, wspec, aspec, aspec, hspec, hspec, pspec, pspec],
        out_specs=[tile, hspec, hspec],
        out_shape=[jax.ShapeDtypeStruct((t, w), BF16), state, state],
        scratch_shapes=scratch,
        compiler_params=_cparams(3),
        name="s5_core",
    )(u, wst_t, kt_t, wo_t, a16r, a16i, h0_layout(h0_re), h0_layout(h0_im), pin, pin.T)

    def fin(f):
        return f.reshape(G_A, bsz, 2, P_A).transpose(1, 2, 0, 3)
    return y, fin(f_re), fin(f_im)


def _diff_lambda(lq1, lk1, lq2, lk2, lam_init):
    return (jnp.exp(jnp.sum(lq1[...] * lk1[...], axis=-1, keepdims=True))
            - jnp.exp(jnp.sum(lq2[...] * lk2[...], axis=-1, keepdims=True)) + lam_init)


def _attn_kernel(*refs, lam_init, cached):
    if cached:
        q_ref, kc_ref, kn_ref, vc_ref, vn_ref, szb_ref, lq1, lk1, lq2, lk2, g_ref, o_ref = refs
    else:
        q_ref, kn_ref, vn_ref, szb_ref, lq1, lk1, lq2, lk2, g_ref, o_ref = refs
    lam = _diff_lambda(lq1, lk1, lq2, lk2, lam_init)
    lq = q_ref.shape[0]
    hw = 2 * DH
    low = lax.broadcasted_iota(jnp.int32, (1, hw), 1) < DH
    zero = jnp.zeros((), BF16)
    ones = jnp.ones((kn_ref.shape[0], hw), BF16)
    for h in range(H_B):
        cols = slice(h * hw, (h + 1) * hw)
        qh = q_ref[:, cols]
        qs = jnp.concatenate([jnp.where(low, qh, zero), jnp.where(low, zero, qh)], axis=0)
        s = _dot_nt(qs, kn_ref[:, cols])
        if cached:
            s = jnp.concatenate([_dot_nt(qs, kc_ref[:, cols]), s], axis=1)
        e = jnp.exp2(s - jnp.max(s, axis=-1, keepdims=True)).astype(BF16)
        vn = jnp.concatenate([vn_ref[:, cols], ones], axis=1)
        if cached:
            lc = kc_ref.shape[0]
            vc = jnp.concatenate([vc_ref[:, cols], ones[:lc]], axis=1)
            oa = _dot(e[:, :lc], vc) + _dot(e[:, lc:], vn)
        else:
            oa = _dot(e, vn)
        on = oa[:, :hw] * (1.0 / oa[:, hw:])
        o = on[:lq] - lam * on[lq:]
        o = o * lax.rsqrt(jnp.mean(o * o, axis=-1, keepdims=True) + LN_EPS)
        o = o * g_ref[...] * (1.0 - lam_init)
        o_ref[:, cols] = (o * szb_ref[:, cols].astype(F32)).astype(BF16)


def _lam_specs(n_axes):
    zero = lambda *idx: (0, 0)
    return [pl.BlockSpec((1, DH), zero)] * 4 + [pl.BlockSpec((1, 2 * DH), zero)]


def _attention_prompt(q, k, v, szb, lam_vecs, subln, lam_init, bsz, seq_len):
    blk = pl.BlockSpec((seq_len, D_MODEL), lambda b: (b, 0))
    return pl.pallas_call(
        functools.partial(_attn_kernel, lam_init=lam_init, cached=False),
        grid=(bsz,),
        in_specs=[blk] * 4 + _lam_specs(1),
        out_specs=blk,
        out_shape=jax.ShapeDtypeStruct(q.shape, BF16),
        compiler_params=_cparams(1),
        name="diff_attention_prompt",
    )(q, k, v, szb, *lam_vecs, subln)


def _attention_sample(q, kc, kn, vc, vn, szb, lam_vecs, subln, lam_init, bsz, seq_len, tq):
    nq = seq_len // tq
    past = kc.shape[1]
    qblk = pl.BlockSpec((tq, D_MODEL), lambda b, i: (b * nq + i, 0))
    cblk = pl.BlockSpec((None, past, D_MODEL), lambda b, i: (b, 0, 0))
    nblk = pl.BlockSpec((seq_len, D_MODEL), lambda b, i: (b, 0))
    return pl.pallas_call(
        functools.partial(_attn_kernel, lam_init=lam_init, cached=True),
        grid=(bsz, nq),
        in_specs=[qblk, cblk, nblk, cblk, nblk, qblk] + _lam_specs(2),
        out_specs=qblk,
        out_shape=jax.ShapeDtypeStruct(q.shape, BF16),
        compiler_params=_cparams(2),
        name="diff_attention_sample",
    )(q, kc, kn, vc, vn, szb, *lam_vecs, subln)


def _even_out_kernel(ys_ref, sza_ref, yb_ref, x_ref, gate_ref, wglu_ref, bglu_ref, wout_ref, g_ref, b_ref,
                     perm_ref, o_ref, *, alpha):
    ga = jax.nn.gelu(_dot(perm_ref[...], ys_ref[...]))
    glu = jax.nn.sigmoid(_dot(ga.astype(BF16), wglu_ref[...]) + bglu_ref[...])
    ya = (ga * glu * sza_ref[...].astype(F32)).astype(BF16)
    wa = ys_ref.shape[1]
    out = _dot(ya, wout_ref[:wa, :]) + _dot(yb_ref[...], wout_ref[wa:, :])
    o_ref[...] = _post_norm(x_ref[...], gate_ref[...], out, g_ref[...], b_ref[...], alpha)


def _even_out_proj(ys, sza, yb, x2, mod3, w_glu_bf, b_glu, w_out_bf, ln_g, ln_b, layer, cond_of, tm, alpha):
    t, d = x2.shape
    assert tm == PERM_TILE
    row = pl.BlockSpec((tm, d), lambda i: (i, 0))
    full = lambda shape: pl.BlockSpec(shape, lambda i: (0, 0))
    return pl.pallas_call(
        functools.partial(_even_out_kernel, alpha=alpha),
        grid=(t // tm,),
        in_specs=[row, row, row, row, _mod_spec(layer, 2, cond_of), full(w_glu_bf.shape), full((1, d)),
                  full(w_out_bf.shape), full((1, d)), full((1, d)), full((tm, tm))],
        out_specs=row,
        out_shape=jax.ShapeDtypeStruct((t, d), F32),
        compiler_params=_cparams(1),
        name="even_out_proj",
    )(ys, sza, yb, x2, mod3, w_glu_bf, b_glu.reshape(1, d), w_out_bf, ln_g.reshape(1, d), ln_b.reshape(1, d),
      _chunk_transpose_matrix())


def _dft_mats(n, scale):
    k = np.arange(n, dtype=np.int64)
    ang = ((k[:, None] * k[None, :]) % n).astype(np.float64) * (2.0 * math.pi / n)
    return (jnp.asarray((np.cos(ang) * scale).astype(np.float32)).astype(BF16),
            jnp.asarray((np.sin(ang) * scale).astype(np.float32)).astype(BF16))


def _odd_in_kernel(x_ref, shift_ref, scale_ref, w_ref, cc_ref, sc_ref, uc_ref, us_ref, sz_ref):
    h = (_ln_rows(x_ref[...]) * (1.0 + scale_ref[...]) + shift_ref[...]).astype(BF16)
    u = _dot(h, w_ref[:, :W_C]).astype(BF16)
    sz_ref[...] = _silu(_dot(h, w_ref[:, W_C:])).astype(BF16)
    for g in range(NG_C):
        cols = slice(g * GC_C, (g + 1) * GC_C)
        uc_ref[:, cols] = _dot(u[:, cols], cc_ref[...]).astype(BF16)
        us_ref[:, cols] = _dot(u[:, cols], sc_ref[...]).astype(BF16)


def _odd_in_proj(x2, mod3, w_in_bf, cc, sc, layer, cond_of, tm):
    t, d = x2.shape
    row = pl.BlockSpec((tm, d), lambda i: (i, 0))
    wide = pl.BlockSpec((tm, W_C), lambda i: (i, 0))
    full = lambda shape: pl.BlockSpec(shape, lambda i: (0, 0))
    wide_bf = jax.ShapeDtypeStruct((t, W_C), BF16)
    return pl.pallas_call(
        _odd_in_kernel,
        grid=(t // tm,),
        in_specs=[row, _mod_spec(layer, 0, cond_of), _mod_spec(layer, 1, cond_of), full(w_in_bf.shape),
                  full(cc.shape), full(sc.shape)],
        out_specs=[wide] * 3,
        out_shape=[wide_bf] * 3,
        compiler_params=_cparams(1),
        name="odd_in_proj",
    )(x2, mod3, mod3, w_in_bf, cc, sc)


def _odd_out_kernel(cl_ref, sl_ref, uc_ref, us_ref, sz_ref, x_ref, gate_ref, wf_ref, bf_ref, wo_ref, g_ref, b_ref,
                    o_ref, *, alpha):
    mixed = (_dot(cl_ref[...], uc_ref[...]) - _dot(sl_ref[...], us_ref[...])).astype(BF16)
    y = ((_dot(mixed, wf_ref[...]) + bf_ref[...]) * sz_ref[...].astype(F32)).astype(BF16)
    o_ref[...] = _post_norm(x_ref[...], gate_ref[...], _dot(y, wo_ref[...]), g_ref[...], b_ref[...], alpha)


def _odd_out_proj(uc, us, sz, x2, mod3, cl, sl, w_fno_bf, b_fno, w_out_bf, ln_g, ln_b, layer, cond_of,
                  bsz, seq_len, tl, alpha):
    t, d = x2.shape
    nl = seq_len // tl
    full = lambda shape: pl.BlockSpec(shape, lambda b, i: (0, 0))
    dft = pl.BlockSpec((tl, seq_len), lambda b, i: (i, 0))
    seq = pl.BlockSpec((seq_len, W_C), lambda b, i: (b, 0))
    wide = pl.BlockSpec((tl, W_C), lambda b, i: (b * nl + i, 0))
    row = pl.BlockSpec((tl, d), lambda b, i: (b * nl + i, 0))
    return pl.pallas_call(
        functools.partial(_odd_out_kernel, alpha=alpha),
        grid=(bsz, nl),
        in_specs=[dft, dft, seq, seq, wide, row, _mod_spec(layer, 2, cond_of), full(w_fno_bf.shape),
                  full((1, W_C)), full(w_out_bf.shape), full((1, d)), full((1, d))],
        out_specs=row,
        out_shape=jax.ShapeDtypeStruct((t, d), F32),
        compiler_params=_cparams(2),
        name="odd_out_proj",
    )(cl, sl, uc, us, sz, x2, mod3, w_fno_bf, b_fno.reshape(1, W_C), w_out_bf, ln_g.reshape(1, d),
      ln_b.reshape(1, d))


def _rope_tables(seq_len):
    rows = seq_len // GRID_W
    row = jnp.repeat(jnp.arange(rows), GRID_W).astype(F32)
    col = jnp.tile(jnp.arange(GRID_W), rows).astype(F32)
    freqs = ROPE_BASE ** (-jnp.arange(ROT_FREQS, dtype=F32) / ROT_FREQS)
    ang = jnp.concatenate([row[:, None] * freqs, col[:, None] * freqs], axis=-1)
    cos, sin = jnp.cos(ang), jnp.sin(ang)
    cos128 = jnp.tile(cos, (1, 128 // ROT_HALF))
    sin128 = jnp.tile(jnp.concatenate([-sin, sin], axis=-1), (1, 128 // DH))
    return cos128, sin128


def kernel(x_prompt, x_sample, cache_k, cache_v, state_ssm_re, state_ssm_im, c, c_ctx, w_mod, b_mod, ln_g, ln_b, w_in_e, ssm_lam_re, ssm_lam_im, ssm_log_dt, ssm_b_re, ssm_b_im, ssm_c_re, ssm_c_im, ssm_d, w_glu, b_glu, lam_q1, lam_k1, lam_q2, lam_k2, subln_g, w_out_e, w_in_o, w_fno, b_fno, w_out_o):
    depth = w_mod.shape[0]
    bp_, lp, d = x_prompt.shape
    bs_, ls, _ = x_sample.shape
    past = cache_k.shape[2]
    alpha = (2 * depth) ** 0.25
    assert bs_ + 1 <= MOD_ROWS and d == D_MODEL

    cond8 = jnp.concatenate([c_ctx[None, :], c, jnp.zeros((MOD_ROWS - 1 - bs_, d), F32)], axis=0).astype(F32)
    mod3 = _modulation(cond8, w_mod, b_mod).reshape(depth * MOD_ROWS * 3, 1, d)

    tm = 256
    cond_p = lambda *idx: 0
    cond_s_row = lambda i: 1 + i // (ls // tm)
    cond_s_grid = lambda b, i: 1 + b
    rope_tabs = _rope_tables(ls)
    xp = x_prompt.reshape(bp_ * lp, d)
    xs = x_sample.reshape(bs_ * ls, d)
    new_k, new_v, new_sr, new_si = [], [], [], []
    zeros_h0 = jnp.zeros((bp_, 2, G_A, P_A), F32)

    for layer in range(depth):
        if layer % 2 == 0:
            e = layer // 2
            lam_init = 0.8 - 0.6 * math.exp(-0.3 * layer)
            w_in_bf = w_in_e[e].astype(BF16)
            w_glu_bf = w_glu[e].astype(BF16)
            w_out_bf = w_out_e[e].astype(BF16)
            ops = _s5_operators(ssm_lam_re[e], ssm_lam_im[e], ssm_log_dt[e], ssm_b_re[e], ssm_b_im[e],
                                ssm_c_re[e], ssm_c_im[e], ssm_d[e])
            lam_vecs = [v[e].reshape(1, DH).astype(F32) for v in (lam_q1, lam_k1, lam_q2, lam_k2)]
            subln = subln_g[e].reshape(1, 2 * DH).astype(F32)

            u, sza, q, kb, vb, szb, kf, vf = _even_in_proj(xp, mod3, w_in_bf, layer, cond_p, tm, None, lp)
            new_k.append(kf.reshape(bp_, lp, H_B, 2, DH))
            new_v.append(vf.reshape(bp_, lp, H_B, 2 * DH))
            ys, s_re, s_im = _s5_mix(u, ops, zeros_h0, zeros_h0, bp_, lp, S5_TILE_ELEMS // (lp * d), d)
            new_sr.append(s_re)
            new_si.append(s_im)
            yb = _attention_prompt(q, kb, vb, szb, lam_vecs, subln, lam_init, bp_, lp)
            xp = _even_out_proj(ys, sza, yb, xp, mod3, w_glu_bf, b_glu[e], w_out_bf, ln_g[layer], ln_b[layer],
                                layer, cond_p, tm, alpha)

            u, sza, q, kb, vb, szb = _even_in_proj(xs, mod3, w_in_bf, layer, cond_s_row, tm, rope_tabs, ls)
            ys, _, _ = _s5_mix(u, ops, state_ssm_re[:, e], state_ssm_im[:, e], bs_, ls, bs_,
                               S5_TILE_ELEMS // (bs_ * ls))
            kc = cache_k[:, e].reshape(bs_, past, d).astype(BF16)
            vc = cache_v[:, e].reshape(bs_, past, d).astype(BF16)
            yb = _attention_sample(q, kc, kb, vc, vb, szb, lam_vecs, subln, lam_init, bs_, ls, 256)
            xs = _even_out_proj(ys, sza, yb, xs, mod3, w_glu_bf, b_glu[e], w_out_bf, ln_g[layer], ln_b[layer],
                                layer, cond_s_row, tm, alpha)
        else:
            o = layer // 2
            w_in_bf = w_in_o[o].astype(BF16)
            w_fno_bf = w_fno[o].astype(BF16)
            w_out_bf = w_out_o[o].astype(BF16)
            cc, sc = _dft_mats(GC_C, GC_C ** -0.5)
            for which in ("prompt", "sample"):
                if which == "prompt":
                    x2, cond_row, cond_grid, bsz, seq = xp, cond_p, cond_p, bp_, lp
                else:
                    x2, cond_row, cond_grid, bsz, seq = xs, cond_s_row, cond_s_grid, bs_, ls
                cl, sl = _dft_mats(seq, seq ** -0.5)
                uc, us, sz = _odd_in_proj(x2, mod3, w_in_bf, cc, sc, layer, cond_row, tm)
                x2 = _odd_out_proj(uc, us, sz, x2, mod3, cl, sl, w_fno_bf, b_fno[o], w_out_bf, ln_g[layer],
                                   ln_b[layer], layer, cond_grid, bsz, seq, min(seq, 256), alpha)
                if which == "prompt":
                    xp = x2
                else:
                    xs = x2

    return (xp.reshape(bp_, lp, d), xs.reshape(bs_, ls, d), jnp.stack(new_k, axis=1), jnp.stack(new_v, axis=1),
            jnp.stack(new_sr, axis=1), jnp.stack(new_si, axis=1))
```

```python
import functools
import math

import jax
import jax.numpy as jnp
import numpy as np
from jax import lax
from jax.experimental import pallas as pl
from jax.experimental.pallas import tpu as pltpu

F32 = jnp.float32
BF16 = jnp.bfloat16

D_MODEL = 1024
GRID_W = 64
SSM_GROUP = 16
G_A = D_MODEL // SSM_GROUP
P_A = 64
DH = 64
H_B = D_MODEL // (2 * DH)
ROPE_BASE = 10000.0
ROT_HALF = DH // 2
ROT_FREQS = DH // 4
NG_C = 8
GC_C = 2 * D_MODEL // NG_C
W_C = 2 * D_MODEL
LN_EPS = 1e-5
LOG2_E = 1.4426950408889634
CHUNK = 16
SSM_TILE = CHUNK * SSM_GROUP
LANES = 128
PERM_TILE = CHUNK * CHUNK
S5_GROUPS_PER_STEP = 16
S5_INTERLEAVE = 4
S5_TILE_ELEMS = 2048 * 1024
MOD_ROWS = 8
VMEM_LIMIT = 56 * 1024 * 1024


def _cparams(n_axes):
    return pltpu.CompilerParams(dimension_semantics=("arbitrary",) * n_axes, vmem_limit_bytes=VMEM_LIMIT)


def _ln_rows(x):
    mu = jnp.mean(x, axis=-1, keepdims=True)
    xc = x - mu
    var = jnp.mean(xc * xc, axis=-1, keepdims=True)
    return xc * lax.rsqrt(var + LN_EPS)


def _silu(z):
    return z * jax.nn.sigmoid(z)


def _dot(a, b):
    return jnp.dot(a, b, preferred_element_type=F32)


def _dot_nt(a, b):
    return lax.dot_general(a, b, (((1,), (1,)), ((), ())), preferred_element_type=F32)


def _post_norm(x, gate, out, g, b, alpha):
    return _ln_rows(alpha * x + gate * out) * g + b


def _mod_kernel(c_ref, w_ref, b_ref, o_ref):
    c = _silu(c_ref[...]).astype(BF16)
    o_ref[...] = _dot(c, w_ref[...].astype(BF16)) + b_ref[...]


def _modulation(cond8, w_mod, b_mod):
    depth, d, n3 = w_mod.shape
    tn = 1024
    return pl.pallas_call(
        _mod_kernel,
        grid=(depth, n3 // tn),
        in_specs=[
            pl.BlockSpec((MOD_ROWS, d), lambda l, j: (0, 0)),
            pl.BlockSpec((None, d, tn), lambda l, j: (l, 0, j)),
            pl.BlockSpec((None, 1, tn), lambda l, j: (l, 0, j)),
        ],
        out_specs=pl.BlockSpec((None, MOD_ROWS, tn), lambda l, j: (l, 0, j)),
        out_shape=jax.ShapeDtypeStruct((depth, MOD_ROWS, n3), F32),
        compiler_params=_cparams(2),
        name="modulation",
    )(cond8, w_mod, b_mod.reshape(depth, 1, n3))


def _mod_spec(layer, part, cond_of):
    return pl.BlockSpec((None, 1, D_MODEL), lambda *idx: ((layer * MOD_ROWS + cond_of(*idx)) * 3 + part, 0, 0))


def _rope(x, cos, sin_signed, first_half):
    blocks = []
    for hh in range(x.shape[1] // 128):
        b = x[:, hh * 128:(hh + 1) * 128]
        partner = jnp.where(first_half, pltpu.roll(b, 128 - ROT_HALF, 1), pltpu.roll(b, ROT_HALF, 1))
        blocks.append(b * cos + partner * sin_signed)
    return jnp.concatenate(blocks, axis=1)


def _chunk_transpose_matrix():
    i = lax.broadcasted_iota(jnp.int32, (PERM_TILE, PERM_TILE), 0)
    j = lax.broadcasted_iota(jnp.int32, (PERM_TILE, PERM_TILE), 1)
    return (j == (i % CHUNK) * CHUNK + i // CHUNK).astype(BF16)


def _even_in_kernel(*refs, rope):
    if rope:
        (x_ref, shift_ref, scale_ref, w_ref, perm_ref, cos_ref, sin_ref,
         u_ref, sza_ref, q_ref, kb_ref, vb_ref, szb_ref) = refs
    else:
        (x_ref, shift_ref, scale_ref, w_ref, perm_ref, u_ref, sza_ref, q_ref, kb_ref, vb_ref, szb_ref,
         kf_ref, vf_ref) = refs
    w = D_MODEL
    h = (_ln_rows(x_ref[...]) * (1.0 + scale_ref[...]) + shift_ref[...]).astype(BF16)

    def proj(j):
        return _dot(h, w_ref[:, j * w:(j + 1) * w])

    u_ref[...] = _dot(perm_ref[...], proj(0).astype(BF16)).astype(BF16)
    sza_ref[...] = _silu(proj(1)).astype(BF16)
    q = proj(2) * (DH ** -0.5 * LOG2_E)
    k = proj(3)
    v = proj(4)
    if rope:
        lane = lax.broadcasted_iota(jnp.int32, (1, 128), 1)
        first_half = (lane % DH) < ROT_HALF
        q = _rope(q, cos_ref[...], sin_ref[...], first_half)
        kb_ref[...] = _rope(k, cos_ref[...], sin_ref[...], first_half).astype(BF16)
    else:
        kf_ref[...] = k.T
        vf_ref[...] = v
        kb_ref[...] = k.astype(BF16)
    q_ref[...] = q.astype(BF16)
    vb_ref[...] = v.astype(BF16)
    szb_ref[...] = _silu(proj(5)).astype(BF16)


def _even_in_proj(x2, mod3, w_in_bf, layer, cond_of, tm, rope_tabs, seq_len):
    t, d = x2.shape
    n = w_in_bf.shape[1]
    assert tm == PERM_TILE
    row = pl.BlockSpec((tm, d), lambda i: (i, 0))
    in_specs = [row, _mod_spec(layer, 0, cond_of), _mod_spec(layer, 1, cond_of),
                pl.BlockSpec((d, n), lambda i: (0, 0)), pl.BlockSpec((tm, tm), lambda i: (0, 0))]
    args = [x2, mod3, mod3, w_in_bf, _chunk_transpose_matrix()]
    bf = jax.ShapeDtypeStruct((t, d), BF16)
    out_shape = [bf] * 6
    out_specs = [row] * 6
    if rope_tabs is not None:
        tiles_per_seq = seq_len // tm
        tab = pl.BlockSpec((tm, 128), lambda i: (i % tiles_per_seq, 0))
        in_specs += [tab, tab]
        args += list(rope_tabs)
    else:
        assert tm == seq_len
        out_shape += [jax.ShapeDtypeStruct((t // tm, d, tm), F32), jax.ShapeDtypeStruct((t, d), F32)]
        out_specs += [pl.BlockSpec((None, d, tm), lambda i: (i, 0, 0)), row]
    return pl.pallas_call(
        functools.partial(_even_in_kernel, rope=rope_tabs is not None),
        grid=(t // tm,),
        in_specs=in_specs,
        out_specs=out_specs,
        out_shape=out_shape,
        compiler_params=_cparams(1),
        name="even_in_proj",
    )(*args)


LAG_BATCH = 16


def _lag_kernel(p_ref, c_ref, o_ref):
    for i in range(LAG_BATCH):
        o_ref[i] = jnp.dot(p_ref[i], c_ref[i], preferred_element_type=F32, precision=lax.Precision.HIGHEST)


def _s5_operators(lam_re, lam_im, log_dt, b_re, b_im, c_re, c_im, d_skip):
    lr, li = lam_re.astype(F32), lam_im.astype(F32)
    dt = jnp.exp(log_dt.astype(F32))[..., None]
    mag = jnp.exp(lr * dt)
    ar = mag * jnp.cos(li * dt)
    ai = mag * jnp.sin(li * dt)
    den = lr * lr + li * li
    fr = ((ar - 1.0) * lr + ai * li) / den
    fi = (ai * lr - (ar - 1.0) * li) / den
    br, bi = b_re.astype(F32), b_im.astype(F32)
    bbr = fr[..., None] * br - fi[..., None] * bi
    bbi = fr[..., None] * bi + fi[..., None] * br
    cr, ci = c_re.astype(F32), c_im.astype(F32)
    e_pow = jnp.arange(CHUNK + 1, dtype=F32)[:, None, None, None]
    pmag = jnp.exp(e_pow * (lr * dt))
    pr = pmag * jnp.cos(e_pow * (li * dt))
    pi = pmag * jnp.sin(e_pow * (li * dt))

    abr = pr[:CHUNK, ..., None] * bbr - pi[:CHUNK, ..., None] * bbi
    abi = pr[:CHUNK, ..., None] * bbi + pi[:CHUNK, ..., None] * bbr

    lhs = jnp.concatenate([abr, abi], axis=3)
    lhs = lhs.transpose(1, 2, 0, 4, 3).reshape(2 * G_A, SSM_TILE, 2 * P_A)
    rhs = jnp.concatenate([cr, -ci], axis=3).transpose(0, 1, 3, 2).reshape(2 * G_A, 2 * P_A, SSM_GROUP)
    klag = pl.pallas_call(
        _lag_kernel,
        grid=(2 * G_A // LAG_BATCH,),
        in_specs=[pl.BlockSpec((LAG_BATCH, SSM_TILE, 2 * P_A), lambda i: (i, 0, 0)),
                  pl.BlockSpec((LAG_BATCH, 2 * P_A, SSM_GROUP), lambda i: (i, 0, 0))],
        out_specs=pl.BlockSpec((LAG_BATCH, SSM_TILE, SSM_GROUP), lambda i: (i, 0, 0)),
        out_shape=jax.ShapeDtypeStruct((2 * G_A, SSM_TILE, SSM_GROUP), F32),
        compiler_params=_cparams(1),
        name="s5_lag_kernels",
    )(lhs, rhs).reshape(2, G_A, CHUNK, SSM_GROUP, SSM_GROUP)

    s_idx = jnp.arange(CHUNK)[:, None]
    t_idx = jnp.arange(CHUNK)[None, :]
    kf = jnp.where((t_idx >= s_idx)[None, :, :, None, None], klag[0][:, jnp.clip(t_idx - s_idx, 0, CHUNK - 1)], 0.0)
    kb = jnp.where((s_idx >= t_idx)[None, :, :, None, None], klag[1][:, jnp.clip(s_idx - t_idx, 0, CHUNK - 1)], 0.0)
    eye_t = jnp.eye(CHUNK, dtype=F32)[None, :, :, None, None]
    eye_n = jnp.eye(SSM_GROUP, dtype=F32)[None, None, None, :, :]
    dsk = d_skip.astype(F32).reshape(G_A, 1, 1, 1, SSM_GROUP)
    ktoep = (kf + kb + eye_t * eye_n * dsk).transpose(0, 1, 3, 2, 4).reshape(G_A, SSM_TILE, SSM_TILE)

    def st(part, d, rev):
        x = part[:, d]
        x = x[::-1] if rev else x
        return x.transpose(1, 0, 3, 2)
    wst = jnp.concatenate([st(abr, 0, True), st(abr, 1, False), st(abi, 0, True), st(abi, 1, False)], axis=-1)
    wst = wst.reshape(G_A, SSM_TILE, 4 * P_A)

    def out_rows(d, rev):
        er = pr[1:, d][::-1] if rev else pr[1:, d]
        ei = pi[1:, d][::-1] if rev else pi[1:, d]
        re = cr[d][None] * er[:, :, None, :] - ci[d][None] * ei[:, :, None, :]
        im = cr[d][None] * ei[:, :, None, :] + ci[d][None] * er[:, :, None, :]
        return re.transpose(1, 3, 0, 2), -im.transpose(1, 3, 0, 2)
    f_re, f_im = out_rows(0, False)
    b_re_, b_im_ = out_rows(1, True)
    wout = jnp.concatenate([f_re, b_re_, f_im, b_im_], axis=1).reshape(G_A, 4 * P_A, SSM_TILE)

    a16r = jnp.concatenate([pr[CHUNK, 0], pr[CHUNK, 1]], axis=-1)[:, None, :]
    a16i = jnp.concatenate([pi[CHUNK, 0], pi[CHUNK, 1]], axis=-1)[:, None, :]
    tr = lambda w: w.transpose(0, 2, 1).astype(BF16)
    return tr(wst), tr(ktoep), tr(wout), a16r, a16i


def _s5_kernel(u_ref, wst_ref, kt_ref, wo_ref, a16r_ref, a16i_ref, h0r_ref, h0i_ref,
               pin_ref, pout_ref, y_ref, fr_ref, fi_ref, ut, yt, s_sc, hf_sc, hb_sc, *, nb, nc, gpb):
    gb = pl.program_id(2)
    seq = nc * CHUNK
    r = nb * nc
    sw = 2 * P_A
    n_cb = u_ref.shape[1] // LANES
    pieces = [b * seq + q * PERM_TILE for b in range(nb) for q in range(seq // PERM_TILE)]

    @pl.when(gb == 0)
    def _():
        for s in range(CHUNK):
            cols = []
            for cb in range(n_cb):
                rows = jnp.concatenate([u_ref[o + s * CHUNK:o + (s + 1) * CHUNK, cb * LANES:(cb + 1) * LANES]
                                        for o in pieces], axis=0)
                cols.append(rows.astype(F32).T.astype(BF16))
            ut[:, s * r:(s + 1) * r] = _dot(jnp.concatenate(cols, axis=0), pin_ref[...]).astype(BF16)

    fwd = (lax.broadcasted_iota(jnp.int32, (1, sw), 1)) < P_A

    def groups(it, carry):
        js = [it * S5_INTERLEAVE + i for i in range(S5_INTERLEAVE)]
        grows = [pl.multiple_of((gb * gpb + j) * SSM_GROUP, SSM_GROUP) for j in js]
        ds = [jnp.concatenate([ut[pl.ds(g, SSM_GROUP), s * r:(s + 1) * r] for s in range(CHUNK)], axis=0)
              for g in grows]
        for i, j in enumerate(js):
            s_t = _dot(wst_ref[j], ds[i])
            s_sc[i, 0] = s_t[:sw].T
            s_sc[i, 1] = s_t[sw:].T
        a_r = [a16r_ref[j] for j in js]
        a_i = [a16i_ref[j] for j in js]
        re = [h0r_ref[j] for j in js]
        im = [h0i_ref[j] for j in js]
        for k in range(nc):
            rf = pl.ds(k * nb, nb)
            rb = pl.ds((nc - 1 - k) * nb, nb)
            for i in range(S5_INTERLEAVE):
                hf_sc[i, 0, rf, :] = re[i]
                hf_sc[i, 1, rf, :] = im[i]
                hb_sc[i, 0, rb, :] = re[i]
                hb_sc[i, 1, rb, :] = im[i]
                xr = jnp.where(fwd, s_sc[i, 0, rf, :], s_sc[i, 0, rb, :])
                xi = jnp.where(fwd, s_sc[i, 1, rf, :], s_sc[i, 1, rb, :])
                re[i], im[i] = a_r[i] * re[i] - a_i[i] * im[i] + xr, a_r[i] * im[i] + a_i[i] * re[i] + xi
        for i, j in enumerate(js):
            fr_ref[j] = re[i]
            fi_ref[j] = im[i]
            hin_t = jnp.concatenate([jnp.where(fwd, hf_sc[i, 0], hb_sc[i, 0]).T,
                                     jnp.where(fwd, hf_sc[i, 1], hb_sc[i, 1]).T], axis=0)
            y_t = (_dot(kt_ref[j], ds[i]) + _dot(wo_ref[j], hin_t.astype(BF16))).astype(BF16)
            for t in range(CHUNK):
                yt[pl.ds(grows[i], SSM_GROUP), t * r:(t + 1) * r] = y_t[t * SSM_GROUP:(t + 1) * SSM_GROUP, :]
        return carry

    lax.fori_loop(0, gpb // S5_INTERLEAVE, groups, 0)

    @pl.when(gb == pl.num_programs(2) - 1)
    def _():
        for t in range(CHUNK):
            full = _dot(yt[:, t * r:(t + 1) * r], pout_ref[...])
            for cb in range(n_cb):
                rows = full[cb * LANES:(cb + 1) * LANES, :].T.astype(BF16)
                for idx, o in enumerate(pieces):
                    y_ref[o + t * CHUNK:o + (t + 1) * CHUNK, cb * LANES:(cb + 1) * LANES] = (
                        rows[idx * CHUNK:(idx + 1) * CHUNK, :])


def _s5_mix(u, ops, h0_re, h0_im, bsz, seq_len, nb, cw):
    wst_t, kt_t, wo_t, a16r, a16i = ops
    t, w = u.shape
    nc = seq_len // CHUNK
    r = nb * nc
    gpb = S5_GROUPS_PER_STEP
    n_row_tiles = bsz // nb
    n_col_tiles = w // cw
    gb_per_tile = cw // SSM_GROUP // gpb
    assert r % 128 == 0 and cw % (SSM_GROUP * gpb) == 0 and bsz % nb == 0

    def h0_layout(h0):
        return h0.astype(F32).transpose(2, 0, 1, 3).reshape(G_A, bsz, 2 * P_A)

    tile = pl.BlockSpec((nb * seq_len, cw), lambda i, c, g: (i, c))
    grp = lambda i, c, g: (c * gb_per_tile + g, 0, 0)
    wspec = pl.BlockSpec((gpb, SSM_TILE, SSM_TILE), grp)
    aspec = pl.BlockSpec((gpb, 1, 2 * P_A), grp)
    hspec = pl.BlockSpec((gpb, nb, 2 * P_A), lambda i, c, g: (c * gb_per_tile + g, i, 0))
    state = jax.ShapeDtypeStruct((G_A, bsz, 2 * P_A), F32)
    scratch = [pltpu.VMEM((cw, CHUNK * r), BF16)] * 2 + [pltpu.VMEM((S5_INTERLEAVE, 2, r, 2 * P_A), F32)] * 3
    src = lax.broadcasted_iota(jnp.int32, (r, r), 0)
    dst = lax.broadcasted_iota(jnp.int32, (r, r), 1)
    pin = (dst == (src % nc) * nb + src // nc).astype(BF16)
    pspec = pl.BlockSpec((r, r), lambda i, c, g: (0, 0))

    y, f_re, f_im = pl.pallas_call(
        functools.partial(_s5_kernel, nb=nb, nc=nc, gpb=gpb),
        grid=(n_row_tiles, n_col_tiles, gb_per_tile),
        in_specs=[tile, wspec, wspec, wspec, aspec, aspec, hspec, hspec, pspec, pspec],
        out_specs=[tile, hspec, hspec],
        out_shape=[jax.ShapeDtypeStruct((t, w), BF16), state, state],
        scratch_shapes=scratch,
        compiler_params=_cparams(3),
        name="s5_core",
    )(u, wst_t, kt_t, wo_t, a16r, a16i, h0_layout(h0_re), h0_layout(h0_im), pin, pin.T)

    def fin(f):
        return f.reshape(G_A, bsz, 2, P_A).transpose(1, 2, 0, 3)
    return y, fin(f_re), fin(f_im)


def _diff_lambda(lq1, lk1, lq2, lk2, lam_init):
    return (jnp.exp(jnp.sum(lq1[...] * lk1[...], axis=-1, keepdims=True))
            - jnp.exp(jnp.sum(lq2[...] * lk2[...], axis=-1, keepdims=True)) + lam_init)


def _attn_kernel(*refs, lam_init, cached):
    if cached:
        q_ref, kc_ref, kn_ref, vc_ref, vn_ref, szb_ref, lq1, lk1, lq2, lk2, g_ref, o_ref = refs
    else:
        q_ref, kn_ref, vn_ref, szb_ref, lq1, lk1, lq2, lk2, g_ref, o_ref = refs
    lam = _diff_lambda(lq1, lk1, lq2, lk2, lam_init)
    lq = q_ref.shape[0]
    hw = 2 * DH
    low = lax.broadcasted_iota(jnp.int32, (1, hw), 1) < DH
    zero = jnp.zeros((), BF16)
    ones = jnp.ones((kn_ref.shape[0], hw), BF16)
    for h in range(H_B):
        cols = slice(h * hw, (h + 1) * hw)
        qh = q_ref[:, cols]
        qs = jnp.concatenate([jnp.where(low, qh, zero), jnp.where(low, zero, qh)], axis=0)
        s = _dot_nt(qs, kn_ref[:, cols])
        if cached:
            s = jnp.concatenate([_dot_nt(qs, kc_ref[:, cols]), s], axis=1)
        e = jnp.exp2(s - jnp.max(s, axis=-1, keepdims=True)).astype(BF16)
        vn = jnp.concatenate([vn_ref[:, cols], ones], axis=1)
        if cached:
            lc = kc_ref.shape[0]
            vc = jnp.concatenate([vc_ref[:, cols], ones[:lc]], axis=1)
            oa = _dot(e[:, :lc], vc) + _dot(e[:, lc:], vn)
        else:
            oa = _dot(e, vn)
        on = oa[:, :hw] * (1.0 / oa[:, hw:])
        o = on[:lq] - lam * on[lq:]
        o = o * lax.rsqrt(jnp.mean(o * o, axis=-1, keepdims=True) + LN_EPS)
        o = o * g_ref[...] * (1.0 - lam_init)
        o_ref[:, cols] = (o * szb_ref[:, cols].astype(F32)).astype(BF16)


def _lam_specs(n_axes):
    zero = lambda *idx: (0, 0)
    return [pl.BlockSpec((1, DH), zero)] * 4 + [pl.BlockSpec((1, 2 * DH), zero)]


def _attention_prompt(q, k, v, szb, lam_vecs, subln, lam_init, bsz, seq_len):
    blk = pl.BlockSpec((seq_len, D_MODEL), lambda b: (b, 0))
    return pl.pallas_call(
        functools.partial(_attn_kernel, lam_init=lam_init, cached=False),
        grid=(bsz,),
        in_specs=[blk] * 4 + _lam_specs(1),
        out_specs=blk,
        out_shape=jax.ShapeDtypeStruct(q.shape, BF16),
        compiler_params=_cparams(1),
        name="diff_attention_prompt",
    )(q, k, v, szb, *lam_vecs, subln)


def _attention_sample(q, kc, kn, vc, vn, szb, lam_vecs, subln, lam_init, bsz, seq_len, tq):
    nq = seq_len // tq
    past = kc.shape[1]
    qblk = pl.BlockSpec((tq, D_MODEL), lambda b, i: (b * nq + i, 0))
    cblk = pl.BlockSpec((None, past, D_MODEL), lambda b, i: (b, 0, 0))
    nblk = pl.BlockSpec((seq_len, D_MODEL), lambda b, i: (b, 0))
    return pl.pallas_call(
        functools.partial(_attn_kernel, lam_init=lam_init, cached=True),
        grid=(bsz, nq),
        in_specs=[qblk, cblk, nblk, cblk, nblk, qblk] + _lam_specs(2),
        out_specs=qblk,
        out_shape=jax.ShapeDtypeStruct(q.shape, BF16),
        compiler_params=_cparams(2),
        name="diff_attention_sample",
    )(q, kc, kn, vc, vn, szb, *lam_vecs, subln)


def _even_out_kernel(ys_ref, sza_ref, yb_ref, x_ref, gate_ref, wglu_ref, bglu_ref, wout_ref, g_ref, b_ref,
                     perm_ref, o_ref, *, alpha):
    ga = jax.nn.gelu(_dot(perm_ref[...], ys_ref[...]))
    glu = jax.nn.sigmoid(_dot(ga.astype(BF16), wglu_ref[...]) + bglu_ref[...])
    ya = (ga * glu * sza_ref[...].astype(F32)).astype(BF16)
    wa = ys_ref.shape[1]
    out = _dot(ya, wout_ref[:wa, :]) + _dot(yb_ref[...], wout_ref[wa:, :])
    o_ref[...] = _post_norm(x_ref[...], gate_ref[...], out, g_ref[...], b_ref[...], alpha)


def _even_out_proj(ys, sza, yb, x2, mod3, w_glu_bf, b_glu, w_out_bf, ln_g, ln_b, layer, cond_of, tm, alpha):
    t, d = x2.shape
    assert tm == PERM_TILE
    row = pl.BlockSpec((tm, d), lambda i: (i, 0))
    full = lambda shape: pl.BlockSpec(shape, lambda i: (0, 0))
    return pl.pallas_call(
        functools.partial(_even_out_kernel, alpha=alpha),
        grid=(t // tm,),
        in_specs=[row, row, row, row, _mod_spec(layer, 2, cond_of), full(w_glu_bf.shape), full((1, d)),
                  full(w_out_bf.shape), full((1, d)), full((1, d)), full((tm, tm))],
        out_specs=row,
        out_shape=jax.ShapeDtypeStruct((t, d), F32),
        compiler_params=_cparams(1),
        name="even_out_proj",
    )(ys, sza, yb, x2, mod3, w_glu_bf, b_glu.reshape(1, d), w_out_bf, ln_g.reshape(1, d), ln_b.reshape(1, d),
      _chunk_transpose_matrix())


def _dft_mats(n, scale):
    k = np.arange(n, dtype=np.int64)
    ang = ((k[:, None] * k[None, :]) % n).astype(np.float64) * (2.0 * math.pi / n)
    return (jnp.asarray((np.cos(ang) * scale).astype(np.float32)).astype(BF16),
            jnp.asarray((np.sin(ang) * scale).astype(np.float32)).astype(BF16))


def _odd_in_kernel(x_ref, shift_ref, scale_ref, w_ref, cc_ref, sc_ref, uc_ref, us_ref, sz_ref):
    h = (_ln_rows(x_ref[...]) * (1.0 + scale_ref[...]) + shift_ref[...]).astype(BF16)
    u = _dot(h, w_ref[:, :W_C]).astype(BF16)
    sz_ref[...] = _silu(_dot(h, w_ref[:, W_C:])).astype(BF16)
    for g in range(NG_C):
        cols = slice(g * GC_C, (g + 1) * GC_C)
        uc_ref[:, cols] = _dot(u[:, cols], cc_ref[...]).astype(BF16)
        us_ref[:, cols] = _dot(u[:, cols], sc_ref[...]).astype(BF16)


def _odd_in_proj(x2, mod3, w_in_bf, cc, sc, layer, cond_of, tm):
    t, d = x2.shape
    row = pl.BlockSpec((tm, d), lambda i: (i, 0))
    wide = pl.BlockSpec((tm, W_C), lambda i: (i, 0))
    full = lambda shape: pl.BlockSpec(shape, lambda i: (0, 0))
    wide_bf = jax.ShapeDtypeStruct((t, W_C), BF16)
    return pl.pallas_call(
        _odd_in_kernel,
        grid=(t // tm,),
        in_specs=[row, _mod_spec(layer, 0, cond_of), _mod_spec(layer, 1, cond_of), full(w_in_bf.shape),
                  full(cc.shape), full(sc.shape)],
        out_specs=[wide] * 3,
        out_shape=[wide_bf] * 3,
        compiler_params=_cparams(1),
        name="odd_in_proj",
    )(x2, mod3, mod3, w_in_bf, cc, sc)


def _odd_out_kernel(cl_ref, sl_ref, uc_ref, us_ref, sz_ref, x_ref, gate_ref, wf_ref, bf_ref, wo_ref, g_ref, b_ref,
                    o_ref, *, alpha):
    mixed = (_dot(cl_ref[...], uc_ref[...]) - _dot(sl_ref[...], us_ref[...])).astype(BF16)
    y = ((_dot(mixed, wf_ref[...]) + bf_ref[...]) * sz_ref[...].astype(F32)).astype(BF16)
    o_ref[...] = _post_norm(x_ref[...], gate_ref[...], _dot(y, wo_ref[...]), g_ref[...], b_ref[...], alpha)


def _odd_out_proj(uc, us, sz, x2, mod3, cl, sl, w_fno_bf, b_fno, w_out_bf, ln_g, ln_b, layer, cond_of,
                  bsz, seq_len, tl, alpha):
    t, d = x2.shape
    nl = seq_len // tl
    full = lambda shape: pl.BlockSpec(shape, lambda b, i: (0, 0))
    dft = pl.BlockSpec((tl, seq_len), lambda b, i: (i, 0))
    seq = pl.BlockSpec((seq_len, W_C), lambda b, i: (b, 0))
    wide = pl.BlockSpec((tl, W_C), lambda b, i: (b * nl + i, 0))
    row = pl.BlockSpec((tl, d), lambda b, i: (b * nl + i, 0))
    return pl.pallas_call(
        functools.partial(_odd_out_kernel, alpha=alpha),
        grid=(bsz, nl),
        in_specs=[dft, dft, seq, seq, wide, row, _mod_spec(layer, 2, cond_of), full(w_fno_bf.shape),
                  full((1, W_C)), full(w_out_bf.shape), full((1, d)), full((1, d))],
        out_specs=row,
        out_shape=jax.ShapeDtypeStruct((t, d), F32),
        compiler_params=_cparams(2),
        name="odd_out_proj",
    )(cl, sl, uc, us, sz, x2, mod3, w_fno_bf, b_fno.reshape(1, W_C), w_out_bf, ln_g.reshape(1, d),
      ln_b.reshape(1, d))


def _rope_tables(seq_len):
    rows = seq_len // GRID_W
    row = jnp.repeat(jnp.arange(rows), GRID_W).astype(F32)
    col = jnp.tile(jnp.arange(GRID_W), rows).astype(F32)
    freqs = ROPE_BASE ** (-jnp.arange(ROT_FREQS, dtype=F32) / ROT_FREQS)
    ang = jnp.concatenate([row[:, None] * freqs, col[:, None] * freqs], axis=-1)
    cos, sin = jnp.cos(ang), jnp.sin(ang)
    cos128 = jnp.tile(cos, (1, 128 // ROT_HALF))
    sin128 = jnp.tile(jnp.concatenate([-sin, sin], axis=-1), (1, 128 // DH))
    return cos128, sin128


def kernel(x_prompt, x_sample, cache_k, cache_v, state_ssm_re, state_ssm_im, c, c_ctx, w_mod, b_mod, ln_g, ln_b, w_in_e, ssm_lam_re, ssm_lam_im, ssm_log_dt, ssm_b_re, ssm_b_im, ssm_c_re, ssm_c_im, ssm_d, w_glu, b_glu, lam_q1, lam_k1, lam_q2, lam_k2, subln_g, w_out_e, w_in_o, w_fno, b_fno, w_out_o):
    depth = w_mod.shape[0]
    bp_, lp, d = x_prompt.shape
    bs_, ls, _ = x_sample.shape
    past = cache_k.shape[2]
    alpha = (2 * depth) ** 0.25
    assert bs_ + 1 <= MOD_ROWS and d == D_MODEL

    cond8 = jnp.concatenate([c_ctx[None, :], c, jnp.zeros((MOD_ROWS - 1 - bs_, d), F32)], axis=0).astype(F32)
    mod3 = _modulation(cond8, w_mod, b_mod).reshape(depth * MOD_ROWS * 3, 1, d)

    tm = 256
    cond_p = lambda *idx: 0
    cond_s_row = lambda i: 1 + i // (ls // tm)
    cond_s_grid = lambda b, i: 1 + b
    rope_tabs = _rope_tables(ls)
    xp = x_prompt.reshape(bp_ * lp, d)
    xs = x_sample.reshape(bs_ * ls, d)
    new_k, new_v, new_sr, new_si = [], [], [], []
    zeros_h0 = jnp.zeros((bp_, 2, G_A, P_A), F32)

    for layer in range(depth):
        if layer % 2 == 0:
            e = layer // 2
            lam_init = 0.8 - 0.6 * math.exp(-0.3 * layer)
            w_in_bf = w_in_e[e].astype(BF16)
            w_glu_bf = w_glu[e].astype(BF16)
            w_out_bf = w_out_e[e].astype(BF16)
            ops = _s5_operators(ssm_lam_re[e], ssm_lam_im[e], ssm_log_dt[e], ssm_b_re[e], ssm_b_im[e],
                                ssm_c_re[e], ssm_c_im[e], ssm_d[e])
            lam_vecs = [v[e].reshape(1, DH).astype(F32) for v in (lam_q1, lam_k1, lam_q2, lam_k2)]
            subln = subln_g[e].reshape(1, 2 * DH).astype(F32)

            u, sza, q, kb, vb, szb, kf, vf = _even_in_proj(xp, mod3, w_in_bf, layer, cond_p, tm, None, lp)
            new_k.append(kf.reshape(bp_, H_B, 2, DH, lp).transpose(0, 4, 1, 2, 3))
            new_v.append(vf.reshape(bp_, lp, H_B, 2 * DH))
            ys, s_re, s_im = _s5_mix(u, ops, zeros_h0, zeros_h0, bp_, lp, S5_TILE_ELEMS // (lp * d), d)
            new_sr.append(s_re)
            new_si.append(s_im)
            yb = _attention_prompt(q, kb, vb, szb, lam_vecs, subln, lam_init, bp_, lp)
            xp = _even_out_proj(ys, sza, yb, xp, mod3, w_glu_bf, b_glu[e], w_out_bf, ln_g[layer], ln_b[layer],
                                layer, cond_p, tm, alpha)

            u, sza, q, kb, vb, szb = _even_in_proj(xs, mod3, w_in_bf, layer, cond_s_row, tm, rope_tabs, ls)
            ys, _, _ = _s5_mix(u, ops, state_ssm_re[:, e], state_ssm_im[:, e], bs_, ls, bs_,
                               S5_TILE_ELEMS // (bs_ * ls))
            kc = cache_k[:, e].reshape(bs_, past, d).astype(BF16)
            vc = cache_v[:, e].reshape(bs_, past, d).astype(BF16)
            yb = _attention_sample(q, kc, kb, vc, vb, szb, lam_vecs, subln, lam_init, bs_, ls, 256)
            xs = _even_out_proj(ys, sza, yb, xs, mod3, w_glu_bf, b_glu[e], w_out_bf, ln_g[layer], ln_b[layer],
                                layer, cond_s_row, tm, alpha)
        else:
            o = layer // 2
            w_in_bf = w_in_o[o].astype(BF16)
            w_fno_bf = w_fno[o].astype(BF16)
            w_out_bf = w_out_o[o].astype(BF16)
            cc, sc = _dft_mats(GC_C, GC_C ** -0.5)
            for which in ("prompt", "sample"):
                if which == "prompt":
                    x2, cond_row, cond_grid, bsz, seq = xp, cond_p, cond_p, bp_, lp
                else:
                    x2, cond_row, cond_grid, bsz, seq = xs, cond_s_row, cond_s_grid, bs_, ls
                cl, sl = _dft_mats(seq, seq ** -0.5)
                uc, us, sz = _odd_in_proj(x2, mod3, w_in_bf, cc, sc, layer, cond_row, tm)
                x2 = _odd_out_proj(uc, us, sz, x2, mod3, cl, sl, w_fno_bf, b_fno[o], w_out_bf, ln_g[layer],
                                   ln_b[layer], layer, cond_grid, bsz, seq, min(seq, 256), alpha)
                if which == "prompt":
                    xp = x2
                else:
                    xs = x2

    return (xp.reshape(bp_, lp, d), xs.reshape(bs_, ls, d), jnp.stack(new_k, axis=1), jnp.stack(new_v, axis=1),
            jnp.stack(new_sr, axis=1), jnp.stack(new_si, axis=1))
```

```python
import functools
import math

import jax
import jax.numpy as jnp
import numpy as np
from jax import lax
from jax.experimental import pallas as pl
from jax.experimental.pallas import tpu as pltpu

F32 = jnp.float32
BF16 = jnp.bfloat16

D_MODEL = 1024
GRID_W = 64
SSM_GROUP = 16
G_A = D_MODEL // SSM_GROUP
P_A = 64
DH = 64
H_B = D_MODEL // (2 * DH)
ROPE_BASE = 10000.0
ROT_HALF = DH // 2
ROT_FREQS = DH // 4
NG_C = 8
GC_C = 2 * D_MODEL // NG_C
W_C = 2 * D_MODEL
LN_EPS = 1e-5
LOG2_E = 1.4426950408889634
CHUNK = 16
SSM_TILE = CHUNK * SSM_GROUP
LANES = 128
PERM_TILE = CHUNK * CHUNK
SUB_ROWS = PERM_TILE
ROW_TILE = 2 * SUB_ROWS
ATTN_Q_TILE = 256
S5_GROUPS_PER_STEP = 16
S5_INTERLEAVE = 4
S5_TILE_ELEMS = 2048 * 1024
MOD_ROWS = 8
VMEM_LIMIT = 56 * 1024 * 1024


def _cparams(n_axes):
    return pltpu.CompilerParams(dimension_semantics=("arbitrary",) * n_axes, vmem_limit_bytes=VMEM_LIMIT)


def _ln_rows(x):
    mu = jnp.mean(x, axis=-1, keepdims=True)
    xc = x - mu
    var = jnp.mean(xc * xc, axis=-1, keepdims=True)
    return xc * lax.rsqrt(var + LN_EPS)


def _silu(z):
    return z * jax.nn.sigmoid(z)


def _dot(a, b):
    return jnp.dot(a, b, preferred_element_type=F32)


def _dot_nt(a, b):
    return lax.dot_general(a, b, (((1,), (1,)), ((), ())), preferred_element_type=F32)


def _post_norm(x, gate, out, g, b, alpha):
    return _ln_rows(alpha * x + gate * out) * g + b


def _mod_kernel(c_ref, w_ref, b_ref, o_ref):
    c = _silu(c_ref[...]).astype(BF16)
    o_ref[...] = _dot(c, w_ref[...].astype(BF16)) + b_ref[...]


def _modulation(cond8, w_mod, b_mod):
    depth, d, n3 = w_mod.shape
    tn = 1024
    return pl.pallas_call(
        _mod_kernel,
        grid=(depth, n3 // tn),
        in_specs=[
            pl.BlockSpec((MOD_ROWS, d), lambda l, j: (0, 0)),
            pl.BlockSpec((None, d, tn), lambda l, j: (l, 0, j)),
            pl.BlockSpec((None, 1, tn), lambda l, j: (l, 0, j)),
        ],
        out_specs=pl.BlockSpec((None, MOD_ROWS, tn), lambda l, j: (l, 0, j)),
        out_shape=jax.ShapeDtypeStruct((depth, MOD_ROWS, n3), F32),
        compiler_params=_cparams(2),
        name="modulation",
    )(cond8, w_mod, b_mod.reshape(depth, 1, n3))


def _resident(shape):
    return pl.BlockSpec(shape, lambda *idx: (0,) * len(shape), pipeline_mode=pl.Buffered(1))


def _mod_spec(layer, part, cond_of):
    return pl.BlockSpec((None, 1, D_MODEL), lambda *idx: ((layer * MOD_ROWS + cond_of(*idx)) * 3 + part, 0, 0))


def _rope(x, cos, sin_signed, first_half):
    blocks = []
    for hh in range(x.shape[1] // 128):
        b = x[:, hh * 128:(hh + 1) * 128]
        partner = jnp.where(first_half, pltpu.roll(b, 128 - ROT_HALF, 1), pltpu.roll(b, ROT_HALF, 1))
        blocks.append(b * cos + partner * sin_signed)
    return jnp.concatenate(blocks, axis=1)


def _chunk_transpose_matrix():
    i = lax.broadcasted_iota(jnp.int32, (PERM_TILE, PERM_TILE), 0)
    j = lax.broadcasted_iota(jnp.int32, (PERM_TILE, PERM_TILE), 1)
    return (j == (i % CHUNK) * CHUNK + i // CHUNK).astype(BF16)


def _even_in_kernel(*refs, rope):
    if rope:
        (x_ref, shift_ref, scale_ref, w_ref, perm_ref, cos_ref, sin_ref,
         u_ref, sza_ref, q_ref, kb_ref, vb_ref, szb_ref) = refs
    else:
        (x_ref, shift_ref, scale_ref, w_ref, perm_ref, u_ref, sza_ref, q_ref, kb_ref, vb_ref, szb_ref,
         kf_ref, vf_ref) = refs
    w = D_MODEL
    for r in range(x_ref.shape[0] // PERM_TILE):
        rows = slice(r * PERM_TILE, (r + 1) * PERM_TILE)
        h = (_ln_rows(x_ref[rows, :]) * (1.0 + scale_ref[...]) + shift_ref[...]).astype(BF16)

        def proj(j):
            return _dot(h, w_ref[:, j * w:(j + 1) * w])

        u_ref[rows, :] = _dot(perm_ref[...], proj(0).astype(BF16)).astype(BF16)
        sza_ref[rows, :] = _silu(proj(1)).astype(BF16)
        q = proj(2) * (DH ** -0.5 * LOG2_E)
        k = proj(3)
        v = proj(4)
        if rope:
            lane = lax.broadcasted_iota(jnp.int32, (1, 128), 1)
            first_half = (lane % DH) < ROT_HALF
            q = _rope(q, cos_ref[rows, :], sin_ref[rows, :], first_half)
            kb_ref[rows, :] = _rope(k, cos_ref[rows, :], sin_ref[rows, :], first_half).astype(BF16)
        else:
            kf_ref[r] = k.T
            vf_ref[rows, :] = v
            kb_ref[rows, :] = k.astype(BF16)
        q_ref[rows, :] = q.astype(BF16)
        vb_ref[rows, :] = v.astype(BF16)
        szb_ref[rows, :] = _silu(proj(5)).astype(BF16)


def _even_in_proj(x2, mod3, w_in_bf, layer, cond_of, tm, rope_tabs, seq_len):
    t, d = x2.shape
    n = w_in_bf.shape[1]
    assert tm % PERM_TILE == 0
    row = pl.BlockSpec((tm, d), lambda i: (i, 0))
    in_specs = [row, _mod_spec(layer, 0, cond_of), _mod_spec(layer, 1, cond_of),
                _resident((d, n)), _resident((PERM_TILE, PERM_TILE))]
    args = [x2, mod3, mod3, w_in_bf, _chunk_transpose_matrix()]
    bf = jax.ShapeDtypeStruct((t, d), BF16)
    out_shape = [bf] * 6
    out_specs = [row] * 6
    if rope_tabs is not None:
        tiles_per_seq = seq_len // tm
        tab = pl.BlockSpec((tm, 128), lambda i: (i % tiles_per_seq, 0))
        in_specs += [tab, tab]
        args += list(rope_tabs)
    else:
        assert seq_len == PERM_TILE
        out_shape += [jax.ShapeDtypeStruct((t // seq_len, d, seq_len), F32), jax.ShapeDtypeStruct((t, d), F32)]
        out_specs += [pl.BlockSpec((tm // seq_len, d, seq_len), lambda i: (i, 0, 0)), row]
    return pl.pallas_call(
        functools.partial(_even_in_kernel, rope=rope_tabs is not None),
        grid=(t // tm,),
        in_specs=in_specs,
        out_specs=out_specs,
        out_shape=out_shape,
        compiler_params=_cparams(1),
        name="even_in_proj",
    )(*args)


LAG_BATCH = 16


def _lag_kernel(c_ref, p_ref, o_ref):
    for i in range(LAG_BATCH):
        o_ref[i] = lax.dot_general(c_ref[i], p_ref[i], (((1,), (1,)), ((), ())), preferred_element_type=F32,
                                   precision=lax.Precision.HIGHEST)


def _s5_operators(lam_re, lam_im, log_dt, b_re, b_im, c_re, c_im, d_skip):
    lr, li = lam_re.astype(F32), lam_im.astype(F32)
    dt = jnp.exp(log_dt.astype(F32))[..., None]
    mag = jnp.exp(lr * dt)
    ar = mag * jnp.cos(li * dt)
    ai = mag * jnp.sin(li * dt)
    den = lr * lr + li * li
    fr = ((ar - 1.0) * lr + ai * li) / den
    fi = (ai * lr - (ar - 1.0) * li) / den
    br, bi = b_re.astype(F32), b_im.astype(F32)
    bbr = fr[..., None] * br - fi[..., None] * bi
    bbi = fr[..., None] * bi + fi[..., None] * br
    cr, ci = c_re.astype(F32), c_im.astype(F32)
    e_pow = jnp.arange(CHUNK + 1, dtype=F32)[:, None, None, None]
    pmag = jnp.exp(e_pow * (lr * dt))
    pr = pmag * jnp.cos(e_pow * (li * dt))
    pi = pmag * jnp.sin(e_pow * (li * dt))

    def by_position(p):
        return jnp.stack([p[:CHUNK, 0][::-1], p[:CHUNK, 1]], axis=0).transpose(0, 2, 1, 3)
    qr, qi = by_position(pr), by_position(pi)

    qr_p, qi_p = qr.transpose(0, 1, 3, 2)[..., None], qi.transpose(0, 1, 3, 2)[..., None]
    st_re = (qr_p * bbr[:, :, :, None, :] - qi_p * bbi[:, :, :, None, :]).reshape(2, G_A, P_A, SSM_TILE)
    st_im = (qr_p * bbi[:, :, :, None, :] + qi_p * bbr[:, :, :, None, :]).reshape(2, G_A, P_A, SSM_TILE)
    wst_t = jnp.concatenate([st_re[0], st_re[1], st_im[0], st_im[1]], axis=1)

    bbr_t, bbi_t = bbr.transpose(0, 1, 3, 2)[:, :, None], bbi.transpose(0, 1, 3, 2)[:, :, None]
    ab_re = qr[:, :, :, None, :] * bbr_t - qi[:, :, :, None, :] * bbi_t
    ab_im = qr[:, :, :, None, :] * bbi_t + qi[:, :, :, None, :] * bbr_t
    lhs = jnp.concatenate([ab_re, ab_im], axis=-1).reshape(2 * G_A, SSM_TILE, 2 * P_A)
    rhs = jnp.concatenate([cr, -ci], axis=-1).reshape(2 * G_A, SSM_GROUP, 2 * P_A)
    klag_t = pl.pallas_call(
        _lag_kernel,
        grid=(2 * G_A // LAG_BATCH,),
        in_specs=[pl.BlockSpec((LAG_BATCH, SSM_GROUP, 2 * P_A), lambda i: (i, 0, 0)),
                  pl.BlockSpec((LAG_BATCH, SSM_TILE, 2 * P_A), lambda i: (i, 0, 0))],
        out_specs=pl.BlockSpec((LAG_BATCH, SSM_GROUP, SSM_TILE), lambda i: (i, 0, 0)),
        out_shape=jax.ShapeDtypeStruct((2 * G_A, SSM_GROUP, SSM_TILE), F32),
        compiler_params=_cparams(1),
        name="s5_lag_kernels",
    )(rhs, lhs).reshape(2, G_A, SSM_GROUP, SSM_TILE)

    pad = jnp.zeros_like(klag_t[0])
    fwd_p = jnp.concatenate([klag_t[0], pad], axis=-1)
    bwd_p = jnp.concatenate([pad, klag_t[1]], axis=-1)
    rows = []
    for t in range(CHUNK):
        lo_f = SSM_GROUP * (CHUNK - 1 - t)
        lo_b = SSM_TILE - SSM_GROUP * t
        rows.append(fwd_p[:, :, lo_f:lo_f + SSM_TILE] + bwd_p[:, :, lo_b:lo_b + SSM_TILE])
    diag = jnp.eye(SSM_TILE, dtype=F32).reshape(1, CHUNK, SSM_GROUP, SSM_TILE)
    kt_t = jnp.stack(rows, axis=1) + diag * d_skip.astype(F32).reshape(G_A, 1, SSM_GROUP, 1)
    kt_t = kt_t.reshape(G_A, SSM_TILE, SSM_TILE)

    def by_output(p):
        return jnp.stack([p[1:, 0], p[1:, 1][::-1]], axis=0).transpose(0, 2, 1, 3)[:, :, :, None, :]
    er, ei = by_output(pr), by_output(pi)
    o_re = cr[:, :, None] * er - ci[:, :, None] * ei
    o_im = -(cr[:, :, None] * ei + ci[:, :, None] * er)
    wo_t = jnp.concatenate([o_re[0], o_re[1], o_im[0], o_im[1]], axis=-1).reshape(G_A, SSM_TILE, 4 * P_A)

    a16r = jnp.concatenate([pr[CHUNK, 0], pr[CHUNK, 1]], axis=-1)[:, None, :]
    a16i = jnp.concatenate([pi[CHUNK, 0], pi[CHUNK, 1]], axis=-1)[:, None, :]
    return wst_t.astype(BF16), kt_t.astype(BF16), wo_t.astype(BF16), a16r, a16i


def _s5_kernel(u_ref, wst_ref, kt_ref, wo_ref, a16r_ref, a16i_ref, h0r_ref, h0i_ref,
               pin_ref, pout_ref, y_ref, fr_ref, fi_ref, ut, yt, s_sc, hf_sc, hb_sc, *, nb, nc, gpb):
    gb = pl.program_id(2)
    seq = nc * CHUNK
    r = nb * nc
    sw = 2 * P_A
    n_cb = u_ref.shape[1] // LANES
    pieces = [b * seq + q * PERM_TILE for b in range(nb) for q in range(seq // PERM_TILE)]

    @pl.when(gb == 0)
    def _():
        for s in range(CHUNK):
            cols = []
            for cb in range(n_cb):
                rows = jnp.concatenate([u_ref[o + s * CHUNK:o + (s + 1) * CHUNK, cb * LANES:(cb + 1) * LANES]
                                        for o in pieces], axis=0)
                cols.append(rows.astype(F32).T.astype(BF16))
            ut[:, s * r:(s + 1) * r] = _dot(jnp.concatenate(cols, axis=0), pin_ref[...]).astype(BF16)

    fwd = (lax.broadcasted_iota(jnp.int32, (1, sw), 1)) < P_A

    def groups(it, carry):
        js = [it * S5_INTERLEAVE + i for i in range(S5_INTERLEAVE)]
        grows = [pl.multiple_of((gb * gpb + j) * SSM_GROUP, SSM_GROUP) for j in js]
        ds = [jnp.concatenate([ut[pl.ds(g, SSM_GROUP), s * r:(s + 1) * r] for s in range(CHUNK)], axis=0)
              for g in grows]
        for i, j in enumerate(js):
            s_t = _dot(wst_ref[j], ds[i])
            s_sc[i, 0] = s_t[:sw].T
            s_sc[i, 1] = s_t[sw:].T
        a_r = [a16r_ref[j] for j in js]
        a_i = [a16i_ref[j] for j in js]
        re = [h0r_ref[j] for j in js]
        im = [h0i_ref[j] for j in js]
        for k in range(nc):
            rf = pl.ds(k * nb, nb)
            rb = pl.ds((nc - 1 - k) * nb, nb)
            for i in range(S5_INTERLEAVE):
                hf_sc[i, 0, rf, :] = re[i]
                hf_sc[i, 1, rf, :] = im[i]
                hb_sc[i, 0, rb, :] = re[i]
                hb_sc[i, 1, rb, :] = im[i]
                xr = jnp.where(fwd, s_sc[i, 0, rf, :], s_sc[i, 0, rb, :])
                xi = jnp.where(fwd, s_sc[i, 1, rf, :], s_sc[i, 1, rb, :])
                re[i], im[i] = a_r[i] * re[i] - a_i[i] * im[i] + xr, a_r[i] * im[i] + a_i[i] * re[i] + xi
        for i, j in enumerate(js):
            fr_ref[j] = re[i]
            fi_ref[j] = im[i]
            hin_t = jnp.concatenate([jnp.where(fwd, hf_sc[i, 0], hb_sc[i, 0]).T,
                                     jnp.where(fwd, hf_sc[i, 1], hb_sc[i, 1]).T], axis=0)
            y_t = (_dot(kt_ref[j], ds[i]) + _dot(wo_ref[j], hin_t.astype(BF16))).astype(BF16)
            for t in range(CHUNK):
                yt[pl.ds(grows[i], SSM_GROUP), t * r:(t + 1) * r] = y_t[t * SSM_GROUP:(t + 1) * SSM_GROUP, :]
        return carry

    lax.fori_loop(0, gpb // S5_INTERLEAVE, groups, 0)

    @pl.when(gb == pl.num_programs(2) - 1)
    def _():
        for t in range(CHUNK):
            full = _dot(yt[:, t * r:(t + 1) * r], pout_ref[...])
            for cb in range(n_cb):
                rows = full[cb * LANES:(cb + 1) * LANES, :].T.astype(BF16)
                for idx, o in enumerate(pieces):
                    y_ref[o + t * CHUNK:o + (t + 1) * CHUNK, cb * LANES:(cb + 1) * LANES] = (
                        rows[idx * CHUNK:(idx + 1) * CHUNK, :])


def _s5_mix(u, ops, h0_re, h0_im, bsz, seq_len, nb, cw):
    wst_t, kt_t, wo_t, a16r, a16i = ops
    t, w = u.shape
    nc = seq_len // CHUNK
    r = nb * nc
    gpb = S5_GROUPS_PER_STEP
    n_row_tiles = bsz // nb
    n_col_tiles = w // cw
    gb_per_tile = cw // SSM_GROUP // gpb
    assert r % 128 == 0 and cw % (SSM_GROUP * gpb) == 0 and bsz % nb == 0

    def h0_layout(h0):
        return h0.astype(F32).transpose(2, 0, 1, 3).reshape(G_A, bsz, 2 * P_A)

    tile = pl.BlockSpec((nb * seq_len, cw), lambda i, c, g: (i, c))
    grp = lambda i, c, g: (c * gb_per_tile + g, 0, 0)
    wspec = pl.BlockSpec((gpb, SSM_TILE, SSM_TILE), grp)
    aspec = pl.BlockSpec((gpb, 1, 2 * P_A), grp)
    hspec = pl.BlockSpec((gpb, nb, 2 * P_A), lambda i, c, g: (c * gb_per_tile + g, i, 0))
    state = jax.ShapeDtypeStruct((G_A, bsz, 2 * P_A), F32)
    scratch = [pltpu.VMEM((cw, CHUNK * r), BF16)] * 2 + [pltpu.VMEM((S5_INTERLEAVE, 2, r, 2 * P_A), F32)] * 3
    src = lax.broadcasted_iota(jnp.int32, (r, r), 0)
    dst = lax.broadcasted_iota(jnp.int32, (r, r), 1)
    pin = (dst == (src % nc) * nb + src // nc).astype(BF16)
    pspec = pl.BlockSpec((r, r), lambda i, c, g: (0, 0))

    y, f_re, f_im = pl.pallas_call(
        functools.partial(_s5_kernel, nb=nb, nc=nc, gpb=gpb),
        grid=(n_row_tiles, n_col_tiles, gb_per_tile),
        in_specs=[tile, wspec, wspec, wspec, aspec, aspec, hspec, hspec, pspec, pspec],
        out_specs=[tile, hspec, hspec],
        out_shape=[jax.ShapeDtypeStruct((t, w), BF16), state, state],
        scratch_shapes=scratch,
        compiler_params=_cparams(3),
        name="s5_core",
    )(u, wst_t, kt_t, wo_t, a16r, a16i, h0_layout(h0_re), h0_layout(h0_im), pin, pin.T)

    def fin(f):
        return f.reshape(G_A, bsz, 2, P_A).transpose(1, 2, 0, 3)
    return y, fin(f_re), fin(f_im)


def _diff_lambda(lq1, lk1, lq2, lk2, lam_init):
    return (jnp.exp(jnp.sum(lq1[...] * lk1[...], axis=-1, keepdims=True))
            - jnp.exp(jnp.sum(lq2[...] * lk2[...], axis=-1, keepdims=True)) + lam_init)


def _attn_kernel(*refs, lam_init, cached):
    if cached:
        q_ref, kc_ref, kn_ref, vc_ref, vn_ref, szb_ref, lq1, lk1, lq2, lk2, g_ref, o_ref = refs
    else:
        q_ref, kn_ref, vn_ref, szb_ref, lq1, lk1, lq2, lk2, g_ref, o_ref = refs
    lam = _diff_lambda(lq1, lk1, lq2, lk2, lam_init)
    lq = q_ref.shape[0]
    hw = 2 * DH
    low = lax.broadcasted_iota(jnp.int32, (1, hw), 1) < DH
    zero = jnp.zeros((), BF16)
    ones = jnp.ones((kn_ref.shape[0], hw), BF16)
    for h in range(H_B):
        cols = slice(h * hw, (h + 1) * hw)
        qh = q_ref[:, cols]
        qs = jnp.concatenate([jnp.where(low, qh, zero), jnp.where(low, zero, qh)], axis=0)
        s = _dot_nt(qs, kn_ref[:, cols])
        if cached:
            s = jnp.concatenate([_dot_nt(qs, kc_ref[:, cols]), s], axis=1)
        e = jnp.exp2(s - jnp.max(s, axis=-1, keepdims=True)).astype(BF16)
        vn = jnp.concatenate([vn_ref[:, cols], ones], axis=1)
        if cached:
            lc = kc_ref.shape[0]
            vc = jnp.concatenate([vc_ref[:, cols], ones[:lc]], axis=1)
            oa = _dot(e[:, :lc], vc) + _dot(e[:, lc:], vn)
        else:
            oa = _dot(e, vn)
        on = oa[:, :hw] * (1.0 / oa[:, hw:])
        o = on[:lq] - lam * on[lq:]
        o = o * lax.rsqrt(jnp.mean(o * o, axis=-1, keepdims=True) + LN_EPS)
        o = o * g_ref[...] * (1.0 - lam_init)
        o_ref[:, cols] = (o * szb_ref[:, cols].astype(F32)).astype(BF16)


def _lam_specs(n_axes):
    zero = lambda *idx: (0, 0)
    return [pl.BlockSpec((1, DH), zero)] * 4 + [pl.BlockSpec((1, 2 * DH), zero)]


def _attention_prompt(q, k, v, szb, lam_vecs, subln, lam_init, bsz, seq_len):
    blk = pl.BlockSpec((seq_len, D_MODEL), lambda b: (b, 0))
    return pl.pallas_call(
        functools.partial(_attn_kernel, lam_init=lam_init, cached=False),
        grid=(bsz,),
        in_specs=[blk] * 4 + _lam_specs(1),
        out_specs=blk,
        out_shape=jax.ShapeDtypeStruct(q.shape, BF16),
        compiler_params=_cparams(1),
        name="diff_attention_prompt",
    )(q, k, v, szb, *lam_vecs, subln)


def _attention_sample(q, kc, kn, vc, vn, szb, lam_vecs, subln, lam_init, bsz, seq_len, tq):
    nq = seq_len // tq
    past = kc.shape[1]
    qblk = pl.BlockSpec((tq, D_MODEL), lambda b, i: (b * nq + i, 0))
    cblk = pl.BlockSpec((None, past, D_MODEL), lambda b, i: (b, 0, 0))
    nblk = pl.BlockSpec((seq_len, D_MODEL), lambda b, i: (b, 0))
    return pl.pallas_call(
        functools.partial(_attn_kernel, lam_init=lam_init, cached=True),
        grid=(bsz, nq),
        in_specs=[qblk, cblk, nblk, cblk, nblk, qblk] + _lam_specs(2),
        out_specs=qblk,
        out_shape=jax.ShapeDtypeStruct(q.shape, BF16),
        compiler_params=_cparams(2),
        name="diff_attention_sample",
    )(q, kc, kn, vc, vn, szb, *lam_vecs, subln)


def _even_out_kernel(ys_ref, sza_ref, yb_ref, x_ref, gate_ref, wglu_ref, bglu_ref, wout_ref, g_ref, b_ref,
                     perm_ref, o_ref, *, alpha):
    wa = ys_ref.shape[1]
    for r in range(x_ref.shape[0] // PERM_TILE):
        rows = slice(r * PERM_TILE, (r + 1) * PERM_TILE)
        ga = jax.nn.gelu(_dot(perm_ref[...], ys_ref[rows, :]))
        glu = jax.nn.sigmoid(_dot(ga.astype(BF16), wglu_ref[...]) + bglu_ref[...])
        ya = (ga * glu * sza_ref[rows, :].astype(F32)).astype(BF16)
        out = _dot(ya, wout_ref[:wa, :]) + _dot(yb_ref[rows, :], wout_ref[wa:, :])
        o_ref[rows, :] = _post_norm(x_ref[rows, :], gate_ref[...], out, g_ref[...], b_ref[...], alpha)


def _even_out_proj(ys, sza, yb, x2, mod3, w_glu_bf, b_glu, w_out_bf, ln_g, ln_b, layer, cond_of, tm, alpha):
    t, d = x2.shape
    assert tm % PERM_TILE == 0
    row = pl.BlockSpec((tm, d), lambda i: (i, 0))
    full = _resident
    return pl.pallas_call(
        functools.partial(_even_out_kernel, alpha=alpha),
        grid=(t // tm,),
        in_specs=[row, row, row, row, _mod_spec(layer, 2, cond_of), full(w_glu_bf.shape), full((1, d)),
                  full(w_out_bf.shape), full((1, d)), full((1, d)), full((PERM_TILE, PERM_TILE))],
        out_specs=row,
        out_shape=jax.ShapeDtypeStruct((t, d), F32),
        compiler_params=_cparams(1),
        name="even_out_proj",
    )(ys, sza, yb, x2, mod3, w_glu_bf, b_glu.reshape(1, d), w_out_bf, ln_g.reshape(1, d), ln_b.reshape(1, d),
      _chunk_transpose_matrix())


def _dft_mats(n, scale):
    k = np.arange(n, dtype=np.int64)
    ang = ((k[:, None] * k[None, :]) % n).astype(np.float64) * (2.0 * math.pi / n)
    return (jnp.asarray((np.cos(ang) * scale).astype(np.float32)).astype(BF16),
            jnp.asarray((np.sin(ang) * scale).astype(np.float32)).astype(BF16))


def _odd_in_kernel(x_ref, shift_ref, scale_ref, w_ref, cc_ref, sc_ref, uc_ref, us_ref, sz_ref):
    for r in range(x_ref.shape[0] // SUB_ROWS):
        rows = slice(r * SUB_ROWS, (r + 1) * SUB_ROWS)
        h = (_ln_rows(x_ref[rows, :]) * (1.0 + scale_ref[...]) + shift_ref[...]).astype(BF16)
        u = _dot(h, w_ref[:, :W_C]).astype(BF16)
        sz_ref[rows, :] = _silu(_dot(h, w_ref[:, W_C:])).astype(BF16)
        for g in range(NG_C):
            cols = slice(g * GC_C, (g + 1) * GC_C)
            uc_ref[rows, cols] = _dot(u[:, cols], cc_ref[...]).astype(BF16)
            us_ref[rows, cols] = _dot(u[:, cols], sc_ref[...]).astype(BF16)


def _odd_in_proj(x2, mod3, w_in_bf, cc, sc, layer, cond_of, tm):
    t, d = x2.shape
    assert tm % SUB_ROWS == 0
    row = pl.BlockSpec((tm, d), lambda i: (i, 0))
    wide = pl.BlockSpec((tm, W_C), lambda i: (i, 0))
    full = _resident
    wide_bf = jax.ShapeDtypeStruct((t, W_C), BF16)
    return pl.pallas_call(
        _odd_in_kernel,
        grid=(t // tm,),
        in_specs=[row, _mod_spec(layer, 0, cond_of), _mod_spec(layer, 1, cond_of), full(w_in_bf.shape),
                  full(cc.shape), full(sc.shape)],
        out_specs=[wide] * 3,
        out_shape=[wide_bf] * 3,
        compiler_params=_cparams(1),
        name="odd_in_proj",
    )(x2, mod3, mod3, w_in_bf, cc, sc)


def _odd_out_kernel(cl_ref, sl_ref, uc_ref, us_ref, sz_ref, x_ref, gate_ref, wf_ref, bf_ref, wo_ref, g_ref, b_ref,
                    o_ref, *, alpha, whole_seqs):
    seq_len = cl_ref.shape[1]
    sub = seq_len if whole_seqs else SUB_ROWS
    for r in range(x_ref.shape[0] // sub):
        rows = slice(r * sub, (r + 1) * sub)
        if whole_seqs:
            mixed = _dot(cl_ref[...], uc_ref[rows, :]) - _dot(sl_ref[...], us_ref[rows, :])
        else:
            mixed = _dot(cl_ref[rows, :], uc_ref[...]) - _dot(sl_ref[rows, :], us_ref[...])
        y = ((_dot(mixed.astype(BF16), wf_ref[...]) + bf_ref[...]) * sz_ref[rows, :].astype(F32)).astype(BF16)
        o_ref[rows, :] = _post_norm(x_ref[rows, :], gate_ref[...], _dot(y, wo_ref[...]), g_ref[...], b_ref[...],
                                    alpha)


def _odd_out_proj(uc, us, sz, x2, mod3, cl, sl, w_fno_bf, b_fno, w_out_bf, ln_g, ln_b, layer, cond_of,
                  bsz, seq_len, tl, alpha):
    t, d = x2.shape
    whole_seqs = tl >= seq_len
    full = _resident
    if whole_seqs:
        assert tl % seq_len == 0
        grid = (t // tl, 1)
        dft = _resident((seq_len, seq_len))
        seq = pl.BlockSpec((tl, W_C), lambda b, i: (b, 0))
        wide = seq
        row = pl.BlockSpec((tl, d), lambda b, i: (b, 0))
    else:
        assert seq_len % tl == 0 and tl % SUB_ROWS == 0
        nl = seq_len // tl
        grid = (bsz, nl)
        dft = pl.BlockSpec((tl, seq_len), lambda b, i: (i, 0))
        seq = pl.BlockSpec((seq_len, W_C), lambda b, i: (b, 0))
        wide = pl.BlockSpec((tl, W_C), lambda b, i: (b * nl + i, 0))
        row = pl.BlockSpec((tl, d), lambda b, i: (b * nl + i, 0))
    return pl.pallas_call(
        functools.partial(_odd_out_kernel, alpha=alpha, whole_seqs=whole_seqs),
        grid=grid,
        in_specs=[dft, dft, seq, seq, wide, row, _mod_spec(layer, 2, cond_of), full(w_fno_bf.shape),
                  full((1, W_C)), full(w_out_bf.shape), full((1, d)), full((1, d))],
        out_specs=row,
        out_shape=jax.ShapeDtypeStruct((t, d), F32),
        compiler_params=_cparams(2),
        name="odd_out_proj",
    )(cl, sl, uc, us, sz, x2, mod3, w_fno_bf, b_fno.reshape(1, W_C), w_out_bf, ln_g.reshape(1, d),
      ln_b.reshape(1, d))


def _rope_tables(seq_len):
    rows = seq_len // GRID_W
    row = jnp.repeat(jnp.arange(rows), GRID_W).astype(F32)
    col = jnp.tile(jnp.arange(GRID_W), rows).astype(F32)
    freqs = ROPE_BASE ** (-jnp.arange(ROT_FREQS, dtype=F32) / ROT_FREQS)
    ang = jnp.concatenate([row[:, None] * freqs, col[:, None] * freqs], axis=-1)
    cos, sin = jnp.cos(ang), jnp.sin(ang)
    cos128 = jnp.tile(cos, (1, 128 // ROT_HALF))
    sin128 = jnp.tile(jnp.concatenate([-sin, sin], axis=-1), (1, 128 // DH))
    return cos128, sin128


def kernel(x_prompt, x_sample, cache_k, cache_v, state_ssm_re, state_ssm_im, c, c_ctx, w_mod, b_mod, ln_g, ln_b, w_in_e, ssm_lam_re, ssm_lam_im, ssm_log_dt, ssm_b_re, ssm_b_im, ssm_c_re, ssm_c_im, ssm_d, w_glu, b_glu, lam_q1, lam_k1, lam_q2, lam_k2, subln_g, w_out_e, w_in_o, w_fno, b_fno, w_out_o):
    depth = w_mod.shape[0]
    bp_, lp, d = x_prompt.shape
    bs_, ls, _ = x_sample.shape
    past = cache_k.shape[2]
    alpha = (2 * depth) ** 0.25
    assert bs_ + 1 <= MOD_ROWS and d == D_MODEL

    cond8 = jnp.concatenate([c_ctx[None, :], c, jnp.zeros((MOD_ROWS - 1 - bs_, d), F32)], axis=0).astype(F32)
    mod3 = _modulation(cond8, w_mod, b_mod).reshape(depth * MOD_ROWS * 3, 1, d)

    tm = ROW_TILE
    cond_p = lambda *idx: 0
    cond_s_row = lambda i: 1 + i // (ls // tm)
    cond_s_grid = lambda b, i: 1 + b
    rope_tabs = _rope_tables(ls)
    xp = x_prompt.reshape(bp_ * lp, d)
    xs = x_sample.reshape(bs_ * ls, d)
    new_k, new_v, new_sr, new_si = [], [], [], []
    zeros_h0 = jnp.zeros((bp_, 2, G_A, P_A), F32)

    for layer in range(depth):
        if layer % 2 == 0:
            e = layer // 2
            lam_init = 0.8 - 0.6 * math.exp(-0.3 * layer)
            w_in_bf = w_in_e[e].astype(BF16)
            w_glu_bf = w_glu[e].astype(BF16)
            w_out_bf = w_out_e[e].astype(BF16)
            ops = _s5_operators(ssm_lam_re[e], ssm_lam_im[e], ssm_log_dt[e], ssm_b_re[e], ssm_b_im[e],
                                ssm_c_re[e], ssm_c_im[e], ssm_d[e])
            lam_vecs = [v[e].reshape(1, DH).astype(F32) for v in (lam_q1, lam_k1, lam_q2, lam_k2)]
            subln = subln_g[e].reshape(1, 2 * DH).astype(F32)

            u, sza, q, kb, vb, szb, kf, vf = _even_in_proj(xp, mod3, w_in_bf, layer, cond_p, tm, None, lp)
            new_k.append(kf.reshape(bp_, H_B, 2, DH, lp).transpose(0, 4, 1, 2, 3))
            new_v.append(vf.reshape(bp_, lp, H_B, 2 * DH))
            ys, s_re, s_im = _s5_mix(u, ops, zeros_h0, zeros_h0, bp_, lp, S5_TILE_ELEMS // (lp * d), d)
            new_sr.append(s_re)
            new_si.append(s_im)
            yb = _attention_prompt(q, kb, vb, szb, lam_vecs, subln, lam_init, bp_, lp)
            xp = _even_out_proj(ys, sza, yb, xp, mod3, w_glu_bf, b_glu[e], w_out_bf, ln_g[layer], ln_b[layer],
                                layer, cond_p, tm, alpha)

            u, sza, q, kb, vb, szb = _even_in_proj(xs, mod3, w_in_bf, layer, cond_s_row, tm, rope_tabs, ls)
            ys, _, _ = _s5_mix(u, ops, state_ssm_re[:, e], state_ssm_im[:, e], bs_, ls, bs_,
                               S5_TILE_ELEMS // (bs_ * ls))
            kc = cache_k[:, e].reshape(bs_, past, d).astype(BF16)
            vc = cache_v[:, e].reshape(bs_, past, d).astype(BF16)
            yb = _attention_sample(q, kc, kb, vc, vb, szb, lam_vecs, subln, lam_init, bs_, ls, ATTN_Q_TILE)
            xs = _even_out_proj(ys, sza, yb, xs, mod3, w_glu_bf, b_glu[e], w_out_bf, ln_g[layer], ln_b[layer],
                                layer, cond_s_row, tm, alpha)
        else:
            o = layer // 2
            w_in_bf = w_in_o[o].astype(BF16)
            w_fno_bf = w_fno[o].astype(BF16)
            w_out_bf = w_out_o[o].astype(BF16)
            cc, sc = _dft_mats(GC_C, GC_C ** -0.5)
            for which in ("prompt", "sample"):
                if which == "prompt":
                    x2, cond_row, cond_grid, bsz, seq = xp, cond_p, cond_p, bp_, lp
                else:
                    x2, cond_row, cond_grid, bsz, seq = xs, cond_s_row, cond_s_grid, bs_, ls
                cl, sl = _dft_mats(seq, seq ** -0.5)
                uc, us, sz = _odd_in_proj(x2, mod3, w_in_bf, cc, sc, layer, cond_row, tm)
                x2 = _odd_out_proj(uc, us, sz, x2, mod3, cl, sl, w_fno_bf, b_fno[o], w_out_bf, ln_g[layer],
                                   ln_b[layer], layer, cond_grid, bsz, seq, tm, alpha)
                if which == "prompt":
                    xp = x2
                else:
                    xs = x2

    return (xp.reshape(bp_, lp, d), xs.reshape(bs_, ls, d), jnp.stack(new_k, axis=1), jnp.stack(new_v, axis=1),
            jnp.stack(new_sr, axis=1), jnp.stack(new_si, axis=1))
```

```python
import functools
import math

import jax
import jax.numpy as jnp
import numpy as np
from jax import lax
from jax.experimental import pallas as pl
from jax.experimental.pallas import tpu as pltpu

F32 = jnp.float32
BF16 = jnp.bfloat16

D_MODEL = 1024
GRID_W = 64
SSM_GROUP = 16
G_A = D_MODEL // SSM_GROUP
P_A = 64
DH = 64
H_B = D_MODEL // (2 * DH)
ROPE_BASE = 10000.0
ROT_HALF = DH // 2
ROT_FREQS = DH // 4
NG_C = 8
GC_C = 2 * D_MODEL // NG_C
W_C = 2 * D_MODEL
LN_EPS = 1e-5
LOG2_E = 1.4426950408889634
CHUNK = 16
SSM_TILE = CHUNK * SSM_GROUP
LANES = 128
PERM_TILE = CHUNK * CHUNK
SUB_ROWS = PERM_TILE
ROW_TILE = 2 * SUB_ROWS
ATTN_Q_TILE = 256
S5_GROUPS_PER_STEP = 16
S5_INTERLEAVE = 4
S5_TILE_ELEMS = 2048 * 1024
MOD_ROWS = 8
VMEM_LIMIT = 56 * 1024 * 1024


def _cparams(n_axes):
    return pltpu.CompilerParams(dimension_semantics=("arbitrary",) * n_axes, vmem_limit_bytes=VMEM_LIMIT)


def _ln_rows(x):
    mu = jnp.mean(x, axis=-1, keepdims=True)
    xc = x - mu
    var = jnp.mean(xc * xc, axis=-1, keepdims=True)
    return xc * lax.rsqrt(var + LN_EPS)


def _silu(z):
    return z * jax.nn.sigmoid(z)


def _dot(a, b):
    return jnp.dot(a, b, preferred_element_type=F32)


def _dot_nt(a, b):
    return lax.dot_general(a, b, (((1,), (1,)), ((), ())), preferred_element_type=F32)


def _post_norm(x, gate, out, g, b, alpha):
    return _ln_rows(alpha * x + gate * out) * g + b


def _mod_kernel(c_ref, w_ref, b_ref, o_ref):
    c = _silu(c_ref[...]).astype(BF16)
    o_ref[...] = _dot(c, w_ref[...].astype(BF16)) + b_ref[...]


def _modulation(cond8, w_mod, b_mod):
    depth, d, n3 = w_mod.shape
    tn = 1024
    return pl.pallas_call(
        _mod_kernel,
        grid=(depth, n3 // tn),
        in_specs=[
            pl.BlockSpec((MOD_ROWS, d), lambda l, j: (0, 0)),
            pl.BlockSpec((None, d, tn), lambda l, j: (l, 0, j)),
            pl.BlockSpec((None, 1, tn), lambda l, j: (l, 0, j)),
        ],
        out_specs=pl.BlockSpec((None, MOD_ROWS, tn), lambda l, j: (l, 0, j)),
        out_shape=jax.ShapeDtypeStruct((depth, MOD_ROWS, n3), F32),
        compiler_params=_cparams(2),
        name="modulation",
    )(cond8, w_mod, b_mod.reshape(depth, 1, n3))


def _resident(shape):
    return pl.BlockSpec(shape, lambda *idx: (0,) * len(shape), pipeline_mode=pl.Buffered(1))


def _mod_spec(layer, part, cond_of):
    return pl.BlockSpec((None, 1, D_MODEL), lambda *idx: ((layer * MOD_ROWS + cond_of(*idx)) * 3 + part, 0, 0))


def _rope(x, cos, sin_signed, first_half):
    blocks = []
    for hh in range(x.shape[1] // 128):
        b = x[:, hh * 128:(hh + 1) * 128]
        partner = jnp.where(first_half, pltpu.roll(b, 128 - ROT_HALF, 1), pltpu.roll(b, ROT_HALF, 1))
        blocks.append(b * cos + partner * sin_signed)
    return jnp.concatenate(blocks, axis=1)


def _chunk_transpose_matrix():
    i = lax.broadcasted_iota(jnp.int32, (PERM_TILE, PERM_TILE), 0)
    j = lax.broadcasted_iota(jnp.int32, (PERM_TILE, PERM_TILE), 1)
    return (j == (i % CHUNK) * CHUNK + i // CHUNK).astype(BF16)


def _even_in_kernel(*refs, rope):
    if rope:
        (x_ref, shift_ref, scale_ref, w_ref, perm_ref, cos_ref, sin_ref,
         u_ref, sza_ref, q_ref, kb_ref, vb_ref, szb_ref) = refs
    else:
        (x_ref, shift_ref, scale_ref, w_ref, perm_ref, u_ref, sza_ref, q_ref, kb_ref, vb_ref, szb_ref,
         kf_ref, vf_ref) = refs
    w = D_MODEL
    for r in range(x_ref.shape[0] // PERM_TILE):
        rows = slice(r * PERM_TILE, (r + 1) * PERM_TILE)
        h = (_ln_rows(x_ref[rows, :]) * (1.0 + scale_ref[...]) + shift_ref[...]).astype(BF16)

        def proj(j):
            return _dot(h, w_ref[:, j * w:(j + 1) * w])

        u_ref[rows, :] = _dot(perm_ref[...], proj(0).astype(BF16)).astype(BF16)
        sza_ref[rows, :] = _silu(proj(1)).astype(BF16)
        q = proj(2) * (DH ** -0.5 * LOG2_E)
        k = proj(3)
        v = proj(4)
        if rope:
            lane = lax.broadcasted_iota(jnp.int32, (1, 128), 1)
            first_half = (lane % DH) < ROT_HALF
            q = _rope(q, cos_ref[rows, :], sin_ref[rows, :], first_half)
            kb_ref[rows, :] = _rope(k, cos_ref[rows, :], sin_ref[rows, :], first_half).astype(BF16)
        else:
            kf_ref[r] = k.T
            vf_ref[rows, :] = v
            kb_ref[rows, :] = k.astype(BF16)
        q_ref[rows, :] = q.astype(BF16)
        vb_ref[rows, :] = v.astype(BF16)
        szb_ref[rows, :] = _silu(proj(5)).astype(BF16)


def _even_in_proj(x2, mod3, w_in_bf, layer, cond_of, tm, rope_tabs, seq_len):
    t, d = x2.shape
    n = w_in_bf.shape[1]
    assert tm % PERM_TILE == 0
    row = pl.BlockSpec((tm, d), lambda i: (i, 0))
    in_specs = [row, _mod_spec(layer, 0, cond_of), _mod_spec(layer, 1, cond_of),
                _resident((d, n)), _resident((PERM_TILE, PERM_TILE))]
    args = [x2, mod3, mod3, w_in_bf, _chunk_transpose_matrix()]
    bf = jax.ShapeDtypeStruct((t, d), BF16)
    out_shape = [bf] * 6
    out_specs = [row] * 6
    if rope_tabs is not None:
        tiles_per_seq = seq_len // tm
        tab = pl.BlockSpec((tm, 128), lambda i: (i % tiles_per_seq, 0))
        in_specs += [tab, tab]
        args += list(rope_tabs)
    else:
        assert seq_len == PERM_TILE
        out_shape += [jax.ShapeDtypeStruct((t // seq_len, d, seq_len), F32), jax.ShapeDtypeStruct((t, d), F32)]
        out_specs += [pl.BlockSpec((tm // seq_len, d, seq_len), lambda i: (i, 0, 0)), row]
    return pl.pallas_call(
        functools.partial(_even_in_kernel, rope=rope_tabs is not None),
        grid=(t // tm,),
        in_specs=in_specs,
        out_specs=out_specs,
        out_shape=out_shape,
        compiler_params=_cparams(1),
        name="even_in_proj",
    )(*args)


BUILD_GROUPS = 4


def _s5_build_kernel(qer_ref, qei_ref, btr_ref, bti_ref, cr_ref, ci_ref, ccr_ref, cci_ref, ecr_ref, eci_ref, d_ref,
                     wst_ref, kt_ref, wo_ref):
    def hdot(a, b):
        return jnp.dot(a, b, preferred_element_type=F32, precision=lax.Precision.HIGHEST)

    lane = lax.broadcasted_iota(jnp.int32, (SSM_GROUP, SSM_TILE), 1)
    sub = lax.broadcasted_iota(jnp.int32, (SSM_GROUP, SSM_TILE), 0)
    zeros = jnp.zeros((SSM_GROUP, SSM_TILE), F32)
    for g in range(BUILD_GROUPS):
        st_re = [qer_ref[d, g] * btr_ref[d, g] - qei_ref[d, g] * bti_ref[d, g] for d in range(2)]
        st_im = [qer_ref[d, g] * bti_ref[d, g] + qei_ref[d, g] * btr_ref[d, g] for d in range(2)]
        wst_ref[g] = jnp.concatenate([st_re[0], st_re[1], st_im[0], st_im[1]], axis=0).astype(BF16)
        klag = [hdot(cr_ref[d, g], st_re[d]) - hdot(ci_ref[d, g], st_im[d]) for d in range(2)]
        fwd_p = jnp.concatenate([klag[0], zeros], axis=1)
        bwd_p = jnp.concatenate([zeros, klag[1]], axis=1)
        ccr, cci = ccr_ref[g], cci_ref[g]
        for t in range(CHUNK):
            lo_f = SSM_GROUP * (CHUNK - 1 - t)
            lo_b = SSM_TILE - SSM_GROUP * t
            skip = jnp.where(lane == t * SSM_GROUP + sub, d_ref[g], 0.0)
            rows = slice(t * SSM_GROUP, (t + 1) * SSM_GROUP)
            kt_ref[g, rows, :] = (fwd_p[:, lo_f:lo_f + SSM_TILE] + bwd_p[:, lo_b:lo_b + SSM_TILE] + skip).astype(BF16)
            er, ei = ecr_ref[g, t:t + 1, :], eci_ref[g, t:t + 1, :]
            wo_ref[g, rows, :] = jnp.concatenate([ccr * er - cci * ei, -(ccr * ei + cci * er)], axis=1).astype(BF16)


def _s5_operators(lam_re, lam_im, log_dt, b_re, b_im, c_re, c_im, d_skip):
    lr, li = lam_re.astype(F32), lam_im.astype(F32)
    dt = jnp.exp(log_dt.astype(F32))[..., None]
    mag = jnp.exp(lr * dt)
    ar = mag * jnp.cos(li * dt)
    ai = mag * jnp.sin(li * dt)
    den = lr * lr + li * li
    fr = ((ar - 1.0) * lr + ai * li) / den
    fi = (ai * lr - (ar - 1.0) * li) / den
    br, bi = b_re.astype(F32), b_im.astype(F32)
    bbr = fr[..., None] * br - fi[..., None] * bi
    bbi = fr[..., None] * bi + fi[..., None] * br
    cr, ci = c_re.astype(F32), c_im.astype(F32)
    e_pow = jnp.arange(CHUNK + 1, dtype=F32)[:, None, None, None]
    pmag = jnp.exp(e_pow * (lr * dt))
    pr = pmag * jnp.cos(e_pow * (li * dt))
    pi = pmag * jnp.sin(e_pow * (li * dt))

    def by_position(p):
        return jnp.stack([p[:CHUNK, 0][::-1], p[:CHUNK, 1]], axis=0).transpose(0, 2, 1, 3)
    qr, qi = by_position(pr), by_position(pi)

    full = (2, G_A, P_A, CHUNK, SSM_GROUP)

    def expand_pos(q):
        return jnp.broadcast_to(q.transpose(0, 1, 3, 2)[..., None], full).reshape(2, G_A, P_A, SSM_TILE)

    def expand_chan(b):
        return jnp.broadcast_to(b[:, :, :, None, :], full).reshape(2, G_A, P_A, SSM_TILE)

    def by_output(p):
        return jnp.concatenate([p[1:, 0], p[1:, 1][::-1]], axis=-1).transpose(1, 0, 2)
    cat_c = lambda c: jnp.concatenate([c[0], c[1]], axis=-1)

    gb = BUILD_GROUPS
    wide = pl.BlockSpec((2, gb, P_A, SSM_TILE), lambda i: (0, i, 0, 0))
    cspec = pl.BlockSpec((2, gb, SSM_GROUP, P_A), lambda i: (0, i, 0, 0))
    half = pl.BlockSpec((gb, SSM_GROUP, 2 * P_A), lambda i: (i, 0, 0))
    dspec = pl.BlockSpec((gb, SSM_GROUP, 1), lambda i: (i, 0, 0))
    ospec = pl.BlockSpec((gb, SSM_TILE, SSM_TILE), lambda i: (i, 0, 0))
    oshape = jax.ShapeDtypeStruct((G_A, SSM_TILE, SSM_TILE), BF16)
    wst_t, kt_t, wo_t = pl.pallas_call(
        _s5_build_kernel,
        grid=(G_A // gb,),
        in_specs=[wide] * 4 + [cspec] * 2 + [half] * 4 + [dspec],
        out_specs=[ospec] * 3,
        out_shape=[oshape] * 3,
        compiler_params=_cparams(1),
        name="s5_build_operators",
    )(expand_pos(qr), expand_pos(qi), expand_chan(bbr), expand_chan(bbi), cr, ci, cat_c(cr), cat_c(ci),
      by_output(pr), by_output(pi), d_skip.astype(F32).reshape(G_A, SSM_GROUP, 1))

    a16r = jnp.concatenate([pr[CHUNK, 0], pr[CHUNK, 1]], axis=-1)[:, None, :]
    a16i = jnp.concatenate([pi[CHUNK, 0], pi[CHUNK, 1]], axis=-1)[:, None, :]
    return wst_t, kt_t, wo_t, a16r, a16i


def _s5_kernel(u_ref, wst_ref, kt_ref, wo_ref, a16r_ref, a16i_ref, h0r_ref, h0i_ref,
               pin_ref, pout_ref, y_ref, fr_ref, fi_ref, ut, yt, s_sc, hf_sc, hb_sc, *, nb, nc, gpb):
    gb = pl.program_id(2)
    seq = nc * CHUNK
    r = nb * nc
    sw = 2 * P_A
    n_cb = u_ref.shape[1] // LANES
    pieces = [b * seq + q * PERM_TILE for b in range(nb) for q in range(seq // PERM_TILE)]

    @pl.when(gb == 0)
    def _():
        for s in range(CHUNK):
            cols = []
            for cb in range(n_cb):
                rows = jnp.concatenate([u_ref[o + s * CHUNK:o + (s + 1) * CHUNK, cb * LANES:(cb + 1) * LANES]
                                        for o in pieces], axis=0)
                cols.append(rows.astype(F32).T.astype(BF16))
            ut[:, s * r:(s + 1) * r] = _dot(jnp.concatenate(cols, axis=0), pin_ref[...]).astype(BF16)

    fwd = (lax.broadcasted_iota(jnp.int32, (1, sw), 1)) < P_A

    def groups(it, carry):
        js = [it * S5_INTERLEAVE + i for i in range(S5_INTERLEAVE)]
        grows = [pl.multiple_of((gb * gpb + j) * SSM_GROUP, SSM_GROUP) for j in js]
        ds = [jnp.concatenate([ut[pl.ds(g, SSM_GROUP), s * r:(s + 1) * r] for s in range(CHUNK)], axis=0)
              for g in grows]
        for i, j in enumerate(js):
            s_t = _dot(wst_ref[j], ds[i])
            s_sc[i, 0] = s_t[:sw].T
            s_sc[i, 1] = s_t[sw:].T
        a_r = [a16r_ref[j] for j in js]
        a_i = [a16i_ref[j] for j in js]
        re = [h0r_ref[j] for j in js]
        im = [h0i_ref[j] for j in js]
        for k in range(nc):
            rf = pl.ds(k * nb, nb)
            rb = pl.ds((nc - 1 - k) * nb, nb)
            for i in range(S5_INTERLEAVE):
                hf_sc[i, 0, rf, :] = re[i]
                hf_sc[i, 1, rf, :] = im[i]
                hb_sc[i, 0, rb, :] = re[i]
                hb_sc[i, 1, rb, :] = im[i]
                xr = jnp.where(fwd, s_sc[i, 0, rf, :], s_sc[i, 0, rb, :])
                xi = jnp.where(fwd, s_sc[i, 1, rf, :], s_sc[i, 1, rb, :])
                re[i], im[i] = a_r[i] * re[i] - a_i[i] * im[i] + xr, a_r[i] * im[i] + a_i[i] * re[i] + xi
        for i, j in enumerate(js):
            fr_ref[j] = re[i]
            fi_ref[j] = im[i]
            hin_t = jnp.concatenate([jnp.where(fwd, hf_sc[i, 0], hb_sc[i, 0]).T,
                                     jnp.where(fwd, hf_sc[i, 1], hb_sc[i, 1]).T], axis=0)
            y_t = (_dot(kt_ref[j], ds[i]) + _dot(wo_ref[j], hin_t.astype(BF16))).astype(BF16)
            for t in range(CHUNK):
                yt[pl.ds(grows[i], SSM_GROUP), t * r:(t + 1) * r] = y_t[t * SSM_GROUP:(t + 1) * SSM_GROUP, :]
        return carry

    lax.fori_loop(0, gpb // S5_INTERLEAVE, groups, 0)

    @pl.when(gb == pl.num_programs(2) - 1)
    def _():
        for t in range(CHUNK):
            full = _dot(yt[:, t * r:(t + 1) * r], pout_ref[...])
            for cb in range(n_cb):
                rows = full[cb * LANES:(cb + 1) * LANES, :].T.astype(BF16)
                for idx, o in enumerate(pieces):
                    y_ref[o + t * CHUNK:o + (t + 1) * CHUNK, cb * LANES:(cb + 1) * LANES] = (
                        rows[idx * CHUNK:(idx + 1) * CHUNK, :])


def _s5_mix(u, ops, h0_re, h0_im, bsz, seq_len, nb, cw):
    wst_t, kt_t, wo_t, a16r, a16i = ops
    t, w = u.shape
    nc = seq_len // CHUNK
    r = nb * nc
    gpb = S5_GROUPS_PER_STEP
    n_row_tiles = bsz // nb
    n_col_tiles = w // cw
    gb_per_tile = cw // SSM_GROUP // gpb
    assert r % 128 == 0 and cw % (SSM_GROUP * gpb) == 0 and bsz % nb == 0

    def h0_layout(h0):
        return h0.astype(F32).transpose(2, 0, 1, 3).reshape(G_A, bsz, 2 * P_A)

    tile = pl.BlockSpec((nb * seq_len, cw), lambda i, c, g: (i, c))
    grp = lambda i, c, g: (c * gb_per_tile + g, 0, 0)
    wspec = pl.BlockSpec((gpb, SSM_TILE, SSM_TILE), grp)
    aspec = pl.BlockSpec((gpb, 1, 2 * P_A), grp)
    hspec = pl.BlockSpec((gpb, nb, 2 * P_A), lambda i, c, g: (c * gb_per_tile + g, i, 0))
    state = jax.ShapeDtypeStruct((G_A, bsz, 2 * P_A), F32)
    scratch = [pltpu.VMEM((cw, CHUNK * r), BF16)] * 2 + [pltpu.VMEM((S5_INTERLEAVE, 2, r, 2 * P_A), F32)] * 3
    src = lax.broadcasted_iota(jnp.int32, (r, r), 0)
    dst = lax.broadcasted_iota(jnp.int32, (r, r), 1)
    pin = (dst == (src % nc) * nb + src // nc).astype(BF16)
    pspec = pl.BlockSpec((r, r), lambda i, c, g: (0, 0))

    y, f_re, f_im = pl.pallas_call(
        functools.partial(_s5_kernel, nb=nb, nc=nc, gpb=gpb),
        grid=(n_row_tiles, n_col_tiles, gb_per_tile),
        in_specs=[tile, wspec, wspec, wspec, aspec, aspec, hspec, hspec, pspec, pspec],
        out_specs=[tile, hspec, hspec],
        out_shape=[jax.ShapeDtypeStruct((t, w), BF16), state, state],
        scratch_shapes=scratch,
        compiler_params=_cparams(3),
        name="s5_core",
    )(u, wst_t, kt_t, wo_t, a16r, a16i, h0_layout(h0_re), h0_layout(h0_im), pin, pin.T)

    def fin(f):
        return f.reshape(G_A, bsz, 2, P_A).transpose(1, 2, 0, 3)
    return y, fin(f_re), fin(f_im)


def _diff_lambda(lq1, lk1, lq2, lk2, lam_init):
    return (jnp.exp(jnp.sum(lq1[...] * lk1[...], axis=-1, keepdims=True))
            - jnp.exp(jnp.sum(lq2[...] * lk2[...], axis=-1, keepdims=True)) + lam_init)


def _attn_kernel(*refs, lam_init, cached):
    if cached:
        q_ref, kc_ref, kn_ref, vc_ref, vn_ref, szb_ref, lq1, lk1, lq2, lk2, g_ref, o_ref = refs
    else:
        q_ref, kn_ref, vn_ref, szb_ref, lq1, lk1, lq2, lk2, g_ref, o_ref = refs
    lam = _diff_lambda(lq1, lk1, lq2, lk2, lam_init)
    lq = q_ref.shape[0]
    hw = 2 * DH
    low = lax.broadcasted_iota(jnp.int32, (1, hw), 1) < DH
    zero = jnp.zeros((), BF16)
    ones = jnp.ones((kn_ref.shape[0], hw), BF16)
    for h in range(H_B):
        cols = slice(h * hw, (h + 1) * hw)
        qh = q_ref[:, cols]
        qs = jnp.concatenate([jnp.where(low, qh, zero), jnp.where(low, zero, qh)], axis=0)
        s = _dot_nt(qs, kn_ref[:, cols])
        if cached:
            s = jnp.concatenate([_dot_nt(qs, kc_ref[:, cols]), s], axis=1)
        e = jnp.exp2(s - jnp.max(s, axis=-1, keepdims=True)).astype(BF16)
        vn = jnp.concatenate([vn_ref[:, cols], ones], axis=1)
        if cached:
            lc = kc_ref.shape[0]
            vc = jnp.concatenate([vc_ref[:, cols], ones[:lc]], axis=1)
            oa = _dot(e[:, :lc], vc) + _dot(e[:, lc:], vn)
        else:
            oa = _dot(e, vn)
        on = oa[:, :hw] * (1.0 / oa[:, hw:])
        o = on[:lq] - lam * on[lq:]
        o = o * lax.rsqrt(jnp.mean(o * o, axis=-1, keepdims=True) + LN_EPS)
        o = o * g_ref[...] * (1.0 - lam_init)
        o_ref[:, cols] = (o * szb_ref[:, cols].astype(F32)).astype(BF16)


def _lam_specs(n_axes):
    zero = lambda *idx: (0, 0)
    return [pl.BlockSpec((1, DH), zero)] * 4 + [pl.BlockSpec((1, 2 * DH), zero)]


def _attention_prompt(q, k, v, szb, lam_vecs, subln, lam_init, bsz, seq_len):
    blk = pl.BlockSpec((seq_len, D_MODEL), lambda b: (b, 0))
    return pl.pallas_call(
        functools.partial(_attn_kernel, lam_init=lam_init, cached=False),
        grid=(bsz,),
        in_specs=[blk] * 4 + _lam_specs(1),
        out_specs=blk,
        out_shape=jax.ShapeDtypeStruct(q.shape, BF16),
        compiler_params=_cparams(1),
        name="diff_attention_prompt",
    )(q, k, v, szb, *lam_vecs, subln)


def _attention_sample(q, kc, kn, vc, vn, szb, lam_vecs, subln, lam_init, bsz, seq_len, tq):
    nq = seq_len // tq
    past = kc.shape[1]
    qblk = pl.BlockSpec((tq, D_MODEL), lambda b, i: (b * nq + i, 0))
    cblk = pl.BlockSpec((None, past, D_MODEL), lambda b, i: (b, 0, 0))
    nblk = pl.BlockSpec((seq_len, D_MODEL), lambda b, i: (b, 0))
    return pl.pallas_call(
        functools.partial(_attn_kernel, lam_init=lam_init, cached=True),
        grid=(bsz, nq),
        in_specs=[qblk, cblk, nblk, cblk, nblk, qblk] + _lam_specs(2),
        out_specs=qblk,
        out_shape=jax.ShapeDtypeStruct(q.shape, BF16),
        compiler_params=_cparams(2),
        name="diff_attention_sample",
    )(q, kc, kn, vc, vn, szb, *lam_vecs, subln)


def _even_out_kernel(ys_ref, sza_ref, yb_ref, x_ref, gate_ref, wglu_ref, bglu_ref, wout_ref, g_ref, b_ref,
                     perm_ref, o_ref, *, alpha):
    wa = ys_ref.shape[1]
    for r in range(x_ref.shape[0] // PERM_TILE):
        rows = slice(r * PERM_TILE, (r + 1) * PERM_TILE)
        ga = jax.nn.gelu(_dot(perm_ref[...], ys_ref[rows, :]))
        glu = jax.nn.sigmoid(_dot(ga.astype(BF16), wglu_ref[...]) + bglu_ref[...])
        ya = (ga * glu * sza_ref[rows, :].astype(F32)).astype(BF16)
        out = _dot(ya, wout_ref[:wa, :]) + _dot(yb_ref[rows, :], wout_ref[wa:, :])
        o_ref[rows, :] = _post_norm(x_ref[rows, :], gate_ref[...], out, g_ref[...], b_ref[...], alpha)


def _even_out_proj(ys, sza, yb, x2, mod3, w_glu_bf, b_glu, w_out_bf, ln_g, ln_b, layer, cond_of, tm, alpha):
    t, d = x2.shape
    assert tm % PERM_TILE == 0
    row = pl.BlockSpec((tm, d), lambda i: (i, 0))
    full = _resident
    return pl.pallas_call(
        functools.partial(_even_out_kernel, alpha=alpha),
        grid=(t // tm,),
        in_specs=[row, row, row, row, _mod_spec(layer, 2, cond_of), full(w_glu_bf.shape), full((1, d)),
                  full(w_out_bf.shape), full((1, d)), full((1, d)), full((PERM_TILE, PERM_TILE))],
        out_specs=row,
        out_shape=jax.ShapeDtypeStruct((t, d), F32),
        compiler_params=_cparams(1),
        name="even_out_proj",
    )(ys, sza, yb, x2, mod3, w_glu_bf, b_glu.reshape(1, d), w_out_bf, ln_g.reshape(1, d), ln_b.reshape(1, d),
      _chunk_transpose_matrix())


def _dft_mats(n, scale):
    k = np.arange(n, dtype=np.int64)
    ang = ((k[:, None] * k[None, :]) % n).astype(np.float64) * (2.0 * math.pi / n)
    return (jnp.asarray((np.cos(ang) * scale).astype(np.float32)).astype(BF16),
            jnp.asarray((np.sin(ang) * scale).astype(np.float32)).astype(BF16))


def _odd_in_kernel(x_ref, shift_ref, scale_ref, w_ref, cc_ref, sc_ref, uc_ref, us_ref, sz_ref):
    for r in range(x_ref.shape[0] // SUB_ROWS):
        rows = slice(r * SUB_ROWS, (r + 1) * SUB_ROWS)
        h = (_ln_rows(x_ref[rows, :]) * (1.0 + scale_ref[...]) + shift_ref[...]).astype(BF16)
        u = _dot(h, w_ref[:, :W_C]).astype(BF16)
        sz_ref[rows, :] = _silu(_dot(h, w_ref[:, W_C:])).astype(BF16)
        for g in range(NG_C):
            cols = slice(g * GC_C, (g + 1) * GC_C)
            uc_ref[rows, cols] = _dot(u[:, cols], cc_ref[...]).astype(BF16)
            us_ref[rows, cols] = _dot(u[:, cols], sc_ref[...]).astype(BF16)


def _odd_in_proj(x2, mod3, w_in_bf, cc, sc, layer, cond_of, tm):
    t, d = x2.shape
    assert tm % SUB_ROWS == 0
    row = pl.BlockSpec((tm, d), lambda i: (i, 0))
    wide = pl.BlockSpec((tm, W_C), lambda i: (i, 0))
    full = _resident
    wide_bf = jax.ShapeDtypeStruct((t, W_C), BF16)
    return pl.pallas_call(
        _odd_in_kernel,
        grid=(t // tm,),
        in_specs=[row, _mod_spec(layer, 0, cond_of), _mod_spec(layer, 1, cond_of), full(w_in_bf.shape),
                  full(cc.shape), full(sc.shape)],
        out_specs=[wide] * 3,
        out_shape=[wide_bf] * 3,
        compiler_params=_cparams(1),
        name="odd_in_proj",
    )(x2, mod3, mod3, w_in_bf, cc, sc)


def _odd_out_kernel(cl_ref, sl_ref, uc_ref, us_ref, sz_ref, x_ref, gate_ref, wf_ref, bf_ref, wo_ref, g_ref, b_ref,
                    o_ref, *, alpha, whole_seqs):
    seq_len = cl_ref.shape[1]
    sub = seq_len if whole_seqs else SUB_ROWS
    for r in range(x_ref.shape[0] // sub):
        rows = slice(r * sub, (r + 1) * sub)
        if whole_seqs:
            mixed = _dot(cl_ref[...], uc_ref[rows, :]) - _dot(sl_ref[...], us_ref[rows, :])
        else:
            mixed = _dot(cl_ref[rows, :], uc_ref[...]) - _dot(sl_ref[rows, :], us_ref[...])
        y = ((_dot(mixed.astype(BF16), wf_ref[...]) + bf_ref[...]) * sz_ref[rows, :].astype(F32)).astype(BF16)
        o_ref[rows, :] = _post_norm(x_ref[rows, :], gate_ref[...], _dot(y, wo_ref[...]), g_ref[...], b_ref[...],
                                    alpha)


def _odd_out_proj(uc, us, sz, x2, mod3, cl, sl, w_fno_bf, b_fno, w_out_bf, ln_g, ln_b, layer, cond_of,
                  bsz, seq_len, tl, alpha):
    t, d = x2.shape
    whole_seqs = tl >= seq_len
    full = _resident
    if whole_seqs:
        assert tl % seq_len == 0
        grid = (t // tl, 1)
        dft = _resident((seq_len, seq_len))
        seq = pl.BlockSpec((tl, W_C), lambda b, i: (b, 0))
        wide = seq
        row = pl.BlockSpec((tl, d), lambda b, i: (b, 0))
    else:
        assert seq_len % tl == 0 and tl % SUB_ROWS == 0
        nl = seq_len // tl
        grid = (bsz, nl)
        dft = pl.BlockSpec((tl, seq_len), lambda b, i: (i, 0))
        seq = pl.BlockSpec((seq_len, W_C), lambda b, i: (b, 0))
        wide = pl.BlockSpec((tl, W_C), lambda b, i: (b * nl + i, 0))
        row = pl.BlockSpec((tl, d), lambda b, i: (b * nl + i, 0))
    return pl.pallas_call(
        functools.partial(_odd_out_kernel, alpha=alpha, whole_seqs=whole_seqs),
        grid=grid,
        in_specs=[dft, dft, seq, seq, wide, row, _mod_spec(layer, 2, cond_of), full(w_fno_bf.shape),
                  full((1, W_C)), full(w_out_bf.shape), full((1, d)), full((1, d))],
        out_specs=row,
        out_shape=jax.ShapeDtypeStruct((t, d), F32),
        compiler_params=_cparams(2),
        name="odd_out_proj",
    )(cl, sl, uc, us, sz, x2, mod3, w_fno_bf, b_fno.reshape(1, W_C), w_out_bf, ln_g.reshape(1, d),
      ln_b.reshape(1, d))


def _rope_tables(seq_len):
    rows = seq_len // GRID_W
    row = jnp.repeat(jnp.arange(rows), GRID_W).astype(F32)
    col = jnp.tile(jnp.arange(GRID_W), rows).astype(F32)
    freqs = ROPE_BASE ** (-jnp.arange(ROT_FREQS, dtype=F32) / ROT_FREQS)
    ang = jnp.concatenate([row[:, None] * freqs, col[:, None] * freqs], axis=-1)
    cos, sin = jnp.cos(ang), jnp.sin(ang)
    cos128 = jnp.tile(cos, (1, 128 // ROT_HALF))
    sin128 = jnp.tile(jnp.concatenate([-sin, sin], axis=-1), (1, 128 // DH))
    return cos128, sin128


def kernel(x_prompt, x_sample, cache_k, cache_v, state_ssm_re, state_ssm_im, c, c_ctx, w_mod, b_mod, ln_g, ln_b, w_in_e, ssm_lam_re, ssm_lam_im, ssm_log_dt, ssm_b_re, ssm_b_im, ssm_c_re, ssm_c_im, ssm_d, w_glu, b_glu, lam_q1, lam_k1, lam_q2, lam_k2, subln_g, w_out_e, w_in_o, w_fno, b_fno, w_out_o):
    depth = w_mod.shape[0]
    bp_, lp, d = x_prompt.shape
    bs_, ls, _ = x_sample.shape
    past = cache_k.shape[2]
    alpha = (2 * depth) ** 0.25
    assert bs_ + 1 <= MOD_ROWS and d == D_MODEL

    cond8 = jnp.concatenate([c_ctx[None, :], c, jnp.zeros((MOD_ROWS - 1 - bs_, d), F32)], axis=0).astype(F32)
    mod3 = _modulation(cond8, w_mod, b_mod).reshape(depth * MOD_ROWS * 3, 1, d)

    tm = ROW_TILE
    cond_p = lambda *idx: 0
    cond_s_row = lambda i: 1 + i // (ls // tm)
    cond_s_grid = lambda b, i: 1 + b
    rope_tabs = _rope_tables(ls)
    xp = x_prompt.reshape(bp_ * lp, d)
    xs = x_sample.reshape(bs_ * ls, d)
    new_k, new_v, new_sr, new_si = [], [], [], []
    zeros_h0 = jnp.zeros((bp_, 2, G_A, P_A), F32)

    for layer in range(depth):
        if layer % 2 == 0:
            e = layer // 2
            lam_init = 0.8 - 0.6 * math.exp(-0.3 * layer)
            w_in_bf = w_in_e[e].astype(BF16)
            w_glu_bf = w_glu[e].astype(BF16)
            w_out_bf = w_out_e[e].astype(BF16)
            ops = _s5_operators(ssm_lam_re[e], ssm_lam_im[e], ssm_log_dt[e], ssm_b_re[e], ssm_b_im[e],
                                ssm_c_re[e], ssm_c_im[e], ssm_d[e])
            lam_vecs = [v[e].reshape(1, DH).astype(F32) for v in (lam_q1, lam_k1, lam_q2, lam_k2)]
            subln = subln_g[e].reshape(1, 2 * DH).astype(F32)

            u, sza, q, kb, vb, szb, kf, vf = _even_in_proj(xp, mod3, w_in_bf, layer, cond_p, tm, None, lp)
            new_k.append(kf.reshape(bp_, H_B, 2, DH, lp).transpose(0, 4, 1, 2, 3))
            new_v.append(vf.reshape(bp_, lp, H_B, 2 * DH))
            ys, s_re, s_im = _s5_mix(u, ops, zeros_h0, zeros_h0, bp_, lp, S5_TILE_ELEMS // (lp * d), d)
            new_sr.append(s_re)
            new_si.append(s_im)
            yb = _attention_prompt(q, kb, vb, szb, lam_vecs, subln, lam_init, bp_, lp)
            xp = _even_out_proj(ys, sza, yb, xp, mod3, w_glu_bf, b_glu[e], w_out_bf, ln_g[layer], ln_b[layer],
                                layer, cond_p, tm, alpha)

            u, sza, q, kb, vb, szb = _even_in_proj(xs, mod3, w_in_bf, layer, cond_s_row, tm, rope_tabs, ls)
            ys, _, _ = _s5_mix(u, ops, state_ssm_re[:, e], state_ssm_im[:, e], bs_, ls, bs_,
                               S5_TILE_ELEMS // (bs_ * ls))
            kc = cache_k[:, e].reshape(bs_, past, d).astype(BF16)
            vc = cache_v[:, e].reshape(bs_, past, d).astype(BF16)
            yb = _attention_sample(q, kc, kb, vc, vb, szb, lam_vecs, subln, lam_init, bs_, ls, ATTN_Q_TILE)
            xs = _even_out_proj(ys, sza, yb, xs, mod3, w_glu_bf, b_glu[e], w_out_bf, ln_g[layer], ln_b[layer],
                                layer, cond_s_row, tm, alpha)
        else:
            o = layer // 2
            w_in_bf = w_in_o[o].astype(BF16)
            w_fno_bf = w_fno[o].astype(BF16)
            w_out_bf = w_out_o[o].astype(BF16)
            cc, sc = _dft_mats(GC_C, GC_C ** -0.5)
            for which in ("prompt", "sample"):
                if which == "prompt":
                    x2, cond_row, cond_grid, bsz, seq = xp, cond_p, cond_p, bp_, lp
                else:
                    x2, cond_row, cond_grid, bsz, seq = xs, cond_s_row, cond_s_grid, bs_, ls
                cl, sl = _dft_mats(seq, seq ** -0.5)
                uc, us, sz = _odd_in_proj(x2, mod3, w_in_bf, cc, sc, layer, cond_row, tm)
                x2 = _odd_out_proj(uc, us, sz, x2, mod3, cl, sl, w_fno_bf, b_fno[o], w_out_bf, ln_g[layer],
                                   ln_b[layer], layer, cond_grid, bsz, seq, tm, alpha)
                if which == "prompt":
                    xp = x2
                else:
                    xs = x2

    return (xp.reshape(bp_, lp, d), xs.reshape(bs_, ls, d), jnp.stack(new_k, axis=1), jnp.stack(new_v, axis=1),
            jnp.stack(new_sr, axis=1), jnp.stack(new_si, axis=1))
```

```python
import functools
import math

import jax
import jax.numpy as jnp
import numpy as np
from jax import lax
from jax.experimental import pallas as pl
from jax.experimental.pallas import tpu as pltpu

F32 = jnp.float32
BF16 = jnp.bfloat16

D_MODEL = 1024
GRID_W = 64
SSM_GROUP = 16
G_A = D_MODEL // SSM_GROUP
P_A = 64
DH = 64
H_B = D_MODEL // (2 * DH)
ROPE_BASE = 10000.0
ROT_HALF = DH // 2
ROT_FREQS = DH // 4
NG_C = 8
GC_C = 2 * D_MODEL // NG_C
W_C = 2 * D_MODEL
LN_EPS = 1e-5
LOG2_E = 1.4426950408889634
CHUNK = 16
SSM_TILE = CHUNK * SSM_GROUP
LANES = 128
PERM_TILE = CHUNK * CHUNK
SUB_ROWS = PERM_TILE
ROW_TILE = 2 * SUB_ROWS
ATTN_Q_TILE = 256
S5_GROUPS_PER_STEP = 16
S5_INTERLEAVE = 4
S5_TILE_ELEMS = 2048 * 1024
MOD_ROWS = 8
VMEM_LIMIT = 56 * 1024 * 1024


def _cparams(n_axes):
    return pltpu.CompilerParams(dimension_semantics=("arbitrary",) * n_axes, vmem_limit_bytes=VMEM_LIMIT)


def _ln_rows(x):
    mu = jnp.mean(x, axis=-1, keepdims=True)
    xc = x - mu
    var = jnp.mean(xc * xc, axis=-1, keepdims=True)
    return xc * lax.rsqrt(var + LN_EPS)


def _silu(z):
    return z * jax.nn.sigmoid(z)


def _dot(a, b):
    return jnp.dot(a, b, preferred_element_type=F32)


def _dot_nt(a, b):
    return lax.dot_general(a, b, (((1,), (1,)), ((), ())), preferred_element_type=F32)


def _post_norm(x, gate, out, g, b, alpha):
    return _ln_rows(alpha * x + gate * out) * g + b


def _mod_kernel(c_ref, w_ref, b_ref, o_ref):
    c = _silu(c_ref[...]).astype(BF16)
    o_ref[...] = _dot(c, w_ref[...].astype(BF16)) + b_ref[...]


def _modulation(cond8, w_mod, b_mod):
    depth, d, n3 = w_mod.shape
    tn = 1024
    return pl.pallas_call(
        _mod_kernel,
        grid=(depth, n3 // tn),
        in_specs=[
            pl.BlockSpec((MOD_ROWS, d), lambda l, j: (0, 0)),
            pl.BlockSpec((None, d, tn), lambda l, j: (l, 0, j)),
            pl.BlockSpec((None, 1, tn), lambda l, j: (l, 0, j)),
        ],
        out_specs=pl.BlockSpec((None, MOD_ROWS, tn), lambda l, j: (l, 0, j)),
        out_shape=jax.ShapeDtypeStruct((depth, MOD_ROWS, n3), F32),
        compiler_params=_cparams(2),
        name="modulation",
    )(cond8, w_mod, b_mod.reshape(depth, 1, n3))


def _resident(shape):
    return pl.BlockSpec(shape, lambda *idx: (0,) * len(shape), pipeline_mode=pl.Buffered(1))


def _mod_spec(layer, part, cond_of):
    return pl.BlockSpec((None, 1, D_MODEL), lambda *idx: ((layer * MOD_ROWS + cond_of(*idx)) * 3 + part, 0, 0))


def _rope(x, cos, sin_signed, first_half):
    blocks = []
    for hh in range(x.shape[1] // 128):
        b = x[:, hh * 128:(hh + 1) * 128]
        partner = jnp.where(first_half, pltpu.roll(b, 128 - ROT_HALF, 1), pltpu.roll(b, ROT_HALF, 1))
        blocks.append(b * cos + partner * sin_signed)
    return jnp.concatenate(blocks, axis=1)


def _chunk_transpose_matrix():
    i = lax.broadcasted_iota(jnp.int32, (PERM_TILE, PERM_TILE), 0)
    j = lax.broadcasted_iota(jnp.int32, (PERM_TILE, PERM_TILE), 1)
    return (j == (i % CHUNK) * CHUNK + i // CHUNK).astype(BF16)


def _even_in_kernel(*refs, rope):
    if rope:
        (x_ref, shift_ref, scale_ref, w_ref, perm_ref, cos_ref, sin_ref,
         u_ref, sza_ref, q_ref, kb_ref, vb_ref, szb_ref) = refs
    else:
        (x_ref, shift_ref, scale_ref, w_ref, perm_ref, u_ref, sza_ref, q_ref, kb_ref, vb_ref, szb_ref,
         kf_ref, vf_ref) = refs
    w = D_MODEL
    for r in range(x_ref.shape[0] // PERM_TILE):
        rows = slice(r * PERM_TILE, (r + 1) * PERM_TILE)
        h = (_ln_rows(x_ref[rows, :]) * (1.0 + scale_ref[...]) + shift_ref[...]).astype(BF16)

        def proj(j):
            return _dot(h, w_ref[:, j * w:(j + 1) * w])

        u_ref[rows, :] = _dot(perm_ref[...], proj(0).astype(BF16)).astype(BF16)
        sza_ref[rows, :] = _silu(proj(1)).astype(BF16)
        q = proj(2) * (DH ** -0.5 * LOG2_E)
        k = proj(3)
        v = proj(4)
        if rope:
            lane = lax.broadcasted_iota(jnp.int32, (1, 128), 1)
            first_half = (lane % DH) < ROT_HALF
            q = _rope(q, cos_ref[rows, :], sin_ref[rows, :], first_half)
            kb_ref[rows, :] = _rope(k, cos_ref[rows, :], sin_ref[rows, :], first_half).astype(BF16)
        else:
            kf_ref[r] = k.T
            vf_ref[rows, :] = v
            kb_ref[rows, :] = k.astype(BF16)
        q_ref[rows, :] = q.astype(BF16)
        vb_ref[rows, :] = v.astype(BF16)
        szb_ref[rows, :] = _silu(proj(5)).astype(BF16)


def _even_in_proj(x2, mod3, w_in_bf, layer, cond_of, tm, rope_tabs, seq_len):
    t, d = x2.shape
    n = w_in_bf.shape[1]
    assert tm % PERM_TILE == 0
    row = pl.BlockSpec((tm, d), lambda i: (i, 0))
    in_specs = [row, _mod_spec(layer, 0, cond_of), _mod_spec(layer, 1, cond_of),
                _resident((d, n)), _resident((PERM_TILE, PERM_TILE))]
    args = [x2, mod3, mod3, w_in_bf, _chunk_transpose_matrix()]
    bf = jax.ShapeDtypeStruct((t, d), BF16)
    out_shape = [bf] * 6
    out_specs = [row] * 6
    if rope_tabs is not None:
        tiles_per_seq = seq_len // tm
        tab = pl.BlockSpec((tm, 128), lambda i: (i % tiles_per_seq, 0))
        in_specs += [tab, tab]
        args += list(rope_tabs)
    else:
        assert seq_len == PERM_TILE
        out_shape += [jax.ShapeDtypeStruct((t // seq_len, d, seq_len), F32), jax.ShapeDtypeStruct((t, d), F32)]
        out_specs += [pl.BlockSpec((tm // seq_len, d, seq_len), lambda i: (i, 0, 0)), row]
    return pl.pallas_call(
        functools.partial(_even_in_kernel, rope=rope_tabs is not None),
        grid=(t // tm,),
        in_specs=in_specs,
        out_specs=out_specs,
        out_shape=out_shape,
        compiler_params=_cparams(1),
        name="even_in_proj",
    )(*args)


BUILD_GROUPS = 4


def _s5_build_kernel(qr_ref, qi_ref, br_ref, bi_ref, cr_ref, ci_ref, ccr_ref, cci_ref, ecr_ref, eci_ref, d_ref,
                     rep_ref, til_ref, wst_ref, kt_ref, wo_ref):
    def hdot(a, b):
        return jnp.dot(a, b, preferred_element_type=F32, precision=lax.Precision.HIGHEST)

    lane = lax.broadcasted_iota(jnp.int32, (SSM_GROUP, SSM_TILE), 1)
    sub = lax.broadcasted_iota(jnp.int32, (SSM_GROUP, SSM_TILE), 0)
    zeros = jnp.zeros((SSM_GROUP, SSM_TILE), F32)
    rep, til = rep_ref[...], til_ref[...]
    for g in range(BUILD_GROUPS):
        qer = [hdot(qr_ref[d, g], rep) for d in range(2)]
        qei = [hdot(qi_ref[d, g], rep) for d in range(2)]
        btr = [hdot(br_ref[d, g], til) for d in range(2)]
        bti = [hdot(bi_ref[d, g], til) for d in range(2)]
        st_re = [qer[d] * btr[d] - qei[d] * bti[d] for d in range(2)]
        st_im = [qer[d] * bti[d] + qei[d] * btr[d] for d in range(2)]
        wst_ref[g] = jnp.concatenate([st_re[0], st_re[1], st_im[0], st_im[1]], axis=0).astype(BF16)
        klag = [hdot(cr_ref[d, g], st_re[d]) - hdot(ci_ref[d, g], st_im[d]) for d in range(2)]
        fwd_p = jnp.concatenate([klag[0], zeros], axis=1)
        bwd_p = jnp.concatenate([zeros, klag[1]], axis=1)
        ccr, cci = ccr_ref[g], cci_ref[g]
        for t in range(CHUNK):
            lo_f = SSM_GROUP * (CHUNK - 1 - t)
            lo_b = SSM_TILE - SSM_GROUP * t
            skip = jnp.where(lane == t * SSM_GROUP + sub, d_ref[g], 0.0)
            rows = slice(t * SSM_GROUP, (t + 1) * SSM_GROUP)
            kt_ref[g, rows, :] = (fwd_p[:, lo_f:lo_f + SSM_TILE] + bwd_p[:, lo_b:lo_b + SSM_TILE] + skip).astype(BF16)
            er, ei = ecr_ref[g, t:t + 1, :], eci_ref[g, t:t + 1, :]
            wo_ref[g, rows, :] = jnp.concatenate([ccr * er - cci * ei, -(ccr * ei + cci * er)], axis=1).astype(BF16)


def _s5_operators(lam_re, lam_im, log_dt, b_re, b_im, c_re, c_im, d_skip):
    lr, li = lam_re.astype(F32), lam_im.astype(F32)
    dt = jnp.exp(log_dt.astype(F32))[..., None]
    mag = jnp.exp(lr * dt)
    ar = mag * jnp.cos(li * dt)
    ai = mag * jnp.sin(li * dt)
    den = lr * lr + li * li
    fr = ((ar - 1.0) * lr + ai * li) / den
    fi = (ai * lr - (ar - 1.0) * li) / den
    br, bi = b_re.astype(F32), b_im.astype(F32)
    bbr = fr[..., None] * br - fi[..., None] * bi
    bbi = fr[..., None] * bi + fi[..., None] * br
    cr, ci = c_re.astype(F32), c_im.astype(F32)
    e_pow = jnp.arange(CHUNK + 1, dtype=F32)[:, None, None, None]
    pmag = jnp.exp(e_pow * (lr * dt))
    pr = pmag * jnp.cos(e_pow * (li * dt))
    pi = pmag * jnp.sin(e_pow * (li * dt))

    def by_position(p):
        return jnp.stack([p[:CHUNK, 0][::-1], p[:CHUNK, 1]], axis=0).transpose(0, 2, 1, 3)
    qr, qi = by_position(pr), by_position(pi)

    lane_sn = np.arange(SSM_TILE)
    rep = jnp.asarray((lane_sn[None, :] // SSM_GROUP == np.arange(CHUNK)[:, None]).astype(np.float32))
    til = jnp.asarray((lane_sn[None, :] % SSM_GROUP == np.arange(SSM_GROUP)[:, None]).astype(np.float32))

    def by_output(p):
        return jnp.concatenate([p[1:, 0], p[1:, 1][::-1]], axis=-1).transpose(1, 0, 2)
    cat_c = lambda c: jnp.concatenate([c[0], c[1]], axis=-1)

    gb = BUILD_GROUPS
    narrow = pl.BlockSpec((2, gb, P_A, SSM_GROUP), lambda i: (0, i, 0, 0))
    expand = pl.BlockSpec((SSM_GROUP, SSM_TILE), lambda i: (0, 0))
    cspec = pl.BlockSpec((2, gb, SSM_GROUP, P_A), lambda i: (0, i, 0, 0))
    half = pl.BlockSpec((gb, SSM_GROUP, 2 * P_A), lambda i: (i, 0, 0))
    dspec = pl.BlockSpec((gb, SSM_GROUP, 1), lambda i: (i, 0, 0))
    ospec = pl.BlockSpec((gb, SSM_TILE, SSM_TILE), lambda i: (i, 0, 0))
    oshape = jax.ShapeDtypeStruct((G_A, SSM_TILE, SSM_TILE), BF16)
    wst_t, kt_t, wo_t = pl.pallas_call(
        _s5_build_kernel,
        grid=(G_A // gb,),
        in_specs=[narrow] * 4 + [cspec] * 2 + [half] * 4 + [dspec, expand, expand],
        out_specs=[ospec] * 3,
        out_shape=[oshape] * 3,
        compiler_params=_cparams(1),
        name="s5_build_operators",
    )(qr.transpose(0, 1, 3, 2), qi.transpose(0, 1, 3, 2), bbr, bbi, cr, ci, cat_c(cr), cat_c(ci),
      by_output(pr), by_output(pi), d_skip.astype(F32).reshape(G_A, SSM_GROUP, 1), rep, til)

    a16r = jnp.concatenate([pr[CHUNK, 0], pr[CHUNK, 1]], axis=-1)[:, None, :]
    a16i = jnp.concatenate([pi[CHUNK, 0], pi[CHUNK, 1]], axis=-1)[:, None, :]
    return wst_t, kt_t, wo_t, a16r, a16i


def _s5_kernel(u_ref, wst_ref, kt_ref, wo_ref, a16r_ref, a16i_ref, h0r_ref, h0i_ref,
               pin_ref, pout_ref, y_ref, fr_ref, fi_ref, ut, yt, s_sc, hf_sc, hb_sc, *, nb, nc, gpb):
    gb = pl.program_id(2)
    seq = nc * CHUNK
    r = nb * nc
    sw = 2 * P_A
    n_cb = u_ref.shape[1] // LANES
    pieces = [b * seq + q * PERM_TILE for b in range(nb) for q in range(seq // PERM_TILE)]

    @pl.when(gb == 0)
    def _():
        for s in range(CHUNK):
            cols = []
            for cb in range(n_cb):
                rows = jnp.concatenate([u_ref[o + s * CHUNK:o + (s + 1) * CHUNK, cb * LANES:(cb + 1) * LANES]
                                        for o in pieces], axis=0)
                cols.append(rows.astype(F32).T.astype(BF16))
            ut[:, s * r:(s + 1) * r] = _dot(jnp.concatenate(cols, axis=0), pin_ref[...]).astype(BF16)

    fwd = (lax.broadcasted_iota(jnp.int32, (1, sw), 1)) < P_A

    def groups(it, carry):
        js = [it * S5_INTERLEAVE + i for i in range(S5_INTERLEAVE)]
        grows = [pl.multiple_of((gb * gpb + j) * SSM_GROUP, SSM_GROUP) for j in js]
        ds = [jnp.concatenate([ut[pl.ds(g, SSM_GROUP), s * r:(s + 1) * r] for s in range(CHUNK)], axis=0)
              for g in grows]
        for i, j in enumerate(js):
            s_t = _dot(wst_ref[j], ds[i])
            s_sc[i, 0] = s_t[:sw].T
            s_sc[i, 1] = s_t[sw:].T
        a_r = [a16r_ref[j] for j in js]
        a_i = [a16i_ref[j] for j in js]
        re = [h0r_ref[j] for j in js]
        im = [h0i_ref[j] for j in js]
        for k in range(nc):
            rf = pl.ds(k * nb, nb)
            rb = pl.ds((nc - 1 - k) * nb, nb)
            for i in range(S5_INTERLEAVE):
                hf_sc[i, 0, rf, :] = re[i]
                hf_sc[i, 1, rf, :] = im[i]
                hb_sc[i, 0, rb, :] = re[i]
                hb_sc[i, 1, rb, :] = im[i]
                xr = jnp.where(fwd, s_sc[i, 0, rf, :], s_sc[i, 0, rb, :])
                xi = jnp.where(fwd, s_sc[i, 1, rf, :], s_sc[i, 1, rb, :])
                re[i], im[i] = a_r[i] * re[i] - a_i[i] * im[i] + xr, a_r[i] * im[i] + a_i[i] * re[i] + xi
        for i, j in enumerate(js):
            fr_ref[j] = re[i]
            fi_ref[j] = im[i]
            hin_t = jnp.concatenate([jnp.where(fwd, hf_sc[i, 0], hb_sc[i, 0]).T,
                                     jnp.where(fwd, hf_sc[i, 1], hb_sc[i, 1]).T], axis=0)
            y_t = (_dot(kt_ref[j], ds[i]) + _dot(wo_ref[j], hin_t.astype(BF16))).astype(BF16)
            for t in range(CHUNK):
                yt[pl.ds(grows[i], SSM_GROUP), t * r:(t + 1) * r] = y_t[t * SSM_GROUP:(t + 1) * SSM_GROUP, :]
        return carry

    lax.fori_loop(0, gpb // S5_INTERLEAVE, groups, 0)

    @pl.when(gb == pl.num_programs(2) - 1)
    def _():
        for t in range(CHUNK):
            full = _dot(yt[:, t * r:(t + 1) * r], pout_ref[...])
            for cb in range(n_cb):
                rows = full[cb * LANES:(cb + 1) * LANES, :].T.astype(BF16)
                for idx, o in enumerate(pieces):
                    y_ref[o + t * CHUNK:o + (t + 1) * CHUNK, cb * LANES:(cb + 1) * LANES] = (
                        rows[idx * CHUNK:(idx + 1) * CHUNK, :])


def _s5_mix(u, ops, h0_re, h0_im, bsz, seq_len, nb, cw):
    wst_t, kt_t, wo_t, a16r, a16i = ops
    t, w = u.shape
    nc = seq_len // CHUNK
    r = nb * nc
    gpb = S5_GROUPS_PER_STEP
    n_row_tiles = bsz // nb
    n_col_tiles = w // cw
    gb_per_tile = cw // SSM_GROUP // gpb
    assert r % 128 == 0 and cw % (SSM_GROUP * gpb) == 0 and bsz % nb == 0

    def h0_layout(h0):
        return h0.astype(F32).transpose(2, 0, 1, 3).reshape(G_A, bsz, 2 * P_A)

    tile = pl.BlockSpec((nb * seq_len, cw), lambda i, c, g: (i, c))
    grp = lambda i, c, g: (c * gb_per_tile + g, 0, 0)
    wspec = pl.BlockSpec((gpb, SSM_TILE, SSM_TILE), grp)
    aspec = pl.BlockSpec((gpb, 1, 2 * P_A), grp)
    hspec = pl.BlockSpec((gpb, nb, 2 * P_A), lambda i, c, g: (c * gb_per_tile + g, i, 0))
    state = jax.ShapeDtypeStruct((G_A, bsz, 2 * P_A), F32)
    scratch = [pltpu.VMEM((cw, CHUNK * r), BF16)] * 2 + [pltpu.VMEM((S5_INTERLEAVE, 2, r, 2 * P_A), F32)] * 3
    src = lax.broadcasted_iota(jnp.int32, (r, r), 0)
    dst = lax.broadcasted_iota(jnp.int32, (r, r), 1)
    pin = (dst == (src % nc) * nb + src // nc).astype(BF16)
    pspec = pl.BlockSpec((r, r), lambda i, c, g: (0, 0))

    y, f_re, f_im = pl.pallas_call(
        functools.partial(_s5_kernel, nb=nb, nc=nc, gpb=gpb),
        grid=(n_row_tiles, n_col_tiles, gb_per_tile),
        in_specs=[tile, wspec, wspec, wspec, aspec, aspec, hspec, hspec, pspec, pspec],
        out_specs=[tile, hspec, hspec],
        out_shape=[jax.ShapeDtypeStruct((t, w), BF16), state, state],
        scratch_shapes=scratch,
        compiler_params=_cparams(3),
        name="s5_core",
    )(u, wst_t, kt_t, wo_t, a16r, a16i, h0_layout(h0_re), h0_layout(h0_im), pin, pin.T)

    def fin(f):
        return f.reshape(G_A, bsz, 2, P_A).transpose(1, 2, 0, 3)
    return y, fin(f_re), fin(f_im)


def _diff_lambda(lq1, lk1, lq2, lk2, lam_init):
    return (jnp.exp(jnp.sum(lq1[...] * lk1[...], axis=-1, keepdims=True))
            - jnp.exp(jnp.sum(lq2[...] * lk2[...], axis=-1, keepdims=True)) + lam_init)


def _attn_kernel(*refs, lam_init, cached):
    if cached:
        q_ref, kc_ref, kn_ref, vc_ref, vn_ref, szb_ref, lq1, lk1, lq2, lk2, g_ref, o_ref = refs
    else:
        q_ref, kn_ref, vn_ref, szb_ref, lq1, lk1, lq2, lk2, g_ref, o_ref = refs
    lam = _diff_lambda(lq1, lk1, lq2, lk2, lam_init)
    lq = q_ref.shape[0]
    hw = 2 * DH
    low = lax.broadcasted_iota(jnp.int32, (1, hw), 1) < DH
    zero = jnp.zeros((), BF16)
    ones = jnp.ones((kn_ref.shape[0], hw), BF16)
    for h in range(H_B):
        cols = slice(h * hw, (h + 1) * hw)
        qh = q_ref[:, cols]
        qs = jnp.concatenate([jnp.where(low, qh, zero), jnp.where(low, zero, qh)], axis=0)
        s = _dot_nt(qs, kn_ref[:, cols])
        if cached:
            s = jnp.concatenate([_dot_nt(qs, kc_ref[:, cols]), s], axis=1)
        e = jnp.exp2(s - jnp.max(s, axis=-1, keepdims=True)).astype(BF16)
        vn = jnp.concatenate([vn_ref[:, cols], ones], axis=1)
        if cached:
            lc = kc_ref.shape[0]
            vc = jnp.concatenate([vc_ref[:, cols], ones[:lc]], axis=1)
            oa = _dot(e[:, :lc], vc) + _dot(e[:, lc:], vn)
        else:
            oa = _dot(e, vn)
        on = oa[:, :hw] * (1.0 / oa[:, hw:])
        o = on[:lq] - lam * on[lq:]
        o = o * lax.rsqrt(jnp.mean(o * o, axis=-1, keepdims=True) + LN_EPS)
        o = o * g_ref[...] * (1.0 - lam_init)
        o_ref[:, cols] = (o * szb_ref[:, cols].astype(F32)).astype(BF16)


def _lam_specs(n_axes):
    zero = lambda *idx: (0, 0)
    return [pl.BlockSpec((1, DH), zero)] * 4 + [pl.BlockSpec((1, 2 * DH), zero)]


def _attention_prompt(q, k, v, szb, lam_vecs, subln, lam_init, bsz, seq_len):
    blk = pl.BlockSpec((seq_len, D_MODEL), lambda b: (b, 0))
    return pl.pallas_call(
        functools.partial(_attn_kernel, lam_init=lam_init, cached=False),
        grid=(bsz,),
        in_specs=[blk] * 4 + _lam_specs(1),
        out_specs=blk,
        out_shape=jax.ShapeDtypeStruct(q.shape, BF16),
        compiler_params=_cparams(1),
        name="diff_attention_prompt",
    )(q, k, v, szb, *lam_vecs, subln)


def _attention_sample(q, kc, kn, vc, vn, szb, lam_vecs, subln, lam_init, bsz, seq_len, tq):
    nq = seq_len // tq
    past = kc.shape[1]
    qblk = pl.BlockSpec((tq, D_MODEL), lambda b, i: (b * nq + i, 0))
    cblk = pl.BlockSpec((None, past, D_MODEL), lambda b, i: (b, 0, 0))
    nblk = pl.BlockSpec((seq_len, D_MODEL), lambda b, i: (b, 0))
    return pl.pallas_call(
        functools.partial(_attn_kernel, lam_init=lam_init, cached=True),
        grid=(bsz, nq),
        in_specs=[qblk, cblk, nblk, cblk, nblk, qblk] + _lam_specs(2),
        out_specs=qblk,
        out_shape=jax.ShapeDtypeStruct(q.shape, BF16),
        compiler_params=_cparams(2),
        name="diff_attention_sample",
    )(q, kc, kn, vc, vn, szb, *lam_vecs, subln)


def _even_out_kernel(ys_ref, sza_ref, yb_ref, x_ref, gate_ref, wglu_ref, bglu_ref, wout_ref, g_ref, b_ref,
                     perm_ref, o_ref, *, alpha):
    wa = ys_ref.shape[1]
    for r in range(x_ref.shape[0] // PERM_TILE):
        rows = slice(r * PERM_TILE, (r + 1) * PERM_TILE)
        ga = jax.nn.gelu(_dot(perm_ref[...], ys_ref[rows, :]))
        glu = jax.nn.sigmoid(_dot(ga.astype(BF16), wglu_ref[...]) + bglu_ref[...])
        ya = (ga * glu * sza_ref[rows, :].astype(F32)).astype(BF16)
        out = _dot(ya, wout_ref[:wa, :]) + _dot(yb_ref[rows, :], wout_ref[wa:, :])
        o_ref[rows, :] = _post_norm(x_ref[rows, :], gate_ref[...], out, g_ref[...], b_ref[...], alpha)


def _even_out_proj(ys, sza, yb, x2, mod3, w_glu_bf, b_glu, w_out_bf, ln_g, ln_b, layer, cond_of, tm, alpha):
    t, d = x2.shape
    assert tm % PERM_TILE == 0
    row = pl.BlockSpec((tm, d), lambda i: (i, 0))
    full = _resident
    return pl.pallas_call(
        functools.partial(_even_out_kernel, alpha=alpha),
        grid=(t // tm,),
        in_specs=[row, row, row, row, _mod_spec(layer, 2, cond_of), full(w_glu_bf.shape), full((1, d)),
                  full(w_out_bf.shape), full((1, d)), full((1, d)), full((PERM_TILE, PERM_TILE))],
        out_specs=row,
        out_shape=jax.ShapeDtypeStruct((t, d), F32),
        compiler_params=_cparams(1),
        name="even_out_proj",
    )(ys, sza, yb, x2, mod3, w_glu_bf, b_glu.reshape(1, d), w_out_bf, ln_g.reshape(1, d), ln_b.reshape(1, d),
      _chunk_transpose_matrix())


def _dft_mats(n, scale):
    k = np.arange(n, dtype=np.int64)
    ang = ((k[:, None] * k[None, :]) % n).astype(np.float64) * (2.0 * math.pi / n)
    return (jnp.asarray((np.cos(ang) * scale).astype(np.float32)).astype(BF16),
            jnp.asarray((np.sin(ang) * scale).astype(np.float32)).astype(BF16))


def _odd_in_kernel(x_ref, shift_ref, scale_ref, w_ref, cc_ref, sc_ref, uc_ref, us_ref, sz_ref):
    for r in range(x_ref.shape[0] // SUB_ROWS):
        rows = slice(r * SUB_ROWS, (r + 1) * SUB_ROWS)
        h = (_ln_rows(x_ref[rows, :]) * (1.0 + scale_ref[...]) + shift_ref[...]).astype(BF16)
        u = _dot(h, w_ref[:, :W_C]).astype(BF16)
        sz_ref[rows, :] = _silu(_dot(h, w_ref[:, W_C:])).astype(BF16)
        for g in range(NG_C):
            cols = slice(g * GC_C, (g + 1) * GC_C)
            uc_ref[rows, cols] = _dot(u[:, cols], cc_ref[...]).astype(BF16)
            us_ref[rows, cols] = _dot(u[:, cols], sc_ref[...]).astype(BF16)


def _odd_in_proj(x2, mod3, w_in_bf, cc, sc, layer, cond_of, tm):
    t, d = x2.shape
    assert tm % SUB_ROWS == 0
    row = pl.BlockSpec((tm, d), lambda i: (i, 0))
    wide = pl.BlockSpec((tm, W_C), lambda i: (i, 0))
    full = _resident
    wide_bf = jax.ShapeDtypeStruct((t, W_C), BF16)
    return pl.pallas_call(
        _odd_in_kernel,
        grid=(t // tm,),
        in_specs=[row, _mod_spec(layer, 0, cond_of), _mod_spec(layer, 1, cond_of), full(w_in_bf.shape),
                  full(cc.shape), full(sc.shape)],
        out_specs=[wide] * 3,
        out_shape=[wide_bf] * 3,
        compiler_params=_cparams(1),
        name="odd_in_proj",
    )(x2, mod3, mod3, w_in_bf, cc, sc)


def _odd_out_kernel(cl_ref, sl_ref, uc_ref, us_ref, sz_ref, x_ref, gate_ref, wf_ref, bf_ref, wo_ref, g_ref, b_ref,
                    o_ref, *, alpha, whole_seqs):
    seq_len = cl_ref.shape[1]
    sub = seq_len if whole_seqs else SUB_ROWS
    for r in range(x_ref.shape[0] // sub):
        rows = slice(r * sub, (r + 1) * sub)
        if whole_seqs:
            mixed = _dot(cl_ref[...], uc_ref[rows, :]) - _dot(sl_ref[...], us_ref[rows, :])
        else:
            mixed = _dot(cl_ref[rows, :], uc_ref[...]) - _dot(sl_ref[rows, :], us_ref[...])
        y = ((_dot(mixed.astype(BF16), wf_ref[...]) + bf_ref[...]) * sz_ref[rows, :].astype(F32)).astype(BF16)
        o_ref[rows, :] = _post_norm(x_ref[rows, :], gate_ref[...], _dot(y, wo_ref[...]), g_ref[...], b_ref[...],
                                    alpha)


def _odd_out_proj(uc, us, sz, x2, mod3, cl, sl, w_fno_bf, b_fno, w_out_bf, ln_g, ln_b, layer, cond_of,
                  bsz, seq_len, tl, alpha):
    t, d = x2.shape
    whole_seqs = tl >= seq_len
    full = _resident
    if whole_seqs:
        assert tl % seq_len == 0
        grid = (t // tl, 1)
        dft = _resident((seq_len, seq_len))
        seq = pl.BlockSpec((tl, W_C), lambda b, i: (b, 0))
        wide = seq
        row = pl.BlockSpec((tl, d), lambda b, i: (b, 0))
    else:
        assert seq_len % tl == 0 and tl % SUB_ROWS == 0
        nl = seq_len // tl
        grid = (bsz, nl)
        dft = pl.BlockSpec((tl, seq_len), lambda b, i: (i, 0))
        seq = pl.BlockSpec((seq_len, W_C), lambda b, i: (b, 0))
        wide = pl.BlockSpec((tl, W_C), lambda b, i: (b * nl + i, 0))
        row = pl.BlockSpec((tl, d), lambda b, i: (b * nl + i, 0))
    return pl.pallas_call(
        functools.partial(_odd_out_kernel, alpha=alpha, whole_seqs=whole_seqs),
        grid=grid,
        in_specs=[dft, dft, seq, seq, wide, row, _mod_spec(layer, 2, cond_of), full(w_fno_bf.shape),
                  full((1, W_C)), full(w_out_bf.shape), full((1, d)), full((1, d))],
        out_specs=row,
        out_shape=jax.ShapeDtypeStruct((t, d), F32),
        compiler_params=_cparams(2),
        name="odd_out_proj",
    )(cl, sl, uc, us, sz, x2, mod3, w_fno_bf, b_fno.reshape(1, W_C), w_out_bf, ln_g.reshape(1, d),
      ln_b.reshape(1, d))


def _rope_tables(seq_len):
    rows = seq_len // GRID_W
    row = jnp.repeat(jnp.arange(rows), GRID_W).astype(F32)
    col = jnp.tile(jnp.arange(GRID_W), rows).astype(F32)
    freqs = ROPE_BASE ** (-jnp.arange(ROT_FREQS, dtype=F32) / ROT_FREQS)
    ang = jnp.concatenate([row[:, None] * freqs, col[:, None] * freqs], axis=-1)
    cos, sin = jnp.cos(ang), jnp.sin(ang)
    cos128 = jnp.tile(cos, (1, 128 // ROT_HALF))
    sin128 = jnp.tile(jnp.concatenate([-sin, sin], axis=-1), (1, 128 // DH))
    return cos128, sin128


def kernel(x_prompt, x_sample, cache_k, cache_v, state_ssm_re, state_ssm_im, c, c_ctx, w_mod, b_mod, ln_g, ln_b, w_in_e, ssm_lam_re, ssm_lam_im, ssm_log_dt, ssm_b_re, ssm_b_im, ssm_c_re, ssm_c_im, ssm_d, w_glu, b_glu, lam_q1, lam_k1, lam_q2, lam_k2, subln_g, w_out_e, w_in_o, w_fno, b_fno, w_out_o):
    depth = w_mod.shape[0]
    bp_, lp, d = x_prompt.shape
    bs_, ls, _ = x_sample.shape
    past = cache_k.shape[2]
    alpha = (2 * depth) ** 0.25
    assert bs_ + 1 <= MOD_ROWS and d == D_MODEL

    cond8 = jnp.concatenate([c_ctx[None, :], c, jnp.zeros((MOD_ROWS - 1 - bs_, d), F32)], axis=0).astype(F32)
    mod3 = _modulation(cond8, w_mod, b_mod).reshape(depth * MOD_ROWS * 3, 1, d)

    tm = ROW_TILE
    cond_p = lambda *idx: 0
    cond_s_row = lambda i: 1 + i // (ls // tm)
    cond_s_grid = lambda b, i: 1 + b
    rope_tabs = _rope_tables(ls)
    xp = x_prompt.reshape(bp_ * lp, d)
    xs = x_sample.reshape(bs_ * ls, d)
    new_k, new_v, new_sr, new_si = [], [], [], []
    zeros_h0 = jnp.zeros((bp_, 2, G_A, P_A), F32)

    for layer in range(depth):
        if layer % 2 == 0:
            e = layer // 2
            lam_init = 0.8 - 0.6 * math.exp(-0.3 * layer)
            w_in_bf = w_in_e[e].astype(BF16)
            w_glu_bf = w_glu[e].astype(BF16)
            w_out_bf = w_out_e[e].astype(BF16)
            ops = _s5_operators(ssm_lam_re[e], ssm_lam_im[e], ssm_log_dt[e], ssm_b_re[e], ssm_b_im[e],
                                ssm_c_re[e], ssm_c_im[e], ssm_d[e])
            lam_vecs = [v[e].reshape(1, DH).astype(F32) for v in (lam_q1, lam_k1, lam_q2, lam_k2)]
            subln = subln_g[e].reshape(1, 2 * DH).astype(F32)

            u, sza, q, kb, vb, szb, kf, vf = _even_in_proj(xp, mod3, w_in_bf, layer, cond_p, tm, None, lp)
            new_k.append(kf.reshape(bp_, H_B, 2, DH, lp).transpose(0, 4, 1, 2, 3))
            new_v.append(vf.reshape(bp_, lp, H_B, 2 * DH))
            ys, s_re, s_im = _s5_mix(u, ops, zeros_h0, zeros_h0, bp_, lp, S5_TILE_ELEMS // (lp * d), d)
            new_sr.append(s_re)
            new_si.append(s_im)
            yb = _attention_prompt(q, kb, vb, szb, lam_vecs, subln, lam_init, bp_, lp)
            xp = _even_out_proj(ys, sza, yb, xp, mod3, w_glu_bf, b_glu[e], w_out_bf, ln_g[layer], ln_b[layer],
                                layer, cond_p, tm, alpha)

            u, sza, q, kb, vb, szb = _even_in_proj(xs, mod3, w_in_bf, layer, cond_s_row, tm, rope_tabs, ls)
            ys, _, _ = _s5_mix(u, ops, state_ssm_re[:, e], state_ssm_im[:, e], bs_, ls, bs_,
                               S5_TILE_ELEMS // (bs_ * ls))
            kc = cache_k[:, e].reshape(bs_, past, d).astype(BF16)
            vc = cache_v[:, e].reshape(bs_, past, d).astype(BF16)
            yb = _attention_sample(q, kc, kb, vc, vb, szb, lam_vecs, subln, lam_init, bs_, ls, ATTN_Q_TILE)
            xs = _even_out_proj(ys, sza, yb, xs, mod3, w_glu_bf, b_glu[e], w_out_bf, ln_g[layer], ln_b[layer],
                                layer, cond_s_row, tm, alpha)
        else:
            o = layer // 2
            w_in_bf = w_in_o[o].astype(BF16)
            w_fno_bf = w_fno[o].astype(BF16)
            w_out_bf = w_out_o[o].astype(BF16)
            cc, sc = _dft_mats(GC_C, GC_C ** -0.5)
            for which in ("prompt", "sample"):
                if which == "prompt":
                    x2, cond_row, cond_grid, bsz, seq = xp, cond_p, cond_p, bp_, lp
                else:
                    x2, cond_row, cond_grid, bsz, seq = xs, cond_s_row, cond_s_grid, bs_, ls
                cl, sl = _dft_mats(seq, seq ** -0.5)
                uc, us, sz = _odd_in_proj(x2, mod3, w_in_bf, cc, sc, layer, cond_row, tm)
                x2 = _odd_out_proj(uc, us, sz, x2, mod3, cl, sl, w_fno_bf, b_fno[o], w_out_bf, ln_g[layer],
                                   ln_b[layer], layer, cond_grid, bsz, seq, tm, alpha)
                if which == "prompt":
                    xp = x2
                else:
                    xs = x2

    return (xp.reshape(bp_, lp, d), xs.reshape(bs_, ls, d), jnp.stack(new_k, axis=1), jnp.stack(new_v, axis=1),
            jnp.stack(new_sr, axis=1), jnp.stack(new_si, axis=1))
```

```python
import functools
import math

import jax
import jax.numpy as jnp
import numpy as np
from jax import lax
from jax.experimental import pallas as pl
from jax.experimental.pallas import tpu as pltpu

F32 = jnp.float32
BF16 = jnp.bfloat16

D_MODEL = 1024
GRID_W = 64
SSM_GROUP = 16
G_A = D_MODEL // SSM_GROUP
P_A = 64
DH = 64
H_B = D_MODEL // (2 * DH)
ROPE_BASE = 10000.0
ROT_HALF = DH // 2
ROT_FREQS = DH // 4
NG_C = 8
GC_C = 2 * D_MODEL // NG_C
W_C = 2 * D_MODEL
LN_EPS = 1e-5
LOG2_E = 1.4426950408889634
CHUNK = 16
SSM_TILE = CHUNK * SSM_GROUP
LANES = 128
PERM_TILE = CHUNK * CHUNK
SUB_ROWS = PERM_TILE
ROW_TILE = 2 * SUB_ROWS
ATTN_Q_TILE = 256
S5_GROUPS_PER_STEP = 16
S5_INTERLEAVE = 8
S5_TILE_ELEMS = 2048 * 1024
MOD_ROWS = 8
VMEM_LIMIT = 56 * 1024 * 1024


def _cparams(n_axes):
    return pltpu.CompilerParams(dimension_semantics=("arbitrary",) * n_axes, vmem_limit_bytes=VMEM_LIMIT)


def _ln_rows(x):
    mu = jnp.mean(x, axis=-1, keepdims=True)
    xc = x - mu
    var = jnp.mean(xc * xc, axis=-1, keepdims=True)
    return xc * lax.rsqrt(var + LN_EPS)


def _silu(z):
    return z * jax.nn.sigmoid(z)


def _dot(a, b):
    return jnp.dot(a, b, preferred_element_type=F32)


def _dot_nt(a, b):
    return lax.dot_general(a, b, (((1,), (1,)), ((), ())), preferred_element_type=F32)


def _post_norm(x, gate, out, g, b, alpha):
    return _ln_rows(alpha * x + gate * out) * g + b


def _mod_kernel(c_ref, w_ref, b_ref, o_ref):
    c = _silu(c_ref[...]).astype(BF16)
    o_ref[...] = _dot(c, w_ref[...].astype(BF16)) + b_ref[...]


def _modulation(cond8, w_mod, b_mod):
    depth, d, n3 = w_mod.shape
    tn = 1024
    return pl.pallas_call(
        _mod_kernel,
        grid=(depth, n3 // tn),
        in_specs=[
            pl.BlockSpec((MOD_ROWS, d), lambda l, j: (0, 0)),
            pl.BlockSpec((None, d, tn), lambda l, j: (l, 0, j)),
            pl.BlockSpec((None, 1, tn), lambda l, j: (l, 0, j)),
        ],
        out_specs=pl.BlockSpec((None, MOD_ROWS, tn), lambda l, j: (l, 0, j)),
        out_shape=jax.ShapeDtypeStruct((depth, MOD_ROWS, n3), F32),
        compiler_params=_cparams(2),
        name="modulation",
    )(cond8, w_mod, b_mod.reshape(depth, 1, n3))


def _resident(shape):
    return pl.BlockSpec(shape, lambda *idx: (0,) * len(shape), pipeline_mode=pl.Buffered(1))


def _mod_spec(layer, part, cond_of):
    return pl.BlockSpec((None, 1, D_MODEL), lambda *idx: ((layer * MOD_ROWS + cond_of(*idx)) * 3 + part, 0, 0))


def _rope(x, cos, sin_signed, first_half):
    blocks = []
    for hh in range(x.shape[1] // 128):
        b = x[:, hh * 128:(hh + 1) * 128]
        partner = jnp.where(first_half, pltpu.roll(b, 128 - ROT_HALF, 1), pltpu.roll(b, ROT_HALF, 1))
        blocks.append(b * cos + partner * sin_signed)
    return jnp.concatenate(blocks, axis=1)


def _chunk_transpose_matrix():
    i = lax.broadcasted_iota(jnp.int32, (PERM_TILE, PERM_TILE), 0)
    j = lax.broadcasted_iota(jnp.int32, (PERM_TILE, PERM_TILE), 1)
    return (j == (i % CHUNK) * CHUNK + i // CHUNK).astype(BF16)


def _even_in_kernel(*refs, rope):
    if rope:
        (x_ref, shift_ref, scale_ref, w_ref, perm_ref, cos_ref, sin_ref,
         u_ref, sza_ref, q_ref, kb_ref, vb_ref, szb_ref) = refs
    else:
        (x_ref, shift_ref, scale_ref, w_ref, perm_ref, u_ref, sza_ref, q_ref, kb_ref, vb_ref, szb_ref,
         kf_ref, vf_ref) = refs
    w = D_MODEL
    for r in range(x_ref.shape[0] // PERM_TILE):
        rows = slice(r * PERM_TILE, (r + 1) * PERM_TILE)
        h = (_ln_rows(x_ref[rows, :]) * (1.0 + scale_ref[...]) + shift_ref[...]).astype(BF16)

        def proj(j):
            return _dot(h, w_ref[:, j * w:(j + 1) * w])

        u_ref[rows, :] = _dot(perm_ref[...], proj(0).astype(BF16)).astype(BF16)
        sza_ref[rows, :] = _silu(proj(1)).astype(BF16)
        q = proj(2) * (DH ** -0.5 * LOG2_E)
        k = proj(3)
        v = proj(4)
        if rope:
            lane = lax.broadcasted_iota(jnp.int32, (1, 128), 1)
            first_half = (lane % DH) < ROT_HALF
            q = _rope(q, cos_ref[rows, :], sin_ref[rows, :], first_half)
            kb_ref[rows, :] = _rope(k, cos_ref[rows, :], sin_ref[rows, :], first_half).astype(BF16)
        else:
            kf_ref[r] = k.T
            vf_ref[rows, :] = v
            kb_ref[rows, :] = k.astype(BF16)
        q_ref[rows, :] = q.astype(BF16)
        vb_ref[rows, :] = v.astype(BF16)
        szb_ref[rows, :] = _silu(proj(5)).astype(BF16)


def _even_in_proj(x2, mod3, w_in_bf, layer, cond_of, tm, rope_tabs, seq_len):
    t, d = x2.shape
    n = w_in_bf.shape[1]
    assert tm % PERM_TILE == 0
    row = pl.BlockSpec((tm, d), lambda i: (i, 0))
    in_specs = [row, _mod_spec(layer, 0, cond_of), _mod_spec(layer, 1, cond_of),
                _resident((d, n)), _resident((PERM_TILE, PERM_TILE))]
    args = [x2, mod3, mod3, w_in_bf, _chunk_transpose_matrix()]
    bf = jax.ShapeDtypeStruct((t, d), BF16)
    out_shape = [bf] * 6
    out_specs = [row] * 6
    if rope_tabs is not None:
        tiles_per_seq = seq_len // tm
        tab = pl.BlockSpec((tm, 128), lambda i: (i % tiles_per_seq, 0))
        in_specs += [tab, tab]
        args += list(rope_tabs)
    else:
        assert seq_len == PERM_TILE
        out_shape += [jax.ShapeDtypeStruct((t // seq_len, d, seq_len), F32), jax.ShapeDtypeStruct((t, d), F32)]
        out_specs += [pl.BlockSpec((tm // seq_len, d, seq_len), lambda i: (i, 0, 0)), row]
    return pl.pallas_call(
        functools.partial(_even_in_kernel, rope=rope_tabs is not None),
        grid=(t // tm,),
        in_specs=in_specs,
        out_specs=out_specs,
        out_shape=out_shape,
        compiler_params=_cparams(1),
        name="even_in_proj",
    )(*args)


BUILD_GROUPS = 4


def _s5_build_kernel(q_ref, b_ref, cn_ref, ccr_ref, cci_ref, ecr_ref, eci_ref, d_ref,
                     rep_ref, til_ref, wst_ref, kt_ref, wo_ref):
    def split(x):
        hi = x.astype(BF16)
        return hi, (x - hi.astype(F32)).astype(BF16)

    def hdot(a, b):
        (ah, al), (bh, bl) = split(a), split(b)
        return _dot(ah, bh) + _dot(ah, bl) + _dot(al, bh)

    lane = lax.broadcasted_iota(jnp.int32, (SSM_GROUP, SSM_TILE), 1)
    sub = lax.broadcasted_iota(jnp.int32, (SSM_GROUP, SSM_TILE), 0)
    zeros = jnp.zeros((SSM_GROUP, SSM_TILE), F32)
    rows4 = 4 * P_A
    def expand(x, sel):
        hi, lo = split(x)
        return _dot(hi, sel) + _dot(lo, sel)

    qe = expand(q_ref[...].reshape(BUILD_GROUPS * rows4, CHUNK), rep_ref[...])
    be = expand(b_ref[...].reshape(BUILD_GROUPS * rows4, SSM_GROUP), til_ref[...])
    for g in range(BUILD_GROUPS):
        part = lambda x, k: x[g * rows4 + k * P_A:g * rows4 + (k + 1) * P_A]
        st_re = [part(qe, d) * part(be, d) - part(qe, 2 + d) * part(be, 2 + d) for d in range(2)]
        st_im = [part(qe, d) * part(be, 2 + d) + part(qe, 2 + d) * part(be, d) for d in range(2)]
        wst_ref[g] = jnp.concatenate([st_re[0], st_re[1], st_im[0], st_im[1]], axis=0).astype(BF16)
        klag = [hdot(cn_ref[d, g], jnp.concatenate([st_re[d], st_im[d]], axis=0)) for d in range(2)]
        fwd_p = jnp.concatenate([klag[0], zeros], axis=1)
        bwd_p = jnp.concatenate([zeros, klag[1]], axis=1)
        ccr, cci = ccr_ref[g], cci_ref[g]
        d_lanes = jnp.broadcast_to(d_ref[g], (SSM_GROUP, SSM_TILE))
        for t in range(CHUNK):
            lo_f = SSM_GROUP * (CHUNK - 1 - t)
            lo_b = SSM_TILE - SSM_GROUP * t
            skip = jnp.where(lane == t * SSM_GROUP + sub, d_lanes, 0.0)
            rows = slice(t * SSM_GROUP, (t + 1) * SSM_GROUP)
            kt_ref[g, rows, :] = (fwd_p[:, lo_f:lo_f + SSM_TILE] + bwd_p[:, lo_b:lo_b + SSM_TILE] + skip).astype(BF16)
            er, ei = ecr_ref[g, t:t + 1, :], eci_ref[g, t:t + 1, :]
            wo_ref[g, rows, :] = jnp.concatenate([ccr * er - cci * ei, -(ccr * ei + cci * er)], axis=1).astype(BF16)


def _s5_operators(lam_re, lam_im, log_dt, b_re, b_im, c_re, c_im, d_skip):
    lr, li = lam_re.astype(F32), lam_im.astype(F32)
    dt = jnp.exp(log_dt.astype(F32))[..., None]
    mag = jnp.exp(lr * dt)
    ar = mag * jnp.cos(li * dt)
    ai = mag * jnp.sin(li * dt)
    den = lr * lr + li * li
    fr = ((ar - 1.0) * lr + ai * li) / den
    fi = (ai * lr - (ar - 1.0) * li) / den
    br, bi = b_re.astype(F32), b_im.astype(F32)
    bbr = fr[..., None] * br - fi[..., None] * bi
    bbi = fr[..., None] * bi + fi[..., None] * br
    cr, ci = c_re.astype(F32), c_im.astype(F32)
    e_pow = jnp.arange(CHUNK + 1, dtype=F32)[:, None, None, None]
    pmag = jnp.exp(e_pow * (lr * dt))
    pr = pmag * jnp.cos(e_pow * (li * dt))
    pi = pmag * jnp.sin(e_pow * (li * dt))

    def by_position(p):
        return jnp.stack([p[:CHUNK, 0][::-1], p[:CHUNK, 1]], axis=0).transpose(0, 2, 1, 3)
    qr, qi = by_position(pr), by_position(pi)

    lane_sn = np.arange(SSM_TILE)
    rep = jnp.asarray(lane_sn[None, :] // SSM_GROUP == np.arange(CHUNK)[:, None], dtype=BF16)
    til = jnp.asarray(lane_sn[None, :] % SSM_GROUP == np.arange(SSM_GROUP)[:, None], dtype=BF16)

    def by_output(p):
        return jnp.concatenate([p[1:, 0], p[1:, 1][::-1]], axis=-1).transpose(1, 0, 2)
    cat_c = lambda c: jnp.concatenate([c[0], c[1]], axis=-1)

    def stack_rows(re, im):
        return jnp.concatenate([re[0], re[1], im[0], im[1]], axis=1)
    q_all = stack_rows(qr.transpose(0, 1, 3, 2), qi.transpose(0, 1, 3, 2))
    b_all = stack_rows(bbr, bbi)
    c_neg = jnp.concatenate([cr, -ci], axis=-1)

    gb = BUILD_GROUPS
    narrow = pl.BlockSpec((gb, 4 * P_A, SSM_GROUP), lambda i: (i, 0, 0))
    expand = pl.BlockSpec((SSM_GROUP, SSM_TILE), lambda i: (0, 0))
    cspec = pl.BlockSpec((2, gb, SSM_GROUP, 2 * P_A), lambda i: (0, i, 0, 0))
    half = pl.BlockSpec((gb, SSM_GROUP, 2 * P_A), lambda i: (i, 0, 0))
    dspec = pl.BlockSpec((gb, SSM_GROUP, 1), lambda i: (i, 0, 0))
    ospec = pl.BlockSpec((gb, SSM_TILE, SSM_TILE), lambda i: (i, 0, 0))
    oshape = jax.ShapeDtypeStruct((G_A, SSM_TILE, SSM_TILE), BF16)
    wst_t, kt_t, wo_t = pl.pallas_call(
        _s5_build_kernel,
        grid=(G_A // gb,),
        in_specs=[narrow] * 2 + [cspec] + [half] * 4 + [dspec, expand, expand],
        out_specs=[ospec] * 3,
        out_shape=[oshape] * 3,
        compiler_params=_cparams(1),
        name="s5_build_operators",
    )(q_all, b_all, c_neg, cat_c(cr), cat_c(ci),
      by_output(pr), by_output(pi), d_skip.astype(F32).reshape(G_A, SSM_GROUP, 1), rep, til)

    a16r = jnp.concatenate([pr[CHUNK, 0], pr[CHUNK, 1]], axis=-1)[:, None, :]
    a16i = jnp.concatenate([pi[CHUNK, 0], pi[CHUNK, 1]], axis=-1)[:, None, :]
    return wst_t, kt_t, wo_t, a16r, a16i


def _s5_kernel(u_ref, wst_ref, kt_ref, wo_ref, a16r_ref, a16i_ref, h0r_ref, h0i_ref,
               pin_ref, pout_ref, y_ref, fr_ref, fi_ref, ut, yt, s_sc, hf_sc, hb_sc, *, nb, nc, gpb):
    gb = pl.program_id(2)
    seq = nc * CHUNK
    r = nb * nc
    sw = 2 * P_A
    n_cb = u_ref.shape[1] // LANES
    pieces = [b * seq + q * PERM_TILE for b in range(nb) for q in range(seq // PERM_TILE)]

    @pl.when(gb == 0)
    def _():
        for s in range(CHUNK):
            cols = []
            for cb in range(n_cb):
                rows = jnp.concatenate([u_ref[o + s * CHUNK:o + (s + 1) * CHUNK, cb * LANES:(cb + 1) * LANES]
                                        for o in pieces], axis=0)
                cols.append(rows.astype(F32).T.astype(BF16))
            ut[:, s * r:(s + 1) * r] = _dot(jnp.concatenate(cols, axis=0), pin_ref[...]).astype(BF16)

    fwd = (lax.broadcasted_iota(jnp.int32, (1, sw), 1)) < P_A

    def groups(it, carry):
        js = [it * S5_INTERLEAVE + i for i in range(S5_INTERLEAVE)]
        grows = [pl.multiple_of((gb * gpb + j) * SSM_GROUP, SSM_GROUP) for j in js]
        ds = [jnp.concatenate([ut[pl.ds(g, SSM_GROUP), s * r:(s + 1) * r] for s in range(CHUNK)], axis=0)
              for g in grows]
        for i, j in enumerate(js):
            s_t = _dot(wst_ref[j], ds[i])
            s_sc[i, 0] = s_t[:sw].T
            s_sc[i, 1] = s_t[sw:].T
        a_r = [a16r_ref[j] for j in js]
        a_i = [a16i_ref[j] for j in js]
        re = [h0r_ref[j] for j in js]
        im = [h0i_ref[j] for j in js]
        for k in range(nc):
            rf = pl.ds(k * nb, nb)
            rb = pl.ds((nc - 1 - k) * nb, nb)
            for i in range(S5_INTERLEAVE):
                hf_sc[i, 0, rf, :] = re[i]
                hf_sc[i, 1, rf, :] = im[i]
                hb_sc[i, 0, rb, :] = re[i]
                hb_sc[i, 1, rb, :] = im[i]
                xr = jnp.where(fwd, s_sc[i, 0, rf, :], s_sc[i, 0, rb, :])
                xi = jnp.where(fwd, s_sc[i, 1, rf, :], s_sc[i, 1, rb, :])
                re[i], im[i] = a_r[i] * re[i] - a_i[i] * im[i] + xr, a_r[i] * im[i] + a_i[i] * re[i] + xi
        for i, j in enumerate(js):
            fr_ref[j] = re[i]
            fi_ref[j] = im[i]
            hin_t = jnp.concatenate([jnp.where(fwd, hf_sc[i, 0], hb_sc[i, 0]).T,
                                     jnp.where(fwd, hf_sc[i, 1], hb_sc[i, 1]).T], axis=0)
            y_t = (_dot(kt_ref[j], ds[i]) + _dot(wo_ref[j], hin_t.astype(BF16))).astype(BF16)
            for t in range(CHUNK):
                yt[pl.ds(grows[i], SSM_GROUP), t * r:(t + 1) * r] = y_t[t * SSM_GROUP:(t + 1) * SSM_GROUP, :]
        return carry

    lax.fori_loop(0, gpb // S5_INTERLEAVE, groups, 0)

    @pl.when(gb == pl.num_programs(2) - 1)
    def _():
        for t in range(CHUNK):
            full = _dot(yt[:, t * r:(t + 1) * r], pout_ref[...])
            for cb in range(n_cb):
                rows = full[cb * LANES:(cb + 1) * LANES, :].T.astype(BF16)
                for idx, o in enumerate(pieces):
                    y_ref[o + t * CHUNK:o + (t + 1) * CHUNK, cb * LANES:(cb + 1) * LANES] = (
                        rows[idx * CHUNK:(idx + 1) * CHUNK, :])


def _s5_mix(u, ops, h0_re, h0_im, bsz, seq_len, nb, cw):
    wst_t, kt_t, wo_t, a16r, a16i = ops
    t, w = u.shape
    nc = seq_len // CHUNK
    r = nb * nc
    gpb = S5_GROUPS_PER_STEP
    n_row_tiles = bsz // nb
    n_col_tiles = w // cw
    gb_per_tile = cw // SSM_GROUP // gpb
    assert r % 128 == 0 and cw % (SSM_GROUP * gpb) == 0 and bsz % nb == 0

    def h0_layout(h0):
        return h0.astype(F32).transpose(2, 0, 1, 3).reshape(G_A, bsz, 2 * P_A)

    tile = pl.BlockSpec((nb * seq_len, cw), lambda i, c, g: (i, c))
    grp = lambda i, c, g: (c * gb_per_tile + g, 0, 0)
    wspec = pl.BlockSpec((gpb, SSM_TILE, SSM_TILE), grp)
    aspec = pl.BlockSpec((gpb, 1, 2 * P_A), grp)
    hspec = pl.BlockSpec((gpb, nb, 2 * P_A), lambda i, c, g: (c * gb_per_tile + g, i, 0))
    state = jax.ShapeDtypeStruct((G_A, bsz, 2 * P_A), F32)
    scratch = [pltpu.VMEM((cw, CHUNK * r), BF16)] * 2 + [pltpu.VMEM((S5_INTERLEAVE, 2, r, 2 * P_A), F32)] * 3
    src = lax.broadcasted_iota(jnp.int32, (r, r), 0)
    dst = lax.broadcasted_iota(jnp.int32, (r, r), 1)
    pin = (dst == (src % nc) * nb + src // nc).astype(BF16)
    pspec = pl.BlockSpec((r, r), lambda i, c, g: (0, 0))

    y, f_re, f_im = pl.pallas_call(
        functools.partial(_s5_kernel, nb=nb, nc=nc, gpb=gpb),
        grid=(n_row_tiles, n_col_tiles, gb_per_tile),
        in_specs=[tile, wspec, wspec, wspec, aspec, aspec, hspec, hspec, pspec, pspec],
        out_specs=[tile, hspec, hspec],
        out_shape=[jax.ShapeDtypeStruct((t, w), BF16), state, state],
        scratch_shapes=scratch,
        compiler_params=_cparams(3),
        name="s5_core",
    )(u, wst_t, kt_t, wo_t, a16r, a16i, h0_layout(h0_re), h0_layout(h0_im), pin, pin.T)

    def fin(f):
        return f.reshape(G_A, bsz, 2, P_A).transpose(1, 2, 0, 3)
    return y, fin(f_re), fin(f_im)


def _diff_lambda(lq1, lk1, lq2, lk2, lam_init):
    return (jnp.exp(jnp.sum(lq1[...] * lk1[...], axis=-1, keepdims=True))
            - jnp.exp(jnp.sum(lq2[...] * lk2[...], axis=-1, keepdims=True)) + lam_init)


def _attn_kernel(*refs, lam_init, cached):
    if cached:
        q_ref, kc_ref, kn_ref, vc_ref, vn_ref, szb_ref, lq1, lk1, lq2, lk2, g_ref, o_ref = refs
    else:
        q_ref, kn_ref, vn_ref, szb_ref, lq1, lk1, lq2, lk2, g_ref, o_ref = refs
    lam = _diff_lambda(lq1, lk1, lq2, lk2, lam_init)
    lq = q_ref.shape[0]
    hw = 2 * DH
    low = lax.broadcasted_iota(jnp.int32, (1, hw), 1) < DH
    zero = jnp.zeros((), BF16)
    ones = jnp.ones((kn_ref.shape[0], hw), BF16)
    for h in range(H_B):
        cols = slice(h * hw, (h + 1) * hw)
        qh = q_ref[:, cols]
        qs = jnp.concatenate([jnp.where(low, qh, zero), jnp.where(low, zero, qh)], axis=0)
        s = _dot_nt(qs, kn_ref[:, cols])
        if cached:
            s = jnp.concatenate([_dot_nt(qs, kc_ref[:, cols]), s], axis=1)
        e = jnp.exp2(s - jnp.max(s, axis=-1, keepdims=True)).astype(BF16)
        vn = jnp.concatenate([vn_ref[:, cols], ones], axis=1)
        if cached:
            lc = kc_ref.shape[0]
            vc = jnp.concatenate([vc_ref[:, cols], ones[:lc]], axis=1)
            oa = _dot(e[:, :lc], vc) + _dot(e[:, lc:], vn)
        else:
            oa = _dot(e, vn)
        on = oa[:, :hw] * (1.0 / oa[:, hw:])
        o = on[:lq] - lam * on[lq:]
        o = o * lax.rsqrt(jnp.mean(o * o, axis=-1, keepdims=True) + LN_EPS)
        o = o * g_ref[...] * (1.0 - lam_init)
        o_ref[:, cols] = (o * szb_ref[:, cols].astype(F32)).astype(BF16)


def _lam_specs(n_axes):
    zero = lambda *idx: (0, 0)
    return [pl.BlockSpec((1, DH), zero)] * 4 + [pl.BlockSpec((1, 2 * DH), zero)]


def _attention_prompt(q, k, v, szb, lam_vecs, subln, lam_init, bsz, seq_len):
    blk = pl.BlockSpec((seq_len, D_MODEL), lambda b: (b, 0))
    return pl.pallas_call(
        functools.partial(_attn_kernel, lam_init=lam_init, cached=False),
        grid=(bsz,),
        in_specs=[blk] * 4 + _lam_specs(1),
        out_specs=blk,
        out_shape=jax.ShapeDtypeStruct(q.shape, BF16),
        compiler_params=_cparams(1),
        name="diff_attention_prompt",
    )(q, k, v, szb, *lam_vecs, subln)


def _attention_sample(q, kc, kn, vc, vn, szb, lam_vecs, subln, lam_init, bsz, seq_len, tq):
    nq = seq_len // tq
    past = kc.shape[1]
    qblk = pl.BlockSpec((tq, D_MODEL), lambda b, i: (b * nq + i, 0))
    cblk = pl.BlockSpec((None, past, D_MODEL), lambda b, i: (b, 0, 0))
    nblk = pl.BlockSpec((seq_len, D_MODEL), lambda b, i: (b, 0))
    return pl.pallas_call(
        functools.partial(_attn_kernel, lam_init=lam_init, cached=True),
        grid=(bsz, nq),
        in_specs=[qblk, cblk, nblk, cblk, nblk, qblk] + _lam_specs(2),
        out_specs=qblk,
        out_shape=jax.ShapeDtypeStruct(q.shape, BF16),
        compiler_params=_cparams(2),
        name="diff_attention_sample",
    )(q, kc, kn, vc, vn, szb, *lam_vecs, subln)


def _even_out_kernel(ys_ref, sza_ref, yb_ref, x_ref, gate_ref, wglu_ref, bglu_ref, wout_ref, g_ref, b_ref,
                     perm_ref, o_ref, *, alpha):
    wa = ys_ref.shape[1]
    for r in range(x_ref.shape[0] // PERM_TILE):
        rows = slice(r * PERM_TILE, (r + 1) * PERM_TILE)
        ga = jax.nn.gelu(_dot(perm_ref[...], ys_ref[rows, :]))
        glu = jax.nn.sigmoid(_dot(ga.astype(BF16), wglu_ref[...]) + bglu_ref[...])
        ya = (ga * glu * sza_ref[rows, :].astype(F32)).astype(BF16)
        out = _dot(ya, wout_ref[:wa, :]) + _dot(yb_ref[rows, :], wout_ref[wa:, :])
        o_ref[rows, :] = _post_norm(x_ref[rows, :], gate_ref[...], out, g_ref[...], b_ref[...], alpha)


def _even_out_proj(ys, sza, yb, x2, mod3, w_glu_bf, b_glu, w_out_bf, ln_g, ln_b, layer, cond_of, tm, alpha):
    t, d = x2.shape
    assert tm % PERM_TILE == 0
    row = pl.BlockSpec((tm, d), lambda i: (i, 0))
    full = _resident
    return pl.pallas_call(
        functools.partial(_even_out_kernel, alpha=alpha),
        grid=(t // tm,),
        in_specs=[row, row, row, row, _mod_spec(layer, 2, cond_of), full(w_glu_bf.shape), full((1, d)),
                  full(w_out_bf.shape), full((1, d)), full((1, d)), full((PERM_TILE, PERM_TILE))],
        out_specs=row,
        out_shape=jax.ShapeDtypeStruct((t, d), F32),
        compiler_params=_cparams(1),
        name="even_out_proj",
    )(ys, sza, yb, x2, mod3, w_glu_bf, b_glu.reshape(1, d), w_out_bf, ln_g.reshape(1, d), ln_b.reshape(1, d),
      _chunk_transpose_matrix())


def _dft_mats(n, scale):
    k = np.arange(n, dtype=np.int64)
    ang = ((k[:, None] * k[None, :]) % n).astype(np.float64) * (2.0 * math.pi / n)
    return (jnp.asarray((np.cos(ang) * scale).astype(np.float32)).astype(BF16),
            jnp.asarray((np.sin(ang) * scale).astype(np.float32)).astype(BF16))


def _odd_in_kernel(x_ref, shift_ref, scale_ref, w_ref, cc_ref, sc_ref, uc_ref, us_ref, sz_ref):
    for r in range(x_ref.shape[0] // SUB_ROWS):
        rows = slice(r * SUB_ROWS, (r + 1) * SUB_ROWS)
        h = (_ln_rows(x_ref[rows, :]) * (1.0 + scale_ref[...]) + shift_ref[...]).astype(BF16)
        u = _dot(h, w_ref[:, :W_C]).astype(BF16)
        sz_ref[rows, :] = _silu(_dot(h, w_ref[:, W_C:])).astype(BF16)
        for g in range(NG_C):
            cols = slice(g * GC_C, (g + 1) * GC_C)
            uc_ref[rows, cols] = _dot(u[:, cols], cc_ref[...]).astype(BF16)
            us_ref[rows, cols] = _dot(u[:, cols], sc_ref[...]).astype(BF16)


def _odd_in_proj(x2, mod3, w_in_bf, cc, sc, layer, cond_of, tm):
    t, d = x2.shape
    assert tm % SUB_ROWS == 0
    row = pl.BlockSpec((tm, d), lambda i: (i, 0))
    wide = pl.BlockSpec((tm, W_C), lambda i: (i, 0))
    full = _resident
    wide_bf = jax.ShapeDtypeStruct((t, W_C), BF16)
    return pl.pallas_call(
        _odd_in_kernel,
        grid=(t // tm,),
        in_specs=[row, _mod_spec(layer, 0, cond_of), _mod_spec(layer, 1, cond_of), full(w_in_bf.shape),
                  full(cc.shape), full(sc.shape)],
        out_specs=[wide] * 3,
        out_shape=[wide_bf] * 3,
        compiler_params=_cparams(1),
        name="odd_in_proj",
    )(x2, mod3, mod3, w_in_bf, cc, sc)


def _odd_out_kernel(cl_ref, sl_ref, uc_ref, us_ref, sz_ref, x_ref, gate_ref, wf_ref, bf_ref, wo_ref, g_ref, b_ref,
                    o_ref, *, alpha, whole_seqs):
    seq_len = cl_ref.shape[1]
    sub = seq_len if whole_seqs else SUB_ROWS
    for r in range(x_ref.shape[0] // sub):
        rows = slice(r * sub, (r + 1) * sub)
        if whole_seqs:
            mixed = _dot(cl_ref[...], uc_ref[rows, :]) - _dot(sl_ref[...], us_ref[rows, :])
        else:
            mixed = _dot(cl_ref[rows, :], uc_ref[...]) - _dot(sl_ref[rows, :], us_ref[...])
        y = ((_dot(mixed.astype(BF16), wf_ref[...]) + bf_ref[...]) * sz_ref[rows, :].astype(F32)).astype(BF16)
        o_ref[rows, :] = _post_norm(x_ref[rows, :], gate_ref[...], _dot(y, wo_ref[...]), g_ref[...], b_ref[...],
                                    alpha)


def _odd_out_proj(uc, us, sz, x2, mod3, cl, sl, w_fno_bf, b_fno, w_out_bf, ln_g, ln_b, layer, cond_of,
                  bsz, seq_len, tl, alpha):
    t, d = x2.shape
    whole_seqs = tl >= seq_len
    full = _resident
    if whole_seqs:
        assert tl % seq_len == 0
        grid = (t // tl, 1)
        dft = _resident((seq_len, seq_len))
        seq = pl.BlockSpec((tl, W_C), lambda b, i: (b, 0))
        wide = seq
        row = pl.BlockSpec((tl, d), lambda b, i: (b, 0))
    else:
        assert seq_len % tl == 0 and tl % SUB_ROWS == 0
        nl = seq_len // tl
        grid = (bsz, nl)
        dft = pl.BlockSpec((tl, seq_len), lambda b, i: (i, 0))
        seq = pl.BlockSpec((seq_len, W_C), lambda b, i: (b, 0))
        wide = pl.BlockSpec((tl, W_C), lambda b, i: (b * nl + i, 0))
        row = pl.BlockSpec((tl, d), lambda b, i: (b * nl + i, 0))
    return pl.pallas_call(
        functools.partial(_odd_out_kernel, alpha=alpha, whole_seqs=whole_seqs),
        grid=grid,
        in_specs=[dft, dft, seq, seq, wide, row, _mod_spec(layer, 2, cond_of), full(w_fno_bf.shape),
                  full((1, W_C)), full(w_out_bf.shape), full((1, d)), full((1, d))],
        out_specs=row,
        out_shape=jax.ShapeDtypeStruct((t, d), F32),
        compiler_params=_cparams(2),
        name="odd_out_proj",
    )(cl, sl, uc, us, sz, x2, mod3, w_fno_bf, b_fno.reshape(1, W_C), w_out_bf, ln_g.reshape(1, d),
      ln_b.reshape(1, d))


def _rope_tables(seq_len):
    rows = seq_len // GRID_W
    row = jnp.repeat(jnp.arange(rows), GRID_W).astype(F32)
    col = jnp.tile(jnp.arange(GRID_W), rows).astype(F32)
    freqs = ROPE_BASE ** (-jnp.arange(ROT_FREQS, dtype=F32) / ROT_FREQS)
    ang = jnp.concatenate([row[:, None] * freqs, col[:, None] * freqs], axis=-1)
    cos, sin = jnp.cos(ang), jnp.sin(ang)
    cos128 = jnp.tile(cos, (1, 128 // ROT_HALF))
    sin128 = jnp.tile(jnp.concatenate([-sin, sin], axis=-1), (1, 128 // DH))
    return cos128, sin128


def kernel(x_prompt, x_sample, cache_k, cache_v, state_ssm_re, state_ssm_im, c, c_ctx, w_mod, b_mod, ln_g, ln_b, w_in_e, ssm_lam_re, ssm_lam_im, ssm_log_dt, ssm_b_re, ssm_b_im, ssm_c_re, ssm_c_im, ssm_d, w_glu, b_glu, lam_q1, lam_k1, lam_q2, lam_k2, subln_g, w_out_e, w_in_o, w_fno, b_fno, w_out_o):
    depth = w_mod.shape[0]
    bp_, lp, d = x_prompt.shape
    bs_, ls, _ = x_sample.shape
    past = cache_k.shape[2]
    alpha = (2 * depth) ** 0.25
    assert bs_ + 1 <= MOD_ROWS and d == D_MODEL

    cond8 = jnp.concatenate([c_ctx[None, :], c, jnp.zeros((MOD_ROWS - 1 - bs_, d), F32)], axis=0).astype(F32)
    mod3 = _modulation(cond8, w_mod, b_mod).reshape(depth * MOD_ROWS * 3, 1, d)

    tm = ROW_TILE
    cond_p = lambda *idx: 0
    cond_s_row = lambda i: 1 + i // (ls // tm)
    cond_s_grid = lambda b, i: 1 + b
    rope_tabs = _rope_tables(ls)
    xp = x_prompt.reshape(bp_ * lp, d)
    xs = x_sample.reshape(bs_ * ls, d)
    new_k, new_v, new_sr, new_si = [], [], [], []
    zeros_h0 = jnp.zeros((bp_, 2, G_A, P_A), F32)

    for layer in range(depth):
        if layer % 2 == 0:
            e = layer // 2
            lam_init = 0.8 - 0.6 * math.exp(-0.3 * layer)
            w_in_bf = w_in_e[e].astype(BF16)
            w_glu_bf = w_glu[e].astype(BF16)
            w_out_bf = w_out_e[e].astype(BF16)
            ops = _s5_operators(ssm_lam_re[e], ssm_lam_im[e], ssm_log_dt[e], ssm_b_re[e], ssm_b_im[e],
                                ssm_c_re[e], ssm_c_im[e], ssm_d[e])
            lam_vecs = [v[e].reshape(1, DH).astype(F32) for v in (lam_q1, lam_k1, lam_q2, lam_k2)]
            subln = subln_g[e].reshape(1, 2 * DH).astype(F32)

            u, sza, q, kb, vb, szb, kf, vf = _even_in_proj(xp, mod3, w_in_bf, layer, cond_p, tm, None, lp)
            new_k.append(kf.reshape(bp_, H_B, 2, DH, lp).transpose(0, 4, 1, 2, 3))
            new_v.append(vf.reshape(bp_, lp, H_B, 2 * DH))
            ys, s_re, s_im = _s5_mix(u, ops, zeros_h0, zeros_h0, bp_, lp, 2 * S5_TILE_ELEMS // (lp * d), d // 2)
            new_sr.append(s_re)
            new_si.append(s_im)
            yb = _attention_prompt(q, kb, vb, szb, lam_vecs, subln, lam_init, bp_, lp)
            xp = _even_out_proj(ys, sza, yb, xp, mod3, w_glu_bf, b_glu[e], w_out_bf, ln_g[layer], ln_b[layer],
                                layer, cond_p, tm, alpha)

            u, sza, q, kb, vb, szb = _even_in_proj(xs, mod3, w_in_bf, layer, cond_s_row, tm, rope_tabs, ls)
            ys, _, _ = _s5_mix(u, ops, state_ssm_re[:, e], state_ssm_im[:, e], bs_, ls, bs_,
                               S5_TILE_ELEMS // (bs_ * ls))
            kc = cache_k[:, e].reshape(bs_, past, d).astype(BF16)
            vc = cache_v[:, e].reshape(bs_, past, d).astype(BF16)
            yb = _attention_sample(q, kc, kb, vc, vb, szb, lam_vecs, subln, lam_init, bs_, ls, ATTN_Q_TILE)
            xs = _even_out_proj(ys, sza, yb, xs, mod3, w_glu_bf, b_glu[e], w_out_bf, ln_g[layer], ln_b[layer],
                                layer, cond_s_row, tm, alpha)
        else:
            o = layer // 2
            w_in_bf = w_in_o[o].astype(BF16)
            w_fno_bf = w_fno[o].astype(BF16)
            w_out_bf = w_out_o[o].astype(BF16)
            cc, sc = _dft_mats(GC_C, GC_C ** -0.5)
            for which in ("prompt", "sample"):
                if which == "prompt":
                    x2, cond_row, cond_grid, bsz, seq = xp, cond_p, cond_p, bp_, lp
                else:
                    x2, cond_row, cond_grid, bsz, seq = xs, cond_s_row, cond_s_grid, bs_, ls
                cl, sl = _dft_mats(seq, seq ** -0.5)
                uc, us, sz = _odd_in_proj(x2, mod3, w_in_bf, cc, sc, layer, cond_row, tm)
                x2 = _odd_out_proj(uc, us, sz, x2, mod3, cl, sl, w_fno_bf, b_fno[o], w_out_bf, ln_g[layer],
                                   ln_b[layer], layer, cond_grid, bsz, seq, tm, alpha)
                if which == "prompt":
                    xp = x2
                else:
                    xs = x2

    return (xp.reshape(bp_, lp, d), xs.reshape(bs_, ls, d), jnp.stack(new_k, axis=1), jnp.stack(new_v, axis=1),
            jnp.stack(new_sr, axis=1), jnp.stack(new_si, axis=1))
```

```python
import functools
import math

import jax
import jax.numpy as jnp
import numpy as np
from jax import lax
from jax.experimental import pallas as pl
from jax.experimental.pallas import tpu as pltpu

F32 = jnp.float32
BF16 = jnp.bfloat16

D_MODEL = 1024
GRID_W = 64
SSM_GROUP = 16
G_A = D_MODEL // SSM_GROUP
P_A = 64
DH = 64
H_B = D_MODEL // (2 * DH)
ROPE_BASE = 10000.0
ROT_HALF = DH // 2
ROT_FREQS = DH // 4
NG_C = 8
GC_C = 2 * D_MODEL // NG_C
W_C = 2 * D_MODEL
LN_EPS = 1e-5
LOG2_E = 1.4426950408889634
CHUNK = 16
SSM_TILE = CHUNK * SSM_GROUP
LANES = 128
PERM_TILE = CHUNK * CHUNK
SUB_ROWS = PERM_TILE
ROW_TILE = 2 * SUB_ROWS
WIDE_ROW_TILE = 4 * SUB_ROWS
ATTN_Q_TILE = 256
S5_GROUPS_PER_STEP = 16
S5_INTERLEAVE = 8
S5_TILE_ELEMS = 2048 * 1024
MOD_ROWS = 8
VMEM_LIMIT = 56 * 1024 * 1024


def _cparams(n_axes):
    return pltpu.CompilerParams(dimension_semantics=("arbitrary",) * n_axes, vmem_limit_bytes=VMEM_LIMIT)


def _ln_rows(x):
    mu = jnp.mean(x, axis=-1, keepdims=True)
    xc = x - mu
    var = jnp.mean(xc * xc, axis=-1, keepdims=True)
    return xc * lax.rsqrt(var + LN_EPS)


def _silu(z):
    return z * jax.nn.sigmoid(z)


def _dot(a, b):
    return jnp.dot(a, b, preferred_element_type=F32)


def _dot_nt(a, b):
    return lax.dot_general(a, b, (((1,), (1,)), ((), ())), preferred_element_type=F32)


def _post_norm(x, gate, out, g, b, alpha):
    return _ln_rows(alpha * x + gate * out) * g + b


def _mod_kernel(c_ref, w_ref, b_ref, o_ref):
    c = _silu(c_ref[...]).astype(BF16)
    o_ref[...] = _dot(c, w_ref[...].astype(BF16)) + b_ref[...]


def _modulation(cond8, w_mod, b_mod):
    depth, d, n3 = w_mod.shape
    tn = 1024
    return pl.pallas_call(
        _mod_kernel,
        grid=(depth, n3 // tn),
        in_specs=[
            pl.BlockSpec((MOD_ROWS, d), lambda l, j: (0, 0)),
            pl.BlockSpec((None, d, tn), lambda l, j: (l, 0, j)),
            pl.BlockSpec((None, 1, tn), lambda l, j: (l, 0, j)),
        ],
        out_specs=pl.BlockSpec((None, MOD_ROWS, tn), lambda l, j: (l, 0, j)),
        out_shape=jax.ShapeDtypeStruct((depth, MOD_ROWS, n3), F32),
        compiler_params=_cparams(2),
        name="modulation",
    )(cond8, w_mod, b_mod.reshape(depth, 1, n3))


def _resident(shape):
    return pl.BlockSpec(shape, lambda *idx: (0,) * len(shape), pipeline_mode=pl.Buffered(1))


def _mod_spec(layer, part, cond_of):
    return pl.BlockSpec((None, 1, D_MODEL), lambda *idx: ((layer * MOD_ROWS + cond_of(*idx)) * 3 + part, 0, 0))


def _rope(x, cos, sin_signed, first_half):
    blocks = []
    for hh in range(x.shape[1] // 128):
        b = x[:, hh * 128:(hh + 1) * 128]
        partner = jnp.where(first_half, pltpu.roll(b, 128 - ROT_HALF, 1), pltpu.roll(b, ROT_HALF, 1))
        blocks.append(b * cos + partner * sin_signed)
    return jnp.concatenate(blocks, axis=1)


def _chunk_transpose_matrix():
    i = lax.broadcasted_iota(jnp.int32, (PERM_TILE, PERM_TILE), 0)
    j = lax.broadcasted_iota(jnp.int32, (PERM_TILE, PERM_TILE), 1)
    return (j == (i % CHUNK) * CHUNK + i // CHUNK).astype(BF16)


def _even_in_kernel(*refs, rope):
    if rope:
        (x_ref, shift_ref, scale_ref, w_ref, perm_ref, cos_ref, sin_ref,
         u_ref, sza_ref, q_ref, kb_ref, vb_ref, szb_ref) = refs
    else:
        (x_ref, shift_ref, scale_ref, w_ref, perm_ref, u_ref, sza_ref, q_ref, kb_ref, vb_ref, szb_ref,
         kf_ref, vf_ref) = refs
    w = D_MODEL
    for r in range(x_ref.shape[0] // PERM_TILE):
        rows = slice(r * PERM_TILE, (r + 1) * PERM_TILE)
        h = (_ln_rows(x_ref[rows, :]) * (1.0 + scale_ref[...]) + shift_ref[...]).astype(BF16)

        def proj(j):
            return _dot(h, w_ref[:, j * w:(j + 1) * w])

        u_ref[rows, :] = _dot(perm_ref[...], proj(0).astype(BF16)).astype(BF16)
        sza_ref[rows, :] = _silu(proj(1)).astype(BF16)
        q = proj(2) * (DH ** -0.5 * LOG2_E)
        k = proj(3)
        v = proj(4)
        if rope:
            lane = lax.broadcasted_iota(jnp.int32, (1, 128), 1)
            first_half = (lane % DH) < ROT_HALF
            q = _rope(q, cos_ref[rows, :], sin_ref[rows, :], first_half)
            kb_ref[rows, :] = _rope(k, cos_ref[rows, :], sin_ref[rows, :], first_half).astype(BF16)
        else:
            kf_ref[r] = k.T
            vf_ref[rows, :] = v
            kb_ref[rows, :] = k.astype(BF16)
        q_ref[rows, :] = q.astype(BF16)
        vb_ref[rows, :] = v.astype(BF16)
        szb_ref[rows, :] = _silu(proj(5)).astype(BF16)


def _even_in_proj(x2, mod3, w_in_bf, layer, cond_of, tm, rope_tabs, seq_len):
    t, d = x2.shape
    n = w_in_bf.shape[1]
    assert tm % PERM_TILE == 0
    row = pl.BlockSpec((tm, d), lambda i: (i, 0))
    in_specs = [row, _mod_spec(layer, 0, cond_of), _mod_spec(layer, 1, cond_of),
                _resident((d, n)), _resident((PERM_TILE, PERM_TILE))]
    args = [x2, mod3, mod3, w_in_bf, _chunk_transpose_matrix()]
    bf = jax.ShapeDtypeStruct((t, d), BF16)
    out_shape = [bf] * 6
    out_specs = [row] * 6
    if rope_tabs is not None:
        tiles_per_seq = seq_len // tm
        tab = pl.BlockSpec((tm, 128), lambda i: (i % tiles_per_seq, 0))
        in_specs += [tab, tab]
        args += list(rope_tabs)
    else:
        assert seq_len == PERM_TILE
        out_shape += [jax.ShapeDtypeStruct((t // seq_len, d, seq_len), F32), jax.ShapeDtypeStruct((t, d), F32)]
        out_specs += [pl.BlockSpec((tm // seq_len, d, seq_len), lambda i: (i, 0, 0)), row]
    return pl.pallas_call(
        functools.partial(_even_in_kernel, rope=rope_tabs is not None),
        grid=(t // tm,),
        in_specs=in_specs,
        out_specs=out_specs,
        out_shape=out_shape,
        compiler_params=_cparams(1),
        name="even_in_proj",
    )(*args)


BUILD_GROUPS = 4


def _s5_build_kernel(q_ref, b_ref, cn_ref, ccr_ref, cci_ref, ecr_ref, eci_ref, d_ref,
                     rep_ref, til_ref, wst_ref, kt_ref, wo_ref):
    def split(x):
        hi = x.astype(BF16)
        return hi, (x - hi.astype(F32)).astype(BF16)

    def hdot(a, b):
        (ah, al), (bh, bl) = split(a), split(b)
        return _dot(ah, bh) + _dot(ah, bl) + _dot(al, bh)

    lane = lax.broadcasted_iota(jnp.int32, (SSM_GROUP, SSM_TILE), 1)
    sub = lax.broadcasted_iota(jnp.int32, (SSM_GROUP, SSM_TILE), 0)
    zeros = jnp.zeros((SSM_GROUP, SSM_TILE), F32)
    rows4 = 4 * P_A
    def expand(x, sel):
        hi, lo = split(x)
        return _dot(hi, sel) + _dot(lo, sel)

    qe = expand(q_ref[...].reshape(BUILD_GROUPS * rows4, LANES), rep_ref[...])
    be = expand(b_ref[...].reshape(BUILD_GROUPS * rows4, LANES), til_ref[...])
    for g in range(BUILD_GROUPS):
        part = lambda x, k: x[g * rows4 + k * P_A:g * rows4 + (k + 1) * P_A]
        st_re = [part(qe, d) * part(be, d) - part(qe, 2 + d) * part(be, 2 + d) for d in range(2)]
        st_im = [part(qe, d) * part(be, 2 + d) + part(qe, 2 + d) * part(be, d) for d in range(2)]
        wst_ref[g] = jnp.concatenate([st_re[0], st_re[1], st_im[0], st_im[1]], axis=0).astype(BF16)
        klag = [hdot(cn_ref[d, g], jnp.concatenate([st_re[d], st_im[d]], axis=0)) for d in range(2)]
        fwd_p = jnp.concatenate([klag[0], zeros], axis=1)
        bwd_p = jnp.concatenate([zeros, klag[1]], axis=1)
        ccr, cci = ccr_ref[g], cci_ref[g]
        d_lanes = jnp.broadcast_to(d_ref[g], (SSM_GROUP, SSM_TILE))
        for t in range(CHUNK):
            lo_f = SSM_GROUP * (CHUNK - 1 - t)
            lo_b = SSM_TILE - SSM_GROUP * t
            skip = jnp.where(lane == t * SSM_GROUP + sub, d_lanes, 0.0)
            rows = slice(t * SSM_GROUP, (t + 1) * SSM_GROUP)
            kt_ref[g, rows, :] = (fwd_p[:, lo_f:lo_f + SSM_TILE] + bwd_p[:, lo_b:lo_b + SSM_TILE] + skip).astype(BF16)
            er, ei = ecr_ref[g, t:t + 1, :], eci_ref[g, t:t + 1, :]
            wo_ref[g, rows, :] = jnp.concatenate([ccr * er - cci * ei, -(ccr * ei + cci * er)], axis=1).astype(BF16)


def _s5_operators(lam_re, lam_im, log_dt, b_re, b_im, c_re, c_im, d_skip):
    lr, li = lam_re.astype(F32), lam_im.astype(F32)
    dt = jnp.exp(log_dt.astype(F32))[..., None]
    mag = jnp.exp(lr * dt)
    ar = mag * jnp.cos(li * dt)
    ai = mag * jnp.sin(li * dt)
    den = lr * lr + li * li
    fr = ((ar - 1.0) * lr + ai * li) / den
    fi = (ai * lr - (ar - 1.0) * li) / den
    br, bi = b_re.astype(F32), b_im.astype(F32)
    bbr = fr[..., None] * br - fi[..., None] * bi
    bbi = fr[..., None] * bi + fi[..., None] * br
    cr, ci = c_re.astype(F32), c_im.astype(F32)
    e_pow = jnp.arange(CHUNK + 1, dtype=F32)[:, None, None, None]
    pmag = jnp.exp(e_pow * (lr * dt))
    pr = pmag * jnp.cos(e_pow * (li * dt))
    pi = pmag * jnp.sin(e_pow * (li * dt))

    def by_position(p):
        return jnp.stack([p[:CHUNK, 0][::-1], p[:CHUNK, 1]], axis=0).transpose(0, 2, 1, 3)
    qr, qi = by_position(pr), by_position(pi)

    lane_sn, src = np.arange(SSM_TILE)[None, :], np.arange(LANES)[:, None]
    rep = jnp.asarray(lane_sn // SSM_GROUP == src, dtype=BF16)
    til = jnp.asarray((lane_sn % SSM_GROUP == src) & (src < SSM_GROUP), dtype=BF16)

    def by_output(p):
        return jnp.concatenate([p[1:, 0], p[1:, 1][::-1]], axis=-1).transpose(1, 0, 2)
    cat_c = lambda c: jnp.concatenate([c[0], c[1]], axis=-1)

    def stack_rows(re, im):
        x = jnp.concatenate([re[0], re[1], im[0], im[1]], axis=1)
        return jnp.pad(x, ((0, 0), (0, 0), (0, LANES - x.shape[-1])))
    q_all = stack_rows(qr.transpose(0, 1, 3, 2), qi.transpose(0, 1, 3, 2))
    b_all = stack_rows(bbr, bbi)
    c_neg = jnp.concatenate([cr, -ci], axis=-1)

    gb = BUILD_GROUPS
    narrow = pl.BlockSpec((gb, 4 * P_A, LANES), lambda i: (i, 0, 0))
    expand = pl.BlockSpec((LANES, SSM_TILE), lambda i: (0, 0))
    cspec = pl.BlockSpec((2, gb, SSM_GROUP, 2 * P_A), lambda i: (0, i, 0, 0))
    half = pl.BlockSpec((gb, SSM_GROUP, 2 * P_A), lambda i: (i, 0, 0))
    dspec = pl.BlockSpec((gb, SSM_GROUP, 1), lambda i: (i, 0, 0))
    ospec = pl.BlockSpec((gb, SSM_TILE, SSM_TILE), lambda i: (i, 0, 0))
    oshape = jax.ShapeDtypeStruct((G_A, SSM_TILE, SSM_TILE), BF16)
    wst_t, kt_t, wo_t = pl.pallas_call(
        _s5_build_kernel,
        grid=(G_A // gb,),
        in_specs=[narrow] * 2 + [cspec] + [half] * 4 + [dspec, expand, expand],
        out_specs=[ospec] * 3,
        out_shape=[oshape] * 3,
        compiler_params=_cparams(1),
        name="s5_build_operators",
    )(q_all, b_all, c_neg, cat_c(cr), cat_c(ci),
      by_output(pr), by_output(pi), d_skip.astype(F32).reshape(G_A, SSM_GROUP, 1), rep, til)

    a16r = jnp.concatenate([pr[CHUNK, 0], pr[CHUNK, 1]], axis=-1)[:, None, :]
    a16i = jnp.concatenate([pi[CHUNK, 0], pi[CHUNK, 1]], axis=-1)[:, None, :]
    return wst_t, kt_t, wo_t, a16r, a16i


def _s5_kernel(u_ref, wst_ref, kt_ref, wo_ref, a16r_ref, a16i_ref, h0r_ref, h0i_ref,
               pin_ref, pout_ref, y_ref, fr_ref, fi_ref, ut, yt, s_sc, hf_sc, hb_sc, *, nb, nc, gpb):
    gb = pl.program_id(2)
    seq = nc * CHUNK
    r = nb * nc
    sw = 2 * P_A
    n_cb = u_ref.shape[1] // LANES
    pieces = [b * seq + q * PERM_TILE for b in range(nb) for q in range(seq // PERM_TILE)]

    @pl.when(gb == 0)
    def _():
        for s in range(CHUNK):
            cols = []
            for cb in range(n_cb):
                rows = jnp.concatenate([u_ref[o + s * CHUNK:o + (s + 1) * CHUNK, cb * LANES:(cb + 1) * LANES]
                                        for o in pieces], axis=0)
                cols.append(rows.astype(F32).T.astype(BF16))
            ut[:, s * r:(s + 1) * r] = _dot(jnp.concatenate(cols, axis=0), pin_ref[...]).astype(BF16)

    fwd = (lax.broadcasted_iota(jnp.int32, (1, sw), 1)) < P_A

    def groups(it, carry):
        js = [it * S5_INTERLEAVE + i for i in range(S5_INTERLEAVE)]
        grows = [pl.multiple_of((gb * gpb + j) * SSM_GROUP, SSM_GROUP) for j in js]
        ds = [jnp.concatenate([ut[pl.ds(g, SSM_GROUP), s * r:(s + 1) * r] for s in range(CHUNK)], axis=0)
              for g in grows]
        for i, j in enumerate(js):
            s_t = _dot(wst_ref[j], ds[i])
            s_sc[i, 0] = s_t[:sw].T
            s_sc[i, 1] = s_t[sw:].T
        a_r = [a16r_ref[j] for j in js]
        a_i = [a16i_ref[j] for j in js]
        re = [h0r_ref[j] for j in js]
        im = [h0i_ref[j] for j in js]
        for k in range(nc):
            rf = pl.ds(k * nb, nb)
            rb = pl.ds((nc - 1 - k) * nb, nb)
            for i in range(S5_INTERLEAVE):
                hf_sc[i, 0, rf, :] = re[i]
                hf_sc[i, 1, rf, :] = im[i]
                hb_sc[i, 0, rb, :] = re[i]
                hb_sc[i, 1, rb, :] = im[i]
                xr = jnp.where(fwd, s_sc[i, 0, rf, :], s_sc[i, 0, rb, :])
                xi = jnp.where(fwd, s_sc[i, 1, rf, :], s_sc[i, 1, rb, :])
                re[i], im[i] = a_r[i] * re[i] - a_i[i] * im[i] + xr, a_r[i] * im[i] + a_i[i] * re[i] + xi
        for i, j in enumerate(js):
            fr_ref[j] = re[i]
            fi_ref[j] = im[i]
            hin_t = jnp.concatenate([jnp.where(fwd, hf_sc[i, 0], hb_sc[i, 0]).T,
                                     jnp.where(fwd, hf_sc[i, 1], hb_sc[i, 1]).T], axis=0)
            y_t = (_dot(kt_ref[j], ds[i]) + _dot(wo_ref[j], hin_t.astype(BF16))).astype(BF16)
            for t in range(CHUNK):
                yt[pl.ds(grows[i], SSM_GROUP), t * r:(t + 1) * r] = y_t[t * SSM_GROUP:(t + 1) * SSM_GROUP, :]
        return carry

    lax.fori_loop(0, gpb // S5_INTERLEAVE, groups, 0)

    @pl.when(gb == pl.num_programs(2) - 1)
    def _():
        for t in range(CHUNK):
            full = _dot(yt[:, t * r:(t + 1) * r], pout_ref[...])
            for cb in range(n_cb):
                rows = full[cb * LANES:(cb + 1) * LANES, :].T.astype(BF16)
                for idx, o in enumerate(pieces):
                    y_ref[o + t * CHUNK:o + (t + 1) * CHUNK, cb * LANES:(cb + 1) * LANES] = (
                        rows[idx * CHUNK:(idx + 1) * CHUNK, :])


def _s5_mix(u, ops, h0_re, h0_im, bsz, seq_len, nb, cw):
    wst_t, kt_t, wo_t, a16r, a16i = ops
    t, w = u.shape
    nc = seq_len // CHUNK
    r = nb * nc
    gpb = S5_GROUPS_PER_STEP
    n_row_tiles = bsz // nb
    n_col_tiles = w // cw
    gb_per_tile = cw // SSM_GROUP // gpb
    assert r % 128 == 0 and cw % (SSM_GROUP * gpb) == 0 and bsz % nb == 0

    def h0_layout(h0):
        return h0.astype(F32).transpose(2, 0, 1, 3).reshape(G_A, bsz, 2 * P_A)

    tile = pl.BlockSpec((nb * seq_len, cw), lambda i, c, g: (i, c))
    grp = lambda i, c, g: (c * gb_per_tile + g, 0, 0)
    wspec = pl.BlockSpec((gpb, SSM_TILE, SSM_TILE), grp)
    aspec = pl.BlockSpec((gpb, 1, 2 * P_A), grp)
    hspec = pl.BlockSpec((gpb, nb, 2 * P_A), lambda i, c, g: (c * gb_per_tile + g, i, 0))
    state = jax.ShapeDtypeStruct((G_A, bsz, 2 * P_A), F32)
    scratch = [pltpu.VMEM((cw, CHUNK * r), BF16)] * 2 + [pltpu.VMEM((S5_INTERLEAVE, 2, r, 2 * P_A), F32)] * 3
    src = lax.broadcasted_iota(jnp.int32, (r, r), 0)
    dst = lax.broadcasted_iota(jnp.int32, (r, r), 1)
    pin = (dst == (src % nc) * nb + src // nc).astype(BF16)
    pspec = pl.BlockSpec((r, r), lambda i, c, g: (0, 0))

    y, f_re, f_im = pl.pallas_call(
        functools.partial(_s5_kernel, nb=nb, nc=nc, gpb=gpb),
        grid=(n_row_tiles, n_col_tiles, gb_per_tile),
        in_specs=[tile, wspec, wspec, wspec, aspec, aspec, hspec, hspec, pspec, pspec],
        out_specs=[tile, hspec, hspec],
        out_shape=[jax.ShapeDtypeStruct((t, w), BF16), state, state],
        scratch_shapes=scratch,
        compiler_params=_cparams(3),
        name="s5_core",
    )(u, wst_t, kt_t, wo_t, a16r, a16i, h0_layout(h0_re), h0_layout(h0_im), pin, pin.T)

    def fin(f):
        return f.reshape(G_A, bsz, 2, P_A).transpose(1, 2, 0, 3)
    return y, fin(f_re), fin(f_im)


def _diff_lambda(lq1, lk1, lq2, lk2, lam_init):
    return (jnp.exp(jnp.sum(lq1[...] * lk1[...], axis=-1, keepdims=True))
            - jnp.exp(jnp.sum(lq2[...] * lk2[...], axis=-1, keepdims=True)) + lam_init)


def _attn_kernel(*refs, lam_init, cached):
    if cached:
        q_ref, kc_ref, kn_ref, vc_ref, vn_ref, szb_ref, lq1, lk1, lq2, lk2, g_ref, o_ref = refs
    else:
        q_ref, kn_ref, vn_ref, szb_ref, lq1, lk1, lq2, lk2, g_ref, o_ref = refs
    lam = _diff_lambda(lq1, lk1, lq2, lk2, lam_init)
    lq = q_ref.shape[0]
    hw = 2 * DH
    low = lax.broadcasted_iota(jnp.int32, (1, hw), 1) < DH
    zero = jnp.zeros((), BF16)
    ones = jnp.ones((kn_ref.shape[0], hw), BF16)
    for h in range(H_B):
        cols = slice(h * hw, (h + 1) * hw)
        qh = q_ref[:, cols]
        qs = jnp.concatenate([jnp.where(low, qh, zero), jnp.where(low, zero, qh)], axis=0)
        s = _dot_nt(qs, kn_ref[:, cols])
        if cached:
            s = jnp.concatenate([_dot_nt(qs, kc_ref[:, cols]), s], axis=1)
        e = jnp.exp2(s - jnp.max(s, axis=-1, keepdims=True)).astype(BF16)
        vn = jnp.concatenate([vn_ref[:, cols], ones], axis=1)
        if cached:
            lc = kc_ref.shape[0]
            vc = jnp.concatenate([vc_ref[:, cols], ones[:lc]], axis=1)
            oa = _dot(e[:, :lc], vc) + _dot(e[:, lc:], vn)
        else:
            oa = _dot(e, vn)
        on = oa[:, :hw] * (1.0 / oa[:, hw:])
        o = on[:lq] - lam * on[lq:]
        o = o * lax.rsqrt(jnp.mean(o * o, axis=-1, keepdims=True) + LN_EPS)
        o = o * g_ref[...] * (1.0 - lam_init)
        o_ref[:, cols] = (o * szb_ref[:, cols].astype(F32)).astype(BF16)


def _lam_specs(n_axes):
    zero = lambda *idx: (0, 0)
    return [pl.BlockSpec((1, DH), zero)] * 4 + [pl.BlockSpec((1, 2 * DH), zero)]


def _attention_prompt(q, k, v, szb, lam_vecs, subln, lam_init, bsz, seq_len):
    blk = pl.BlockSpec((seq_len, D_MODEL), lambda b: (b, 0))
    return pl.pallas_call(
        functools.partial(_attn_kernel, lam_init=lam_init, cached=False),
        grid=(bsz,),
        in_specs=[blk] * 4 + _lam_specs(1),
        out_specs=blk,
        out_shape=jax.ShapeDtypeStruct(q.shape, BF16),
        compiler_params=_cparams(1),
        name="diff_attention_prompt",
    )(q, k, v, szb, *lam_vecs, subln)


def _attention_sample(q, kc, kn, vc, vn, szb, lam_vecs, subln, lam_init, bsz, seq_len, tq):
    nq = seq_len // tq
    past = kc.shape[1]
    qblk = pl.BlockSpec((tq, D_MODEL), lambda b, i: (b * nq + i, 0))
    cblk = pl.BlockSpec((None, past, D_MODEL), lambda b, i: (b, 0, 0))
    nblk = pl.BlockSpec((seq_len, D_MODEL), lambda b, i: (b, 0))
    return pl.pallas_call(
        functools.partial(_attn_kernel, lam_init=lam_init, cached=True),
        grid=(bsz, nq),
        in_specs=[qblk, cblk, nblk, cblk, nblk, qblk] + _lam_specs(2),
        out_specs=qblk,
        out_shape=jax.ShapeDtypeStruct(q.shape, BF16),
        compiler_params=_cparams(2),
        name="diff_attention_sample",
    )(q, kc, kn, vc, vn, szb, *lam_vecs, subln)


def _even_out_kernel(ys_ref, sza_ref, yb_ref, x_ref, gate_ref, wglu_ref, bglu_ref, wout_ref, g_ref, b_ref,
                     perm_ref, o_ref, *, alpha):
    wa = ys_ref.shape[1]
    for r in range(x_ref.shape[0] // PERM_TILE):
        rows = slice(r * PERM_TILE, (r + 1) * PERM_TILE)
        ga = jax.nn.gelu(_dot(perm_ref[...], ys_ref[rows, :]))
        glu = jax.nn.sigmoid(_dot(ga.astype(BF16), wglu_ref[...]) + bglu_ref[...])
        ya = (ga * glu * sza_ref[rows, :].astype(F32)).astype(BF16)
        out = _dot(ya, wout_ref[:wa, :]) + _dot(yb_ref[rows, :], wout_ref[wa:, :])
        o_ref[rows, :] = _post_norm(x_ref[rows, :], gate_ref[...], out, g_ref[...], b_ref[...], alpha)


def _even_out_proj(ys, sza, yb, x2, mod3, w_glu_bf, b_glu, w_out_bf, ln_g, ln_b, layer, cond_of, tm, alpha):
    t, d = x2.shape
    assert tm % PERM_TILE == 0
    row = pl.BlockSpec((tm, d), lambda i: (i, 0))
    full = _resident
    return pl.pallas_call(
        functools.partial(_even_out_kernel, alpha=alpha),
        grid=(t // tm,),
        in_specs=[row, row, row, row, _mod_spec(layer, 2, cond_of), full(w_glu_bf.shape), full((1, d)),
                  full(w_out_bf.shape), full((1, d)), full((1, d)), full((PERM_TILE, PERM_TILE))],
        out_specs=row,
        out_shape=jax.ShapeDtypeStruct((t, d), F32),
        compiler_params=_cparams(1),
        name="even_out_proj",
    )(ys, sza, yb, x2, mod3, w_glu_bf, b_glu.reshape(1, d), w_out_bf, ln_g.reshape(1, d), ln_b.reshape(1, d),
      _chunk_transpose_matrix())


def _dft_mats(n, scale):
    k = np.arange(n, dtype=np.int64)
    ang = ((k[:, None] * k[None, :]) % n).astype(np.float64) * (2.0 * math.pi / n)
    return (jnp.asarray((np.cos(ang) * scale).astype(np.float32)).astype(BF16),
            jnp.asarray((np.sin(ang) * scale).astype(np.float32)).astype(BF16))


def _odd_in_kernel(x_ref, shift_ref, scale_ref, w_ref, cc_ref, sc_ref, uc_ref, us_ref, sz_ref):
    for r in range(x_ref.shape[0] // SUB_ROWS):
        rows = slice(r * SUB_ROWS, (r + 1) * SUB_ROWS)
        h = (_ln_rows(x_ref[rows, :]) * (1.0 + scale_ref[...]) + shift_ref[...]).astype(BF16)
        u = _dot(h, w_ref[:, :W_C]).astype(BF16)
        sz_ref[rows, :] = _silu(_dot(h, w_ref[:, W_C:])).astype(BF16)
        for g in range(NG_C):
            cols = slice(g * GC_C, (g + 1) * GC_C)
            uc_ref[rows, cols] = _dot(u[:, cols], cc_ref[...]).astype(BF16)
            us_ref[rows, cols] = _dot(u[:, cols], sc_ref[...]).astype(BF16)


def _odd_in_proj(x2, mod3, w_in_bf, cc, sc, layer, cond_of, tm):
    t, d = x2.shape
    assert tm % SUB_ROWS == 0
    row = pl.BlockSpec((tm, d), lambda i: (i, 0))
    wide = pl.BlockSpec((tm, W_C), lambda i: (i, 0))
    full = _resident
    wide_bf = jax.ShapeDtypeStruct((t, W_C), BF16)
    return pl.pallas_call(
        _odd_in_kernel,
        grid=(t // tm,),
        in_specs=[row, _mod_spec(layer, 0, cond_of), _mod_spec(layer, 1, cond_of), full(w_in_bf.shape),
                  full(cc.shape), full(sc.shape)],
        out_specs=[wide] * 3,
        out_shape=[wide_bf] * 3,
        compiler_params=_cparams(1),
        name="odd_in_proj",
    )(x2, mod3, mod3, w_in_bf, cc, sc)


def _odd_out_kernel(cl_ref, sl_ref, uc_ref, us_ref, sz_ref, x_ref, gate_ref, wf_ref, bf_ref, wo_ref, g_ref, b_ref,
                    o_ref, *, alpha, whole_seqs):
    seq_len = cl_ref.shape[1]
    sub = seq_len if whole_seqs else SUB_ROWS
    for r in range(x_ref.shape[0] // sub):
        rows = slice(r * sub, (r + 1) * sub)
        if whole_seqs:
            mixed = _dot(cl_ref[...], uc_ref[rows, :]) - _dot(sl_ref[...], us_ref[rows, :])
        else:
            mixed = _dot(cl_ref[rows, :], uc_ref[...]) - _dot(sl_ref[rows, :], us_ref[...])
        y = ((_dot(mixed.astype(BF16), wf_ref[...]) + bf_ref[...]) * sz_ref[rows, :].astype(F32)).astype(BF16)
        o_ref[rows, :] = _post_norm(x_ref[rows, :], gate_ref[...], _dot(y, wo_ref[...]), g_ref[...], b_ref[...],
                                    alpha)


def _odd_out_proj(uc, us, sz, x2, mod3, cl, sl, w_fno_bf, b_fno, w_out_bf, ln_g, ln_b, layer, cond_of,
                  bsz, seq_len, tl, alpha):
    t, d = x2.shape
    whole_seqs = tl >= seq_len
    full = _resident
    if whole_seqs:
        assert tl % seq_len == 0
        grid = (t // tl, 1)
        dft = _resident((seq_len, seq_len))
        seq = pl.BlockSpec((tl, W_C), lambda b, i: (b, 0))
        wide = seq
        row = pl.BlockSpec((tl, d), lambda b, i: (b, 0))
    else:
        assert seq_len % tl == 0 and tl % SUB_ROWS == 0
        nl = seq_len // tl
        grid = (bsz, nl)
        dft = pl.BlockSpec((tl, seq_len), lambda b, i: (i, 0))
        seq = pl.BlockSpec((seq_len, W_C), lambda b, i: (b, 0))
        wide = pl.BlockSpec((tl, W_C), lambda b, i: (b * nl + i, 0))
        row = pl.BlockSpec((tl, d), lambda b, i: (b * nl + i, 0))
    return pl.pallas_call(
        functools.partial(_odd_out_kernel, alpha=alpha, whole_seqs=whole_seqs),
        grid=grid,
        in_specs=[dft, dft, seq, seq, wide, row, _mod_spec(layer, 2, cond_of), full(w_fno_bf.shape),
                  full((1, W_C)), full(w_out_bf.shape), full((1, d)), full((1, d))],
        out_specs=row,
        out_shape=jax.ShapeDtypeStruct((t, d), F32),
        compiler_params=_cparams(2),
        name="odd_out_proj",
    )(cl, sl, uc, us, sz, x2, mod3, w_fno_bf, b_fno.reshape(1, W_C), w_out_bf, ln_g.reshape(1, d),
      ln_b.reshape(1, d))


def _rope_tables(seq_len):
    rows = seq_len // GRID_W
    row = jnp.repeat(jnp.arange(rows), GRID_W).astype(F32)
    col = jnp.tile(jnp.arange(GRID_W), rows).astype(F32)
    freqs = ROPE_BASE ** (-jnp.arange(ROT_FREQS, dtype=F32) / ROT_FREQS)
    ang = jnp.concatenate([row[:, None] * freqs, col[:, None] * freqs], axis=-1)
    cos, sin = jnp.cos(ang), jnp.sin(ang)
    cos128 = jnp.tile(cos, (1, 128 // ROT_HALF))
    sin128 = jnp.tile(jnp.concatenate([-sin, sin], axis=-1), (1, 128 // DH))
    return cos128, sin128


def kernel(x_prompt, x_sample, cache_k, cache_v, state_ssm_re, state_ssm_im, c, c_ctx, w_mod, b_mod, ln_g, ln_b, w_in_e, ssm_lam_re, ssm_lam_im, ssm_log_dt, ssm_b_re, ssm_b_im, ssm_c_re, ssm_c_im, ssm_d, w_glu, b_glu, lam_q1, lam_k1, lam_q2, lam_k2, subln_g, w_out_e, w_in_o, w_fno, b_fno, w_out_o):
    depth = w_mod.shape[0]
    bp_, lp, d = x_prompt.shape
    bs_, ls, _ = x_sample.shape
    past = cache_k.shape[2]
    alpha = (2 * depth) ** 0.25
    assert bs_ + 1 <= MOD_ROWS and d == D_MODEL

    cond8 = jnp.concatenate([c_ctx[None, :], c, jnp.zeros((MOD_ROWS - 1 - bs_, d), F32)], axis=0).astype(F32)
    mod3 = _modulation(cond8, w_mod, b_mod).reshape(depth * MOD_ROWS * 3, 1, d)

    tm, tm_wide = ROW_TILE, WIDE_ROW_TILE
    cond_p = lambda *idx: 0
    cond_s_rows = lambda tile: (lambda i: 1 + i // (ls // tile))
    cond_s_row = cond_s_rows(tm)
    cond_s_grid = lambda b, i: 1 + b
    rope_tabs = _rope_tables(ls)
    xp = x_prompt.reshape(bp_ * lp, d)
    xs = x_sample.reshape(bs_ * ls, d)
    new_k, new_v, new_sr, new_si = [], [], [], []
    zeros_h0 = jnp.zeros((bp_, 2, G_A, P_A), F32)

    for layer in range(depth):
        if layer % 2 == 0:
            e = layer // 2
            lam_init = 0.8 - 0.6 * math.exp(-0.3 * layer)
            w_in_bf = w_in_e[e].astype(BF16)
            w_glu_bf = w_glu[e].astype(BF16)
            w_out_bf = w_out_e[e].astype(BF16)
            ops = _s5_operators(ssm_lam_re[e], ssm_lam_im[e], ssm_log_dt[e], ssm_b_re[e], ssm_b_im[e],
                                ssm_c_re[e], ssm_c_im[e], ssm_d[e])
            lam_vecs = [v[e].reshape(1, DH).astype(F32) for v in (lam_q1, lam_k1, lam_q2, lam_k2)]
            subln = subln_g[e].reshape(1, 2 * DH).astype(F32)

            u, sza, q, kb, vb, szb, kf, vf = _even_in_proj(xp, mod3, w_in_bf, layer, cond_p, tm, None, lp)
            new_k.append(kf.reshape(bp_, H_B, 2, DH, lp).transpose(0, 4, 1, 2, 3))
            new_v.append(vf.reshape(bp_, lp, H_B, 2 * DH))
            ys, s_re, s_im = _s5_mix(u, ops, zeros_h0, zeros_h0, bp_, lp, 2 * S5_TILE_ELEMS // (lp * d), d // 2)
            new_sr.append(s_re)
            new_si.append(s_im)
            yb = _attention_prompt(q, kb, vb, szb, lam_vecs, subln, lam_init, bp_, lp)
            xp = _even_out_proj(ys, sza, yb, xp, mod3, w_glu_bf, b_glu[e], w_out_bf, ln_g[layer], ln_b[layer],
                                layer, cond_p, tm_wide, alpha)

            u, sza, q, kb, vb, szb = _even_in_proj(xs, mod3, w_in_bf, layer, cond_s_row, tm, rope_tabs, ls)
            ys, _, _ = _s5_mix(u, ops, state_ssm_re[:, e], state_ssm_im[:, e], bs_, ls, bs_,
                               S5_TILE_ELEMS // (bs_ * ls))
            kc = cache_k[:, e].reshape(bs_, past, d).astype(BF16)
            vc = cache_v[:, e].reshape(bs_, past, d).astype(BF16)
            yb = _attention_sample(q, kc, kb, vc, vb, szb, lam_vecs, subln, lam_init, bs_, ls, ATTN_Q_TILE)
            xs = _even_out_proj(ys, sza, yb, xs, mod3, w_glu_bf, b_glu[e], w_out_bf, ln_g[layer], ln_b[layer],
                                layer, cond_s_rows(tm_wide), tm_wide, alpha)
        else:
            o = layer // 2
            w_in_bf = w_in_o[o].astype(BF16)
            w_fno_bf = w_fno[o].astype(BF16)
            w_out_bf = w_out_o[o].astype(BF16)
            cc, sc = _dft_mats(GC_C, GC_C ** -0.5)
            for which in ("prompt", "sample"):
                if which == "prompt":
                    x2, cond_row, cond_grid, bsz, seq = xp, cond_p, cond_p, bp_, lp
                else:
                    x2, cond_row, cond_grid, bsz, seq = xs, cond_s_rows(tm_wide), cond_s_grid, bs_, ls
                cl, sl = _dft_mats(seq, seq ** -0.5)
                uc, us, sz = _odd_in_proj(x2, mod3, w_in_bf, cc, sc, layer, cond_row, tm_wide)
                x2 = _odd_out_proj(uc, us, sz, x2, mod3, cl, sl, w_fno_bf, b_fno[o], w_out_bf, ln_g[layer],
                                   ln_b[layer], layer, cond_grid, bsz, seq, tm, alpha)
                if which == "prompt":
                    xp = x2
                else:
                    xs = x2

    return (xp.reshape(bp_, lp, d), xs.reshape(bs_, ls, d), jnp.stack(new_k, axis=1), jnp.stack(new_v, axis=1),
            jnp.stack(new_sr, axis=1), jnp.stack(new_si, axis=1))
```

```python
import functools
import math

import jax
import jax.numpy as jnp
import numpy as np
from jax import lax
from jax.experimental import pallas as pl
from jax.experimental.pallas import tpu as pltpu

F32 = jnp.float32
BF16 = jnp.bfloat16

D_MODEL = 1024
GRID_W = 64
SSM_GROUP = 16
G_A = D_MODEL // SSM_GROUP
P_A = 64
DH = 64
H_B = D_MODEL // (2 * DH)
ROPE_BASE = 10000.0
ROT_HALF = DH // 2
ROT_FREQS = DH // 4
NG_C = 8
GC_C = 2 * D_MODEL // NG_C
W_C = 2 * D_MODEL
LN_EPS = 1e-5
LOG2_E = 1.4426950408889634
CHUNK = 16
SSM_TILE = CHUNK * SSM_GROUP
LANES = 128
PERM_TILE = CHUNK * CHUNK
ROW_TILE = 2 * PERM_TILE
SUB_ROWS = ROW_TILE
ATTN_Q_TILE = 256
S5_GROUPS_PER_STEP = 16
S5_INTERLEAVE = 8
S5_TILE_ELEMS = 2048 * 1024
MOD_ROWS = 8
VMEM_LIMIT = 56 * 1024 * 1024


def _cparams(n_axes):
    return pltpu.CompilerParams(dimension_semantics=("arbitrary",) * n_axes, vmem_limit_bytes=VMEM_LIMIT)


def _ln_rows(x):
    mu = jnp.mean(x, axis=-1, keepdims=True)
    xc = x - mu
    var = jnp.mean(xc * xc, axis=-1, keepdims=True)
    return xc * lax.rsqrt(var + LN_EPS)


def _silu(z):
    return z * jax.nn.sigmoid(z)


def _dot(a, b):
    return jnp.dot(a, b, preferred_element_type=F32)


def _dot_nt(a, b):
    return lax.dot_general(a, b, (((1,), (1,)), ((), ())), preferred_element_type=F32)


def _post_norm(x, gate, out, g, b, alpha):
    return _ln_rows(alpha * x + gate * out) * g + b


def _mod_kernel(c_ref, w_ref, b_ref, o_ref):
    c = _silu(c_ref[...]).astype(BF16)
    o_ref[...] = _dot(c, w_ref[...].astype(BF16)) + b_ref[...]


def _modulation(cond8, w_mod, b_mod):
    depth, d, n3 = w_mod.shape
    tn = 1024
    return pl.pallas_call(
        _mod_kernel,
        grid=(depth, n3 // tn),
        in_specs=[
            pl.BlockSpec((MOD_ROWS, d), lambda l, j: (0, 0)),
            pl.BlockSpec((None, d, tn), lambda l, j: (l, 0, j)),
            pl.BlockSpec((None, 1, tn), lambda l, j: (l, 0, j)),
        ],
        out_specs=pl.BlockSpec((None, MOD_ROWS, tn), lambda l, j: (l, 0, j)),
        out_shape=jax.ShapeDtypeStruct((depth, MOD_ROWS, n3), F32),
        compiler_params=_cparams(2),
        name="modulation",
    )(cond8, w_mod, b_mod.reshape(depth, 1, n3))


def _resident(shape):
    return pl.BlockSpec(shape, lambda *idx: (0,) * len(shape), pipeline_mode=pl.Buffered(1))


def _mod_spec(layer, part, cond_of):
    return pl.BlockSpec((None, 1, D_MODEL), lambda *idx: ((layer * MOD_ROWS + cond_of(*idx)) * 3 + part, 0, 0))


def _rope(x, cos, sin_signed, first_half):
    blocks = []
    for hh in range(x.shape[1] // 128):
        b = x[:, hh * 128:(hh + 1) * 128]
        partner = jnp.where(first_half, pltpu.roll(b, 128 - ROT_HALF, 1), pltpu.roll(b, ROT_HALF, 1))
        blocks.append(b * cos + partner * sin_signed)
    return jnp.concatenate(blocks, axis=1)


def _chunk_transpose_matrix():
    i = lax.broadcasted_iota(jnp.int32, (PERM_TILE, PERM_TILE), 0)
    j = lax.broadcasted_iota(jnp.int32, (PERM_TILE, PERM_TILE), 1)
    return (j == (i % CHUNK) * CHUNK + i // CHUNK).astype(BF16)


def _even_in_kernel(*refs, rope):
    if rope:
        (x_ref, shift_ref, scale_ref, w_ref, perm_ref, cos_ref, sin_ref,
         u_ref, sza_ref, q_ref, kb_ref, vb_ref, szb_ref) = refs
    else:
        (x_ref, shift_ref, scale_ref, w_ref, perm_ref, u_ref, sza_ref, q_ref, kb_ref, vb_ref, szb_ref,
         kf_ref, vf_ref) = refs
    w = D_MODEL
    for r in range(x_ref.shape[0] // PERM_TILE):
        rows = slice(r * PERM_TILE, (r + 1) * PERM_TILE)
        h = (_ln_rows(x_ref[rows, :]) * (1.0 + scale_ref[...]) + shift_ref[...]).astype(BF16)

        def proj(j):
            return _dot(h, w_ref[:, j * w:(j + 1) * w])

        u_ref[rows, :] = _dot(perm_ref[...], proj(0).astype(BF16)).astype(BF16)
        sza_ref[rows, :] = _silu(proj(1)).astype(BF16)
        q = proj(2) * (DH ** -0.5 * LOG2_E)
        k = proj(3)
        v = proj(4)
        if rope:
            lane = lax.broadcasted_iota(jnp.int32, (1, 128), 1)
            first_half = (lane % DH) < ROT_HALF
            q = _rope(q, cos_ref[rows, :], sin_ref[rows, :], first_half)
            kb_ref[rows, :] = _rope(k, cos_ref[rows, :], sin_ref[rows, :], first_half).astype(BF16)
        else:
            kf_ref[r] = k.T
            vf_ref[rows, :] = v
            kb_ref[rows, :] = k.astype(BF16)
        q_ref[rows, :] = q.astype(BF16)
        vb_ref[rows, :] = v.astype(BF16)
        szb_ref[rows, :] = _silu(proj(5)).astype(BF16)


def _even_in_proj(x2, mod3, w_in_bf, layer, cond_of, tm, rope_tabs, seq_len):
    t, d = x2.shape
    n = w_in_bf.shape[1]
    assert tm % PERM_TILE == 0
    row = pl.BlockSpec((tm, d), lambda i: (i, 0))
    in_specs = [row, _mod_spec(layer, 0, cond_of), _mod_spec(layer, 1, cond_of),
                _resident((d, n)), _resident((PERM_TILE, PERM_TILE))]
    args = [x2, mod3, mod3, w_in_bf, _chunk_transpose_matrix()]
    bf = jax.ShapeDtypeStruct((t, d), BF16)
    out_shape = [bf] * 6
    out_specs = [row] * 6
    if rope_tabs is not None:
        tiles_per_seq = seq_len // tm
        tab = pl.BlockSpec((tm, 128), lambda i: (i % tiles_per_seq, 0))
        in_specs += [tab, tab]
        args += list(rope_tabs)
    else:
        assert seq_len == PERM_TILE
        out_shape += [jax.ShapeDtypeStruct((t // seq_len, d, seq_len), F32), jax.ShapeDtypeStruct((t, d), F32)]
        out_specs += [pl.BlockSpec((tm // seq_len, d, seq_len), lambda i: (i, 0, 0)), row]
    return pl.pallas_call(
        functools.partial(_even_in_kernel, rope=rope_tabs is not None),
        grid=(t // tm,),
        in_specs=in_specs,
        out_specs=out_specs,
        out_shape=out_shape,
        compiler_params=_cparams(1),
        name="even_in_proj",
    )(*args)


BUILD_GROUPS = 4


def _s5_build_kernel(q_ref, b_ref, cn_ref, ccr_ref, cci_ref, ecr_ref, eci_ref, d_ref,
                     rep_ref, til_ref, wst_ref, kt_ref, wo_ref):
    def split(x):
        hi = x.astype(BF16)
        return hi, (x - hi.astype(F32)).astype(BF16)

    def hdot(a, b):
        (ah, al), (bh, bl) = split(a), split(b)
        return _dot(ah, bh) + _dot(ah, bl) + _dot(al, bh)

    lane = lax.broadcasted_iota(jnp.int32, (SSM_GROUP, SSM_TILE), 1)
    sub = lax.broadcasted_iota(jnp.int32, (SSM_GROUP, SSM_TILE), 0)
    zeros = jnp.zeros((SSM_GROUP, SSM_TILE), F32)
    rows4 = 4 * P_A
    def expand(x_t, sel):
        tn = lambda a: lax.dot_general(a, sel, (((0,), (0,)), ((), ())), preferred_element_type=F32)
        hi, lo = split(x_t)
        return tn(hi) + tn(lo)

    for g in range(BUILD_GROUPS):
        qe = expand(q_ref[g], rep_ref[...])
        be = expand(b_ref[g], til_ref[...])
        part = lambda x, k: x[k * P_A:(k + 1) * P_A]
        st_re = [part(qe, d) * part(be, d) - part(qe, 2 + d) * part(be, 2 + d) for d in range(2)]
        st_im = [part(qe, d) * part(be, 2 + d) + part(qe, 2 + d) * part(be, d) for d in range(2)]
        wst_ref[g] = jnp.concatenate([st_re[0], st_re[1], st_im[0], st_im[1]], axis=0).astype(BF16)
        klag = [hdot(cn_ref[d, g], jnp.concatenate([st_re[d], st_im[d]], axis=0)) for d in range(2)]
        fwd_p = jnp.concatenate([klag[0], zeros], axis=1)
        bwd_p = jnp.concatenate([zeros, klag[1]], axis=1)
        ccr, cci = ccr_ref[g], cci_ref[g]
        d_lanes = jnp.broadcast_to(d_ref[g], (SSM_GROUP, SSM_TILE))
        for t in range(CHUNK):
            lo_f = SSM_GROUP * (CHUNK - 1 - t)
            lo_b = SSM_TILE - SSM_GROUP * t
            skip = jnp.where(lane == t * SSM_GROUP + sub, d_lanes, 0.0)
            rows = slice(t * SSM_GROUP, (t + 1) * SSM_GROUP)
            kt_ref[g, rows, :] = (fwd_p[:, lo_f:lo_f + SSM_TILE] + bwd_p[:, lo_b:lo_b + SSM_TILE] + skip).astype(BF16)
            er, ei = ecr_ref[g, t:t + 1, :], eci_ref[g, t:t + 1, :]
            wo_ref[g, rows, :] = jnp.concatenate([ccr * er - cci * ei, -(ccr * ei + cci * er)], axis=1).astype(BF16)


def _s5_operators(lam_re, lam_im, log_dt, b_re, b_im, c_re, c_im, d_skip):
    lr, li = lam_re.astype(F32), lam_im.astype(F32)
    dt = jnp.exp(log_dt.astype(F32))[..., None]
    mag = jnp.exp(lr * dt)
    ar = mag * jnp.cos(li * dt)
    ai = mag * jnp.sin(li * dt)
    den = lr * lr + li * li
    fr = ((ar - 1.0) * lr + ai * li) / den
    fi = (ai * lr - (ar - 1.0) * li) / den
    br, bi = b_re.astype(F32), b_im.astype(F32)
    bbr = fr[..., None] * br - fi[..., None] * bi
    bbi = fr[..., None] * bi + fi[..., None] * br
    cr, ci = c_re.astype(F32), c_im.astype(F32)
    e_pow = jnp.arange(CHUNK + 1, dtype=F32)[:, None, None, None]
    pmag = jnp.exp(e_pow * (lr * dt))
    pr = pmag * jnp.cos(e_pow * (li * dt))
    pi = pmag * jnp.sin(e_pow * (li * dt))

    def by_position(p):
        return jnp.stack([p[:CHUNK, 0][::-1], p[:CHUNK, 1]], axis=0).transpose(0, 2, 1, 3)
    qr, qi = by_position(pr), by_position(pi)

    lane_sn = np.arange(SSM_TILE)
    rep = jnp.asarray(lane_sn[None, :] // SSM_GROUP == np.arange(CHUNK)[:, None], dtype=BF16)
    til = jnp.asarray(lane_sn[None, :] % SSM_GROUP == np.arange(SSM_GROUP)[:, None], dtype=BF16)

    def by_output(p):
        return jnp.concatenate([p[1:, 0], p[1:, 1][::-1]], axis=-1).transpose(1, 0, 2)
    cat_c = lambda c: jnp.concatenate([c[0], c[1]], axis=-1)

    def stack_lanes(re, im):
        return jnp.concatenate([re[0], re[1], im[0], im[1]], axis=-1)
    q_all = stack_lanes(qr, qi)
    b_all = stack_lanes(bbr.transpose(0, 1, 3, 2), bbi.transpose(0, 1, 3, 2))
    c_neg = jnp.concatenate([cr, -ci], axis=-1)

    gb = BUILD_GROUPS
    narrow = pl.BlockSpec((gb, SSM_GROUP, 4 * P_A), lambda i: (i, 0, 0))
    expand = pl.BlockSpec((SSM_GROUP, SSM_TILE), lambda i: (0, 0))
    cspec = pl.BlockSpec((2, gb, SSM_GROUP, 2 * P_A), lambda i: (0, i, 0, 0))
    half = pl.BlockSpec((gb, SSM_GROUP, 2 * P_A), lambda i: (i, 0, 0))
    dspec = pl.BlockSpec((gb, SSM_GROUP, 1), lambda i: (i, 0, 0))
    ospec = pl.BlockSpec((gb, SSM_TILE, SSM_TILE), lambda i: (i, 0, 0))
    oshape = jax.ShapeDtypeStruct((G_A, SSM_TILE, SSM_TILE), BF16)
    wst_t, kt_t, wo_t = pl.pallas_call(
        _s5_build_kernel,
        grid=(G_A // gb,),
        in_specs=[narrow] * 2 + [cspec] + [half] * 4 + [dspec, expand, expand],
        out_specs=[ospec] * 3,
        out_shape=[oshape] * 3,
        compiler_params=_cparams(1),
        name="s5_build_operators",
    )(q_all, b_all, c_neg, cat_c(cr), cat_c(ci),
      by_output(pr), by_output(pi), d_skip.astype(F32).reshape(G_A, SSM_GROUP, 1), rep, til)

    a16r = jnp.concatenate([pr[CHUNK, 0], pr[CHUNK, 1]], axis=-1)[:, None, :]
    a16i = jnp.concatenate([pi[CHUNK, 0], pi[CHUNK, 1]], axis=-1)[:, None, :]
    return wst_t, kt_t, wo_t, a16r, a16i


def _s5_kernel(u_ref, wst_ref, kt_ref, wo_ref, a16r_ref, a16i_ref, h0r_ref, h0i_ref,
               pin_ref, pout_ref, y_ref, fr_ref, fi_ref, ut, yt, s_sc, hf_sc, hb_sc, *, nb, nc, gpb):
    gb = pl.program_id(2)
    seq = nc * CHUNK
    r = nb * nc
    sw = 2 * P_A
    n_cb = u_ref.shape[1] // LANES
    pieces = [b * seq + q * PERM_TILE for b in range(nb) for q in range(seq // PERM_TILE)]

    @pl.when(gb == 0)
    def _():
        for s in range(CHUNK):
            cols = []
            for cb in range(n_cb):
                rows = jnp.concatenate([u_ref[o + s * CHUNK:o + (s + 1) * CHUNK, cb * LANES:(cb + 1) * LANES]
                                        for o in pieces], axis=0)
                cols.append(rows.astype(F32).T.astype(BF16))
            ut[:, s * r:(s + 1) * r] = _dot(jnp.concatenate(cols, axis=0), pin_ref[...]).astype(BF16)

    fwd = (lax.broadcasted_iota(jnp.int32, (1, sw), 1)) < P_A

    def groups(it, carry):
        js = [it * S5_INTERLEAVE + i for i in range(S5_INTERLEAVE)]
        grows = [pl.multiple_of((gb * gpb + j) * SSM_GROUP, SSM_GROUP) for j in js]
        ds = [jnp.concatenate([ut[pl.ds(g, SSM_GROUP), s * r:(s + 1) * r] for s in range(CHUNK)], axis=0)
              for g in grows]
        for i, j in enumerate(js):
            s_t = _dot(wst_ref[j], ds[i])
            s_sc[i, 0] = s_t[:sw].T
            s_sc[i, 1] = s_t[sw:].T
        a_r = [a16r_ref[j] for j in js]
        a_i = [a16i_ref[j] for j in js]
        re = [h0r_ref[j] for j in js]
        im = [h0i_ref[j] for j in js]
        for k in range(nc):
            rf = pl.ds(k * nb, nb)
            rb = pl.ds((nc - 1 - k) * nb, nb)
            for i in range(S5_INTERLEAVE):
                hf_sc[i, 0, rf, :] = re[i]
                hf_sc[i, 1, rf, :] = im[i]
                hb_sc[i, 0, rb, :] = re[i]
                hb_sc[i, 1, rb, :] = im[i]
                xr = jnp.where(fwd, s_sc[i, 0, rf, :], s_sc[i, 0, rb, :])
                xi = jnp.where(fwd, s_sc[i, 1, rf, :], s_sc[i, 1, rb, :])
                re[i], im[i] = a_r[i] * re[i] - a_i[i] * im[i] + xr, a_r[i] * im[i] + a_i[i] * re[i] + xi
        for i, j in enumerate(js):
            fr_ref[j] = re[i]
            fi_ref[j] = im[i]
            hin_t = jnp.concatenate([jnp.where(fwd, hf_sc[i, 0], hb_sc[i, 0]).T,
                                     jnp.where(fwd, hf_sc[i, 1], hb_sc[i, 1]).T], axis=0)
            y_t = (_dot(kt_ref[j], ds[i]) + _dot(wo_ref[j], hin_t.astype(BF16))).astype(BF16)
            for t in range(CHUNK):
                yt[pl.ds(grows[i], SSM_GROUP), t * r:(t + 1) * r] = y_t[t * SSM_GROUP:(t + 1) * SSM_GROUP, :]
        return carry

    lax.fori_loop(0, gpb // S5_INTERLEAVE, groups, 0)

    @pl.when(gb == pl.num_programs(2) - 1)
    def _():
        for t in range(CHUNK):
            full = _dot(yt[:, t * r:(t + 1) * r], pout_ref[...])
            for cb in range(n_cb):
                rows = full[cb * LANES:(cb + 1) * LANES, :].T.astype(BF16)
                for idx, o in enumerate(pieces):
                    y_ref[o + t * CHUNK:o + (t + 1) * CHUNK, cb * LANES:(cb + 1) * LANES] = (
                        rows[idx * CHUNK:(idx + 1) * CHUNK, :])


def _s5_mix(u, ops, h0_re, h0_im, bsz, seq_len, nb, cw):
    wst_t, kt_t, wo_t, a16r, a16i = ops
    t, w = u.shape
    nc = seq_len // CHUNK
    r = nb * nc
    gpb = S5_GROUPS_PER_STEP
    n_row_tiles = bsz // nb
    n_col_tiles = w // cw
    gb_per_tile = cw // SSM_GROUP // gpb
    assert r % 128 == 0 and cw % (SSM_GROUP * gpb) == 0 and bsz % nb == 0

    def h0_layout(h0):
        return h0.astype(F32).transpose(2, 0, 1, 3).reshape(G_A, bsz, 2 * P_A)

    tile = pl.BlockSpec((nb * seq_len, cw), lambda i, c, g: (i, c))
    grp = lambda i, c, g: (c * gb_per_tile + g, 0, 0)
    wspec = pl.BlockSpec((gpb, SSM_TILE, SSM_TILE), grp)
    aspec = pl.BlockSpec((gpb, 1, 2 * P_A), grp)
    hspec = pl.BlockSpec((gpb, nb, 2 * P_A), lambda i, c, g: (c * gb_per_tile + g, i, 0))
    state = jax.ShapeDtypeStruct((G_A, bsz, 2 * P_A), F32)
    scratch = [pltpu.VMEM((cw, CHUNK * r), BF16)] * 2 + [pltpu.VMEM((S5_INTERLEAVE, 2, r, 2 * P_A), F32)] * 3
    src = lax.broadcasted_iota(jnp.int32, (r, r), 0)
    dst = lax.broadcasted_iota(jnp.int32, (r, r), 1)
    pin = (dst == (src % nc) * nb + src // nc).astype(BF16)
    pspec = pl.BlockSpec((r, r), lambda i, c, g: (0, 0))

    y, f_re, f_im = pl.pallas_call(
        functools.partial(_s5_kernel, nb=nb, nc=nc, gpb=gpb),
        grid=(n_row_tiles, n_col_tiles, gb_per_tile),
        in_specs=[tile, wspec, wspec, wspec, aspec, aspec, hspec, hspec, pspec, pspec],
        out_specs=[tile, hspec, hspec],
        out_shape=[jax.ShapeDtypeStruct((t, w), BF16), state, state],
        scratch_shapes=scratch,
        compiler_params=_cparams(3),
        name="s5_core",
    )(u, wst_t, kt_t, wo_t, a16r, a16i, h0_layout(h0_re), h0_layout(h0_im), pin, pin.T)

    def fin(f):
        return f.reshape(G_A, bsz, 2, P_A).transpose(1, 2, 0, 3)
    return y, fin(f_re), fin(f_im)


def _diff_lambda(lq1, lk1, lq2, lk2, lam_init):
    return (jnp.exp(jnp.sum(lq1[...] * lk1[...], axis=-1, keepdims=True))
            - jnp.exp(jnp.sum(lq2[...] * lk2[...], axis=-1, keepdims=True)) + lam_init)


def _attn_kernel(*refs, lam_init, cached):
    if cached:
        q_ref, kc_ref, kn_ref, vc_ref, vn_ref, szb_ref, lq1, lk1, lq2, lk2, g_ref, o_ref = refs
    else:
        q_ref, kn_ref, vn_ref, szb_ref, lq1, lk1, lq2, lk2, g_ref, o_ref = refs
    lam = _diff_lambda(lq1, lk1, lq2, lk2, lam_init)
    lq = q_ref.shape[0]
    hw = 2 * DH
    low = lax.broadcasted_iota(jnp.int32, (1, hw), 1) < DH
    zero = jnp.zeros((), BF16)
    ones = jnp.ones((kn_ref.shape[0], hw), BF16)
    for h in range(H_B):
        cols = slice(h * hw, (h + 1) * hw)
        qh = q_ref[:, cols]
        qs = jnp.concatenate([jnp.where(low, qh, zero), jnp.where(low, zero, qh)], axis=0)
        s = _dot_nt(qs, kn_ref[:, cols])
        if cached:
            s = jnp.concatenate([_dot_nt(qs, kc_ref[:, cols]), s], axis=1)
        e = jnp.exp2(s - jnp.max(s, axis=-1, keepdims=True)).astype(BF16)
        vn = jnp.concatenate([vn_ref[:, cols], ones], axis=1)
        if cached:
            lc = kc_ref.shape[0]
            vc = jnp.concatenate([vc_ref[:, cols], ones[:lc]], axis=1)
            oa = _dot(e[:, :lc], vc) + _dot(e[:, lc:], vn)
        else:
            oa = _dot(e, vn)
        on = oa[:, :hw] * (1.0 / oa[:, hw:])
        o = on[:lq] - lam * on[lq:]
        o = o * lax.rsqrt(jnp.mean(o * o, axis=-1, keepdims=True) + LN_EPS)
        o = o * g_ref[...] * (1.0 - lam_init)
        o_ref[:, cols] = (o * szb_ref[:, cols].astype(F32)).astype(BF16)


def _lam_specs(n_axes):
    zero = lambda *idx: (0, 0)
    return [pl.BlockSpec((1, DH), zero)] * 4 + [pl.BlockSpec((1, 2 * DH), zero)]


def _attention_prompt(q, k, v, szb, lam_vecs, subln, lam_init, bsz, seq_len):
    blk = pl.BlockSpec((seq_len, D_MODEL), lambda b: (b, 0))
    return pl.pallas_call(
        functools.partial(_attn_kernel, lam_init=lam_init, cached=False),
        grid=(bsz,),
        in_specs=[blk] * 4 + _lam_specs(1),
        out_specs=blk,
        out_shape=jax.ShapeDtypeStruct(q.shape, BF16),
        compiler_params=_cparams(1),
        name="diff_attention_prompt",
    )(q, k, v, szb, *lam_vecs, subln)


def _attention_sample(q, kc, kn, vc, vn, szb, lam_vecs, subln, lam_init, bsz, seq_len, tq):
    nq = seq_len // tq
    past = kc.shape[1]
    qblk = pl.BlockSpec((tq, D_MODEL), lambda b, i: (b * nq + i, 0))
    cblk = pl.BlockSpec((None, past, D_MODEL), lambda b, i: (b, 0, 0))
    nblk = pl.BlockSpec((seq_len, D_MODEL), lambda b, i: (b, 0))
    return pl.pallas_call(
        functools.partial(_attn_kernel, lam_init=lam_init, cached=True),
        grid=(bsz, nq),
        in_specs=[qblk, cblk, nblk, cblk, nblk, qblk] + _lam_specs(2),
        out_specs=qblk,
        out_shape=jax.ShapeDtypeStruct(q.shape, BF16),
        compiler_params=_cparams(2),
        name="diff_attention_sample",
    )(q, kc, kn, vc, vn, szb, *lam_vecs, subln)


def _even_out_kernel(ys_ref, sza_ref, yb_ref, x_ref, gate_ref, wglu_ref, bglu_ref, wout_ref, g_ref, b_ref,
                     perm_ref, o_ref, *, alpha):
    wa = ys_ref.shape[1]
    for r in range(x_ref.shape[0] // PERM_TILE):
        rows = slice(r * PERM_TILE, (r + 1) * PERM_TILE)
        ga = jax.nn.gelu(_dot(perm_ref[...], ys_ref[rows, :]))
        glu = jax.nn.sigmoid(_dot(ga.astype(BF16), wglu_ref[...]) + bglu_ref[...])
        ya = (ga * glu * sza_ref[rows, :].astype(F32)).astype(BF16)
        out = _dot(ya, wout_ref[:wa, :]) + _dot(yb_ref[rows, :], wout_ref[wa:, :])
        o_ref[rows, :] = _post_norm(x_ref[rows, :], gate_ref[...], out, g_ref[...], b_ref[...], alpha)


def _even_out_proj(ys, sza, yb, x2, mod3, w_glu_bf, b_glu, w_out_bf, ln_g, ln_b, layer, cond_of, tm, alpha):
    t, d = x2.shape
    assert tm % PERM_TILE == 0
    row = pl.BlockSpec((tm, d), lambda i: (i, 0))
    full = _resident
    return pl.pallas_call(
        functools.partial(_even_out_kernel, alpha=alpha),
        grid=(t // tm,),
        in_specs=[row, row, row, row, _mod_spec(layer, 2, cond_of), full(w_glu_bf.shape), full((1, d)),
                  full(w_out_bf.shape), full((1, d)), full((1, d)), full((PERM_TILE, PERM_TILE))],
        out_specs=row,
        out_shape=jax.ShapeDtypeStruct((t, d), F32),
        compiler_params=_cparams(1),
        name="even_out_proj",
    )(ys, sza, yb, x2, mod3, w_glu_bf, b_glu.reshape(1, d), w_out_bf, ln_g.reshape(1, d), ln_b.reshape(1, d),
      _chunk_transpose_matrix())


def _dft_mats(n, scale):
    k = np.arange(n, dtype=np.int64)
    ang = ((k[:, None] * k[None, :]) % n).astype(np.float64) * (2.0 * math.pi / n)
    return (jnp.asarray((np.cos(ang) * scale).astype(np.float32)).astype(BF16),
            jnp.asarray((np.sin(ang) * scale).astype(np.float32)).astype(BF16))


def _odd_in_kernel(x_ref, shift_ref, scale_ref, w_ref, cc_ref, sc_ref, uc_ref, us_ref, sz_ref):
    for r in range(x_ref.shape[0] // SUB_ROWS):
        rows = slice(r * SUB_ROWS, (r + 1) * SUB_ROWS)
        h = (_ln_rows(x_ref[rows, :]) * (1.0 + scale_ref[...]) + shift_ref[...]).astype(BF16)
        u = _dot(h, w_ref[:, :W_C]).astype(BF16)
        sz_ref[rows, :] = _silu(_dot(h, w_ref[:, W_C:])).astype(BF16)
        for g in range(NG_C):
            cols = slice(g * GC_C, (g + 1) * GC_C)
            uc_ref[rows, cols] = _dot(u[:, cols], cc_ref[...]).astype(BF16)
            us_ref[rows, cols] = _dot(u[:, cols], sc_ref[...]).astype(BF16)


def _odd_in_proj(x2, mod3, w_in_bf, cc, sc, layer, cond_of, tm):
    t, d = x2.shape
    assert tm % SUB_ROWS == 0
    row = pl.BlockSpec((tm, d), lambda i: (i, 0))
    wide = pl.BlockSpec((tm, W_C), lambda i: (i, 0))
    full = _resident
    wide_bf = jax.ShapeDtypeStruct((t, W_C), BF16)
    return pl.pallas_call(
        _odd_in_kernel,
        grid=(t // tm,),
        in_specs=[row, _mod_spec(layer, 0, cond_of), _mod_spec(layer, 1, cond_of), full(w_in_bf.shape),
                  full(cc.shape), full(sc.shape)],
        out_specs=[wide] * 3,
        out_shape=[wide_bf] * 3,
        compiler_params=_cparams(1),
        name="odd_in_proj",
    )(x2, mod3, mod3, w_in_bf, cc, sc)


def _odd_out_kernel(cl_ref, sl_ref, uc_ref, us_ref, sz_ref, x_ref, gate_ref, wf_ref, bf_ref, wo_ref, g_ref, b_ref,
                    o_ref, *, alpha, whole_seqs):
    seq_len = cl_ref.shape[1]
    if whole_seqs:
        parts = [_dot(cl_ref[...], uc_ref[r * seq_len:(r + 1) * seq_len, :])
                 - _dot(sl_ref[...], us_ref[r * seq_len:(r + 1) * seq_len, :]) for r in range(x_ref.shape[0] // seq_len)]
        mixed = jnp.concatenate([p.astype(BF16) for p in parts], axis=0)
    else:
        mixed = (_dot(cl_ref[...], uc_ref[...]) - _dot(sl_ref[...], us_ref[...])).astype(BF16)
    for r in range(x_ref.shape[0] // SUB_ROWS):
        rows = slice(r * SUB_ROWS, (r + 1) * SUB_ROWS)
        y = ((_dot(mixed[rows, :], wf_ref[...]) + bf_ref[...]) * sz_ref[rows, :].astype(F32)).astype(BF16)
        o_ref[rows, :] = _post_norm(x_ref[rows, :], gate_ref[...], _dot(y, wo_ref[...]), g_ref[...], b_ref[...],
                                    alpha)


def _odd_out_proj(uc, us, sz, x2, mod3, cl, sl, w_fno_bf, b_fno, w_out_bf, ln_g, ln_b, layer, cond_of,
                  bsz, seq_len, tl, alpha):
    t, d = x2.shape
    whole_seqs = tl >= seq_len
    full = _resident
    if whole_seqs:
        assert tl % seq_len == 0
        grid = (t // tl, 1)
        dft = _resident((seq_len, seq_len))
        seq = pl.BlockSpec((tl, W_C), lambda b, i: (b, 0))
        wide = seq
        row = pl.BlockSpec((tl, d), lambda b, i: (b, 0))
    else:
        assert seq_len % tl == 0 and tl % SUB_ROWS == 0
        nl = seq_len // tl
        grid = (bsz, nl)
        dft = pl.BlockSpec((tl, seq_len), lambda b, i: (i, 0))
        seq = pl.BlockSpec((seq_len, W_C), lambda b, i: (b, 0))
        wide = pl.BlockSpec((tl, W_C), lambda b, i: (b * nl + i, 0))
        row = pl.BlockSpec((tl, d), lambda b, i: (b * nl + i, 0))
    return pl.pallas_call(
        functools.partial(_odd_out_kernel, alpha=alpha, whole_seqs=whole_seqs),
        grid=grid,
        in_specs=[dft, dft, seq, seq, wide, row, _mod_spec(layer, 2, cond_of), full(w_fno_bf.shape),
                  full((1, W_C)), full(w_out_bf.shape), full((1, d)), full((1, d))],
        out_specs=row,
        out_shape=jax.ShapeDtypeStruct((t, d), F32),
        compiler_params=_cparams(2),
        name="odd_out_proj",
    )(cl, sl, uc, us, sz, x2, mod3, w_fno_bf, b_fno.reshape(1, W_C), w_out_bf, ln_g.reshape(1, d),
      ln_b.reshape(1, d))


def _rope_tables(seq_len):
    rows = seq_len // GRID_W
    row = jnp.repeat(jnp.arange(rows), GRID_W).astype(F32)
    col = jnp.tile(jnp.arange(GRID_W), rows).astype(F32)
    freqs = ROPE_BASE ** (-jnp.arange(ROT_FREQS, dtype=F32) / ROT_FREQS)
    ang = jnp.concatenate([row[:, None] * freqs, col[:, None] * freqs], axis=-1)
    cos, sin = jnp.cos(ang), jnp.sin(ang)
    cos128 = jnp.tile(cos, (1, 128 // ROT_HALF))
    sin128 = jnp.tile(jnp.concatenate([-sin, sin], axis=-1), (1, 128 // DH))
    return cos128, sin128


def kernel(x_prompt, x_sample, cache_k, cache_v, state_ssm_re, state_ssm_im, c, c_ctx, w_mod, b_mod, ln_g, ln_b, w_in_e, ssm_lam_re, ssm_lam_im, ssm_log_dt, ssm_b_re, ssm_b_im, ssm_c_re, ssm_c_im, ssm_d, w_glu, b_glu, lam_q1, lam_k1, lam_q2, lam_k2, subln_g, w_out_e, w_in_o, w_fno, b_fno, w_out_o):
    depth = w_mod.shape[0]
    bp_, lp, d = x_prompt.shape
    bs_, ls, _ = x_sample.shape
    past = cache_k.shape[2]
    alpha = (2 * depth) ** 0.25
    assert bs_ + 1 <= MOD_ROWS and d == D_MODEL

    cond8 = jnp.concatenate([c_ctx[None, :], c, jnp.zeros((MOD_ROWS - 1 - bs_, d), F32)], axis=0).astype(F32)
    mod3 = _modulation(cond8, w_mod, b_mod).reshape(depth * MOD_ROWS * 3, 1, d)

    tm = ROW_TILE
    cond_p = lambda *idx: 0
    cond_s_row = lambda i: 1 + i // (ls // tm)
    cond_s_grid = lambda b, i: 1 + b
    rope_tabs = _rope_tables(ls)
    xp = x_prompt.reshape(bp_ * lp, d)
    xs = x_sample.reshape(bs_ * ls, d)
    new_k, new_v, new_sr, new_si = [], [], [], []
    zeros_h0 = jnp.zeros((bp_, 2, G_A, P_A), F32)

    for layer in range(depth):
        if layer % 2 == 0:
            e = layer // 2
            lam_init = 0.8 - 0.6 * math.exp(-0.3 * layer)
            w_in_bf = w_in_e[e].astype(BF16)
            w_glu_bf = w_glu[e].astype(BF16)
            w_out_bf = w_out_e[e].astype(BF16)
            ops = _s5_operators(ssm_lam_re[e], ssm_lam_im[e], ssm_log_dt[e], ssm_b_re[e], ssm_b_im[e],
                                ssm_c_re[e], ssm_c_im[e], ssm_d[e])
            lam_vecs = [v[e].reshape(1, DH).astype(F32) for v in (lam_q1, lam_k1, lam_q2, lam_k2)]
            subln = subln_g[e].reshape(1, 2 * DH).astype(F32)

            u, sza, q, kb, vb, szb, kf, vf = _even_in_proj(xp, mod3, w_in_bf, layer, cond_p, tm, None, lp)
            new_k.append(kf.reshape(bp_, H_B, 2, DH, lp).transpose(0, 4, 1, 2, 3))
            new_v.append(vf.reshape(bp_, lp, H_B, 2 * DH))
            ys, s_re, s_im = _s5_mix(u, ops, zeros_h0, zeros_h0, bp_, lp, 2 * S5_TILE_ELEMS // (lp * d), d // 2)
            new_sr.append(s_re)
            new_si.append(s_im)
            yb = _attention_prompt(q, kb, vb, szb, lam_vecs, subln, lam_init, bp_, lp)
            xp = _even_out_proj(ys, sza, yb, xp, mod3, w_glu_bf, b_glu[e], w_out_bf, ln_g[layer], ln_b[layer],
                                layer, cond_p, tm, alpha)

            u, sza, q, kb, vb, szb = _even_in_proj(xs, mod3, w_in_bf, layer, cond_s_row, tm, rope_tabs, ls)
            ys, _, _ = _s5_mix(u, ops, state_ssm_re[:, e], state_ssm_im[:, e], bs_, ls, bs_,
                               S5_TILE_ELEMS // (bs_ * ls))
            kc = cache_k[:, e].reshape(bs_, past, d).astype(BF16)
            vc = cache_v[:, e].reshape(bs_, past, d).astype(BF16)
            yb = _attention_sample(q, kc, kb, vc, vb, szb, lam_vecs, subln, lam_init, bs_, ls, ATTN_Q_TILE)
            xs = _even_out_proj(ys, sza, yb, xs, mod3, w_glu_bf, b_glu[e], w_out_bf, ln_g[layer], ln_b[layer],
                                layer, cond_s_row, tm, alpha)
        else:
            o = layer // 2
            w_in_bf = w_in_o[o].astype(BF16)
            w_fno_bf = w_fno[o].astype(BF16)
            w_out_bf = w_out_o[o].astype(BF16)
            cc, sc = _dft_mats(GC_C, GC_C ** -0.5)
            for which in ("prompt", "sample"):
                if which == "prompt":
                    x2, cond_row, cond_grid, bsz, seq = xp, cond_p, cond_p, bp_, lp
                else:
                    x2, cond_row, cond_grid, bsz, seq = xs, cond_s_row, cond_s_grid, bs_, ls
                cl, sl = _dft_mats(seq, seq ** -0.5)
                uc, us, sz = _odd_in_proj(x2, mod3, w_in_bf, cc, sc, layer, cond_row, tm)
                x2 = _odd_out_proj(uc, us, sz, x2, mod3, cl, sl, w_fno_bf, b_fno[o], w_out_bf, ln_g[layer],
                                   ln_b[layer], layer, cond_grid, bsz, seq, tm, alpha)
                if which == "prompt":
                    xp = x2
                else:
                    xs = x2

    return (xp.reshape(bp_, lp, d), xs.reshape(bs_, ls, d), jnp.stack(new_k, axis=1), jnp.stack(new_v, axis=1),
            jnp.stack(new_sr, axis=1), jnp.stack(new_si, axis=1))
```

```python
import functools
import math

import jax
import jax.numpy as jnp
import numpy as np
from jax import lax
from jax.experimental import pallas as pl
from jax.experimental.pallas import tpu as pltpu

F32 = jnp.float32
BF16 = jnp.bfloat16

D_MODEL = 1024
GRID_W = 64
SSM_GROUP = 16
G_A = D_MODEL // SSM_GROUP
P_A = 64
DH = 64
H_B = D_MODEL // (2 * DH)
ROPE_BASE = 10000.0
ROT_HALF = DH // 2
ROT_FREQS = DH // 4
NG_C = 8
GC_C = 2 * D_MODEL // NG_C
W_C = 2 * D_MODEL
LN_EPS = 1e-5
LOG2_E = 1.4426950408889634
CHUNK = 16
SSM_TILE = CHUNK * SSM_GROUP
LANES = 128
PERM_TILE = CHUNK * CHUNK
SUB_ROWS = PERM_TILE
ROW_TILE = 2 * SUB_ROWS
ATTN_Q_TILE = 256
S5_GROUPS_PER_STEP = 16
S5_INTERLEAVE = 16
S5_TILE_ELEMS = 2048 * 1024
MOD_ROWS = 8
VMEM_LIMIT = 56 * 1024 * 1024


def _cparams(n_axes):
    return pltpu.CompilerParams(dimension_semantics=("arbitrary",) * n_axes, vmem_limit_bytes=VMEM_LIMIT)


def _ln_rows(x):
    mu = jnp.mean(x, axis=-1, keepdims=True)
    xc = x - mu
    var = jnp.mean(xc * xc, axis=-1, keepdims=True)
    return xc * lax.rsqrt(var + LN_EPS)


def _silu(z):
    return z * jax.nn.sigmoid(z)


def _dot(a, b):
    return jnp.dot(a, b, preferred_element_type=F32)


def _dot_nt(a, b):
    return lax.dot_general(a, b, (((1,), (1,)), ((), ())), preferred_element_type=F32)


def _post_norm(x, gate, out, g, b, alpha):
    return _ln_rows(alpha * x + gate * out) * g + b


def _mod_kernel(c_ref, w_ref, b_ref, o_ref):
    c = _silu(c_ref[...]).astype(BF16)
    o_ref[...] = _dot(c, w_ref[...].astype(BF16)) + b_ref[...]


def _modulation(cond8, w_mod, b_mod):
    depth, d, n3 = w_mod.shape
    tn = 1024
    return pl.pallas_call(
        _mod_kernel,
        grid=(depth, n3 // tn),
        in_specs=[
            pl.BlockSpec((MOD_ROWS, d), lambda l, j: (0, 0)),
            pl.BlockSpec((None, d, tn), lambda l, j: (l, 0, j)),
            pl.BlockSpec((None, 1, tn), lambda l, j: (l, 0, j)),
        ],
        out_specs=pl.BlockSpec((None, MOD_ROWS, tn), lambda l, j: (l, 0, j)),
        out_shape=jax.ShapeDtypeStruct((depth, MOD_ROWS, n3), F32),
        compiler_params=_cparams(2),
        name="modulation",
    )(cond8, w_mod, b_mod.reshape(depth, 1, n3))


def _resident(shape):
    return pl.BlockSpec(shape, lambda *idx: (0,) * len(shape), pipeline_mode=pl.Buffered(1))


def _mod_spec(layer, part, cond_of):
    return pl.BlockSpec((None, 1, D_MODEL), lambda *idx: ((layer * MOD_ROWS + cond_of(*idx)) * 3 + part, 0, 0))


def _rope(x, cos, sin_signed, first_half):
    blocks = []
    for hh in range(x.shape[1] // 128):
        b = x[:, hh * 128:(hh + 1) * 128]
        partner = jnp.where(first_half, pltpu.roll(b, 128 - ROT_HALF, 1), pltpu.roll(b, ROT_HALF, 1))
        blocks.append(b * cos + partner * sin_signed)
    return jnp.concatenate(blocks, axis=1)


def _chunk_transpose_matrix():
    i = lax.broadcasted_iota(jnp.int32, (PERM_TILE, PERM_TILE), 0)
    j = lax.broadcasted_iota(jnp.int32, (PERM_TILE, PERM_TILE), 1)
    return (j == (i % CHUNK) * CHUNK + i // CHUNK).astype(BF16)


def _even_in_kernel(*refs, rope):
    if rope:
        (x_ref, shift_ref, scale_ref, w_ref, perm_ref, cos_ref, sin_ref,
         u_ref, sza_ref, q_ref, kb_ref, vb_ref, szb_ref) = refs
    else:
        (x_ref, shift_ref, scale_ref, w_ref, perm_ref, u_ref, sza_ref, q_ref, kb_ref, vb_ref, szb_ref,
         kf_ref, vf_ref) = refs
    w = D_MODEL
    for r in range(x_ref.shape[0] // PERM_TILE):
        rows = slice(r * PERM_TILE, (r + 1) * PERM_TILE)
        h = (_ln_rows(x_ref[rows, :]) * (1.0 + scale_ref[...]) + shift_ref[...]).astype(BF16)

        def proj(j):
            return _dot(h, w_ref[:, j * w:(j + 1) * w])

        u_ref[rows, :] = _dot(perm_ref[...], proj(0).astype(BF16)).astype(BF16)
        sza_ref[rows, :] = _silu(proj(1)).astype(BF16)
        q = proj(2) * (DH ** -0.5 * LOG2_E)
        k = proj(3)
        v = proj(4)
        if rope:
            lane = lax.broadcasted_iota(jnp.int32, (1, 128), 1)
            first_half = (lane % DH) < ROT_HALF
            q = _rope(q, cos_ref[rows, :], sin_ref[rows, :], first_half)
            kb_ref[rows, :] = _rope(k, cos_ref[rows, :], sin_ref[rows, :], first_half).astype(BF16)
        else:
            kf_ref[r] = k.T
            vf_ref[rows, :] = v
            kb_ref[rows, :] = k.astype(BF16)
        q_ref[rows, :] = q.astype(BF16)
        vb_ref[rows, :] = v.astype(BF16)
        szb_ref[rows, :] = _silu(proj(5)).astype(BF16)


def _even_in_proj(x2, mod3, w_in_bf, layer, cond_of, tm, rope_tabs, seq_len):
    t, d = x2.shape
    n = w_in_bf.shape[1]
    assert tm % PERM_TILE == 0
    row = pl.BlockSpec((tm, d), lambda i: (i, 0))
    in_specs = [row, _mod_spec(layer, 0, cond_of), _mod_spec(layer, 1, cond_of),
                _resident((d, n)), _resident((PERM_TILE, PERM_TILE))]
    args = [x2, mod3, mod3, w_in_bf, _chunk_transpose_matrix()]
    bf = jax.ShapeDtypeStruct((t, d), BF16)
    out_shape = [bf] * 6
    out_specs = [row] * 6
    if rope_tabs is not None:
        tiles_per_seq = seq_len // tm
        tab = pl.BlockSpec((tm, 128), lambda i: (i % tiles_per_seq, 0))
        in_specs += [tab, tab]
        args += list(rope_tabs)
    else:
        assert seq_len == PERM_TILE
        out_shape += [jax.ShapeDtypeStruct((t // seq_len, d, seq_len), F32), jax.ShapeDtypeStruct((t, d), F32)]
        out_specs += [pl.BlockSpec((tm // seq_len, d, seq_len), lambda i: (i, 0, 0)), row]
    return pl.pallas_call(
        functools.partial(_even_in_kernel, rope=rope_tabs is not None),
        grid=(t // tm,),
        in_specs=in_specs,
        out_specs=out_specs,
        out_shape=out_shape,
        compiler_params=_cparams(1),
        name="even_in_proj",
    )(*args)


BUILD_GROUPS = 4


def _s5_build_kernel(q_ref, b_ref, cn_ref, ccr_ref, cci_ref, ecr_ref, eci_ref, d_ref,
                     rep_ref, til_ref, wst_ref, kt_ref, wo_ref):
    def split(x):
        hi = x.astype(BF16)
        return hi, (x - hi.astype(F32)).astype(BF16)

    def hdot(a, b):
        (ah, al), (bh, bl) = split(a), split(b)
        return _dot(ah, bh) + _dot(ah, bl) + _dot(al, bh)

    lane = lax.broadcasted_iota(jnp.int32, (SSM_GROUP, SSM_TILE), 1)
    sub = lax.broadcasted_iota(jnp.int32, (SSM_GROUP, SSM_TILE), 0)
    zeros = jnp.zeros((SSM_GROUP, SSM_TILE), F32)
    rows4 = 4 * P_A
    def expand(x_t, sel):
        tn = lambda a: lax.dot_general(a, sel, (((0,), (0,)), ((), ())), preferred_element_type=F32)
        hi, lo = split(x_t)
        return tn(hi) + tn(lo)

    for g in range(BUILD_GROUPS):
        qe = expand(q_ref[g], rep_ref[...])
        be = expand(b_ref[g], til_ref[...])
        part = lambda x, k: x[k * P_A:(k + 1) * P_A]
        st_re = [part(qe, d) * part(be, d) - part(qe, 2 + d) * part(be, 2 + d) for d in range(2)]
        st_im = [part(qe, d) * part(be, 2 + d) + part(qe, 2 + d) * part(be, d) for d in range(2)]
        wst_ref[g] = jnp.concatenate([st_re[0], st_re[1], st_im[0], st_im[1]], axis=0).astype(BF16)
        klag = [hdot(cn_ref[d, g], jnp.concatenate([st_re[d], st_im[d]], axis=0)) for d in range(2)]
        fwd_p = jnp.concatenate([klag[0], zeros], axis=1)
        bwd_p = jnp.concatenate([zeros, klag[1]], axis=1)
        ccr, cci = ccr_ref[g], cci_ref[g]
        d_lanes = jnp.broadcast_to(d_ref[g], (SSM_GROUP, SSM_TILE))
        for t in range(CHUNK):
            lo_f = SSM_GROUP * (CHUNK - 1 - t)
            lo_b = SSM_TILE - SSM_GROUP * t
            skip = jnp.where(lane == t * SSM_GROUP + sub, d_lanes, 0.0)
            rows = slice(t * SSM_GROUP, (t + 1) * SSM_GROUP)
            kt_ref[g, rows, :] = (fwd_p[:, lo_f:lo_f + SSM_TILE] + bwd_p[:, lo_b:lo_b + SSM_TILE] + skip).astype(BF16)
            er, ei = ecr_ref[g, t:t + 1, :], eci_ref[g, t:t + 1, :]
            wo_ref[g, rows, :] = jnp.concatenate([ccr * er - cci * ei, -(ccr * ei + cci * er)], axis=1).astype(BF16)


def _s5_operators(lam_re, lam_im, log_dt, b_re, b_im, c_re, c_im, d_skip):
    lr, li = lam_re.astype(F32), lam_im.astype(F32)
    dt = jnp.exp(log_dt.astype(F32))[..., None]
    mag = jnp.exp(lr * dt)
    ar = mag * jnp.cos(li * dt)
    ai = mag * jnp.sin(li * dt)
    den = lr * lr + li * li
    fr = ((ar - 1.0) * lr + ai * li) / den
    fi = (ai * lr - (ar - 1.0) * li) / den
    br, bi = b_re.astype(F32), b_im.astype(F32)
    bbr = fr[..., None] * br - fi[..., None] * bi
    bbi = fr[..., None] * bi + fi[..., None] * br
    cr, ci = c_re.astype(F32), c_im.astype(F32)
    e_pow = jnp.arange(CHUNK + 1, dtype=F32)[:, None, None, None]
    pmag = jnp.exp(e_pow * (lr * dt))
    pr = pmag * jnp.cos(e_pow * (li * dt))
    pi = pmag * jnp.sin(e_pow * (li * dt))

    def by_position(p):
        return jnp.stack([p[:CHUNK, 0][::-1], p[:CHUNK, 1]], axis=0).transpose(0, 2, 1, 3)
    qr, qi = by_position(pr), by_position(pi)

    lane_sn = np.arange(SSM_TILE)
    rep = jnp.asarray(lane_sn[None, :] // SSM_GROUP == np.arange(CHUNK)[:, None], dtype=BF16)
    til = jnp.asarray(lane_sn[None, :] % SSM_GROUP == np.arange(SSM_GROUP)[:, None], dtype=BF16)

    def by_output(p):
        return jnp.concatenate([p[1:, 0], p[1:, 1][::-1]], axis=-1).transpose(1, 0, 2)
    cat_c = lambda c: jnp.concatenate([c[0], c[1]], axis=-1)

    def stack_lanes(re, im):
        return jnp.concatenate([re[0], re[1], im[0], im[1]], axis=-1)
    q_all = stack_lanes(qr, qi)
    b_all = stack_lanes(bbr.transpose(0, 1, 3, 2), bbi.transpose(0, 1, 3, 2))
    c_neg = jnp.concatenate([cr, -ci], axis=-1)

    gb = BUILD_GROUPS
    narrow = pl.BlockSpec((gb, SSM_GROUP, 4 * P_A), lambda i: (i, 0, 0))
    expand = pl.BlockSpec((SSM_GROUP, SSM_TILE), lambda i: (0, 0))
    cspec = pl.BlockSpec((2, gb, SSM_GROUP, 2 * P_A), lambda i: (0, i, 0, 0))
    half = pl.BlockSpec((gb, SSM_GROUP, 2 * P_A), lambda i: (i, 0, 0))
    dspec = pl.BlockSpec((gb, SSM_GROUP, 1), lambda i: (i, 0, 0))
    ospec = pl.BlockSpec((gb, SSM_TILE, SSM_TILE), lambda i: (i, 0, 0))
    oshape = jax.ShapeDtypeStruct((G_A, SSM_TILE, SSM_TILE), BF16)
    wst_t, kt_t, wo_t = pl.pallas_call(
        _s5_build_kernel,
        grid=(G_A // gb,),
        in_specs=[narrow] * 2 + [cspec] + [half] * 4 + [dspec, expand, expand],
        out_specs=[ospec] * 3,
        out_shape=[oshape] * 3,
        compiler_params=_cparams(1),
        name="s5_build_operators",
    )(q_all, b_all, c_neg, cat_c(cr), cat_c(ci),
      by_output(pr), by_output(pi), d_skip.astype(F32).reshape(G_A, SSM_GROUP, 1), rep, til)

    a16r = jnp.concatenate([pr[CHUNK, 0], pr[CHUNK, 1]], axis=-1)[:, None, :]
    a16i = jnp.concatenate([pi[CHUNK, 0], pi[CHUNK, 1]], axis=-1)[:, None, :]
    return wst_t, kt_t, wo_t, a16r, a16i


def _s5_kernel(u_ref, wst_ref, kt_ref, wo_ref, a16r_ref, a16i_ref, h0r_ref, h0i_ref,
               pin_ref, pout_ref, y_ref, fr_ref, fi_ref, ut, yt, s_sc, hf_sc, hb_sc, *, nb, nc, gpb):
    gb = pl.program_id(2)
    seq = nc * CHUNK
    r = nb * nc
    sw = 2 * P_A
    n_cb = u_ref.shape[1] // LANES
    pieces = [b * seq + q * PERM_TILE for b in range(nb) for q in range(seq // PERM_TILE)]

    @pl.when(gb == 0)
    def _():
        for s in range(CHUNK):
            cols = []
            for cb in range(n_cb):
                rows = jnp.concatenate([u_ref[o + s * CHUNK:o + (s + 1) * CHUNK, cb * LANES:(cb + 1) * LANES]
                                        for o in pieces], axis=0)
                cols.append(rows.astype(F32).T.astype(BF16))
            ut[:, s * r:(s + 1) * r] = _dot(jnp.concatenate(cols, axis=0), pin_ref[...]).astype(BF16)

    fwd = (lax.broadcasted_iota(jnp.int32, (1, sw), 1)) < P_A

    def groups(it, carry):
        js = [it * S5_INTERLEAVE + i for i in range(S5_INTERLEAVE)]
        grows = [pl.multiple_of((gb * gpb + j) * SSM_GROUP, SSM_GROUP) for j in js]
        ds = [jnp.concatenate([ut[pl.ds(g, SSM_GROUP), s * r:(s + 1) * r] for s in range(CHUNK)], axis=0)
              for g in grows]
        for i, j in enumerate(js):
            s_t = _dot(wst_ref[j], ds[i])
            s_sc[i, 0] = s_t[:sw].T
            s_sc[i, 1] = s_t[sw:].T
        a_r = [a16r_ref[j] for j in js]
        a_i = [a16i_ref[j] for j in js]
        re = [h0r_ref[j] for j in js]
        im = [h0i_ref[j] for j in js]
        for k in range(nc):
            rf = pl.ds(k * nb, nb)
            rb = pl.ds((nc - 1 - k) * nb, nb)
            for i in range(S5_INTERLEAVE):
                hf_sc[i, 0, rf, :] = re[i]
                hf_sc[i, 1, rf, :] = im[i]
                hb_sc[i, 0, rb, :] = re[i]
                hb_sc[i, 1, rb, :] = im[i]
                xr = jnp.where(fwd, s_sc[i, 0, rf, :], s_sc[i, 0, rb, :])
                xi = jnp.where(fwd, s_sc[i, 1, rf, :], s_sc[i, 1, rb, :])
                re[i], im[i] = a_r[i] * re[i] - a_i[i] * im[i] + xr, a_r[i] * im[i] + a_i[i] * re[i] + xi
        for i, j in enumerate(js):
            fr_ref[j] = re[i]
            fi_ref[j] = im[i]
            hin_t = jnp.concatenate([jnp.where(fwd, hf_sc[i, 0], hb_sc[i, 0]).T,
                                     jnp.where(fwd, hf_sc[i, 1], hb_sc[i, 1]).T], axis=0)
            y_t = (_dot(kt_ref[j], ds[i]) + _dot(wo_ref[j], hin_t.astype(BF16))).astype(BF16)
            for t in range(CHUNK):
                yt[pl.ds(grows[i], SSM_GROUP), t * r:(t + 1) * r] = y_t[t * SSM_GROUP:(t + 1) * SSM_GROUP, :]
        return carry

    lax.fori_loop(0, gpb // S5_INTERLEAVE, groups, 0)

    @pl.when(gb == pl.num_programs(2) - 1)
    def _():
        for t in range(CHUNK):
            full = _dot(yt[:, t * r:(t + 1) * r], pout_ref[...])
            for cb in range(n_cb):
                rows = full[cb * LANES:(cb + 1) * LANES, :].T.astype(BF16)
                for idx, o in enumerate(pieces):
                    y_ref[o + t * CHUNK:o + (t + 1) * CHUNK, cb * LANES:(cb + 1) * LANES] = (
                        rows[idx * CHUNK:(idx + 1) * CHUNK, :])


def _s5_mix(u, ops, h0_re, h0_im, bsz, seq_len, nb, cw):
    wst_t, kt_t, wo_t, a16r, a16i = ops
    t, w = u.shape
    nc = seq_len // CHUNK
    r = nb * nc
    gpb = S5_GROUPS_PER_STEP
    n_row_tiles = bsz // nb
    n_col_tiles = w // cw
    gb_per_tile = cw // SSM_GROUP // gpb
    assert r % 128 == 0 and cw % (SSM_GROUP * gpb) == 0 and bsz % nb == 0

    def h0_layout(h0):
        return h0.astype(F32).transpose(2, 0, 1, 3).reshape(G_A, bsz, 2 * P_A)

    tile = pl.BlockSpec((nb * seq_len, cw), lambda i, c, g: (i, c))
    grp = lambda i, c, g: (c * gb_per_tile + g, 0, 0)
    wspec = pl.BlockSpec((gpb, SSM_TILE, SSM_TILE), grp)
    aspec = pl.BlockSpec((gpb, 1, 2 * P_A), grp)
    hspec = pl.BlockSpec((gpb, nb, 2 * P_A), lambda i, c, g: (c * gb_per_tile + g, i, 0))
    state = jax.ShapeDtypeStruct((G_A, bsz, 2 * P_A), F32)
    scratch = [pltpu.VMEM((cw, CHUNK * r), BF16)] * 2 + [pltpu.VMEM((S5_INTERLEAVE, 2, r, 2 * P_A), F32)] * 3
    src = lax.broadcasted_iota(jnp.int32, (r, r), 0)
    dst = lax.broadcasted_iota(jnp.int32, (r, r), 1)
    pin = (dst == (src % nc) * nb + src // nc).astype(BF16)
    pspec = pl.BlockSpec((r, r), lambda i, c, g: (0, 0))

    y, f_re, f_im = pl.pallas_call(
        functools.partial(_s5_kernel, nb=nb, nc=nc, gpb=gpb),
        grid=(n_row_tiles, n_col_tiles, gb_per_tile),
        in_specs=[tile, wspec, wspec, wspec, aspec, aspec, hspec, hspec, pspec, pspec],
        out_specs=[tile, hspec, hspec],
        out_shape=[jax.ShapeDtypeStruct((t, w), BF16), state, state],
        scratch_shapes=scratch,
        compiler_params=_cparams(3),
        name="s5_core",
    )(u, wst_t, kt_t, wo_t, a16r, a16i, h0_layout(h0_re), h0_layout(h0_im), pin, pin.T)

    def fin(f):
        return f.reshape(G_A, bsz, 2, P_A).transpose(1, 2, 0, 3)
    return y, fin(f_re), fin(f_im)


def _diff_lambda(lq1, lk1, lq2, lk2, lam_init):
    return (jnp.exp(jnp.sum(lq1[...] * lk1[...], axis=-1, keepdims=True))
            - jnp.exp(jnp.sum(lq2[...] * lk2[...], axis=-1, keepdims=True)) + lam_init)


def _attn_kernel(*refs, lam_init, cached):
    if cached:
        q_ref, kc_ref, kn_ref, vc_ref, vn_ref, szb_ref, lq1, lk1, lq2, lk2, g_ref, o_ref = refs
    else:
        q_ref, kn_ref, vn_ref, szb_ref, lq1, lk1, lq2, lk2, g_ref, o_ref = refs
    lam = _diff_lambda(lq1, lk1, lq2, lk2, lam_init)
    lq = q_ref.shape[0]
    hw = 2 * DH
    low = lax.broadcasted_iota(jnp.int32, (1, hw), 1) < DH
    zero = jnp.zeros((), BF16)
    ones = jnp.ones((kn_ref.shape[0], hw), BF16)
    for h in range(H_B):
        cols = slice(h * hw, (h + 1) * hw)
        qh = q_ref[:, cols]
        qs = jnp.concatenate([jnp.where(low, qh, zero), jnp.where(low, zero, qh)], axis=0)
        s = _dot_nt(qs, kn_ref[:, cols])
        if cached:
            s = jnp.concatenate([_dot_nt(qs, kc_ref[:, cols]), s], axis=1)
        e = jnp.exp2(s - jnp.max(s, axis=-1, keepdims=True)).astype(BF16)
        vn = jnp.concatenate([vn_ref[:, cols], ones], axis=1)
        if cached:
            lc = kc_ref.shape[0]
            vc = jnp.concatenate([vc_ref[:, cols], ones[:lc]], axis=1)
            oa = _dot(e[:, :lc], vc) + _dot(e[:, lc:], vn)
        else:
            oa = _dot(e, vn)
        on = oa[:, :hw] * (1.0 / oa[:, hw:])
        o = on[:lq] - lam * on[lq:]
        o = o * lax.rsqrt(jnp.mean(o * o, axis=-1, keepdims=True) + LN_EPS)
        o = o * g_ref[...] * (1.0 - lam_init)
        o_ref[:, cols] = (o * szb_ref[:, cols].astype(F32)).astype(BF16)


def _lam_specs(n_axes):
    zero = lambda *idx: (0, 0)
    return [pl.BlockSpec((1, DH), zero)] * 4 + [pl.BlockSpec((1, 2 * DH), zero)]


def _attention_prompt(q, k, v, szb, lam_vecs, subln, lam_init, bsz, seq_len):
    blk = pl.BlockSpec((seq_len, D_MODEL), lambda b: (b, 0))
    return pl.pallas_call(
        functools.partial(_attn_kernel, lam_init=lam_init, cached=False),
        grid=(bsz,),
        in_specs=[blk] * 4 + _lam_specs(1),
        out_specs=blk,
        out_shape=jax.ShapeDtypeStruct(q.shape, BF16),
        compiler_params=_cparams(1),
        name="diff_attention_prompt",
    )(q, k, v, szb, *lam_vecs, subln)


def _attention_sample(q, kc, kn, vc, vn, szb, lam_vecs, subln, lam_init, bsz, seq_len, tq):
    nq = seq_len // tq
    past = kc.shape[1]
    qblk = pl.BlockSpec((tq, D_MODEL), lambda b, i: (b * nq + i, 0))
    cblk = pl.BlockSpec((None, past, D_MODEL), lambda b, i: (b, 0, 0))
    nblk = pl.BlockSpec((seq_len, D_MODEL), lambda b, i: (b, 0))
    return pl.pallas_call(
        functools.partial(_attn_kernel, lam_init=lam_init, cached=True),
        grid=(bsz, nq),
        in_specs=[qblk, cblk, nblk, cblk, nblk, qblk] + _lam_specs(2),
        out_specs=qblk,
        out_shape=jax.ShapeDtypeStruct(q.shape, BF16),
        compiler_params=_cparams(2),
        name="diff_attention_sample",
    )(q, kc, kn, vc, vn, szb, *lam_vecs, subln)


def _even_out_kernel(ys_ref, sza_ref, yb_ref, x_ref, gate_ref, wglu_ref, bglu_ref, wout_ref, g_ref, b_ref,
                     perm_ref, *rest, alpha):
    o_ref = rest[0] if len(rest) == 1 else rest[5]
    wa = ys_ref.shape[1]
    for r in range(x_ref.shape[0] // PERM_TILE):
        rows = slice(r * PERM_TILE, (r + 1) * PERM_TILE)
        ga = jax.nn.gelu(_dot(perm_ref[...], ys_ref[rows, :]))
        glu = jax.nn.sigmoid(_dot(ga.astype(BF16), wglu_ref[...]) + bglu_ref[...])
        ya = (ga * glu * sza_ref[rows, :].astype(F32)).astype(BF16)
        out = _dot(ya, wout_ref[:wa, :]) + _dot(yb_ref[rows, :], wout_ref[wa:, :])
        x_new = _post_norm(x_ref[rows, :], gate_ref[...], out, g_ref[...], b_ref[...], alpha)
        o_ref[rows, :] = x_new
        if len(rest) > 1:
            _odd_in_rows(x_new, rows, *rest[:5], *rest[6:])


def _even_out_proj(ys, sza, yb, x2, mod3, w_glu_bf, b_glu, w_out_bf, ln_g, ln_b, layer, cond_of, tm, alpha,
                   next_odd=None):
    t, d = x2.shape
    assert tm % PERM_TILE == 0 and PERM_TILE == SUB_ROWS
    row = pl.BlockSpec((tm, d), lambda i: (i, 0))
    full = _resident
    in_specs = [row, row, row, row, _mod_spec(layer, 2, cond_of), full(w_glu_bf.shape), full((1, d)),
                full(w_out_bf.shape), full((1, d)), full((1, d)), full((PERM_TILE, PERM_TILE))]
    args = [ys, sza, yb, x2, mod3, w_glu_bf, b_glu.reshape(1, d), w_out_bf, ln_g.reshape(1, d), ln_b.reshape(1, d),
            _chunk_transpose_matrix()]
    out_specs, out_shape = [row], [jax.ShapeDtypeStruct((t, d), F32)]
    if next_odd is not None:
        w_in_bf, cc, sc = next_odd
        wide = pl.BlockSpec((tm, W_C), lambda i: (i, 0))
        in_specs += [_mod_spec(layer + 1, 0, cond_of), _mod_spec(layer + 1, 1, cond_of), full(w_in_bf.shape),
                     full(cc.shape), full(sc.shape)]
        args += [mod3, mod3, w_in_bf, cc, sc]
        out_specs += [wide] * 3
        out_shape += [jax.ShapeDtypeStruct((t, W_C), BF16)] * 3
    res = pl.pallas_call(
        functools.partial(_even_out_kernel, alpha=alpha),
        grid=(t // tm,),
        in_specs=in_specs,
        out_specs=out_specs,
        out_shape=out_shape,
        compiler_params=_cparams(1),
        name="even_out_proj",
    )(*args)
    return res[0] if next_odd is None else tuple(res)


def _dft_mats(n, scale):
    k = np.arange(n, dtype=np.int64)
    ang = ((k[:, None] * k[None, :]) % n).astype(np.float64) * (2.0 * math.pi / n)
    return (jnp.asarray((np.cos(ang) * scale).astype(np.float32)).astype(BF16),
            jnp.asarray((np.sin(ang) * scale).astype(np.float32)).astype(BF16))


def _odd_in_rows(x, rows, shift_ref, scale_ref, w_ref, cc_ref, sc_ref, uc_ref, us_ref, sz_ref):
    h = (_ln_rows(x) * (1.0 + scale_ref[...]) + shift_ref[...]).astype(BF16)
    u = _dot(h, w_ref[:, :W_C]).astype(BF16)
    sz_ref[rows, :] = _silu(_dot(h, w_ref[:, W_C:])).astype(BF16)
    for g in range(NG_C):
        cols = slice(g * GC_C, (g + 1) * GC_C)
        uc_ref[rows, cols] = _dot(u[:, cols], cc_ref[...]).astype(BF16)
        us_ref[rows, cols] = _dot(u[:, cols], sc_ref[...]).astype(BF16)


def _odd_out_kernel(cl_ref, sl_ref, uc_ref, us_ref, sz_ref, x_ref, gate_ref, wf_ref, bf_ref, wo_ref, g_ref, b_ref,
                    o_ref, *, alpha, whole_seqs):
    seq_len = cl_ref.shape[1]
    sub = seq_len if whole_seqs else SUB_ROWS
    for r in range(x_ref.shape[0] // sub):
        rows = slice(r * sub, (r + 1) * sub)
        if whole_seqs:
            mixed = _dot(cl_ref[...], uc_ref[rows, :]) - _dot(sl_ref[...], us_ref[rows, :])
        else:
            mixed = _dot(cl_ref[rows, :], uc_ref[...]) - _dot(sl_ref[rows, :], us_ref[...])
        y = ((_dot(mixed.astype(BF16), wf_ref[...]) + bf_ref[...]) * sz_ref[rows, :].astype(F32)).astype(BF16)
        o_ref[rows, :] = _post_norm(x_ref[rows, :], gate_ref[...], _dot(y, wo_ref[...]), g_ref[...], b_ref[...],
                                    alpha)


def _odd_out_proj(uc, us, sz, x2, mod3, cl, sl, w_fno_bf, b_fno, w_out_bf, ln_g, ln_b, layer, cond_of,
                  bsz, seq_len, tl, alpha):
    t, d = x2.shape
    whole_seqs = tl >= seq_len
    full = _resident
    if whole_seqs:
        assert tl % seq_len == 0
        grid = (t // tl, 1)
        dft = _resident((seq_len, seq_len))
        seq = pl.BlockSpec((tl, W_C), lambda b, i: (b, 0))
        wide = seq
        row = pl.BlockSpec((tl, d), lambda b, i: (b, 0))
    else:
        assert seq_len % tl == 0 and tl % SUB_ROWS == 0
        nl = seq_len // tl
        grid = (bsz, nl)
        dft = pl.BlockSpec((tl, seq_len), lambda b, i: (i, 0))
        seq = pl.BlockSpec((seq_len, W_C), lambda b, i: (b, 0))
        wide = pl.BlockSpec((tl, W_C), lambda b, i: (b * nl + i, 0))
        row = pl.BlockSpec((tl, d), lambda b, i: (b * nl + i, 0))
    return pl.pallas_call(
        functools.partial(_odd_out_kernel, alpha=alpha, whole_seqs=whole_seqs),
        grid=grid,
        in_specs=[dft, dft, seq, seq, wide, row, _mod_spec(layer, 2, cond_of), full(w_fno_bf.shape),
                  full((1, W_C)), full(w_out_bf.shape), full((1, d)), full((1, d))],
        out_specs=row,
        out_shape=jax.ShapeDtypeStruct((t, d), F32),
        compiler_params=_cparams(2),
        name="odd_out_proj",
    )(cl, sl, uc, us, sz, x2, mod3, w_fno_bf, b_fno.reshape(1, W_C), w_out_bf, ln_g.reshape(1, d),
      ln_b.reshape(1, d))


def _rope_tables(seq_len):
    rows = seq_len // GRID_W
    row = jnp.repeat(jnp.arange(rows), GRID_W).astype(F32)
    col = jnp.tile(jnp.arange(GRID_W), rows).astype(F32)
    freqs = ROPE_BASE ** (-jnp.arange(ROT_FREQS, dtype=F32) / ROT_FREQS)
    ang = jnp.concatenate([row[:, None] * freqs, col[:, None] * freqs], axis=-1)
    cos, sin = jnp.cos(ang), jnp.sin(ang)
    cos128 = jnp.tile(cos, (1, 128 // ROT_HALF))
    sin128 = jnp.tile(jnp.concatenate([-sin, sin], axis=-1), (1, 128 // DH))
    return cos128, sin128


def kernel(x_prompt, x_sample, cache_k, cache_v, state_ssm_re, state_ssm_im, c, c_ctx, w_mod, b_mod, ln_g, ln_b, w_in_e, ssm_lam_re, ssm_lam_im, ssm_log_dt, ssm_b_re, ssm_b_im, ssm_c_re, ssm_c_im, ssm_d, w_glu, b_glu, lam_q1, lam_k1, lam_q2, lam_k2, subln_g, w_out_e, w_in_o, w_fno, b_fno, w_out_o):
    depth = w_mod.shape[0]
    bp_, lp, d = x_prompt.shape
    bs_, ls, _ = x_sample.shape
    past = cache_k.shape[2]
    alpha = (2 * depth) ** 0.25
    assert bs_ + 1 <= MOD_ROWS and d == D_MODEL

    cond8 = jnp.concatenate([c_ctx[None, :], c, jnp.zeros((MOD_ROWS - 1 - bs_, d), F32)], axis=0).astype(F32)
    mod3 = _modulation(cond8, w_mod, b_mod).reshape(depth * MOD_ROWS * 3, 1, d)

    tm = ROW_TILE
    cond_p = lambda *idx: 0
    cond_s_row = lambda i: 1 + i // (ls // tm)
    cond_s_grid = lambda b, i: 1 + b
    rope_tabs = _rope_tables(ls)
    xp = x_prompt.reshape(bp_ * lp, d)
    xs = x_sample.reshape(bs_ * ls, d)
    new_k, new_v, new_sr, new_si = [], [], [], []
    zeros_h0 = jnp.zeros((bp_, 2, G_A, P_A), F32)

    for layer in range(depth):
        if layer % 2 == 0:
            e = layer // 2
            lam_init = 0.8 - 0.6 * math.exp(-0.3 * layer)
            w_in_bf = w_in_e[e].astype(BF16)
            w_glu_bf = w_glu[e].astype(BF16)
            w_out_bf = w_out_e[e].astype(BF16)
            ops = _s5_operators(ssm_lam_re[e], ssm_lam_im[e], ssm_log_dt[e], ssm_b_re[e], ssm_b_im[e],
                                ssm_c_re[e], ssm_c_im[e], ssm_d[e])
            lam_vecs = [v[e].reshape(1, DH).astype(F32) for v in (lam_q1, lam_k1, lam_q2, lam_k2)]
            subln = subln_g[e].reshape(1, 2 * DH).astype(F32)

            u, sza, q, kb, vb, szb, kf, vf = _even_in_proj(xp, mod3, w_in_bf, layer, cond_p, tm, None, lp)
            new_k.append(kf.reshape(bp_, H_B, 2, DH, lp).transpose(0, 4, 1, 2, 3))
            new_v.append(vf.reshape(bp_, lp, H_B, 2 * DH))
            ys, s_re, s_im = _s5_mix(u, ops, zeros_h0, zeros_h0, bp_, lp, 2 * S5_TILE_ELEMS // (lp * d), d // 2)
            new_sr.append(s_re)
            new_si.append(s_im)
            yb = _attention_prompt(q, kb, vb, szb, lam_vecs, subln, lam_init, bp_, lp)
            next_odd = None
            if layer + 1 < depth:
                next_odd = (w_in_o[(layer + 1) // 2].astype(BF16),) + _dft_mats(GC_C, GC_C ** -0.5)
            res = _even_out_proj(ys, sza, yb, xp, mod3, w_glu_bf, b_glu[e], w_out_bf, ln_g[layer], ln_b[layer],
                                 layer, cond_p, tm, alpha, next_odd)
            xp, fused_p = (res, None) if next_odd is None else (res[0], res[1:])

            u, sza, q, kb, vb, szb = _even_in_proj(xs, mod3, w_in_bf, layer, cond_s_row, tm, rope_tabs, ls)
            ys, _, _ = _s5_mix(u, ops, state_ssm_re[:, e], state_ssm_im[:, e], bs_, ls, bs_,
                               S5_TILE_ELEMS // (bs_ * ls))
            kc = cache_k[:, e].reshape(bs_, past, d).astype(BF16)
            vc = cache_v[:, e].reshape(bs_, past, d).astype(BF16)
            yb = _attention_sample(q, kc, kb, vc, vb, szb, lam_vecs, subln, lam_init, bs_, ls, ATTN_Q_TILE)
            res = _even_out_proj(ys, sza, yb, xs, mod3, w_glu_bf, b_glu[e], w_out_bf, ln_g[layer], ln_b[layer],
                                 layer, cond_s_row, tm, alpha, next_odd)
            xs, fused_s = (res, None) if next_odd is None else (res[0], res[1:])
        else:
            o = layer // 2
            w_fno_bf = w_fno[o].astype(BF16)
            w_out_bf = w_out_o[o].astype(BF16)
            for which in ("prompt", "sample"):
                if which == "prompt":
                    x2, cond_row, cond_grid, bsz, seq, fused = xp, cond_p, cond_p, bp_, lp, fused_p
                else:
                    x2, cond_row, cond_grid, bsz, seq, fused = xs, cond_s_row, cond_s_grid, bs_, ls, fused_s
                cl, sl = _dft_mats(seq, seq ** -0.5)
                uc, us, sz = fused
                x2 = _odd_out_proj(uc, us, sz, x2, mod3, cl, sl, w_fno_bf, b_fno[o], w_out_bf, ln_g[layer],
                                   ln_b[layer], layer, cond_grid, bsz, seq, tm, alpha)
                if which == "prompt":
                    xp = x2
                else:
                    xs = x2

    return (xp.reshape(bp_, lp, d), xs.reshape(bs_, ls, d), jnp.stack(new_k, axis=1), jnp.stack(new_v, axis=1),
            jnp.stack(new_sr, axis=1), jnp.stack(new_si, axis=1))
```

```python
import functools
import math

import jax
import jax.numpy as jnp
import numpy as np
from jax import lax
from jax.experimental import pallas as pl
from jax.experimental.pallas import tpu as pltpu

F32 = jnp.float32
BF16 = jnp.bfloat16

D_MODEL = 1024
GRID_W = 64
SSM_GROUP = 16
G_A = D_MODEL // SSM_GROUP
P_A = 64
DH = 64
H_B = D_MODEL // (2 * DH)
ROPE_BASE = 10000.0
ROT_HALF = DH // 2
ROT_FREQS = DH // 4
NG_C = 8
GC_C = 2 * D_MODEL // NG_C
W_C = 2 * D_MODEL
LN_EPS = 1e-5
LOG2_E = 1.4426950408889634
CHUNK = 16
SSM_TILE = CHUNK * SSM_GROUP
LANES = 128
PERM_TILE = CHUNK * CHUNK
SUB_ROWS = PERM_TILE
ROW_TILE = 2 * SUB_ROWS
ATTN_Q_TILE = 256
ATTN_SEQS_PER_STEP = 4
S5_GROUPS_PER_STEP = 16
S5_SCAN_ROWS = 4096
S5_TILE_ELEMS = 2048 * 1024
MOD_ROWS = 8
VMEM_LIMIT = 56 * 1024 * 1024


def _cparams(n_axes):
    return pltpu.CompilerParams(dimension_semantics=("arbitrary",) * n_axes, vmem_limit_bytes=VMEM_LIMIT)


def _ln_rows(x):
    mu = jnp.mean(x, axis=-1, keepdims=True)
    xc = x - mu
    var = jnp.mean(xc * xc, axis=-1, keepdims=True)
    return xc * lax.rsqrt(var + LN_EPS)


def _silu(z):
    return z * jax.nn.sigmoid(z)


def _dot(a, b):
    return jnp.dot(a, b, preferred_element_type=F32)


def _dot_nt(a, b):
    return lax.dot_general(a, b, (((1,), (1,)), ((), ())), preferred_element_type=F32)


def _post_norm(x, gate, out, g, b, alpha):
    return _ln_rows(alpha * x + gate * out) * g + b


def _mod_kernel(c_ref, w_ref, b_ref, o_ref):
    c = _silu(c_ref[...]).astype(BF16)
    o_ref[...] = _dot(c, w_ref[...].astype(BF16)) + b_ref[...]


def _modulation(cond8, w_mod, b_mod):
    depth, d, n3 = w_mod.shape
    tn = 1024
    return pl.pallas_call(
        _mod_kernel,
        grid=(depth, n3 // tn),
        in_specs=[
            pl.BlockSpec((MOD_ROWS, d), lambda l, j: (0, 0)),
            pl.BlockSpec((None, d, tn), lambda l, j: (l, 0, j)),
            pl.BlockSpec((None, 1, tn), lambda l, j: (l, 0, j)),
        ],
        out_specs=pl.BlockSpec((None, MOD_ROWS, tn), lambda l, j: (l, 0, j)),
        out_shape=jax.ShapeDtypeStruct((depth, MOD_ROWS, n3), F32),
        compiler_params=_cparams(2),
        name="modulation",
    )(cond8, w_mod, b_mod.reshape(depth, 1, n3))


def _resident(shape):
    return pl.BlockSpec(shape, lambda *idx: (0,) * len(shape), pipeline_mode=pl.Buffered(1))


def _mod_spec(layer, part, cond_of):
    return pl.BlockSpec((None, 1, D_MODEL), lambda *idx: ((layer * MOD_ROWS + cond_of(*idx)) * 3 + part, 0, 0))


def _rope(x, cos, sin_signed, first_half):
    blocks = []
    for hh in range(x.shape[1] // 128):
        b = x[:, hh * 128:(hh + 1) * 128]
        partner = jnp.where(first_half, pltpu.roll(b, 128 - ROT_HALF, 1), pltpu.roll(b, ROT_HALF, 1))
        blocks.append(b * cos + partner * sin_signed)
    return jnp.concatenate(blocks, axis=1)


def _chunk_transpose_matrix():
    i = lax.broadcasted_iota(jnp.int32, (PERM_TILE, PERM_TILE), 0)
    j = lax.broadcasted_iota(jnp.int32, (PERM_TILE, PERM_TILE), 1)
    return (j == (i % CHUNK) * CHUNK + i // CHUNK).astype(BF16)


def _even_in_kernel(*refs, rope):
    if rope:
        (x_ref, shift_ref, scale_ref, w_ref, perm_ref, cos_ref, sin_ref,
         u_ref, sza_ref, q_ref, kb_ref, vb_ref, szb_ref) = refs
    else:
        (x_ref, shift_ref, scale_ref, w_ref, perm_ref, u_ref, sza_ref, q_ref, kb_ref, vb_ref, szb_ref,
         kf_ref, vf_ref) = refs
    w = D_MODEL
    for r in range(x_ref.shape[0] // PERM_TILE):
        rows = slice(r * PERM_TILE, (r + 1) * PERM_TILE)
        h = (_ln_rows(x_ref[rows, :]) * (1.0 + scale_ref[...]) + shift_ref[...]).astype(BF16)

        def proj(j):
            return _dot(h, w_ref[:, j * w:(j + 1) * w])

        u_ref[rows, :] = _dot(perm_ref[...], proj(0).astype(BF16)).astype(BF16)
        sza_ref[rows, :] = _silu(proj(1)).astype(BF16)
        q = proj(2) * (DH ** -0.5 * LOG2_E)
        k = proj(3)
        v = proj(4)
        if rope:
            lane = lax.broadcasted_iota(jnp.int32, (1, 128), 1)
            first_half = (lane % DH) < ROT_HALF
            q = _rope(q, cos_ref[rows, :], sin_ref[rows, :], first_half)
            kb_ref[rows, :] = _rope(k, cos_ref[rows, :], sin_ref[rows, :], first_half).astype(BF16)
        else:
            kf_ref[r] = k.T
            vf_ref[rows, :] = v
            kb_ref[rows, :] = k.astype(BF16)
        q_ref[rows, :] = q.astype(BF16)
        vb_ref[rows, :] = v.astype(BF16)
        szb_ref[rows, :] = _silu(proj(5)).astype(BF16)


def _even_in_proj(x2, mod3, w_in_bf, layer, cond_of, tm, rope_tabs, seq_len):
    t, d = x2.shape
    n = w_in_bf.shape[1]
    assert tm % PERM_TILE == 0
    row = pl.BlockSpec((tm, d), lambda i: (i, 0))
    in_specs = [row, _mod_spec(layer, 0, cond_of), _mod_spec(layer, 1, cond_of),
                _resident((d, n)), _resident((PERM_TILE, PERM_TILE))]
    args = [x2, mod3, mod3, w_in_bf, _chunk_transpose_matrix()]
    bf = jax.ShapeDtypeStruct((t, d), BF16)
    out_shape = [bf] * 6
    out_specs = [row] * 6
    if rope_tabs is not None:
        tiles_per_seq = seq_len // tm
        tab = pl.BlockSpec((tm, 128), lambda i: (i % tiles_per_seq, 0))
        in_specs += [tab, tab]
        args += list(rope_tabs)
    else:
        assert seq_len == PERM_TILE
        out_shape += [jax.ShapeDtypeStruct((t // seq_len, d, seq_len), F32), jax.ShapeDtypeStruct((t, d), F32)]
        out_specs += [pl.BlockSpec((tm // seq_len, d, seq_len), lambda i: (i, 0, 0)), row]
    return pl.pallas_call(
        functools.partial(_even_in_kernel, rope=rope_tabs is not None),
        grid=(t // tm,),
        in_specs=in_specs,
        out_specs=out_specs,
        out_shape=out_shape,
        compiler_params=_cparams(1),
        name="even_in_proj",
    )(*args)


BUILD_GROUPS = 4


def _s5_build_kernel(q_ref, b_ref, cn_ref, ccr_ref, cci_ref, ecr_ref, eci_ref, d_ref,
                     rep_ref, til_ref, wst_ref, kt_ref, wo_ref):
    def split(x):
        hi = x.astype(BF16)
        return hi, (x - hi.astype(F32)).astype(BF16)

    def hdot(a, b):
        (ah, al), (bh, bl) = split(a), split(b)
        return _dot(ah, bh) + _dot(ah, bl) + _dot(al, bh)

    lane = lax.broadcasted_iota(jnp.int32, (SSM_GROUP, SSM_TILE), 1)
    sub = lax.broadcasted_iota(jnp.int32, (SSM_GROUP, SSM_TILE), 0)
    zeros = jnp.zeros((SSM_GROUP, SSM_TILE), F32)
    rows4 = 4 * P_A
    def expand(x_t, sel):
        tn = lambda a: lax.dot_general(a, sel, (((0,), (0,)), ((), ())), preferred_element_type=F32)
        hi, lo = split(x_t)
        return tn(hi) + tn(lo)

    for g in range(BUILD_GROUPS):
        qe = expand(q_ref[g], rep_ref[...])
        be = expand(b_ref[g], til_ref[...])
        part = lambda x, k: x[k * P_A:(k + 1) * P_A]
        st_re = [part(qe, d) * part(be, d) - part(qe, 2 + d) * part(be, 2 + d) for d in range(2)]
        st_im = [part(qe, d) * part(be, 2 + d) + part(qe, 2 + d) * part(be, d) for d in range(2)]
        wst_ref[g] = jnp.concatenate([st_re[0], st_re[1], st_im[0], st_im[1]], axis=0).astype(BF16)
        klag = [hdot(cn_ref[d, g], jnp.concatenate([st_re[d], st_im[d]], axis=0)) for d in range(2)]
        fwd_p = jnp.concatenate([klag[0], zeros], axis=1)
        bwd_p = jnp.concatenate([zeros, klag[1]], axis=1)
        ccr, cci = ccr_ref[g], cci_ref[g]
        d_lanes = jnp.broadcast_to(d_ref[g], (SSM_GROUP, SSM_TILE))
        for t in range(CHUNK):
            lo_f = SSM_GROUP * (CHUNK - 1 - t)
            lo_b = SSM_TILE - SSM_GROUP * t
            skip = jnp.where(lane == t * SSM_GROUP + sub, d_lanes, 0.0)
            rows = slice(t * SSM_GROUP, (t + 1) * SSM_GROUP)
            kt_ref[g, rows, :] = (fwd_p[:, lo_f:lo_f + SSM_TILE] + bwd_p[:, lo_b:lo_b + SSM_TILE] + skip).astype(BF16)
            er, ei = ecr_ref[g, t:t + 1, :], eci_ref[g, t:t + 1, :]
            wo_ref[g, rows, :] = jnp.concatenate([ccr * er - cci * ei, -(ccr * ei + cci * er)], axis=1).astype(BF16)


def _s5_operators(lam_re, lam_im, log_dt, b_re, b_im, c_re, c_im, d_skip):
    lr, li = lam_re.astype(F32), lam_im.astype(F32)
    dt = jnp.exp(log_dt.astype(F32))[..., None]
    mag = jnp.exp(lr * dt)
    ar = mag * jnp.cos(li * dt)
    ai = mag * jnp.sin(li * dt)
    den = lr * lr + li * li
    fr = ((ar - 1.0) * lr + ai * li) / den
    fi = (ai * lr - (ar - 1.0) * li) / den
    br, bi = b_re.astype(F32), b_im.astype(F32)
    bbr = fr[..., None] * br - fi[..., None] * bi
    bbi = fr[..., None] * bi + fi[..., None] * br
    cr, ci = c_re.astype(F32), c_im.astype(F32)
    e_pow = jnp.arange(CHUNK + 1, dtype=F32)[:, None, None, None]
    pmag = jnp.exp(e_pow * (lr * dt))
    pr = pmag * jnp.cos(e_pow * (li * dt))
    pi = pmag * jnp.sin(e_pow * (li * dt))

    def by_position(p):
        return jnp.stack([p[:CHUNK, 0][::-1], p[:CHUNK, 1]], axis=0).transpose(0, 2, 1, 3)
    qr, qi = by_position(pr), by_position(pi)

    lane_sn = np.arange(SSM_TILE)
    rep = jnp.asarray(lane_sn[None, :] // SSM_GROUP == np.arange(CHUNK)[:, None], dtype=BF16)
    til = jnp.asarray(lane_sn[None, :] % SSM_GROUP == np.arange(SSM_GROUP)[:, None], dtype=BF16)

    def by_output(p):
        return jnp.concatenate([p[1:, 0], p[1:, 1][::-1]], axis=-1).transpose(1, 0, 2)
    cat_c = lambda c: jnp.concatenate([c[0], c[1]], axis=-1)

    def stack_lanes(re, im):
        return jnp.concatenate([re[0], re[1], im[0], im[1]], axis=-1)
    q_all = stack_lanes(qr, qi)
    b_all = stack_lanes(bbr.transpose(0, 1, 3, 2), bbi.transpose(0, 1, 3, 2))
    c_neg = jnp.concatenate([cr, -ci], axis=-1)

    gb = BUILD_GROUPS
    narrow = pl.BlockSpec((gb, SSM_GROUP, 4 * P_A), lambda i: (i, 0, 0))
    expand = pl.BlockSpec((SSM_GROUP, SSM_TILE), lambda i: (0, 0))
    cspec = pl.BlockSpec((2, gb, SSM_GROUP, 2 * P_A), lambda i: (0, i, 0, 0))
    half = pl.BlockSpec((gb, SSM_GROUP, 2 * P_A), lambda i: (i, 0, 0))
    dspec = pl.BlockSpec((gb, SSM_GROUP, 1), lambda i: (i, 0, 0))
    ospec = pl.BlockSpec((gb, SSM_TILE, SSM_TILE), lambda i: (i, 0, 0))
    oshape = jax.ShapeDtypeStruct((G_A, SSM_TILE, SSM_TILE), BF16)
    wst_t, kt_t, wo_t = pl.pallas_call(
        _s5_build_kernel,
        grid=(G_A // gb,),
        in_specs=[narrow] * 2 + [cspec] + [half] * 4 + [dspec, expand, expand],
        out_specs=[ospec] * 3,
        out_shape=[oshape] * 3,
        compiler_params=_cparams(1),
        name="s5_build_operators",
    )(q_all, b_all, c_neg, cat_c(cr), cat_c(ci),
      by_output(pr), by_output(pi), d_skip.astype(F32).reshape(G_A, SSM_GROUP, 1), rep, til)

    a16r = jnp.concatenate([pr[CHUNK, 0], pr[CHUNK, 1]], axis=-1)[:, None, :]
    a16i = jnp.concatenate([pi[CHUNK, 0], pi[CHUNK, 1]], axis=-1)[:, None, :]
    return wst_t, kt_t, wo_t, a16r, a16i


def _s5_kernel(u_ref, wst_ref, kt_ref, wo_ref, a16r_ref, a16i_ref, h0r_ref, h0i_ref,
               pin_ref, pout_ref, y_ref, fr_ref, fi_ref, ut, yt, s_sc, hf_sc, hb_sc, *, nb, nc, gpb, il):
    gb = pl.program_id(2)
    seq = nc * CHUNK
    r = nb * nc
    sw = 2 * P_A
    n_cb = u_ref.shape[1] // LANES
    pieces = [b * seq + q * PERM_TILE for b in range(nb) for q in range(seq // PERM_TILE)]

    @pl.when(gb == 0)
    def _():
        for s in range(CHUNK):
            cols = []
            for cb in range(n_cb):
                rows = jnp.concatenate([u_ref[o + s * CHUNK:o + (s + 1) * CHUNK, cb * LANES:(cb + 1) * LANES]
                                        for o in pieces], axis=0)
                cols.append(rows.astype(F32).T.astype(BF16))
            ut[:, s * r:(s + 1) * r] = _dot(jnp.concatenate(cols, axis=0), pin_ref[...]).astype(BF16)

    fwd = (lax.broadcasted_iota(jnp.int32, (1, sw), 1)) < P_A

    def groups(it, carry):
        js = [it * il + i for i in range(il)]
        grows = [pl.multiple_of((gb * gpb + j) * SSM_GROUP, SSM_GROUP) for j in js]
        ds = [jnp.concatenate([ut[pl.ds(g, SSM_GROUP), s * r:(s + 1) * r] for s in range(CHUNK)], axis=0)
              for g in grows]
        for i, j in enumerate(js):
            s_t = _dot(wst_ref[j], ds[i])
            s_sc[i, 0] = s_t[:sw].T
            s_sc[i, 1] = s_t[sw:].T
        a_r = [a16r_ref[j] for j in js]
        a_i = [a16i_ref[j] for j in js]
        re = [h0r_ref[j] for j in js]
        im = [h0i_ref[j] for j in js]
        for k in range(nc):
            rf = pl.ds(k * nb, nb)
            rb = pl.ds((nc - 1 - k) * nb, nb)
            for i in range(il):
                hf_sc[i, 0, rf, :] = re[i]
                hf_sc[i, 1, rf, :] = im[i]
                hb_sc[i, 0, rb, :] = re[i]
                hb_sc[i, 1, rb, :] = im[i]
                xr = jnp.where(fwd, s_sc[i, 0, rf, :], s_sc[i, 0, rb, :])
                xi = jnp.where(fwd, s_sc[i, 1, rf, :], s_sc[i, 1, rb, :])
                re[i], im[i] = a_r[i] * re[i] - a_i[i] * im[i] + xr, a_r[i] * im[i] + a_i[i] * re[i] + xi
        for i, j in enumerate(js):
            fr_ref[j] = re[i]
            fi_ref[j] = im[i]
            hin_t = jnp.concatenate([jnp.where(fwd, hf_sc[i, 0], hb_sc[i, 0]).T,
                                     jnp.where(fwd, hf_sc[i, 1], hb_sc[i, 1]).T], axis=0)
            y_t = (_dot(kt_ref[j], ds[i]) + _dot(wo_ref[j], hin_t.astype(BF16))).astype(BF16)
            for t in range(CHUNK):
                yt[pl.ds(grows[i], SSM_GROUP), t * r:(t + 1) * r] = y_t[t * SSM_GROUP:(t + 1) * SSM_GROUP, :]
        return carry

    lax.fori_loop(0, gpb // il, groups, 0)

    @pl.when(gb == pl.num_programs(2) - 1)
    def _():
        for t in range(CHUNK):
            full = _dot(yt[:, t * r:(t + 1) * r], pout_ref[...])
            for cb in range(n_cb):
                rows = full[cb * LANES:(cb + 1) * LANES, :].T.astype(BF16)
                for idx, o in enumerate(pieces):
                    y_ref[o + t * CHUNK:o + (t + 1) * CHUNK, cb * LANES:(cb + 1) * LANES] = (
                        rows[idx * CHUNK:(idx + 1) * CHUNK, :])


def _s5_mix(u, ops, h0_re, h0_im, bsz, seq_len, nb, cw):
    wst_t, kt_t, wo_t, a16r, a16i = ops
    t, w = u.shape
    nc = seq_len // CHUNK
    r = nb * nc
    gpb = S5_GROUPS_PER_STEP
    n_row_tiles = bsz // nb
    n_col_tiles = w // cw
    gb_per_tile = cw // SSM_GROUP // gpb
    assert r % 128 == 0 and cw % (SSM_GROUP * gpb) == 0 and bsz % nb == 0

    def h0_layout(h0):
        return h0.astype(F32).transpose(2, 0, 1, 3).reshape(G_A, bsz, 2 * P_A)

    tile = pl.BlockSpec((nb * seq_len, cw), lambda i, c, g: (i, c))
    grp = lambda i, c, g: (c * gb_per_tile + g, 0, 0)
    wspec = pl.BlockSpec((gpb, SSM_TILE, SSM_TILE), grp)
    aspec = pl.BlockSpec((gpb, 1, 2 * P_A), grp)
    hspec = pl.BlockSpec((gpb, nb, 2 * P_A), lambda i, c, g: (c * gb_per_tile + g, i, 0))
    state = jax.ShapeDtypeStruct((G_A, bsz, 2 * P_A), F32)
    il = min(gpb, S5_SCAN_ROWS // r)
    scratch = [pltpu.VMEM((cw, CHUNK * r), BF16)] * 2 + [pltpu.VMEM((il, 2, r, 2 * P_A), F32)] * 3
    src = lax.broadcasted_iota(jnp.int32, (r, r), 0)
    dst = lax.broadcasted_iota(jnp.int32, (r, r), 1)
    pin = (dst == (src % nc) * nb + src // nc).astype(BF16)
    pspec = pl.BlockSpec((r, r), lambda i, c, g: (0, 0))

    y, f_re, f_im = pl.pallas_call(
        functools.partial(_s5_kernel, nb=nb, nc=nc, gpb=gpb, il=il),
        grid=(n_row_tiles, n_col_tiles, gb_per_tile),
        in_specs=[tile, wspec, wspec, wspec, aspec, aspec, hspec, hspec, pspec, pspec],
        out_specs=[tile, hspec, hspec],
        out_shape=[jax.ShapeDtypeStruct((t, w), BF16), state, state],
        scratch_shapes=scratch,
        compiler_params=_cparams(3),
        name="s5_core",
    )(u, wst_t, kt_t, wo_t, a16r, a16i, h0_layout(h0_re), h0_layout(h0_im), pin, pin.T)

    def fin(f):
        return f.reshape(G_A, bsz, 2, P_A).transpose(1, 2, 0, 3)
    return y, fin(f_re), fin(f_im)


def _diff_lambda(lq1, lk1, lq2, lk2, lam_init):
    return (jnp.exp(jnp.sum(lq1[...] * lk1[...], axis=-1, keepdims=True))
            - jnp.exp(jnp.sum(lq2[...] * lk2[...], axis=-1, keepdims=True)) + lam_init)


def _attn_kernel(*refs, lam_init, cached, seq_len):
    if cached:
        q_ref, kc_ref, kn_ref, vc_ref, vn_ref, szb_ref, lq1, lk1, lq2, lk2, g_ref, o_ref = refs
    else:
        q_ref, kn_ref, vn_ref, szb_ref, lq1, lk1, lq2, lk2, g_ref, o_ref = refs
    lam = _diff_lambda(lq1, lk1, lq2, lk2, lam_init)
    lq = q_ref.shape[0] if cached else seq_len
    hw = 2 * DH
    low = lax.broadcasted_iota(jnp.int32, (1, hw), 1) < DH
    zero = jnp.zeros((), BF16)
    ones = jnp.ones((kn_ref.shape[0] if cached else seq_len, hw), BF16)
    for h in range(H_B * (q_ref.shape[0] // lq)):
        rows = slice((h // H_B) * lq, (h // H_B + 1) * lq)
        krows = slice(None) if cached else rows
        cols = slice((h % H_B) * hw, (h % H_B + 1) * hw)
        qh = q_ref[rows, cols]
        qs = jnp.concatenate([jnp.where(low, qh, zero), jnp.where(low, zero, qh)], axis=0)
        s = _dot_nt(qs, kn_ref[krows, cols])
        if cached:
            s = jnp.concatenate([_dot_nt(qs, kc_ref[:, cols]), s], axis=1)
        e = jnp.exp2(s - jnp.max(s, axis=-1, keepdims=True)).astype(BF16)
        vn = jnp.concatenate([vn_ref[krows, cols], ones], axis=1)
        if cached:
            lc = kc_ref.shape[0]
            vc = jnp.concatenate([vc_ref[:, cols], ones[:lc]], axis=1)
            oa = _dot(e[:, :lc], vc) + _dot(e[:, lc:], vn)
        else:
            oa = _dot(e, vn)
        on = oa[:, :hw] * (1.0 / oa[:, hw:])
        o = on[:lq] - lam * on[lq:]
        o = o * lax.rsqrt(jnp.mean(o * o, axis=-1, keepdims=True) + LN_EPS)
        o = o * g_ref[...] * (1.0 - lam_init)
        o_ref[rows, cols] = (o * szb_ref[rows, cols].astype(F32)).astype(BF16)


def _lam_specs(n_axes):
    zero = lambda *idx: (0, 0)
    return [pl.BlockSpec((1, DH), zero)] * 4 + [pl.BlockSpec((1, 2 * DH), zero)]


def _attention_prompt(q, k, v, szb, lam_vecs, subln, lam_init, bsz, seq_len):
    blk = pl.BlockSpec((ATTN_SEQS_PER_STEP * seq_len, D_MODEL), lambda b: (b, 0))
    return pl.pallas_call(
        functools.partial(_attn_kernel, lam_init=lam_init, cached=False, seq_len=seq_len),
        grid=(bsz // ATTN_SEQS_PER_STEP,),
        in_specs=[blk] * 4 + _lam_specs(1),
        out_specs=blk,
        out_shape=jax.ShapeDtypeStruct(q.shape, BF16),
        compiler_params=_cparams(1),
        name="diff_attention_prompt",
    )(q, k, v, szb, *lam_vecs, subln)


def _attention_sample(q, kc, kn, vc, vn, szb, lam_vecs, subln, lam_init, bsz, seq_len, tq):
    nq = seq_len // tq
    past = kc.shape[1]
    qblk = pl.BlockSpec((tq, D_MODEL), lambda b, i: (b * nq + i, 0))
    cblk = pl.BlockSpec((None, past, D_MODEL), lambda b, i: (b, 0, 0))
    nblk = pl.BlockSpec((seq_len, D_MODEL), lambda b, i: (b, 0))
    return pl.pallas_call(
        functools.partial(_attn_kernel, lam_init=lam_init, cached=True, seq_len=seq_len),
        grid=(bsz, nq),
        in_specs=[qblk, cblk, nblk, cblk, nblk, qblk] + _lam_specs(2),
        out_specs=qblk,
        out_shape=jax.ShapeDtypeStruct(q.shape, BF16),
        compiler_params=_cparams(2),
        name="diff_attention_sample",
    )(q, kc, kn, vc, vn, szb, *lam_vecs, subln)


def _even_out_kernel(ys_ref, sza_ref, yb_ref, x_ref, gate_ref, wglu_ref, bglu_ref, wout_ref, g_ref, b_ref,
                     perm_ref, *rest, alpha):
    o_ref = rest[0] if len(rest) == 1 else rest[5]
    wa = ys_ref.shape[1]
    for r in range(x_ref.shape[0] // PERM_TILE):
        rows = slice(r * PERM_TILE, (r + 1) * PERM_TILE)
        ga = jax.nn.gelu(_dot(perm_ref[...], ys_ref[rows, :]))
        glu = jax.nn.sigmoid(_dot(ga.astype(BF16), wglu_ref[...]) + bglu_ref[...])
        ya = (ga * glu * sza_ref[rows, :].astype(F32)).astype(BF16)
        out = _dot(ya, wout_ref[:wa, :]) + _dot(yb_ref[rows, :], wout_ref[wa:, :])
        x_new = _post_norm(x_ref[rows, :], gate_ref[...], out, g_ref[...], b_ref[...], alpha)
        o_ref[rows, :] = x_new
        if len(rest) > 1:
            _odd_in_rows(x_new, rows, *rest[:5], *rest[6:])


def _even_out_proj(ys, sza, yb, x2, mod3, w_glu_bf, b_glu, w_out_bf, ln_g, ln_b, layer, cond_of, tm, alpha,
                   next_odd=None):
    t, d = x2.shape
    assert tm % PERM_TILE == 0 and PERM_TILE == SUB_ROWS
    row = pl.BlockSpec((tm, d), lambda i: (i, 0))
    full = _resident
    in_specs = [row, row, row, row, _mod_spec(layer, 2, cond_of), full(w_glu_bf.shape), full((1, d)),
                full(w_out_bf.shape), full((1, d)), full((1, d)), full((PERM_TILE, PERM_TILE))]
    args = [ys, sza, yb, x2, mod3, w_glu_bf, b_glu.reshape(1, d), w_out_bf, ln_g.reshape(1, d), ln_b.reshape(1, d),
            _chunk_transpose_matrix()]
    out_specs, out_shape = [row], [jax.ShapeDtypeStruct((t, d), F32)]
    if next_odd is not None:
        w_in_bf, cc, sc = next_odd
        wide = pl.BlockSpec((tm, W_C), lambda i: (i, 0))
        in_specs += [_mod_spec(layer + 1, 0, cond_of), _mod_spec(layer + 1, 1, cond_of), full(w_in_bf.shape),
                     full(cc.shape), full(sc.shape)]
        args += [mod3, mod3, w_in_bf, cc, sc]
        out_specs += [wide] * 3
        out_shape += [jax.ShapeDtypeStruct((t, W_C), BF16)] * 3
    res = pl.pallas_call(
        functools.partial(_even_out_kernel, alpha=alpha),
        grid=(t // tm,),
        in_specs=in_specs,
        out_specs=out_specs,
        out_shape=out_shape,
        compiler_params=_cparams(1),
        name="even_out_proj",
    )(*args)
    return res[0] if next_odd is None else tuple(res)


def _dft_mats(n, scale):
    k = np.arange(n, dtype=np.int64)
    ang = ((k[:, None] * k[None, :]) % n).astype(np.float64) * (2.0 * math.pi / n)
    return (jnp.asarray((np.cos(ang) * scale).astype(np.float32)).astype(BF16),
            jnp.asarray((np.sin(ang) * scale).astype(np.float32)).astype(BF16))


def _odd_in_rows(x, rows, shift_ref, scale_ref, w_ref, cc_ref, sc_ref, uc_ref, us_ref, sz_ref):
    h = (_ln_rows(x) * (1.0 + scale_ref[...]) + shift_ref[...]).astype(BF16)
    u = _dot(h, w_ref[:, :W_C]).astype(BF16)
    sz_ref[rows, :] = _silu(_dot(h, w_ref[:, W_C:])).astype(BF16)
    for g in range(NG_C):
        cols = slice(g * GC_C, (g + 1) * GC_C)
        uc_ref[rows, cols] = _dot(u[:, cols], cc_ref[...]).astype(BF16)
        us_ref[rows, cols] = _dot(u[:, cols], sc_ref[...]).astype(BF16)


def _odd_out_kernel(cl_ref, sl_ref, uc_ref, us_ref, sz_ref, x_ref, gate_ref, wf_ref, bf_ref, wo_ref, g_ref, b_ref,
                    o_ref, *, alpha, whole_seqs):
    seq_len = cl_ref.shape[1]
    sub = seq_len if whole_seqs else SUB_ROWS
    for r in range(x_ref.shape[0] // sub):
        rows = slice(r * sub, (r + 1) * sub)
        if whole_seqs:
            mixed = _dot(cl_ref[...], uc_ref[rows, :]) - _dot(sl_ref[...], us_ref[rows, :])
        else:
            mixed = _dot(cl_ref[rows, :], uc_ref[...]) - _dot(sl_ref[rows, :], us_ref[...])
        y = ((_dot(mixed.astype(BF16), wf_ref[...]) + bf_ref[...]) * sz_ref[rows, :].astype(F32)).astype(BF16)
        o_ref[rows, :] = _post_norm(x_ref[rows, :], gate_ref[...], _dot(y, wo_ref[...]), g_ref[...], b_ref[...],
                                    alpha)


def _odd_out_proj(uc, us, sz, x2, mod3, cl, sl, w_fno_bf, b_fno, w_out_bf, ln_g, ln_b, layer, cond_of,
                  bsz, seq_len, tl, alpha):
    t, d = x2.shape
    whole_seqs = tl >= seq_len
    full = _resident
    if whole_seqs:
        assert tl % seq_len == 0
        grid = (t // tl, 1)
        dft = _resident((seq_len, seq_len))
        seq = pl.BlockSpec((tl, W_C), lambda b, i: (b, 0))
        wide = seq
        row = pl.BlockSpec((tl, d), lambda b, i: (b, 0))
    else:
        assert seq_len % tl == 0 and tl % SUB_ROWS == 0
        nl = seq_len // tl
        grid = (bsz, nl)
        dft = pl.BlockSpec((tl, seq_len), lambda b, i: (i, 0))
        seq = pl.BlockSpec((seq_len, W_C), lambda b, i: (b, 0))
        wide = pl.BlockSpec((tl, W_C), lambda b, i: (b * nl + i, 0))
        row = pl.BlockSpec((tl, d), lambda b, i: (b * nl + i, 0))
    return pl.pallas_call(
        functools.partial(_odd_out_kernel, alpha=alpha, whole_seqs=whole_seqs),
        grid=grid,
        in_specs=[dft, dft, seq, seq, wide, row, _mod_spec(layer, 2, cond_of), full(w_fno_bf.shape),
                  full((1, W_C)), full(w_out_bf.shape), full((1, d)), full((1, d))],
        out_specs=row,
        out_shape=jax.ShapeDtypeStruct((t, d), F32),
        compiler_params=_cparams(2),
        name="odd_out_proj",
    )(cl, sl, uc, us, sz, x2, mod3, w_fno_bf, b_fno.reshape(1, W_C), w_out_bf, ln_g.reshape(1, d),
      ln_b.reshape(1, d))


def _rope_tables(seq_len):
    rows = seq_len // GRID_W
    row = jnp.repeat(jnp.arange(rows), GRID_W).astype(F32)
    col = jnp.tile(jnp.arange(GRID_W), rows).astype(F32)
    freqs = ROPE_BASE ** (-jnp.arange(ROT_FREQS, dtype=F32) / ROT_FREQS)
    ang = jnp.concatenate([row[:, None] * freqs, col[:, None] * freqs], axis=-1)
    cos, sin = jnp.cos(ang), jnp.sin(ang)
    cos128 = jnp.tile(cos, (1, 128 // ROT_HALF))
    sin128 = jnp.tile(jnp.concatenate([-sin, sin], axis=-1), (1, 128 // DH))
    return cos128, sin128


def kernel(x_prompt, x_sample, cache_k, cache_v, state_ssm_re, state_ssm_im, c, c_ctx, w_mod, b_mod, ln_g, ln_b, w_in_e, ssm_lam_re, ssm_lam_im, ssm_log_dt, ssm_b_re, ssm_b_im, ssm_c_re, ssm_c_im, ssm_d, w_glu, b_glu, lam_q1, lam_k1, lam_q2, lam_k2, subln_g, w_out_e, w_in_o, w_fno, b_fno, w_out_o):
    depth = w_mod.shape[0]
    bp_, lp, d = x_prompt.shape
    bs_, ls, _ = x_sample.shape
    past = cache_k.shape[2]
    alpha = (2 * depth) ** 0.25
    assert bs_ + 1 <= MOD_ROWS and d == D_MODEL

    cond8 = jnp.concatenate([c_ctx[None, :], c, jnp.zeros((MOD_ROWS - 1 - bs_, d), F32)], axis=0).astype(F32)
    mod3 = _modulation(cond8, w_mod, b_mod).reshape(depth * MOD_ROWS * 3, 1, d)

    tm = ROW_TILE
    cond_p = lambda *idx: 0
    cond_s_row = lambda i: 1 + i // (ls // tm)
    cond_s_grid = lambda b, i: 1 + b
    rope_tabs = _rope_tables(ls)
    xp = x_prompt.reshape(bp_ * lp, d)
    xs = x_sample.reshape(bs_ * ls, d)
    new_k, new_v, new_sr, new_si = [], [], [], []
    zeros_h0 = jnp.zeros((bp_, 2, G_A, P_A), F32)

    for layer in range(depth):
        if layer % 2 == 0:
            e = layer // 2
            lam_init = 0.8 - 0.6 * math.exp(-0.3 * layer)
            w_in_bf = w_in_e[e].astype(BF16)
            w_glu_bf = w_glu[e].astype(BF16)
            w_out_bf = w_out_e[e].astype(BF16)
            ops = _s5_operators(ssm_lam_re[e], ssm_lam_im[e], ssm_log_dt[e], ssm_b_re[e], ssm_b_im[e],
                                ssm_c_re[e], ssm_c_im[e], ssm_d[e])
            lam_vecs = [v[e].reshape(1, DH).astype(F32) for v in (lam_q1, lam_k1, lam_q2, lam_k2)]
            subln = subln_g[e].reshape(1, 2 * DH).astype(F32)

            u, sza, q, kb, vb, szb, kf, vf = _even_in_proj(xp, mod3, w_in_bf, layer, cond_p, tm, None, lp)
            new_k.append(kf.reshape(bp_, H_B, 2, DH, lp).transpose(0, 4, 1, 2, 3))
            new_v.append(vf.reshape(bp_, lp, H_B, 2 * DH))
            ys, s_re, s_im = _s5_mix(u, ops, zeros_h0, zeros_h0, bp_, lp, 2 * S5_TILE_ELEMS // (lp * d), d // 2)
            new_sr.append(s_re)
            new_si.append(s_im)
            yb = _attention_prompt(q, kb, vb, szb, lam_vecs, subln, lam_init, bp_, lp)
            next_odd = None
            if layer + 1 < depth:
                next_odd = (w_in_o[(layer + 1) // 2].astype(BF16),) + _dft_mats(GC_C, GC_C ** -0.5)
            res = _even_out_proj(ys, sza, yb, xp, mod3, w_glu_bf, b_glu[e], w_out_bf, ln_g[layer], ln_b[layer],
                                 layer, cond_p, tm, alpha, next_odd)
            xp, fused_p = (res, None) if next_odd is None else (res[0], res[1:])

            u, sza, q, kb, vb, szb = _even_in_proj(xs, mod3, w_in_bf, layer, cond_s_row, tm, rope_tabs, ls)
            ys, _, _ = _s5_mix(u, ops, state_ssm_re[:, e], state_ssm_im[:, e], bs_, ls, bs_,
                               S5_TILE_ELEMS // (bs_ * ls))
            kc = cache_k[:, e].reshape(bs_, past, d).astype(BF16)
            vc = cache_v[:, e].reshape(bs_, past, d).astype(BF16)
            yb = _attention_sample(q, kc, kb, vc, vb, szb, lam_vecs, subln, lam_init, bs_, ls, ATTN_Q_TILE)
            res = _even_out_proj(ys, sza, yb, xs, mod3, w_glu_bf, b_glu[e], w_out_bf, ln_g[layer], ln_b[layer],
                                 layer, cond_s_row, tm, alpha, next_odd)
            xs, fused_s = (res, None) if next_odd is None else (res[0], res[1:])
        else:
            o = layer // 2
            w_fno_bf = w_fno[o].astype(BF16)
            w_out_bf = w_out_o[o].astype(BF16)
            for which in ("prompt", "sample"):
                if which == "prompt":
                    x2, cond_row, cond_grid, bsz, seq, fused = xp, cond_p, cond_p, bp_, lp, fused_p
                else:
                    x2, cond_row, cond_grid, bsz, seq, fused = xs, cond_s_row, cond_s_grid, bs_, ls, fused_s
                cl, sl = _dft_mats(seq, seq ** -0.5)
                uc, us, sz = fused
                x2 = _odd_out_proj(uc, us, sz, x2, mod3, cl, sl, w_fno_bf, b_fno[o], w_out_bf, ln_g[layer],
                                   ln_b[layer], layer, cond_grid, bsz, seq, tm, alpha)
                if which == "prompt":
                    xp = x2
                else:
                    xs = x2

    return (xp.reshape(bp_, lp, d), xs.reshape(bs_, ls, d), jnp.stack(new_k, axis=1), jnp.stack(new_v, axis=1),
            jnp.stack(new_sr, axis=1), jnp.stack(new_si, axis=1))
```

```python
import functools
import math

import jax
import jax.numpy as jnp
import numpy as np
from jax import lax
from jax.experimental import pallas as pl
from jax.experimental.pallas import tpu as pltpu

F32 = jnp.float32
BF16 = jnp.bfloat16

D_MODEL = 1024
GRID_W = 64
SSM_GROUP = 16
G_A = D_MODEL // SSM_GROUP
P_A = 64
DH = 64
H_B = D_MODEL // (2 * DH)
ROPE_BASE = 10000.0
ROT_HALF = DH // 2
ROT_FREQS = DH // 4
NG_C = 8
GC_C = 2 * D_MODEL // NG_C
W_C = 2 * D_MODEL
LN_EPS = 1e-5
LOG2_E = 1.4426950408889634
CHUNK = 16
SSM_TILE = CHUNK * SSM_GROUP
LANES = 128
PERM_TILE = CHUNK * CHUNK
SUB_ROWS = PERM_TILE
ROW_TILE = 2 * SUB_ROWS
ATTN_Q_TILE = 512
ATTN_SEQS_PER_STEP = 4
S5_GROUPS_PER_STEP = 16
S5_SCAN_ROWS = 4096
S5_TILE_ELEMS = 2048 * 1024
MOD_ROWS = 8
VMEM_LIMIT = 56 * 1024 * 1024


def _cparams(n_axes):
    return pltpu.CompilerParams(dimension_semantics=("arbitrary",) * n_axes, vmem_limit_bytes=VMEM_LIMIT)


def _ln_rows(x):
    mu = jnp.mean(x, axis=-1, keepdims=True)
    xc = x - mu
    var = jnp.mean(xc * xc, axis=-1, keepdims=True)
    return xc * lax.rsqrt(var + LN_EPS)


def _silu(z):
    return z * jax.nn.sigmoid(z)


def _dot(a, b):
    return jnp.dot(a, b, preferred_element_type=F32)


def _dot_nt(a, b):
    return lax.dot_general(a, b, (((1,), (1,)), ((), ())), preferred_element_type=F32)


def _post_norm(x, gate, out, g, b, alpha):
    return _ln_rows(alpha * x + gate * out) * g + b


def _mod_kernel(c_ref, w_ref, b_ref, o_ref):
    c = _silu(c_ref[...]).astype(BF16)
    o_ref[...] = _dot(c, w_ref[...].astype(BF16)) + b_ref[...]


def _modulation(cond8, w_mod, b_mod):
    depth, d, n3 = w_mod.shape
    tn = 1024
    return pl.pallas_call(
        _mod_kernel,
        grid=(depth, n3 // tn),
        in_specs=[
            pl.BlockSpec((MOD_ROWS, d), lambda l, j: (0, 0)),
            pl.BlockSpec((None, d, tn), lambda l, j: (l, 0, j)),
            pl.BlockSpec((None, 1, tn), lambda l, j: (l, 0, j)),
        ],
        out_specs=pl.BlockSpec((None, MOD_ROWS, tn), lambda l, j: (l, 0, j)),
        out_shape=jax.ShapeDtypeStruct((depth, MOD_ROWS, n3), F32),
        compiler_params=_cparams(2),
        name="modulation",
    )(cond8, w_mod, b_mod.reshape(depth, 1, n3))


def _resident(shape):
    return pl.BlockSpec(shape, lambda *idx: (0,) * len(shape), pipeline_mode=pl.Buffered(1))


def _mod_spec(layer, part, cond_of):
    return pl.BlockSpec((None, 1, D_MODEL), lambda *idx: ((layer * MOD_ROWS + cond_of(*idx)) * 3 + part, 0, 0))


def _rope(x, cos, sin_signed, first_half):
    blocks = []
    for hh in range(x.shape[1] // 128):
        b = x[:, hh * 128:(hh + 1) * 128]
        partner = jnp.where(first_half, pltpu.roll(b, 128 - ROT_HALF, 1), pltpu.roll(b, ROT_HALF, 1))
        blocks.append(b * cos + partner * sin_signed)
    return jnp.concatenate(blocks, axis=1)


def _chunk_transpose_matrix():
    i = lax.broadcasted_iota(jnp.int32, (PERM_TILE, PERM_TILE), 0)
    j = lax.broadcasted_iota(jnp.int32, (PERM_TILE, PERM_TILE), 1)
    return (j == (i % CHUNK) * CHUNK + i // CHUNK).astype(BF16)


def _even_in_kernel(*refs, rope):
    if rope:
        (x_ref, shift_ref, scale_ref, w_ref, perm_ref, cos_ref, sin_ref,
         u_ref, sza_ref, q_ref, kb_ref, vb_ref, szb_ref) = refs
    else:
        (x_ref, shift_ref, scale_ref, w_ref, perm_ref, u_ref, sza_ref, q_ref, kb_ref, vb_ref, szb_ref,
         kf_ref, vf_ref) = refs
    w = D_MODEL
    for r in range(x_ref.shape[0] // PERM_TILE):
        rows = slice(r * PERM_TILE, (r + 1) * PERM_TILE)
        h = (_ln_rows(x_ref[rows, :]) * (1.0 + scale_ref[...]) + shift_ref[...]).astype(BF16)

        def proj(j):
            return _dot(h, w_ref[:, j * w:(j + 1) * w])

        u_ref[rows, :] = _dot(perm_ref[...], proj(0).astype(BF16)).astype(BF16)
        sza_ref[rows, :] = _silu(proj(1)).astype(BF16)
        q = proj(2) * (DH ** -0.5 * LOG2_E)
        k = proj(3)
        v = proj(4)
        if rope:
            lane = lax.broadcasted_iota(jnp.int32, (1, 128), 1)
            first_half = (lane % DH) < ROT_HALF
            q = _rope(q, cos_ref[rows, :], sin_ref[rows, :], first_half)
            kb_ref[rows, :] = _rope(k, cos_ref[rows, :], sin_ref[rows, :], first_half).astype(BF16)
        else:
            kf_ref[r] = k.T
            vf_ref[rows, :] = v
            kb_ref[rows, :] = k.astype(BF16)
        q_ref[rows, :] = q.astype(BF16)
        vb_ref[rows, :] = v.astype(BF16)
        szb_ref[rows, :] = _silu(proj(5)).astype(BF16)


def _even_in_proj(x2, mod3, w_in_bf, layer, cond_of, tm, rope_tabs, seq_len):
    t, d = x2.shape
    n = w_in_bf.shape[1]
    assert tm % PERM_TILE == 0
    row = pl.BlockSpec((tm, d), lambda i: (i, 0))
    in_specs = [row, _mod_spec(layer, 0, cond_of), _mod_spec(layer, 1, cond_of),
                _resident((d, n)), _resident((PERM_TILE, PERM_TILE))]
    args = [x2, mod3, mod3, w_in_bf, _chunk_transpose_matrix()]
    bf = jax.ShapeDtypeStruct((t, d), BF16)
    out_shape = [bf] * 6
    out_specs = [row] * 6
    if rope_tabs is not None:
        tiles_per_seq = seq_len // tm
        tab = pl.BlockSpec((tm, 128), lambda i: (i % tiles_per_seq, 0))
        in_specs += [tab, tab]
        args += list(rope_tabs)
    else:
        assert seq_len == PERM_TILE
        out_shape += [jax.ShapeDtypeStruct((t // seq_len, d, seq_len), F32), jax.ShapeDtypeStruct((t, d), F32)]
        out_specs += [pl.BlockSpec((tm // seq_len, d, seq_len), lambda i: (i, 0, 0)), row]
    return pl.pallas_call(
        functools.partial(_even_in_kernel, rope=rope_tabs is not None),
        grid=(t // tm,),
        in_specs=in_specs,
        out_specs=out_specs,
        out_shape=out_shape,
        compiler_params=_cparams(1),
        name="even_in_proj",
    )(*args)


BUILD_GROUPS = 4


def _s5_build_kernel(q_ref, b_ref, cn_ref, ccr_ref, cci_ref, ecr_ref, eci_ref, d_ref,
                     rep_ref, til_ref, wst_ref, kt_ref, wo_ref):
    def split(x):
        hi = x.astype(BF16)
        return hi, (x - hi.astype(F32)).astype(BF16)

    def hdot(a, b):
        (ah, al), (bh, bl) = split(a), split(b)
        return _dot(ah, bh) + _dot(ah, bl) + _dot(al, bh)

    lane = lax.broadcasted_iota(jnp.int32, (SSM_GROUP, SSM_TILE), 1)
    sub = lax.broadcasted_iota(jnp.int32, (SSM_GROUP, SSM_TILE), 0)
    zeros = jnp.zeros((SSM_GROUP, SSM_TILE), F32)
    rows4 = 4 * P_A
    def expand(x_t, sel):
        tn = lambda a: lax.dot_general(a, sel, (((0,), (0,)), ((), ())), preferred_element_type=F32)
        hi, lo = split(x_t)
        return tn(hi) + tn(lo)

    for g in range(BUILD_GROUPS):
        qe = expand(q_ref[g], rep_ref[...])
        be = expand(b_ref[g], til_ref[...])
        part = lambda x, k: x[k * P_A:(k + 1) * P_A]
        st_re = [part(qe, d) * part(be, d) - part(qe, 2 + d) * part(be, 2 + d) for d in range(2)]
        st_im = [part(qe, d) * part(be, 2 + d) + part(qe, 2 + d) * part(be, d) for d in range(2)]
        wst_ref[g] = jnp.concatenate([st_re[0], st_re[1], st_im[0], st_im[1]], axis=0).astype(BF16)
        klag = [hdot(cn_ref[d, g], jnp.concatenate([st_re[d], st_im[d]], axis=0)) for d in range(2)]
        fwd_p = jnp.concatenate([klag[0], zeros], axis=1)
        bwd_p = jnp.concatenate([zeros, klag[1]], axis=1)
        ccr, cci = ccr_ref[g], cci_ref[g]
        d_lanes = jnp.broadcast_to(d_ref[g], (SSM_GROUP, SSM_TILE))
        for t in range(CHUNK):
            lo_f = SSM_GROUP * (CHUNK - 1 - t)
            lo_b = SSM_TILE - SSM_GROUP * t
            skip = jnp.where(lane == t * SSM_GROUP + sub, d_lanes, 0.0)
            rows = slice(t * SSM_GROUP, (t + 1) * SSM_GROUP)
            kt_ref[g, rows, :] = (fwd_p[:, lo_f:lo_f + SSM_TILE] + bwd_p[:, lo_b:lo_b + SSM_TILE] + skip).astype(BF16)
            er, ei = ecr_ref[g, t:t + 1, :], eci_ref[g, t:t + 1, :]
            wo_ref[g, rows, :] = jnp.concatenate([ccr * er - cci * ei, -(ccr * ei + cci * er)], axis=1).astype(BF16)


def _s5_operators(lam_re, lam_im, log_dt, b_re, b_im, c_re, c_im, d_skip):
    lr, li = lam_re.astype(F32), lam_im.astype(F32)
    dt = jnp.exp(log_dt.astype(F32))[..., None]
    mag = jnp.exp(lr * dt)
    ar = mag * jnp.cos(li * dt)
    ai = mag * jnp.sin(li * dt)
    den = lr * lr + li * li
    fr = ((ar - 1.0) * lr + ai * li) / den
    fi = (ai * lr - (ar - 1.0) * li) / den
    br, bi = b_re.astype(F32), b_im.astype(F32)
    bbr = fr[..., None] * br - fi[..., None] * bi
    bbi = fr[..., None] * bi + fi[..., None] * br
    cr, ci = c_re.astype(F32), c_im.astype(F32)
    e_pow = jnp.arange(CHUNK + 1, dtype=F32)[:, None, None, None]
    pmag = jnp.exp(e_pow * (lr * dt))
    pr = pmag * jnp.cos(e_pow * (li * dt))
    pi = pmag * jnp.sin(e_pow * (li * dt))

    def by_position(p):
        return jnp.stack([p[:CHUNK, 0][::-1], p[:CHUNK, 1]], axis=0).transpose(0, 2, 1, 3)
    qr, qi = by_position(pr), by_position(pi)

    lane_sn = np.arange(SSM_TILE)
    rep = jnp.asarray(lane_sn[None, :] // SSM_GROUP == np.arange(CHUNK)[:, None], dtype=BF16)
    til = jnp.asarray(lane_sn[None, :] % SSM_GROUP == np.arange(SSM_GROUP)[:, None], dtype=BF16)

    def by_output(p):
        return jnp.concatenate([p[1:, 0], p[1:, 1][::-1]], axis=-1).transpose(1, 0, 2)
    cat_c = lambda c: jnp.concatenate([c[0], c[1]], axis=-1)

    def stack_lanes(re, im):
        return jnp.concatenate([re[0], re[1], im[0], im[1]], axis=-1)
    q_all = stack_lanes(qr, qi)
    b_all = stack_lanes(bbr.transpose(0, 1, 3, 2), bbi.transpose(0, 1, 3, 2))
    c_neg = jnp.concatenate([cr, -ci], axis=-1)

    gb = BUILD_GROUPS
    narrow = pl.BlockSpec((gb, SSM_GROUP, 4 * P_A), lambda i: (i, 0, 0))
    expand = pl.BlockSpec((SSM_GROUP, SSM_TILE), lambda i: (0, 0))
    cspec = pl.BlockSpec((2, gb, SSM_GROUP, 2 * P_A), lambda i: (0, i, 0, 0))
    half = pl.BlockSpec((gb, SSM_GROUP, 2 * P_A), lambda i: (i, 0, 0))
    dspec = pl.BlockSpec((gb, SSM_GROUP, 1), lambda i: (i, 0, 0))
    ospec = pl.BlockSpec((gb, SSM_TILE, SSM_TILE), lambda i: (i, 0, 0))
    oshape = jax.ShapeDtypeStruct((G_A, SSM_TILE, SSM_TILE), BF16)
    wst_t, kt_t, wo_t = pl.pallas_call(
        _s5_build_kernel,
        grid=(G_A // gb,),
        in_specs=[narrow] * 2 + [cspec] + [half] * 4 + [dspec, expand, expand],
        out_specs=[ospec] * 3,
        out_shape=[oshape] * 3,
        compiler_params=_cparams(1),
        name="s5_build_operators",
    )(q_all, b_all, c_neg, cat_c(cr), cat_c(ci),
      by_output(pr), by_output(pi), d_skip.astype(F32).reshape(G_A, SSM_GROUP, 1), rep, til)

    a16r = jnp.concatenate([pr[CHUNK, 0], pr[CHUNK, 1]], axis=-1)[:, None, :]
    a16i = jnp.concatenate([pi[CHUNK, 0], pi[CHUNK, 1]], axis=-1)[:, None, :]
    return wst_t, kt_t, wo_t, a16r, a16i


def _s5_kernel(u_ref, wst_ref, kt_ref, wo_ref, a16r_ref, a16i_ref, h0r_ref, h0i_ref,
               pin_ref, pout_ref, y_ref, fr_ref, br_ref, fi_ref, bi_ref, ut, yt, s_sc, hf_sc, hb_sc,
               *, nb, nc, gpb, il):
    gb = pl.program_id(2)
    seq = nc * CHUNK
    r = nb * nc
    sw = 2 * P_A
    n_cb = u_ref.shape[1] // LANES
    pieces = [b * seq + q * PERM_TILE for b in range(nb) for q in range(seq // PERM_TILE)]

    @pl.when(gb == 0)
    def _():
        for s in range(CHUNK):
            cols = []
            for cb in range(n_cb):
                rows = jnp.concatenate([u_ref[o + s * CHUNK:o + (s + 1) * CHUNK, cb * LANES:(cb + 1) * LANES]
                                        for o in pieces], axis=0)
                cols.append(rows.astype(F32).T.astype(BF16))
            ut[:, s * r:(s + 1) * r] = _dot(jnp.concatenate(cols, axis=0), pin_ref[...]).astype(BF16)

    fwd = (lax.broadcasted_iota(jnp.int32, (1, sw), 1)) < P_A

    def groups(it, carry):
        js = [it * il + i for i in range(il)]
        grows = [pl.multiple_of((gb * gpb + j) * SSM_GROUP, SSM_GROUP) for j in js]
        ds = [jnp.concatenate([ut[pl.ds(g, SSM_GROUP), s * r:(s + 1) * r] for s in range(CHUNK)], axis=0)
              for g in grows]
        for i, j in enumerate(js):
            s_t = _dot(wst_ref[j], ds[i])
            s_sc[i, 0] = s_t[:sw].T
            s_sc[i, 1] = s_t[sw:].T
        a_r = [a16r_ref[j] for j in js]
        a_i = [a16i_ref[j] for j in js]
        re = [h0r_ref[j] for j in js]
        im = [h0i_ref[j] for j in js]
        for k in range(nc):
            rf = pl.ds(k * nb, nb)
            rb = pl.ds((nc - 1 - k) * nb, nb)
            for i in range(il):
                hf_sc[i, 0, rf, :] = re[i]
                hf_sc[i, 1, rf, :] = im[i]
                hb_sc[i, 0, rb, :] = re[i]
                hb_sc[i, 1, rb, :] = im[i]
                xr = jnp.where(fwd, s_sc[i, 0, rf, :], s_sc[i, 0, rb, :])
                xi = jnp.where(fwd, s_sc[i, 1, rf, :], s_sc[i, 1, rb, :])
                re[i], im[i] = a_r[i] * re[i] - a_i[i] * im[i] + xr, a_r[i] * im[i] + a_i[i] * re[i] + xi
        for i, j in enumerate(js):
            fr_ref[j] = re[i]
            br_ref[j] = pltpu.roll(re[i], P_A, 1)
            fi_ref[j] = im[i]
            bi_ref[j] = pltpu.roll(im[i], P_A, 1)
            hin_t = jnp.concatenate([jnp.where(fwd, hf_sc[i, 0], hb_sc[i, 0]).T,
                                     jnp.where(fwd, hf_sc[i, 1], hb_sc[i, 1]).T], axis=0)
            y_t = (_dot(kt_ref[j], ds[i]) + _dot(wo_ref[j], hin_t.astype(BF16))).astype(BF16)
            for t in range(CHUNK):
                yt[pl.ds(grows[i], SSM_GROUP), t * r:(t + 1) * r] = y_t[t * SSM_GROUP:(t + 1) * SSM_GROUP, :]
        return carry

    lax.fori_loop(0, gpb // il, groups, 0)

    @pl.when(gb == pl.num_programs(2) - 1)
    def _():
        for t in range(CHUNK):
            full = _dot(yt[:, t * r:(t + 1) * r], pout_ref[...])
            for cb in range(n_cb):
                rows = full[cb * LANES:(cb + 1) * LANES, :].T.astype(BF16)
                for idx, o in enumerate(pieces):
                    y_ref[o + t * CHUNK:o + (t + 1) * CHUNK, cb * LANES:(cb + 1) * LANES] = (
                        rows[idx * CHUNK:(idx + 1) * CHUNK, :])


def _s5_mix(u, ops, h0_re, h0_im, bsz, seq_len, nb, cw):
    wst_t, kt_t, wo_t, a16r, a16i = ops
    t, w = u.shape
    nc = seq_len // CHUNK
    r = nb * nc
    gpb = S5_GROUPS_PER_STEP
    n_row_tiles = bsz // nb
    n_col_tiles = w // cw
    gb_per_tile = cw // SSM_GROUP // gpb
    assert r % 128 == 0 and cw % (SSM_GROUP * gpb) == 0 and bsz % nb == 0

    def h0_layout(h0):
        return h0.astype(F32).transpose(2, 0, 1, 3).reshape(G_A, bsz, 2 * P_A)

    tile = pl.BlockSpec((nb * seq_len, cw), lambda i, c, g: (i, c))
    grp = lambda i, c, g: (c * gb_per_tile + g, 0, 0)
    wspec = pl.BlockSpec((gpb, SSM_TILE, SSM_TILE), grp)
    aspec = pl.BlockSpec((gpb, 1, 2 * P_A), grp)
    hspec = pl.BlockSpec((gpb, nb, 2 * P_A), lambda i, c, g: (c * gb_per_tile + g, i, 0))
    state = jax.ShapeDtypeStruct((G_A, bsz, 2 * P_A), F32)
    il = min(gpb, S5_SCAN_ROWS // r)
    scratch = [pltpu.VMEM((cw, CHUNK * r), BF16)] * 2 + [pltpu.VMEM((il, 2, r, 2 * P_A), F32)] * 3
    src = lax.broadcasted_iota(jnp.int32, (r, r), 0)
    dst = lax.broadcasted_iota(jnp.int32, (r, r), 1)
    pin = (dst == (src % nc) * nb + src // nc).astype(BF16)
    pspec = pl.BlockSpec((r, r), lambda i, c, g: (0, 0))

    y, fwd_re, bwd_re, fwd_im, bwd_im = pl.pallas_call(
        functools.partial(_s5_kernel, nb=nb, nc=nc, gpb=gpb, il=il),
        grid=(n_row_tiles, n_col_tiles, gb_per_tile),
        in_specs=[tile, wspec, wspec, wspec, aspec, aspec, hspec, hspec, pspec, pspec],
        out_specs=[tile] + [hspec] * 4,
        out_shape=[jax.ShapeDtypeStruct((t, w), BF16)] + [state] * 4,
        scratch_shapes=scratch,
        compiler_params=_cparams(3),
        name="s5_core",
    )(u, wst_t, kt_t, wo_t, a16r, a16i, h0_layout(h0_re), h0_layout(h0_im), pin, pin.T)

    def fin(f, b):
        return jnp.stack([f[:, :, :P_A], b[:, :, :P_A]], axis=0).transpose(2, 0, 1, 3)
    return y, fin(fwd_re, bwd_re), fin(fwd_im, bwd_im)


def _diff_lambda(lq1, lk1, lq2, lk2, lam_init):
    return (jnp.exp(jnp.sum(lq1[...] * lk1[...], axis=-1, keepdims=True))
            - jnp.exp(jnp.sum(lq2[...] * lk2[...], axis=-1, keepdims=True)) + lam_init)


def _attn_kernel(*refs, lam_init, cached, seq_len):
    if cached:
        q_ref, kc_ref, kn_ref, vc_ref, vn_ref, szb_ref, lq1, lk1, lq2, lk2, g_ref, o_ref = refs
    else:
        q_ref, kn_ref, vn_ref, szb_ref, lq1, lk1, lq2, lk2, g_ref, o_ref = refs
    lam = _diff_lambda(lq1, lk1, lq2, lk2, lam_init)
    lq = q_ref.shape[0] if cached else seq_len
    hw = 2 * DH
    low = lax.broadcasted_iota(jnp.int32, (1, hw), 1) < DH
    zero = jnp.zeros((), BF16)
    ones = jnp.ones((kn_ref.shape[0] if cached else seq_len, hw), BF16)
    for h in range(H_B * (q_ref.shape[0] // lq)):
        rows = slice((h // H_B) * lq, (h // H_B + 1) * lq)
        krows = slice(None) if cached else rows
        cols = slice((h % H_B) * hw, (h % H_B + 1) * hw)
        qh = q_ref[rows, cols]
        qs = jnp.concatenate([jnp.where(low, qh, zero), jnp.where(low, zero, qh)], axis=0)
        s = _dot_nt(qs, kn_ref[krows, cols])
        if cached:
            s = jnp.concatenate([_dot_nt(qs, kc_ref[:, cols]), s], axis=1)
        e = jnp.exp2(s - jnp.max(s, axis=-1, keepdims=True)).astype(BF16)
        vn = jnp.concatenate([vn_ref[krows, cols], ones], axis=1)
        if cached:
            lc = kc_ref.shape[0]
            vc = jnp.concatenate([vc_ref[:, cols], ones[:lc]], axis=1)
            oa = _dot(e[:, :lc], vc) + _dot(e[:, lc:], vn)
        else:
            oa = _dot(e, vn)
        on = oa[:, :hw] * (1.0 / oa[:, hw:])
        o = on[:lq] - lam * on[lq:]
        o = o * lax.rsqrt(jnp.mean(o * o, axis=-1, keepdims=True) + LN_EPS)
        o = o * g_ref[...] * (1.0 - lam_init)
        o_ref[rows, cols] = (o * szb_ref[rows, cols].astype(F32)).astype(BF16)


def _lam_specs(n_axes):
    zero = lambda *idx: (0, 0)
    return [pl.BlockSpec((1, DH), zero)] * 4 + [pl.BlockSpec((1, 2 * DH), zero)]


def _attention_prompt(q, k, v, szb, lam_vecs, subln, lam_init, bsz, seq_len):
    blk = pl.BlockSpec((ATTN_SEQS_PER_STEP * seq_len, D_MODEL), lambda b: (b, 0))
    return pl.pallas_call(
        functools.partial(_attn_kernel, lam_init=lam_init, cached=False, seq_len=seq_len),
        grid=(bsz // ATTN_SEQS_PER_STEP,),
        in_specs=[blk] * 4 + _lam_specs(1),
        out_specs=blk,
        out_shape=jax.ShapeDtypeStruct(q.shape, BF16),
        compiler_params=_cparams(1),
        name="diff_attention_prompt",
    )(q, k, v, szb, *lam_vecs, subln)


def _attention_sample(q, kc, kn, vc, vn, szb, lam_vecs, subln, lam_init, bsz, seq_len, tq):
    nq = seq_len // tq
    past = kc.shape[1]
    qblk = pl.BlockSpec((tq, D_MODEL), lambda b, i: (b * nq + i, 0))
    cblk = pl.BlockSpec((None, past, D_MODEL), lambda b, i: (b, 0, 0))
    nblk = pl.BlockSpec((seq_len, D_MODEL), lambda b, i: (b, 0))
    return pl.pallas_call(
        functools.partial(_attn_kernel, lam_init=lam_init, cached=True, seq_len=seq_len),
        grid=(bsz, nq),
        in_specs=[qblk, cblk, nblk, cblk, nblk, qblk] + _lam_specs(2),
        out_specs=qblk,
        out_shape=jax.ShapeDtypeStruct(q.shape, BF16),
        compiler_params=_cparams(2),
        name="diff_attention_sample",
    )(q, kc, kn, vc, vn, szb, *lam_vecs, subln)


def _even_out_kernel(ys_ref, sza_ref, yb_ref, x_ref, gate_ref, wglu_ref, bglu_ref, wout_ref, g_ref, b_ref,
                     perm_ref, *rest, alpha):
    o_ref = rest[0] if len(rest) == 1 else rest[5]
    wa = ys_ref.shape[1]
    for r in range(x_ref.shape[0] // PERM_TILE):
        rows = slice(r * PERM_TILE, (r + 1) * PERM_TILE)
        ga = jax.nn.gelu(_dot(perm_ref[...], ys_ref[rows, :]))
        glu = jax.nn.sigmoid(_dot(ga.astype(BF16), wglu_ref[...]) + bglu_ref[...])
        ya = (ga * glu * sza_ref[rows, :].astype(F32)).astype(BF16)
        out = _dot(ya, wout_ref[:wa, :]) + _dot(yb_ref[rows, :], wout_ref[wa:, :])
        x_new = _post_norm(x_ref[rows, :], gate_ref[...], out, g_ref[...], b_ref[...], alpha)
        o_ref[rows, :] = x_new
        if len(rest) > 1:
            _odd_in_rows(x_new, rows, *rest[:5], *rest[6:])


def _even_out_proj(ys, sza, yb, x2, mod3, w_glu_bf, b_glu, w_out_bf, ln_g, ln_b, layer, cond_of, tm, alpha,
                   next_odd=None):
    t, d = x2.shape
    assert tm % PERM_TILE == 0 and PERM_TILE == SUB_ROWS
    row = pl.BlockSpec((tm, d), lambda i: (i, 0))
    full = _resident
    in_specs = [row, row, row, row, _mod_spec(layer, 2, cond_of), full(w_glu_bf.shape), full((1, d)),
                full(w_out_bf.shape), full((1, d)), full((1, d)), full((PERM_TILE, PERM_TILE))]
    args = [ys, sza, yb, x2, mod3, w_glu_bf, b_glu.reshape(1, d), w_out_bf, ln_g.reshape(1, d), ln_b.reshape(1, d),
            _chunk_transpose_matrix()]
    out_specs, out_shape = [row], [jax.ShapeDtypeStruct((t, d), F32)]
    if next_odd is not None:
        w_in_bf, cc, sc = next_odd
        wide = pl.BlockSpec((tm, W_C), lambda i: (i, 0))
        in_specs += [_mod_spec(layer + 1, 0, cond_of), _mod_spec(layer + 1, 1, cond_of), full(w_in_bf.shape),
                     full(cc.shape), full(sc.shape)]
        args += [mod3, mod3, w_in_bf, cc, sc]
        out_specs += [wide] * 3
        out_shape += [jax.ShapeDtypeStruct((t, W_C), BF16)] * 3
    res = pl.pallas_call(
        functools.partial(_even_out_kernel, alpha=alpha),
        grid=(t // tm,),
        in_specs=in_specs,
        out_specs=out_specs,
        out_shape=out_shape,
        compiler_params=_cparams(1),
        name="even_out_proj",
    )(*args)
    return res[0] if next_odd is None else tuple(res)


def _dft_mats(n, scale):
    k = np.arange(n, dtype=np.int64)
    ang = ((k[:, None] * k[None, :]) % n).astype(np.float64) * (2.0 * math.pi / n)
    return (jnp.asarray((np.cos(ang) * scale).astype(np.float32)).astype(BF16),
            jnp.asarray((np.sin(ang) * scale).astype(np.float32)).astype(BF16))


def _odd_in_rows(x, rows, shift_ref, scale_ref, w_ref, cc_ref, sc_ref, uc_ref, us_ref, sz_ref):
    h = (_ln_rows(x) * (1.0 + scale_ref[...]) + shift_ref[...]).astype(BF16)
    u = _dot(h, w_ref[:, :W_C]).astype(BF16)
    sz_ref[rows, :] = _silu(_dot(h, w_ref[:, W_C:])).astype(BF16)
    for g in range(NG_C):
        cols = slice(g * GC_C, (g + 1) * GC_C)
        uc_ref[rows, cols] = _dot(u[:, cols], cc_ref[...]).astype(BF16)
        us_ref[rows, cols] = _dot(u[:, cols], sc_ref[...]).astype(BF16)


def _odd_out_kernel(cl_ref, sl_ref, uc_ref, us_ref, sz_ref, x_ref, gate_ref, wf_ref, bf_ref, wo_ref, g_ref, b_ref,
                    o_ref, *, alpha, whole_seqs):
    seq_len = cl_ref.shape[1]
    sub = seq_len if whole_seqs else SUB_ROWS
    for r in range(x_ref.shape[0] // sub):
        rows = slice(r * sub, (r + 1) * sub)
        if whole_seqs:
            mixed = _dot(cl_ref[...], uc_ref[rows, :]) - _dot(sl_ref[...], us_ref[rows, :])
        else:
            mixed = _dot(cl_ref[rows, :], uc_ref[...]) - _dot(sl_ref[rows, :], us_ref[...])
        y = ((_dot(mixed.astype(BF16), wf_ref[...]) + bf_ref[...]) * sz_ref[rows, :].astype(F32)).astype(BF16)
        o_ref[rows, :] = _post_norm(x_ref[rows, :], gate_ref[...], _dot(y, wo_ref[...]), g_ref[...], b_ref[...],
                                    alpha)


def _odd_out_proj(uc, us, sz, x2, mod3, cl, sl, w_fno_bf, b_fno, w_out_bf, ln_g, ln_b, layer, cond_of,
                  bsz, seq_len, tl, alpha):
    t, d = x2.shape
    whole_seqs = tl >= seq_len
    full = _resident
    if whole_seqs:
        assert tl % seq_len == 0
        grid = (t // tl, 1)
        dft = _resident((seq_len, seq_len))
        seq = pl.BlockSpec((tl, W_C), lambda b, i: (b, 0))
        wide = seq
        row = pl.BlockSpec((tl, d), lambda b, i: (b, 0))
    else:
        assert seq_len % tl == 0 and tl % SUB_ROWS == 0
        nl = seq_len // tl
        grid = (bsz, nl)
        dft = pl.BlockSpec((tl, seq_len), lambda b, i: (i, 0))
        seq = pl.BlockSpec((seq_len, W_C), lambda b, i: (b, 0))
        wide = pl.BlockSpec((tl, W_C), lambda b, i: (b * nl + i, 0))
        row = pl.BlockSpec((tl, d), lambda b, i: (b * nl + i, 0))
    return pl.pallas_call(
        functools.partial(_odd_out_kernel, alpha=alpha, whole_seqs=whole_seqs),
        grid=grid,
        in_specs=[dft, dft, seq, seq, wide, row, _mod_spec(layer, 2, cond_of), full(w_fno_bf.shape),
                  full((1, W_C)), full(w_out_bf.shape), full((1, d)), full((1, d))],
        out_specs=row,
        out_shape=jax.ShapeDtypeStruct((t, d), F32),
        compiler_params=_cparams(2),
        name="odd_out_proj",
    )(cl, sl, uc, us, sz, x2, mod3, w_fno_bf, b_fno.reshape(1, W_C), w_out_bf, ln_g.reshape(1, d),
      ln_b.reshape(1, d))


def _rope_tables(seq_len):
    rows = seq_len // GRID_W
    row = jnp.repeat(jnp.arange(rows), GRID_W).astype(F32)
    col = jnp.tile(jnp.arange(GRID_W), rows).astype(F32)
    freqs = ROPE_BASE ** (-jnp.arange(ROT_FREQS, dtype=F32) / ROT_FREQS)
    ang = jnp.concatenate([row[:, None] * freqs, col[:, None] * freqs], axis=-1)
    cos, sin = jnp.cos(ang), jnp.sin(ang)
    cos128 = jnp.tile(cos, (1, 128 // ROT_HALF))
    sin128 = jnp.tile(jnp.concatenate([-sin, sin], axis=-1), (1, 128 // DH))
    return cos128, sin128


def kernel(x_prompt, x_sample, cache_k, cache_v, state_ssm_re, state_ssm_im, c, c_ctx, w_mod, b_mod, ln_g, ln_b, w_in_e, ssm_lam_re, ssm_lam_im, ssm_log_dt, ssm_b_re, ssm_b_im, ssm_c_re, ssm_c_im, ssm_d, w_glu, b_glu, lam_q1, lam_k1, lam_q2, lam_k2, subln_g, w_out_e, w_in_o, w_fno, b_fno, w_out_o):
    depth = w_mod.shape[0]
    bp_, lp, d = x_prompt.shape
    bs_, ls, _ = x_sample.shape
    past = cache_k.shape[2]
    alpha = (2 * depth) ** 0.25
    assert bs_ + 1 <= MOD_ROWS and d == D_MODEL

    cond8 = jnp.concatenate([c_ctx[None, :], c, jnp.zeros((MOD_ROWS - 1 - bs_, d), F32)], axis=0).astype(F32)
    mod3 = _modulation(cond8, w_mod, b_mod).reshape(depth * MOD_ROWS * 3, 1, d)

    tm = ROW_TILE
    cond_p = lambda *idx: 0
    cond_s_row = lambda i: 1 + i // (ls // tm)
    cond_s_grid = lambda b, i: 1 + b
    rope_tabs = _rope_tables(ls)
    xp = x_prompt.reshape(bp_ * lp, d)
    xs = x_sample.reshape(bs_ * ls, d)
    new_k, new_v, new_sr, new_si = [], [], [], []
    zeros_h0 = jnp.zeros((bp_, 2, G_A, P_A), F32)

    for layer in range(depth):
        if layer % 2 == 0:
            e = layer // 2
            lam_init = 0.8 - 0.6 * math.exp(-0.3 * layer)
            w_in_bf = w_in_e[e].astype(BF16)
            w_glu_bf = w_glu[e].astype(BF16)
            w_out_bf = w_out_e[e].astype(BF16)
            ops = _s5_operators(ssm_lam_re[e], ssm_lam_im[e], ssm_log_dt[e], ssm_b_re[e], ssm_b_im[e],
                                ssm_c_re[e], ssm_c_im[e], ssm_d[e])
            lam_vecs = [v[e].reshape(1, DH).astype(F32) for v in (lam_q1, lam_k1, lam_q2, lam_k2)]
            subln = subln_g[e].reshape(1, 2 * DH).astype(F32)

            u, sza, q, kb, vb, szb, kf, vf = _even_in_proj(xp, mod3, w_in_bf, layer, cond_p, tm, None, lp)
            new_k.append(kf.reshape(bp_, H_B, 2, DH, lp).transpose(0, 4, 1, 2, 3))
            new_v.append(vf.reshape(bp_, lp, H_B, 2 * DH))
            ys, s_re, s_im = _s5_mix(u, ops, zeros_h0, zeros_h0, bp_, lp, 2 * S5_TILE_ELEMS // (lp * d), d // 2)
            new_sr.append(s_re)
            new_si.append(s_im)
            yb = _attention_prompt(q, kb, vb, szb, lam_vecs, subln, lam_init, bp_, lp)
            next_odd = None
            if layer + 1 < depth:
                next_odd = (w_in_o[(layer + 1) // 2].astype(BF16),) + _dft_mats(GC_C, GC_C ** -0.5)
            res = _even_out_proj(ys, sza, yb, xp, mod3, w_glu_bf, b_glu[e], w_out_bf, ln_g[layer], ln_b[layer],
                                 layer, cond_p, tm, alpha, next_odd)
            xp, fused_p = (res, None) if next_odd is None else (res[0], res[1:])

            u, sza, q, kb, vb, szb = _even_in_proj(xs, mod3, w_in_bf, layer, cond_s_row, tm, rope_tabs, ls)
            ys, _, _ = _s5_mix(u, ops, state_ssm_re[:, e], state_ssm_im[:, e], bs_, ls, bs_,
                               S5_TILE_ELEMS // (bs_ * ls))
            kc = cache_k[:, e].reshape(bs_, past, d).astype(BF16)
            vc = cache_v[:, e].reshape(bs_, past, d).astype(BF16)
            yb = _attention_sample(q, kc, kb, vc, vb, szb, lam_vecs, subln, lam_init, bs_, ls, ATTN_Q_TILE)
            res = _even_out_proj(ys, sza, yb, xs, mod3, w_glu_bf, b_glu[e], w_out_bf, ln_g[layer], ln_b[layer],
                                 layer, cond_s_row, tm, alpha, next_odd)
            xs, fused_s = (res, None) if next_odd is None else (res[0], res[1:])
        else:
            o = layer // 2
            w_fno_bf = w_fno[o].astype(BF16)
            w_out_bf = w_out_o[o].astype(BF16)
            for which in ("prompt", "sample"):
                if which == "prompt":
                    x2, cond_row, cond_grid, bsz, seq, fused = xp, cond_p, cond_p, bp_, lp, fused_p
                else:
                    x2, cond_row, cond_grid, bsz, seq, fused = xs, cond_s_row, cond_s_grid, bs_, ls, fused_s
                cl, sl = _dft_mats(seq, seq ** -0.5)
                uc, us, sz = fused
                x2 = _odd_out_proj(uc, us, sz, x2, mod3, cl, sl, w_fno_bf, b_fno[o], w_out_bf, ln_g[layer],
                                   ln_b[layer], layer, cond_grid, bsz, seq, tm, alpha)
                if which == "prompt":
                    xp = x2
                else:
                    xs = x2

    return (xp.reshape(bp_, lp, d), xs.reshape(bs_, ls, d), jnp.stack(new_k, axis=1), jnp.stack(new_v, axis=1),
            jnp.stack(new_sr, axis=1), jnp.stack(new_si, axis=1))
```

```python
import functools
import math

import jax
import jax.numpy as jnp
import numpy as np
from jax import lax
from jax.experimental import pallas as pl
from jax.experimental.pallas import tpu as pltpu

F32 = jnp.float32
BF16 = jnp.bfloat16

D_MODEL = 1024
GRID_W = 64
SSM_GROUP = 16
G_A = D_MODEL // SSM_GROUP
P_A = 64
DH = 64
H_B = D_MODEL // (2 * DH)
ROPE_BASE = 10000.0
ROT_HALF = DH // 2
ROT_FREQS = DH // 4
NG_C = 8
GC_C = 2 * D_MODEL // NG_C
W_C = 2 * D_MODEL
LN_EPS = 1e-5
LOG2_E = 1.4426950408889634
CHUNK = 16
SSM_TILE = CHUNK * SSM_GROUP
LANES = 128
PERM_TILE = CHUNK * CHUNK
SUB_ROWS = PERM_TILE
ROW_TILE = 2 * SUB_ROWS
ATTN_Q_TILE = 512
ATTN_SEQS_PER_STEP = 4
S5_GROUPS_PER_STEP = 16
S5_SCAN_ROWS = 4096
S5_TILE_ELEMS = 2048 * 1024
MOD_ROWS = 8
VMEM_LIMIT = 56 * 1024 * 1024


def _cparams(n_axes):
    return pltpu.CompilerParams(dimension_semantics=("arbitrary",) * n_axes, vmem_limit_bytes=VMEM_LIMIT)


def _ln_rows(x):
    mu = jnp.mean(x, axis=-1, keepdims=True)
    xc = x - mu
    var = jnp.mean(xc * xc, axis=-1, keepdims=True)
    return xc * lax.rsqrt(var + LN_EPS)


def _silu(z):
    return z * jax.nn.sigmoid(z)


def _dot(a, b):
    return jnp.dot(a, b, preferred_element_type=F32)


def _dot_nt(a, b):
    return lax.dot_general(a, b, (((1,), (1,)), ((), ())), preferred_element_type=F32)


def _post_norm(x, gate, out, g, b, alpha):
    return _ln_rows(alpha * x + gate * out) * g + b


def _mod_kernel(c_ref, w_ref, b_ref, o_ref):
    c = _silu(c_ref[...]).astype(BF16)
    o_ref[...] = _dot(c, w_ref[...].astype(BF16)) + b_ref[...]


def _modulation(cond8, w_mod, b_mod):
    depth, d, n3 = w_mod.shape
    tn = 1024
    return pl.pallas_call(
        _mod_kernel,
        grid=(depth, n3 // tn),
        in_specs=[
            pl.BlockSpec((MOD_ROWS, d), lambda l, j: (0, 0)),
            pl.BlockSpec((None, d, tn), lambda l, j: (l, 0, j)),
            pl.BlockSpec((None, 1, tn), lambda l, j: (l, 0, j)),
        ],
        out_specs=pl.BlockSpec((None, MOD_ROWS, tn), lambda l, j: (l, 0, j)),
        out_shape=jax.ShapeDtypeStruct((depth, MOD_ROWS, n3), F32),
        compiler_params=_cparams(2),
        name="modulation",
    )(cond8, w_mod, b_mod.reshape(depth, 1, n3))


def _resident(shape):
    return pl.BlockSpec(shape, lambda *idx: (0,) * len(shape), pipeline_mode=pl.Buffered(1))


def _mod_spec(layer, part, cond_of):
    return pl.BlockSpec((None, 1, D_MODEL), lambda *idx: ((layer * MOD_ROWS + cond_of(*idx)) * 3 + part, 0, 0))


def _rope(x, cos, sin_signed, first_half):
    blocks = []
    for hh in range(x.shape[1] // 128):
        b = x[:, hh * 128:(hh + 1) * 128]
        partner = jnp.where(first_half, pltpu.roll(b, 128 - ROT_HALF, 1), pltpu.roll(b, ROT_HALF, 1))
        blocks.append(b * cos + partner * sin_signed)
    return jnp.concatenate(blocks, axis=1)


def _chunk_transpose_matrix():
    i = lax.broadcasted_iota(jnp.int32, (PERM_TILE, PERM_TILE), 0)
    j = lax.broadcasted_iota(jnp.int32, (PERM_TILE, PERM_TILE), 1)
    return (j == (i % CHUNK) * CHUNK + i // CHUNK).astype(BF16)


def _even_in_kernel(*refs, rope):
    if rope:
        (x_ref, shift_ref, scale_ref, w_ref, perm_ref, cos_ref, sin_ref,
         u_ref, sza_ref, q_ref, kb_ref, vb_ref, szb_ref) = refs
    else:
        (x_ref, shift_ref, scale_ref, w_ref, perm_ref, u_ref, sza_ref, q_ref, kb_ref, vb_ref, szb_ref,
         kf_ref, vf_ref) = refs
    w = D_MODEL
    for r in range(x_ref.shape[0] // PERM_TILE):
        rows = slice(r * PERM_TILE, (r + 1) * PERM_TILE)
        h = (_ln_rows(x_ref[rows, :]) * (1.0 + scale_ref[...]) + shift_ref[...]).astype(BF16)

        def proj(j):
            return _dot(h, w_ref[:, j * w:(j + 1) * w])

        u_ref[rows, :] = _dot(perm_ref[...], proj(0).astype(BF16)).astype(BF16)
        sza_ref[rows, :] = _silu(proj(1)).astype(BF16)
        q = proj(2) * (DH ** -0.5 * LOG2_E)
        k = proj(3)
        v = proj(4)
        if rope:
            lane = lax.broadcasted_iota(jnp.int32, (1, 128), 1)
            first_half = (lane % DH) < ROT_HALF
            q = _rope(q, cos_ref[rows, :], sin_ref[rows, :], first_half)
            kb_ref[rows, :] = _rope(k, cos_ref[rows, :], sin_ref[rows, :], first_half).astype(BF16)
        else:
            kf_ref[r] = k.T
            vf_ref[rows, :] = v
            kb_ref[rows, :] = k.astype(BF16)
        q_ref[rows, :] = q.astype(BF16)
        vb_ref[rows, :] = v.astype(BF16)
        szb_ref[rows, :] = _silu(proj(5)).astype(BF16)


def _even_in_proj(x2, mod3, w_in_bf, layer, cond_of, tm, rope_tabs, seq_len):
    t, d = x2.shape
    n = w_in_bf.shape[1]
    assert tm % PERM_TILE == 0
    row = pl.BlockSpec((tm, d), lambda i: (i, 0))
    in_specs = [row, _mod_spec(layer, 0, cond_of), _mod_spec(layer, 1, cond_of),
                _resident((d, n)), _resident((PERM_TILE, PERM_TILE))]
    args = [x2, mod3, mod3, w_in_bf, _chunk_transpose_matrix()]
    bf = jax.ShapeDtypeStruct((t, d), BF16)
    out_shape = [bf] * 6
    out_specs = [row] * 6
    if rope_tabs is not None:
        tiles_per_seq = seq_len // tm
        tab = pl.BlockSpec((tm, 128), lambda i: (i % tiles_per_seq, 0))
        in_specs += [tab, tab]
        args += list(rope_tabs)
    else:
        assert seq_len == PERM_TILE
        out_shape += [jax.ShapeDtypeStruct((t // seq_len, d, seq_len), F32), jax.ShapeDtypeStruct((t, d), F32)]
        out_specs += [pl.BlockSpec((tm // seq_len, d, seq_len), lambda i: (i, 0, 0)), row]
    return pl.pallas_call(
        functools.partial(_even_in_kernel, rope=rope_tabs is not None),
        grid=(t // tm,),
        in_specs=in_specs,
        out_specs=out_specs,
        out_shape=out_shape,
        compiler_params=_cparams(1),
        name="even_in_proj",
    )(*args)


BUILD_GROUPS = 8


def _s5_build_kernel(q_ref, b_ref, cn_ref, ccr_ref, cci_ref, ecr_ref, eci_ref, d_ref,
                     rep_ref, til_ref, wst_ref, kt_ref, wo_ref):
    def split(x):
        hi = x.astype(BF16)
        return hi, (x - hi.astype(F32)).astype(BF16)

    def hdot(a, b):
        (ah, al), (bh, bl) = split(a), split(b)
        return _dot(ah, bh) + _dot(ah, bl) + _dot(al, bh)

    lane = lax.broadcasted_iota(jnp.int32, (SSM_GROUP, SSM_TILE), 1)
    sub = lax.broadcasted_iota(jnp.int32, (SSM_GROUP, SSM_TILE), 0)
    zeros = jnp.zeros((SSM_GROUP, SSM_TILE), F32)
    rows4 = 4 * P_A
    def expand(x_t, sel):
        tn = lambda a: lax.dot_general(a, sel, (((0,), (0,)), ((), ())), preferred_element_type=F32)
        hi, lo = split(x_t)
        return tn(hi) + tn(lo)

    for g in range(BUILD_GROUPS):
        qe = expand(q_ref[g], rep_ref[...])
        be = expand(b_ref[g], til_ref[...])
        part = lambda x, k: x[k * P_A:(k + 1) * P_A]
        st_re = [part(qe, d) * part(be, d) - part(qe, 2 + d) * part(be, 2 + d) for d in range(2)]
        st_im = [part(qe, d) * part(be, 2 + d) + part(qe, 2 + d) * part(be, d) for d in range(2)]
        wst_ref[g] = jnp.concatenate([st_re[0], st_re[1], st_im[0], st_im[1]], axis=0).astype(BF16)
        klag = [hdot(cn_ref[d, g], jnp.concatenate([st_re[d], st_im[d]], axis=0)) for d in range(2)]
        fwd_p = jnp.concatenate([klag[0], zeros], axis=1)
        bwd_p = jnp.concatenate([zeros, klag[1]], axis=1)
        ccr, cci = ccr_ref[g], cci_ref[g]
        d_lanes = jnp.broadcast_to(d_ref[g], (SSM_GROUP, SSM_TILE))
        for t in range(CHUNK):
            lo_f = SSM_GROUP * (CHUNK - 1 - t)
            lo_b = SSM_TILE - SSM_GROUP * t
            skip = jnp.where(lane == t * SSM_GROUP + sub, d_lanes, 0.0)
            rows = slice(t * SSM_GROUP, (t + 1) * SSM_GROUP)
            kt_ref[g, rows, :] = (fwd_p[:, lo_f:lo_f + SSM_TILE] + bwd_p[:, lo_b:lo_b + SSM_TILE] + skip).astype(BF16)
            er, ei = ecr_ref[g, t:t + 1, :], eci_ref[g, t:t + 1, :]
            wo_ref[g, rows, :] = jnp.concatenate([ccr * er - cci * ei, -(ccr * ei + cci * er)], axis=1).astype(BF16)


def _s5_operators(lam_re, lam_im, log_dt, b_re, b_im, c_re, c_im, d_skip):
    lr, li = lam_re.astype(F32), lam_im.astype(F32)
    dt = jnp.exp(log_dt.astype(F32))[..., None]
    mag = jnp.exp(lr * dt)
    ar = mag * jnp.cos(li * dt)
    ai = mag * jnp.sin(li * dt)
    den = lr * lr + li * li
    fr = ((ar - 1.0) * lr + ai * li) / den
    fi = (ai * lr - (ar - 1.0) * li) / den
    br, bi = b_re.astype(F32), b_im.astype(F32)
    bbr = fr[..., None] * br - fi[..., None] * bi
    bbi = fr[..., None] * bi + fi[..., None] * br
    cr, ci = c_re.astype(F32), c_im.astype(F32)
    e_pow = jnp.arange(CHUNK + 1, dtype=F32)[:, None, None, None]
    pmag = jnp.exp(e_pow * (lr * dt))
    pr = pmag * jnp.cos(e_pow * (li * dt))
    pi = pmag * jnp.sin(e_pow * (li * dt))

    def by_position(p):
        return jnp.stack([p[:CHUNK, 0][::-1], p[:CHUNK, 1]], axis=0).transpose(0, 2, 1, 3)
    qr, qi = by_position(pr), by_position(pi)

    lane_sn = np.arange(SSM_TILE)
    rep = jnp.asarray(lane_sn[None, :] // SSM_GROUP == np.arange(CHUNK)[:, None], dtype=BF16)
    til = jnp.asarray(lane_sn[None, :] % SSM_GROUP == np.arange(SSM_GROUP)[:, None], dtype=BF16)

    def by_output(p):
        return jnp.concatenate([p[1:, 0], p[1:, 1][::-1]], axis=-1).transpose(1, 0, 2)
    cat_c = lambda c: jnp.concatenate([c[0], c[1]], axis=-1)

    def stack_lanes(re, im):
        return jnp.concatenate([re[0], re[1], im[0], im[1]], axis=-1)
    q_all = stack_lanes(qr, qi)
    b_all = stack_lanes(bbr.transpose(0, 1, 3, 2), bbi.transpose(0, 1, 3, 2))
    c_neg = jnp.concatenate([cr, -ci], axis=-1)

    gb = BUILD_GROUPS
    narrow = pl.BlockSpec((gb, SSM_GROUP, 4 * P_A), lambda i: (i, 0, 0))
    expand = pl.BlockSpec((SSM_GROUP, SSM_TILE), lambda i: (0, 0))
    cspec = pl.BlockSpec((2, gb, SSM_GROUP, 2 * P_A), lambda i: (0, i, 0, 0))
    half = pl.BlockSpec((gb, SSM_GROUP, 2 * P_A), lambda i: (i, 0, 0))
    dspec = pl.BlockSpec((gb, SSM_GROUP, 1), lambda i: (i, 0, 0))
    ospec = pl.BlockSpec((gb, SSM_TILE, SSM_TILE), lambda i: (i, 0, 0))
    oshape = jax.ShapeDtypeStruct((G_A, SSM_TILE, SSM_TILE), BF16)
    wst_t, kt_t, wo_t = pl.pallas_call(
        _s5_build_kernel,
        grid=(G_A // gb,),
        in_specs=[narrow] * 2 + [cspec] + [half] * 4 + [dspec, expand, expand],
        out_specs=[ospec] * 3,
        out_shape=[oshape] * 3,
        compiler_params=_cparams(1),
        name="s5_build_operators",
    )(q_all, b_all, c_neg, cat_c(cr), cat_c(ci),
      by_output(pr), by_output(pi), d_skip.astype(F32).reshape(G_A, SSM_GROUP, 1), rep, til)

    a16r = jnp.concatenate([pr[CHUNK, 0], pr[CHUNK, 1]], axis=-1)[:, None, :]
    a16i = jnp.concatenate([pi[CHUNK, 0], pi[CHUNK, 1]], axis=-1)[:, None, :]
    return wst_t, kt_t, wo_t, a16r, a16i


def _s5_kernel(u_ref, wst_ref, kt_ref, wo_ref, a16r_ref, a16i_ref, h0r_ref, h0i_ref,
               pin_ref, pout_ref, y_ref, fr_ref, br_ref, fi_ref, bi_ref, ut, yt, s_sc, hf_sc, hb_sc,
               *, nb, nc, gpb, il):
    gb = pl.program_id(2)
    seq = nc * CHUNK
    r = nb * nc
    sw = 2 * P_A
    n_cb = u_ref.shape[1] // LANES
    pieces = [b * seq + q * PERM_TILE for b in range(nb) for q in range(seq // PERM_TILE)]

    @pl.when(gb == 0)
    def _():
        for s in range(CHUNK):
            cols = []
            for cb in range(n_cb):
                rows = jnp.concatenate([u_ref[o + s * CHUNK:o + (s + 1) * CHUNK, cb * LANES:(cb + 1) * LANES]
                                        for o in pieces], axis=0)
                cols.append(rows.astype(F32).T.astype(BF16))
            ut[:, s * r:(s + 1) * r] = _dot(jnp.concatenate(cols, axis=0), pin_ref[...]).astype(BF16)

    fwd = (lax.broadcasted_iota(jnp.int32, (1, sw), 1)) < P_A

    def groups(it, carry):
        js = [it * il + i for i in range(il)]
        grows = [pl.multiple_of((gb * gpb + j) * SSM_GROUP, SSM_GROUP) for j in js]
        ds = [jnp.concatenate([ut[pl.ds(g, SSM_GROUP), s * r:(s + 1) * r] for s in range(CHUNK)], axis=0)
              for g in grows]
        for i, j in enumerate(js):
            s_t = _dot(wst_ref[j], ds[i])
            s_sc[i, 0] = s_t[:sw].T
            s_sc[i, 1] = s_t[sw:].T
        a_r = [a16r_ref[j] for j in js]
        a_i = [a16i_ref[j] for j in js]
        re = [h0r_ref[j] for j in js]
        im = [h0i_ref[j] for j in js]
        for k in range(nc):
            rf = pl.ds(k * nb, nb)
            rb = pl.ds((nc - 1 - k) * nb, nb)
            for i in range(il):
                hf_sc[i, 0, rf, :] = re[i]
                hf_sc[i, 1, rf, :] = im[i]
                hb_sc[i, 0, rb, :] = re[i]
                hb_sc[i, 1, rb, :] = im[i]
                xr = jnp.where(fwd, s_sc[i, 0, rf, :], s_sc[i, 0, rb, :])
                xi = jnp.where(fwd, s_sc[i, 1, rf, :], s_sc[i, 1, rb, :])
                re[i], im[i] = a_r[i] * re[i] - a_i[i] * im[i] + xr, a_r[i] * im[i] + a_i[i] * re[i] + xi
        for i, j in enumerate(js):
            fr_ref[j] = re[i]
            br_ref[j] = pltpu.roll(re[i], P_A, 1)
            fi_ref[j] = im[i]
            bi_ref[j] = pltpu.roll(im[i], P_A, 1)
            hin_t = jnp.concatenate([jnp.where(fwd, hf_sc[i, 0], hb_sc[i, 0]).T,
                                     jnp.where(fwd, hf_sc[i, 1], hb_sc[i, 1]).T], axis=0)
            y_t = (_dot(kt_ref[j], ds[i]) + _dot(wo_ref[j], hin_t.astype(BF16))).astype(BF16)
            for t in range(CHUNK):
                yt[pl.ds(grows[i], SSM_GROUP), t * r:(t + 1) * r] = y_t[t * SSM_GROUP:(t + 1) * SSM_GROUP, :]
        return carry

    lax.fori_loop(0, gpb // il, groups, 0)

    @pl.when(gb == pl.num_programs(2) - 1)
    def _():
        for t in range(CHUNK):
            full = _dot(yt[:, t * r:(t + 1) * r], pout_ref[...])
            for cb in range(n_cb):
                rows = full[cb * LANES:(cb + 1) * LANES, :].T.astype(BF16)
                for idx, o in enumerate(pieces):
                    y_ref[o + t * CHUNK:o + (t + 1) * CHUNK, cb * LANES:(cb + 1) * LANES] = (
                        rows[idx * CHUNK:(idx + 1) * CHUNK, :])


def _s5_mix(u, ops, h0_re, h0_im, bsz, seq_len, nb, cw):
    wst_t, kt_t, wo_t, a16r, a16i = ops
    t, w = u.shape
    nc = seq_len // CHUNK
    r = nb * nc
    gpb = S5_GROUPS_PER_STEP
    n_row_tiles = bsz // nb
    n_col_tiles = w // cw
    gb_per_tile = cw // SSM_GROUP // gpb
    assert r % 128 == 0 and cw % (SSM_GROUP * gpb) == 0 and bsz % nb == 0

    def h0_layout(h0):
        return h0.astype(F32).transpose(2, 0, 1, 3).reshape(G_A, bsz, 2 * P_A)

    tile = pl.BlockSpec((nb * seq_len, cw), lambda i, c, g: (i, c))
    grp = lambda i, c, g: (c * gb_per_tile + g, 0, 0)
    wspec = pl.BlockSpec((gpb, SSM_TILE, SSM_TILE), grp)
    aspec = pl.BlockSpec((gpb, 1, 2 * P_A), grp)
    hspec = pl.BlockSpec((gpb, nb, 2 * P_A), lambda i, c, g: (c * gb_per_tile + g, i, 0))
    state = jax.ShapeDtypeStruct((G_A, bsz, 2 * P_A), F32)
    il = min(gpb, S5_SCAN_ROWS // r)
    scratch = [pltpu.VMEM((cw, CHUNK * r), BF16)] * 2 + [pltpu.VMEM((il, 2, r, 2 * P_A), F32)] * 3
    src = lax.broadcasted_iota(jnp.int32, (r, r), 0)
    dst = lax.broadcasted_iota(jnp.int32, (r, r), 1)
    pin = (dst == (src % nc) * nb + src // nc).astype(BF16)
    pspec = pl.BlockSpec((r, r), lambda i, c, g: (0, 0))

    y, fwd_re, bwd_re, fwd_im, bwd_im = pl.pallas_call(
        functools.partial(_s5_kernel, nb=nb, nc=nc, gpb=gpb, il=il),
        grid=(n_row_tiles, n_col_tiles, gb_per_tile),
        in_specs=[tile, wspec, wspec, wspec, aspec, aspec, hspec, hspec, pspec, pspec],
        out_specs=[tile] + [hspec] * 4,
        out_shape=[jax.ShapeDtypeStruct((t, w), BF16)] + [state] * 4,
        scratch_shapes=scratch,
        compiler_params=_cparams(3),
        name="s5_core",
    )(u, wst_t, kt_t, wo_t, a16r, a16i, h0_layout(h0_re), h0_layout(h0_im), pin, pin.T)

    def fin(f, b):
        return jnp.stack([f[:, :, :P_A], b[:, :, :P_A]], axis=0).transpose(2, 0, 1, 3)
    return y, fin(fwd_re, bwd_re), fin(fwd_im, bwd_im)


def _diff_lambda(lq1, lk1, lq2, lk2, lam_init):
    return (jnp.exp(jnp.sum(lq1[...] * lk1[...], axis=-1, keepdims=True))
            - jnp.exp(jnp.sum(lq2[...] * lk2[...], axis=-1, keepdims=True)) + lam_init)


def _attn_kernel(*refs, lam_init, cached, seq_len):
    if cached:
        q_ref, kc_ref, kn_ref, vc_ref, vn_ref, szb_ref, lq1, lk1, lq2, lk2, g_ref, o_ref = refs
    else:
        q_ref, kn_ref, vn_ref, szb_ref, lq1, lk1, lq2, lk2, g_ref, o_ref = refs
    lam = _diff_lambda(lq1, lk1, lq2, lk2, lam_init)
    lq = q_ref.shape[0] if cached else seq_len
    hw = 2 * DH
    low = lax.broadcasted_iota(jnp.int32, (1, hw), 1) < DH
    zero = jnp.zeros((), BF16)
    ones = jnp.ones((kn_ref.shape[0] if cached else seq_len, hw), BF16)
    for h in range(H_B * (q_ref.shape[0] // lq)):
        rows = slice((h // H_B) * lq, (h // H_B + 1) * lq)
        krows = slice(None) if cached else rows
        cols = slice((h % H_B) * hw, (h % H_B + 1) * hw)
        qh = q_ref[rows, cols]
        qs = jnp.concatenate([jnp.where(low, qh, zero), jnp.where(low, zero, qh)], axis=0)
        s = _dot_nt(qs, kn_ref[krows, cols])
        if cached:
            s = jnp.concatenate([_dot_nt(qs, kc_ref[:, cols]), s], axis=1)
        e = jnp.exp2(s - jnp.max(s, axis=-1, keepdims=True)).astype(BF16)
        vn = jnp.concatenate([vn_ref[krows, cols], ones], axis=1)
        if cached:
            lc = kc_ref.shape[0]
            vc = jnp.concatenate([vc_ref[:, cols], ones[:lc]], axis=1)
            oa = _dot(e[:, :lc], vc) + _dot(e[:, lc:], vn)
        else:
            oa = _dot(e, vn)
        on = oa[:, :hw] * (1.0 / oa[:, hw:])
        o = on[:lq] - lam * on[lq:]
        o = o * lax.rsqrt(jnp.mean(o * o, axis=-1, keepdims=True) + LN_EPS)
        o = o * g_ref[...] * (1.0 - lam_init)
        o_ref[rows, cols] = (o * szb_ref[rows, cols].astype(F32)).astype(BF16)


def _lam_specs(n_axes):
    zero = lambda *idx: (0, 0)
    return [pl.BlockSpec((1, DH), zero)] * 4 + [pl.BlockSpec((1, 2 * DH), zero)]


def _attention_prompt(q, k, v, szb, lam_vecs, subln, lam_init, bsz, seq_len):
    blk = pl.BlockSpec((ATTN_SEQS_PER_STEP * seq_len, D_MODEL), lambda b: (b, 0))
    return pl.pallas_call(
        functools.partial(_attn_kernel, lam_init=lam_init, cached=False, seq_len=seq_len),
        grid=(bsz // ATTN_SEQS_PER_STEP,),
        in_specs=[blk] * 4 + _lam_specs(1),
        out_specs=blk,
        out_shape=jax.ShapeDtypeStruct(q.shape, BF16),
        compiler_params=_cparams(1),
        name="diff_attention_prompt",
    )(q, k, v, szb, *lam_vecs, subln)


def _attention_sample(q, kc, kn, vc, vn, szb, lam_vecs, subln, lam_init, bsz, seq_len, tq):
    nq = seq_len // tq
    past = kc.shape[1]
    qblk = pl.BlockSpec((tq, D_MODEL), lambda b, i: (b * nq + i, 0))
    cblk = pl.BlockSpec((None, past, D_MODEL), lambda b, i: (b, 0, 0))
    nblk = pl.BlockSpec((seq_len, D_MODEL), lambda b, i: (b, 0))
    return pl.pallas_call(
        functools.partial(_attn_kernel, lam_init=lam_init, cached=True, seq_len=seq_len),
        grid=(bsz, nq),
        in_specs=[qblk, cblk, nblk, cblk, nblk, qblk] + _lam_specs(2),
        out_specs=qblk,
        out_shape=jax.ShapeDtypeStruct(q.shape, BF16),
        compiler_params=_cparams(2),
        name="diff_attention_sample",
    )(q, kc, kn, vc, vn, szb, *lam_vecs, subln)


def _even_out_kernel(ys_ref, sza_ref, yb_ref, x_ref, gate_ref, wglu_ref, bglu_ref, wout_ref, g_ref, b_ref,
                     perm_ref, *rest, alpha):
    o_ref = rest[0] if len(rest) == 1 else rest[5]
    wa = ys_ref.shape[1]
    for r in range(x_ref.shape[0] // PERM_TILE):
        rows = slice(r * PERM_TILE, (r + 1) * PERM_TILE)
        ga = jax.nn.gelu(_dot(perm_ref[...], ys_ref[rows, :]))
        glu = jax.nn.sigmoid(_dot(ga.astype(BF16), wglu_ref[...]) + bglu_ref[...])
        ya = (ga * glu * sza_ref[rows, :].astype(F32)).astype(BF16)
        out = _dot(ya, wout_ref[:wa, :]) + _dot(yb_ref[rows, :], wout_ref[wa:, :])
        x_new = _post_norm(x_ref[rows, :], gate_ref[...], out, g_ref[...], b_ref[...], alpha)
        o_ref[rows, :] = x_new
        if len(rest) > 1:
            _odd_in_rows(x_new, rows, *rest[:5], *rest[6:])


def _even_out_proj(ys, sza, yb, x2, mod3, w_glu_bf, b_glu, w_out_bf, ln_g, ln_b, layer, cond_of, tm, alpha,
                   next_odd=None):
    t, d = x2.shape
    assert tm % PERM_TILE == 0 and PERM_TILE == SUB_ROWS
    row = pl.BlockSpec((tm, d), lambda i: (i, 0))
    full = _resident
    in_specs = [row, row, row, row, _mod_spec(layer, 2, cond_of), full(w_glu_bf.shape), full((1, d)),
                full(w_out_bf.shape), full((1, d)), full((1, d)), full((PERM_TILE, PERM_TILE))]
    args = [ys, sza, yb, x2, mod3, w_glu_bf, b_glu.reshape(1, d), w_out_bf, ln_g.reshape(1, d), ln_b.reshape(1, d),
            _chunk_transpose_matrix()]
    out_specs, out_shape = [row], [jax.ShapeDtypeStruct((t, d), F32)]
    if next_odd is not None:
        w_in_bf, cc, sc = next_odd
        wide = pl.BlockSpec((tm, W_C), lambda i: (i, 0))
        in_specs += [_mod_spec(layer + 1, 0, cond_of), _mod_spec(layer + 1, 1, cond_of), full(w_in_bf.shape),
                     full(cc.shape), full(sc.shape)]
        args += [mod3, mod3, w_in_bf, cc, sc]
        out_specs += [wide] * 3
        out_shape += [jax.ShapeDtypeStruct((t, W_C), BF16)] * 3
    res = pl.pallas_call(
        functools.partial(_even_out_kernel, alpha=alpha),
        grid=(t // tm,),
        in_specs=in_specs,
        out_specs=out_specs,
        out_shape=out_shape,
        compiler_params=_cparams(1),
        name="even_out_proj",
    )(*args)
    return res[0] if next_odd is None else tuple(res)


def _dft_mats(n, scale):
    k = np.arange(n, dtype=np.int64)
    ang = ((k[:, None] * k[None, :]) % n).astype(np.float64) * (2.0 * math.pi / n)
    return (jnp.asarray((np.cos(ang) * scale).astype(np.float32)).astype(BF16),
            jnp.asarray((np.sin(ang) * scale).astype(np.float32)).astype(BF16))


def _odd_in_rows(x, rows, shift_ref, scale_ref, w_ref, cc_ref, sc_ref, uc_ref, us_ref, sz_ref):
    h = (_ln_rows(x) * (1.0 + scale_ref[...]) + shift_ref[...]).astype(BF16)
    u = _dot(h, w_ref[:, :W_C]).astype(BF16)
    sz_ref[rows, :] = _silu(_dot(h, w_ref[:, W_C:])).astype(BF16)
    for g in range(NG_C):
        cols = slice(g * GC_C, (g + 1) * GC_C)
        uc_ref[rows, cols] = _dot(u[:, cols], cc_ref[...]).astype(BF16)
        us_ref[rows, cols] = _dot(u[:, cols], sc_ref[...]).astype(BF16)


def _odd_out_kernel(cl_ref, sl_ref, uc_ref, us_ref, sz_ref, x_ref, gate_ref, wf_ref, bf_ref, wo_ref, g_ref, b_ref,
                    o_ref, *, alpha, whole_seqs):
    seq_len = cl_ref.shape[1]
    sub = seq_len if whole_seqs else SUB_ROWS
    for r in range(x_ref.shape[0] // sub):
        rows = slice(r * sub, (r + 1) * sub)
        if whole_seqs:
            mixed = _dot(cl_ref[...], uc_ref[rows, :]) - _dot(sl_ref[...], us_ref[rows, :])
        else:
            mixed = _dot(cl_ref[rows, :], uc_ref[...]) - _dot(sl_ref[rows, :], us_ref[...])
        y = ((_dot(mixed.astype(BF16), wf_ref[...]) + bf_ref[...]) * sz_ref[rows, :].astype(F32)).astype(BF16)
        o_ref[rows, :] = _post_norm(x_ref[rows, :], gate_ref[...], _dot(y, wo_ref[...]), g_ref[...], b_ref[...],
                                    alpha)


def _odd_out_proj(uc, us, sz, x2, mod3, cl, sl, w_fno_bf, b_fno, w_out_bf, ln_g, ln_b, layer, cond_of,
                  bsz, seq_len, tl, alpha):
    t, d = x2.shape
    whole_seqs = tl >= seq_len
    full = _resident
    if whole_seqs:
        assert tl % seq_len == 0
        grid = (t // tl, 1)
        dft = _resident((seq_len, seq_len))
        seq = pl.BlockSpec((tl, W_C), lambda b, i: (b, 0))
        wide = seq
        row = pl.BlockSpec((tl, d), lambda b, i: (b, 0))
    else:
        assert seq_len % tl == 0 and tl % SUB_ROWS == 0
        nl = seq_len // tl
        grid = (bsz, nl)
        dft = pl.BlockSpec((tl, seq_len), lambda b, i: (i, 0))
        seq = pl.BlockSpec((seq_len, W_C), lambda b, i: (b, 0))
        wide = pl.BlockSpec((tl, W_C), lambda b, i: (b * nl + i, 0))
        row = pl.BlockSpec((tl, d), lambda b, i: (b * nl + i, 0))
    return pl.pallas_call(
        functools.partial(_odd_out_kernel, alpha=alpha, whole_seqs=whole_seqs),
        grid=grid,
        in_specs=[dft, dft, seq, seq, wide, row, _mod_spec(layer, 2, cond_of), full(w_fno_bf.shape),
                  full((1, W_C)), full(w_out_bf.shape), full((1, d)), full((1, d))],
        out_specs=row,
        out_shape=jax.ShapeDtypeStruct((t, d), F32),
        compiler_params=_cparams(2),
        name="odd_out_proj",
    )(cl, sl, uc, us, sz, x2, mod3, w_fno_bf, b_fno.reshape(1, W_C), w_out_bf, ln_g.reshape(1, d),
      ln_b.reshape(1, d))


def _rope_tables(seq_len):
    rows = seq_len // GRID_W
    row = jnp.repeat(jnp.arange(rows), GRID_W).astype(F32)
    col = jnp.tile(jnp.arange(GRID_W), rows).astype(F32)
    freqs = ROPE_BASE ** (-jnp.arange(ROT_FREQS, dtype=F32) / ROT_FREQS)
    ang = jnp.concatenate([row[:, None] * freqs, col[:, None] * freqs], axis=-1)
    cos, sin = jnp.cos(ang), jnp.sin(ang)
    cos128 = jnp.tile(cos, (1, 128 // ROT_HALF))
    sin128 = jnp.tile(jnp.concatenate([-sin, sin], axis=-1), (1, 128 // DH))
    return cos128, sin128


def kernel(x_prompt, x_sample, cache_k, cache_v, state_ssm_re, state_ssm_im, c, c_ctx, w_mod, b_mod, ln_g, ln_b, w_in_e, ssm_lam_re, ssm_lam_im, ssm_log_dt, ssm_b_re, ssm_b_im, ssm_c_re, ssm_c_im, ssm_d, w_glu, b_glu, lam_q1, lam_k1, lam_q2, lam_k2, subln_g, w_out_e, w_in_o, w_fno, b_fno, w_out_o):
    depth = w_mod.shape[0]
    bp_, lp, d = x_prompt.shape
    bs_, ls, _ = x_sample.shape
    past = cache_k.shape[2]
    alpha = (2 * depth) ** 0.25
    assert bs_ + 1 <= MOD_ROWS and d == D_MODEL

    cond8 = jnp.concatenate([c_ctx[None, :], c, jnp.zeros((MOD_ROWS - 1 - bs_, d), F32)], axis=0).astype(F32)
    mod3 = _modulation(cond8, w_mod, b_mod).reshape(depth * MOD_ROWS * 3, 1, d)

    tm = ROW_TILE
    cond_p = lambda *idx: 0
    cond_s_row = lambda i: 1 + i // (ls // tm)
    cond_s_grid = lambda b, i: 1 + b
    rope_tabs = _rope_tables(ls)
    xp = x_prompt.reshape(bp_ * lp, d)
    xs = x_sample.reshape(bs_ * ls, d)
    new_k, new_v, new_sr, new_si = [], [], [], []
    zeros_h0 = jnp.zeros((bp_, 2, G_A, P_A), F32)

    for layer in range(depth):
        if layer % 2 == 0:
            e = layer // 2
            lam_init = 0.8 - 0.6 * math.exp(-0.3 * layer)
            w_in_bf = w_in_e[e].astype(BF16)
            w_glu_bf = w_glu[e].astype(BF16)
            w_out_bf = w_out_e[e].astype(BF16)
            ops = _s5_operators(ssm_lam_re[e], ssm_lam_im[e], ssm_log_dt[e], ssm_b_re[e], ssm_b_im[e],
                                ssm_c_re[e], ssm_c_im[e], ssm_d[e])
            lam_vecs = [v[e].reshape(1, DH).astype(F32) for v in (lam_q1, lam_k1, lam_q2, lam_k2)]
            subln = subln_g[e].reshape(1, 2 * DH).astype(F32)

            u, sza, q, kb, vb, szb, kf, vf = _even_in_proj(xp, mod3, w_in_bf, layer, cond_p, tm, None, lp)
            new_k.append(kf.reshape(bp_, H_B, 2, DH, lp).transpose(0, 4, 1, 2, 3))
            new_v.append(vf.reshape(bp_, lp, H_B, 2 * DH))
            ys, s_re, s_im = _s5_mix(u, ops, zeros_h0, zeros_h0, bp_, lp, 2 * S5_TILE_ELEMS // (lp * d), d // 2)
            new_sr.append(s_re)
            new_si.append(s_im)
            yb = _attention_prompt(q, kb, vb, szb, lam_vecs, subln, lam_init, bp_, lp)
            next_odd = None
            if layer + 1 < depth:
                next_odd = (w_in_o[(layer + 1) // 2].astype(BF16),) + _dft_mats(GC_C, GC_C ** -0.5)
            res = _even_out_proj(ys, sza, yb, xp, mod3, w_glu_bf, b_glu[e], w_out_bf, ln_g[layer], ln_b[layer],
                                 layer, cond_p, tm, alpha, next_odd)
            xp, fused_p = (res, None) if next_odd is None else (res[0], res[1:])

            u, sza, q, kb, vb, szb = _even_in_proj(xs, mod3, w_in_bf, layer, cond_s_row, tm, rope_tabs, ls)
            ys, _, _ = _s5_mix(u, ops, state_ssm_re[:, e], state_ssm_im[:, e], bs_, ls, bs_,
                               S5_TILE_ELEMS // (bs_ * ls))
            kc = cache_k[:, e].reshape(bs_, past, d).astype(BF16)
            vc = cache_v[:, e].reshape(bs_, past, d).astype(BF16)
            yb = _attention_sample(q, kc, kb, vc, vb, szb, lam_vecs, subln, lam_init, bs_, ls, ATTN_Q_TILE)
            res = _even_out_proj(ys, sza, yb, xs, mod3, w_glu_bf, b_glu[e], w_out_bf, ln_g[layer], ln_b[layer],
                                 layer, cond_s_row, tm, alpha, next_odd)
            xs, fused_s = (res, None) if next_odd is None else (res[0], res[1:])
        else:
            o = layer // 2
            w_fno_bf = w_fno[o].astype(BF16)
            w_out_bf = w_out_o[o].astype(BF16)
            for which in ("prompt", "sample"):
                if which == "prompt":
                    x2, cond_row, cond_grid, bsz, seq, fused = xp, cond_p, cond_p, bp_, lp, fused_p
                else:
                    x2, cond_row, cond_grid, bsz, seq, fused = xs, cond_s_row, cond_s_grid, bs_, ls, fused_s
                cl, sl = _dft_mats(seq, seq ** -0.5)
                uc, us, sz = fused
                x2 = _odd_out_proj(uc, us, sz, x2, mod3, cl, sl, w_fno_bf, b_fno[o], w_out_bf, ln_g[layer],
                                   ln_b[layer], layer, cond_grid, bsz, seq, tm, alpha)
                if which == "prompt":
                    xp = x2
                else:
                    xs = x2

    return (xp.reshape(bp_, lp, d), xs.reshape(bs_, ls, d), jnp.stack(new_k, axis=1), jnp.stack(new_v, axis=1),
            jnp.stack(new_sr, axis=1), jnp.stack(new_si, axis=1))
```

```python
import functools
import math

import jax
import jax.numpy as jnp
import numpy as np
from jax import lax
from jax.experimental import pallas as pl
from jax.experimental.pallas import tpu as pltpu

F32 = jnp.float32
BF16 = jnp.bfloat16

D_MODEL = 1024
GRID_W = 64
SSM_GROUP = 16
G_A = D_MODEL // SSM_GROUP
P_A = 64
DH = 64
H_B = D_MODEL // (2 * DH)
ROPE_BASE = 10000.0
ROT_HALF = DH // 2
ROT_FREQS = DH // 4
NG_C = 8
GC_C = 2 * D_MODEL // NG_C
W_C = 2 * D_MODEL
LN_EPS = 1e-5
LOG2_E = 1.4426950408889634
CHUNK = 16
SSM_TILE = CHUNK * SSM_GROUP
LANES = 128
PERM_TILE = CHUNK * CHUNK
SUB_ROWS = PERM_TILE
ROW_TILE = 2 * SUB_ROWS
ATTN_Q_TILE = 512
ATTN_SEQS_PER_STEP = 4
S5_GROUPS_PER_STEP = 16
S5_SCAN_ROWS = 4096
S5_TILE_ELEMS = 2048 * 1024
MOD_ROWS = 8
VMEM_LIMIT = 56 * 1024 * 1024


def _cparams(n_axes):
    return pltpu.CompilerParams(dimension_semantics=("arbitrary",) * n_axes, vmem_limit_bytes=VMEM_LIMIT)


def _ln_rows(x):
    mu = jnp.mean(x, axis=-1, keepdims=True)
    xc = x - mu
    var = jnp.mean(xc * xc, axis=-1, keepdims=True)
    return xc * lax.rsqrt(var + LN_EPS)


def _silu(z):
    return z * jax.nn.sigmoid(z)


def _dot(a, b):
    return jnp.dot(a, b, preferred_element_type=F32)


def _dot_nt(a, b):
    return lax.dot_general(a, b, (((1,), (1,)), ((), ())), preferred_element_type=F32)


def _post_norm(x, gate, out, g, b, alpha):
    return _ln_rows(alpha * x + gate * out) * g + b


def _mod_kernel(c_ref, w_ref, b_ref, o_ref):
    c = _silu(c_ref[...]).astype(BF16)
    o_ref[...] = _dot(c, w_ref[...].astype(BF16)) + b_ref[...]


def _modulation(cond8, w_mod, b_mod):
    depth, d, n3 = w_mod.shape
    tn = 1024
    return pl.pallas_call(
        _mod_kernel,
        grid=(depth, n3 // tn),
        in_specs=[
            pl.BlockSpec((MOD_ROWS, d), lambda l, j: (0, 0)),
            pl.BlockSpec((None, d, tn), lambda l, j: (l, 0, j)),
            pl.BlockSpec((None, 1, tn), lambda l, j: (l, 0, j)),
        ],
        out_specs=pl.BlockSpec((None, MOD_ROWS, tn), lambda l, j: (l, 0, j)),
        out_shape=jax.ShapeDtypeStruct((depth, MOD_ROWS, n3), F32),
        compiler_params=_cparams(2),
        name="modulation",
    )(cond8, w_mod, b_mod.reshape(depth, 1, n3))


def _resident(shape):
    return pl.BlockSpec(shape, lambda *idx: (0,) * len(shape), pipeline_mode=pl.Buffered(1))


def _mod_spec(layer, part, cond_of):
    return pl.BlockSpec((None, 1, D_MODEL), lambda *idx: ((layer * MOD_ROWS + cond_of(*idx)) * 3 + part, 0, 0))


def _rope(x, cos, sin_signed, first_half):
    blocks = []
    for hh in range(x.shape[1] // 128):
        b = x[:, hh * 128:(hh + 1) * 128]
        partner = jnp.where(first_half, pltpu.roll(b, 128 - ROT_HALF, 1), pltpu.roll(b, ROT_HALF, 1))
        blocks.append(b * cos + partner * sin_signed)
    return jnp.concatenate(blocks, axis=1)


def _chunk_transpose_matrix():
    i = lax.broadcasted_iota(jnp.int32, (PERM_TILE, PERM_TILE), 0)
    j = lax.broadcasted_iota(jnp.int32, (PERM_TILE, PERM_TILE), 1)
    return (j == (i % CHUNK) * CHUNK + i // CHUNK).astype(BF16)


def _even_in_kernel(*refs, rope):
    if rope:
        (x_ref, shift_ref, scale_ref, w_ref, perm_ref, cos_ref, sin_ref,
         u_ref, sza_ref, q_ref, kb_ref, vb_ref, szb_ref) = refs
    else:
        (x_ref, shift_ref, scale_ref, w_ref, perm_ref, u_ref, sza_ref, q_ref, kb_ref, vb_ref, szb_ref,
         kf_ref, vf_ref) = refs
    w = D_MODEL
    for r in range(x_ref.shape[0] // PERM_TILE):
        rows = slice(r * PERM_TILE, (r + 1) * PERM_TILE)
        h = (_ln_rows(x_ref[rows, :]) * (1.0 + scale_ref[...]) + shift_ref[...]).astype(BF16)

        def proj(j):
            return _dot(h, w_ref[:, j * w:(j + 1) * w])

        u_ref[rows, :] = _dot(perm_ref[...], proj(0).astype(BF16)).astype(BF16)
        sza_ref[rows, :] = _silu(proj(1)).astype(BF16)
        q = proj(2) * (DH ** -0.5 * LOG2_E)
        k = proj(3)
        v = proj(4)
        if rope:
            lane = lax.broadcasted_iota(jnp.int32, (1, 128), 1)
            first_half = (lane % DH) < ROT_HALF
            q = _rope(q, cos_ref[rows, :], sin_ref[rows, :], first_half)
            kb_ref[rows, :] = _rope(k, cos_ref[rows, :], sin_ref[rows, :], first_half).astype(BF16)
        else:
            kf_ref[r] = k.T
            vf_ref[rows, :] = v
            kb_ref[rows, :] = k.astype(BF16)
        q_ref[rows, :] = q.astype(BF16)
        vb_ref[rows, :] = v.astype(BF16)
        szb_ref[rows, :] = _silu(proj(5)).astype(BF16)


def _even_in_proj(x2, mod3, w_in_bf, layer, cond_of, tm, rope_tabs, seq_len):
    t, d = x2.shape
    n = w_in_bf.shape[1]
    assert tm % PERM_TILE == 0
    row = pl.BlockSpec((tm, d), lambda i: (i, 0))
    in_specs = [row, _mod_spec(layer, 0, cond_of), _mod_spec(layer, 1, cond_of),
                _resident((d, n)), _resident((PERM_TILE, PERM_TILE))]
    args = [x2, mod3, mod3, w_in_bf, _chunk_transpose_matrix()]
    bf = jax.ShapeDtypeStruct((t, d), BF16)
    out_shape = [bf] * 6
    out_specs = [row] * 6
    if rope_tabs is not None:
        tiles_per_seq = seq_len // tm
        tab = pl.BlockSpec((tm, 128), lambda i: (i % tiles_per_seq, 0))
        in_specs += [tab, tab]
        args += list(rope_tabs)
    else:
        assert seq_len == PERM_TILE
        out_shape += [jax.ShapeDtypeStruct((t // seq_len, d, seq_len), F32), jax.ShapeDtypeStruct((t, d), F32)]
        out_specs += [pl.BlockSpec((tm // seq_len, d, seq_len), lambda i: (i, 0, 0)), row]
    return pl.pallas_call(
        functools.partial(_even_in_kernel, rope=rope_tabs is not None),
        grid=(t // tm,),
        in_specs=in_specs,
        out_specs=out_specs,
        out_shape=out_shape,
        compiler_params=_cparams(1),
        name="even_in_proj",
    )(*args)


BUILD_GROUPS = 8


def _s5_build_kernel(q_ref, b_ref, cn_ref, ccr_ref, cci_ref, ecr_ref, eci_ref, d_ref,
                     rep_ref, til_ref, wst_ref, kt_ref, wo_ref):
    def split(x):
        hi = x.astype(BF16)
        return hi, (x - hi.astype(F32)).astype(BF16)

    def hdot(a, b):
        (ah, al), (bh, bl) = split(a), split(b)
        return _dot(ah, bh) + _dot(ah, bl) + _dot(al, bh)

    lane = lax.broadcasted_iota(jnp.int32, (SSM_GROUP, SSM_TILE), 1)
    sub = lax.broadcasted_iota(jnp.int32, (SSM_GROUP, SSM_TILE), 0)
    zeros = jnp.zeros((SSM_GROUP, SSM_TILE), F32)
    rows4 = 4 * P_A
    def expand(x_t, sel):
        tn = lambda a: lax.dot_general(a, sel, (((0,), (0,)), ((), ())), preferred_element_type=F32)
        hi, lo = split(x_t)
        return tn(hi) + tn(lo)

    for g in range(BUILD_GROUPS):
        qe = expand(q_ref[g], rep_ref[...])
        be = expand(b_ref[g], til_ref[...])
        part = lambda x, k: x[k * P_A:(k + 1) * P_A]
        st_re = [part(qe, d) * part(be, d) - part(qe, 2 + d) * part(be, 2 + d) for d in range(2)]
        st_im = [part(qe, d) * part(be, 2 + d) + part(qe, 2 + d) * part(be, d) for d in range(2)]
        wst_ref[g] = jnp.concatenate([st_re[0], st_re[1], st_im[0], st_im[1]], axis=0).astype(BF16)
        klag = [hdot(cn_ref[d, g], jnp.concatenate([st_re[d], st_im[d]], axis=0)) for d in range(2)]
        fwd_p = jnp.concatenate([klag[0], zeros], axis=1)
        bwd_p = jnp.concatenate([zeros, klag[1]], axis=1)
        ccr, cci = ccr_ref[g], cci_ref[g]
        d_lanes = jnp.broadcast_to(d_ref[g], (SSM_GROUP, SSM_TILE))
        for t in range(CHUNK):
            lo_f = SSM_GROUP * (CHUNK - 1 - t)
            lo_b = SSM_TILE - SSM_GROUP * t
            skip = jnp.where(lane == t * SSM_GROUP + sub, d_lanes, 0.0)
            rows = slice(t * SSM_GROUP, (t + 1) * SSM_GROUP)
            kt_ref[g, rows, :] = (fwd_p[:, lo_f:lo_f + SSM_TILE] + bwd_p[:, lo_b:lo_b + SSM_TILE] + skip).astype(BF16)
            er, ei = ecr_ref[g, t:t + 1, :], eci_ref[g, t:t + 1, :]
            wo_ref[g, rows, :] = jnp.concatenate([ccr * er - cci * ei, -(ccr * ei + cci * er)], axis=1).astype(BF16)


def _s5_operators(lam_re, lam_im, log_dt, b_re, b_im, c_re, c_im, d_skip):
    lr, li = lam_re.astype(F32), lam_im.astype(F32)
    dt = jnp.exp(log_dt.astype(F32))[..., None]
    mag = jnp.exp(lr * dt)
    ar = mag * jnp.cos(li * dt)
    ai = mag * jnp.sin(li * dt)
    den = lr * lr + li * li
    fr = ((ar - 1.0) * lr + ai * li) / den
    fi = (ai * lr - (ar - 1.0) * li) / den
    br, bi = b_re.astype(F32), b_im.astype(F32)
    bbr = fr[..., None] * br - fi[..., None] * bi
    bbi = fr[..., None] * bi + fi[..., None] * br
    cr, ci = c_re.astype(F32), c_im.astype(F32)
    e_pow = jnp.arange(CHUNK + 1, dtype=F32)[:, None, None, None]
    pmag = jnp.exp(e_pow * (lr * dt))
    pr = pmag * jnp.cos(e_pow * (li * dt))
    pi = pmag * jnp.sin(e_pow * (li * dt))

    def by_position(p):
        return jnp.stack([p[:CHUNK, 0][::-1], p[:CHUNK, 1]], axis=0).transpose(0, 2, 1, 3)
    qr, qi = by_position(pr), by_position(pi)

    lane_sn = np.arange(SSM_TILE)
    rep = jnp.asarray(lane_sn[None, :] // SSM_GROUP == np.arange(CHUNK)[:, None], dtype=BF16)
    til = jnp.asarray(lane_sn[None, :] % SSM_GROUP == np.arange(SSM_GROUP)[:, None], dtype=BF16)

    def by_output(p):
        return jnp.concatenate([p[1:, 0], p[1:, 1][::-1]], axis=-1).transpose(1, 0, 2)
    cat_c = lambda c: jnp.concatenate([c[0], c[1]], axis=-1)

    def stack_lanes(re, im):
        return jnp.concatenate([re[0], re[1], im[0], im[1]], axis=-1)
    q_all = stack_lanes(qr, qi)
    b_all = stack_lanes(bbr.transpose(0, 1, 3, 2), bbi.transpose(0, 1, 3, 2))
    c_neg = jnp.concatenate([cr, -ci], axis=-1)

    gb = BUILD_GROUPS
    narrow = pl.BlockSpec((gb, SSM_GROUP, 4 * P_A), lambda i: (i, 0, 0))
    expand = pl.BlockSpec((SSM_GROUP, SSM_TILE), lambda i: (0, 0))
    cspec = pl.BlockSpec((2, gb, SSM_GROUP, 2 * P_A), lambda i: (0, i, 0, 0))
    half = pl.BlockSpec((gb, SSM_GROUP, 2 * P_A), lambda i: (i, 0, 0))
    dspec = pl.BlockSpec((gb, SSM_GROUP, 1), lambda i: (i, 0, 0))
    ospec = pl.BlockSpec((gb, SSM_TILE, SSM_TILE), lambda i: (i, 0, 0))
    oshape = jax.ShapeDtypeStruct((G_A, SSM_TILE, SSM_TILE), BF16)
    wst_t, kt_t, wo_t = pl.pallas_call(
        _s5_build_kernel,
        grid=(G_A // gb,),
        in_specs=[narrow] * 2 + [cspec] + [half] * 4 + [dspec, expand, expand],
        out_specs=[ospec] * 3,
        out_shape=[oshape] * 3,
        compiler_params=_cparams(1),
        name="s5_build_operators",
    )(q_all, b_all, c_neg, cat_c(cr), cat_c(ci),
      by_output(pr), by_output(pi), d_skip.astype(F32).reshape(G_A, SSM_GROUP, 1), rep, til)

    a16r = jnp.concatenate([pr[CHUNK, 0], pr[CHUNK, 1]], axis=-1)[:, None, :]
    a16i = jnp.concatenate([pi[CHUNK, 0], pi[CHUNK, 1]], axis=-1)[:, None, :]
    return wst_t, kt_t, wo_t, a16r, a16i


def _s5_kernel(u_ref, wst_ref, kt_ref, wo_ref, a16r_ref, a16i_ref, h0r_ref, h0i_ref,
               pin_ref, pout_ref, y_ref, fr_ref, br_ref, fi_ref, bi_ref, ut, yt, s_sc, hf_sc, hb_sc,
               *, nb, nc, gpb, il):
    gb = pl.program_id(2)
    seq = nc * CHUNK
    r = nb * nc
    sw = 2 * P_A
    n_cb = u_ref.shape[1] // LANES
    pieces = [b * seq + q * PERM_TILE for b in range(nb) for q in range(seq // PERM_TILE)]

    @pl.when(gb == 0)
    def _():
        for s in range(CHUNK):
            cols = []
            for cb in range(n_cb):
                rows = jnp.concatenate([u_ref[o + s * CHUNK:o + (s + 1) * CHUNK, cb * LANES:(cb + 1) * LANES]
                                        for o in pieces], axis=0)
                cols.append(rows.astype(F32).T.astype(BF16))
            ut[:, s * r:(s + 1) * r] = _dot(jnp.concatenate(cols, axis=0), pin_ref[...]).astype(BF16)

    fwd = (lax.broadcasted_iota(jnp.int32, (1, sw), 1)) < P_A

    def groups(it, carry):
        js = [it * il + i for i in range(il)]
        grows = [pl.multiple_of((gb * gpb + j) * SSM_GROUP, SSM_GROUP) for j in js]
        ds = [jnp.concatenate([ut[pl.ds(g, SSM_GROUP), s * r:(s + 1) * r] for s in range(CHUNK)], axis=0)
              for g in grows]
        for i, j in enumerate(js):
            s_t = _dot(wst_ref[j], ds[i])
            s_sc[i, 0] = s_t[:sw].T
            s_sc[i, 1] = s_t[sw:].T
        a_r = [a16r_ref[j] for j in js]
        a_i = [a16i_ref[j] for j in js]
        re = [h0r_ref[j] for j in js]
        im = [h0i_ref[j] for j in js]
        for k in range(nc):
            rf = pl.ds(k * nb, nb)
            rb = pl.ds((nc - 1 - k) * nb, nb)
            for i in range(il):
                hf_sc[i, 0, rf, :] = re[i]
                hf_sc[i, 1, rf, :] = im[i]
                hb_sc[i, 0, rb, :] = re[i]
                hb_sc[i, 1, rb, :] = im[i]
                xr = jnp.where(fwd, s_sc[i, 0, rf, :], s_sc[i, 0, rb, :])
                xi = jnp.where(fwd, s_sc[i, 1, rf, :], s_sc[i, 1, rb, :])
                re[i], im[i] = a_r[i] * re[i] - a_i[i] * im[i] + xr, a_r[i] * im[i] + a_i[i] * re[i] + xi
        for i, j in enumerate(js):
            fr_ref[j] = re[i]
            br_ref[j] = pltpu.roll(re[i], P_A, 1)
            fi_ref[j] = im[i]
            bi_ref[j] = pltpu.roll(im[i], P_A, 1)
            hin_t = jnp.concatenate([jnp.where(fwd, hf_sc[i, 0], hb_sc[i, 0]).T,
                                     jnp.where(fwd, hf_sc[i, 1], hb_sc[i, 1]).T], axis=0)
            y_t = (_dot(kt_ref[j], ds[i]) + _dot(wo_ref[j], hin_t.astype(BF16))).astype(BF16)
            for t in range(CHUNK):
                yt[pl.ds(grows[i], SSM_GROUP), t * r:(t + 1) * r] = y_t[t * SSM_GROUP:(t + 1) * SSM_GROUP, :]
        return carry

    lax.fori_loop(0, gpb // il, groups, 0)

    @pl.when(gb == pl.num_programs(2) - 1)
    def _():
        for t in range(CHUNK):
            full = _dot(yt[:, t * r:(t + 1) * r], pout_ref[...])
            for cb in range(n_cb):
                rows = full[cb * LANES:(cb + 1) * LANES, :].T.astype(BF16)
                for idx, o in enumerate(pieces):
                    y_ref[o + t * CHUNK:o + (t + 1) * CHUNK, cb * LANES:(cb + 1) * LANES] = (
                        rows[idx * CHUNK:(idx + 1) * CHUNK, :])


def _s5_mix(u, ops, h0_re, h0_im, bsz, seq_len, nb, cw):
    wst_t, kt_t, wo_t, a16r, a16i = ops
    t, w = u.shape
    nc = seq_len // CHUNK
    r = nb * nc
    gpb = S5_GROUPS_PER_STEP
    n_row_tiles = bsz // nb
    n_col_tiles = w // cw
    gb_per_tile = cw // SSM_GROUP // gpb
    assert r % 128 == 0 and cw % (SSM_GROUP * gpb) == 0 and bsz % nb == 0

    def h0_layout(h0):
        return h0.astype(F32).transpose(2, 0, 1, 3).reshape(G_A, bsz, 2 * P_A)

    tile = pl.BlockSpec((nb * seq_len, cw), lambda i, c, g: (i, c))
    grp = lambda i, c, g: (c * gb_per_tile + g, 0, 0)
    wspec = pl.BlockSpec((gpb, SSM_TILE, SSM_TILE), grp)
    aspec = pl.BlockSpec((gpb, 1, 2 * P_A), grp)
    hspec = pl.BlockSpec((gpb, nb, 2 * P_A), lambda i, c, g: (c * gb_per_tile + g, i, 0))
    state = jax.ShapeDtypeStruct((G_A, bsz, 2 * P_A), F32)
    il = min(gpb, S5_SCAN_ROWS // r)
    scratch = [pltpu.VMEM((cw, CHUNK * r), BF16)] * 2 + [pltpu.VMEM((il, 2, r, 2 * P_A), F32)] * 3
    src = lax.broadcasted_iota(jnp.int32, (r, r), 0)
    dst = lax.broadcasted_iota(jnp.int32, (r, r), 1)
    pin = (dst == (src % nc) * nb + src // nc).astype(BF16)
    pspec = pl.BlockSpec((r, r), lambda i, c, g: (0, 0))

    y, fwd_re, bwd_re, fwd_im, bwd_im = pl.pallas_call(
        functools.partial(_s5_kernel, nb=nb, nc=nc, gpb=gpb, il=il),
        grid=(n_row_tiles, n_col_tiles, gb_per_tile),
        in_specs=[tile, wspec, wspec, wspec, aspec, aspec, hspec, hspec, pspec, pspec],
        out_specs=[tile] + [hspec] * 4,
        out_shape=[jax.ShapeDtypeStruct((t, w), BF16)] + [state] * 4,
        scratch_shapes=scratch,
        compiler_params=_cparams(3),
        name="s5_core",
    )(u, wst_t, kt_t, wo_t, a16r, a16i, h0_layout(h0_re), h0_layout(h0_im), pin, pin.T)

    def fin(f, b):
        return jnp.stack([f[:, :, :P_A], b[:, :, :P_A]], axis=0).transpose(2, 0, 1, 3)
    return y, fin(fwd_re, bwd_re), fin(fwd_im, bwd_im)


def _diff_lambda(lq1, lk1, lq2, lk2, lam_init):
    return (jnp.exp(jnp.sum(lq1[...] * lk1[...], axis=-1, keepdims=True))
            - jnp.exp(jnp.sum(lq2[...] * lk2[...], axis=-1, keepdims=True)) + lam_init)


def _attn_kernel(*refs, lam_init, cached, seq_len):
    if cached:
        q_ref, kc_ref, kn_ref, vc_ref, vn_ref, szb_ref, lq1, lk1, lq2, lk2, g_ref, o_ref = refs
    else:
        q_ref, kn_ref, vn_ref, szb_ref, lq1, lk1, lq2, lk2, g_ref, o_ref = refs
    lam = _diff_lambda(lq1, lk1, lq2, lk2, lam_init)
    lq = q_ref.shape[0] if cached else seq_len
    hw = 2 * DH
    low = lax.broadcasted_iota(jnp.int32, (1, hw), 1) < DH
    zero = jnp.zeros((), BF16)
    ones = jnp.ones((kn_ref.shape[0] if cached else seq_len, hw), BF16)
    for h in range(H_B * (q_ref.shape[0] // lq)):
        rows = slice((h // H_B) * lq, (h // H_B + 1) * lq)
        krows = slice(None) if cached else rows
        cols = slice((h % H_B) * hw, (h % H_B + 1) * hw)
        qh = q_ref[rows, cols]
        qs = jnp.concatenate([jnp.where(low, qh, zero), jnp.where(low, zero, qh)], axis=0)
        s = _dot_nt(qs, kn_ref[krows, cols])
        if cached:
            s = jnp.concatenate([_dot_nt(qs, kc_ref[:, cols]), s], axis=1)
        e = jnp.exp2(s - jnp.max(s, axis=-1, keepdims=True)).astype(BF16)
        vn = jnp.concatenate([vn_ref[krows, cols], ones], axis=1)
        if cached:
            lc = kc_ref.shape[0]
            vc = jnp.concatenate([vc_ref[:, cols], ones[:lc]], axis=1)
            oa = _dot(e[:, :lc], vc) + _dot(e[:, lc:], vn)
        else:
            oa = _dot(e, vn)
        on = oa[:, :hw] * (1.0 / oa[:, hw:])
        o = on[:lq] - lam * on[lq:]
        o = o * lax.rsqrt(jnp.mean(o * o, axis=-1, keepdims=True) + LN_EPS)
        o = o * g_ref[...] * (1.0 - lam_init)
        o_ref[rows, cols] = (o * szb_ref[rows, cols].astype(F32)).astype(BF16)


def _lam_specs(n_axes):
    zero = lambda *idx: (0, 0)
    return [pl.BlockSpec((1, DH), zero)] * 4 + [pl.BlockSpec((1, 2 * DH), zero)]


def _attention_prompt(q, k, v, szb, lam_vecs, subln, lam_init, bsz, seq_len):
    blk = pl.BlockSpec((ATTN_SEQS_PER_STEP * seq_len, D_MODEL), lambda b: (b, 0))
    return pl.pallas_call(
        functools.partial(_attn_kernel, lam_init=lam_init, cached=False, seq_len=seq_len),
        grid=(bsz // ATTN_SEQS_PER_STEP,),
        in_specs=[blk] * 4 + _lam_specs(1),
        out_specs=blk,
        out_shape=jax.ShapeDtypeStruct(q.shape, BF16),
        compiler_params=_cparams(1),
        name="diff_attention_prompt",
    )(q, k, v, szb, *lam_vecs, subln)


def _attention_sample(q, kc, kn, vc, vn, szb, lam_vecs, subln, lam_init, bsz, seq_len, tq):
    nq = seq_len // tq
    past = kc.shape[1]
    qblk = pl.BlockSpec((tq, D_MODEL), lambda b, i: (b * nq + i, 0))
    cblk = pl.BlockSpec((None, past, D_MODEL), lambda b, i: (b, 0, 0))
    nblk = pl.BlockSpec((seq_len, D_MODEL), lambda b, i: (b, 0))
    return pl.pallas_call(
        functools.partial(_attn_kernel, lam_init=lam_init, cached=True, seq_len=seq_len),
        grid=(bsz, nq),
        in_specs=[qblk, cblk, nblk, cblk, nblk, qblk] + _lam_specs(2),
        out_specs=qblk,
        out_shape=jax.ShapeDtypeStruct(q.shape, BF16),
        compiler_params=_cparams(2),
        name="diff_attention_sample",
    )(q, kc, kn, vc, vn, szb, *lam_vecs, subln)


def _even_out_kernel(ys_ref, sza_ref, yb_ref, x_ref, gate_ref, wglu_ref, bglu_ref, wout_ref, g_ref, b_ref,
                     perm_ref, *rest, alpha):
    o_ref = rest[0] if len(rest) == 1 else rest[5]
    wa = ys_ref.shape[1]
    for r in range(x_ref.shape[0] // PERM_TILE):
        rows = slice(r * PERM_TILE, (r + 1) * PERM_TILE)
        ga = jax.nn.gelu(_dot(perm_ref[...], ys_ref[rows, :]))
        glu = jax.nn.sigmoid(_dot(ga.astype(BF16), wglu_ref[...]) + bglu_ref[...])
        ya = (ga * glu * sza_ref[rows, :].astype(F32)).astype(BF16)
        out = _dot(ya, wout_ref[:wa, :]) + _dot(yb_ref[rows, :], wout_ref[wa:, :])
        o_ref[rows, :] = _post_norm(x_ref[rows, :], gate_ref[...], out, g_ref[...], b_ref[...], alpha)
    if len(rest) > 1:
        for r in range(x_ref.shape[0] // PERM_TILE):
            rows = slice(r * PERM_TILE, (r + 1) * PERM_TILE)
            _odd_in_rows(o_ref[rows, :], rows, *rest[:5], *rest[6:])


def _even_out_proj(ys, sza, yb, x2, mod3, w_glu_bf, b_glu, w_out_bf, ln_g, ln_b, layer, cond_of, tm, alpha,
                   next_odd=None):
    t, d = x2.shape
    assert tm % PERM_TILE == 0 and PERM_TILE == SUB_ROWS
    row = pl.BlockSpec((tm, d), lambda i: (i, 0))
    full = _resident
    in_specs = [row, row, row, row, _mod_spec(layer, 2, cond_of), full(w_glu_bf.shape), full((1, d)),
                full(w_out_bf.shape), full((1, d)), full((1, d)), full((PERM_TILE, PERM_TILE))]
    args = [ys, sza, yb, x2, mod3, w_glu_bf, b_glu.reshape(1, d), w_out_bf, ln_g.reshape(1, d), ln_b.reshape(1, d),
            _chunk_transpose_matrix()]
    out_specs, out_shape = [row], [jax.ShapeDtypeStruct((t, d), F32)]
    if next_odd is not None:
        w_in_bf, cc, sc = next_odd
        wide = pl.BlockSpec((tm, W_C), lambda i: (i, 0))
        in_specs += [_mod_spec(layer + 1, 0, cond_of), _mod_spec(layer + 1, 1, cond_of), full(w_in_bf.shape),
                     full(cc.shape), full(sc.shape)]
        args += [mod3, mod3, w_in_bf, cc, sc]
        out_specs += [wide] * 3
        out_shape += [jax.ShapeDtypeStruct((t, W_C), BF16)] * 3
    res = pl.pallas_call(
        functools.partial(_even_out_kernel, alpha=alpha),
        grid=(t // tm,),
        in_specs=in_specs,
        out_specs=out_specs,
        out_shape=out_shape,
        compiler_params=_cparams(1),
        name="even_out_proj",
    )(*args)
    return res[0] if next_odd is None else tuple(res)


def _dft_mats(n, scale):
    k = np.arange(n, dtype=np.int64)
    ang = ((k[:, None] * k[None, :]) % n).astype(np.float64) * (2.0 * math.pi / n)
    return (jnp.asarray((np.cos(ang) * scale).astype(np.float32)).astype(BF16),
            jnp.asarray((np.sin(ang) * scale).astype(np.float32)).astype(BF16))


def _odd_in_rows(x, rows, shift_ref, scale_ref, w_ref, cc_ref, sc_ref, uc_ref, us_ref, sz_ref):
    h = (_ln_rows(x) * (1.0 + scale_ref[...]) + shift_ref[...]).astype(BF16)
    u = _dot(h, w_ref[:, :W_C]).astype(BF16)
    sz_ref[rows, :] = _silu(_dot(h, w_ref[:, W_C:])).astype(BF16)
    for g in range(NG_C):
        cols = slice(g * GC_C, (g + 1) * GC_C)
        uc_ref[rows, cols] = _dot(u[:, cols], cc_ref[...]).astype(BF16)
        us_ref[rows, cols] = _dot(u[:, cols], sc_ref[...]).astype(BF16)


def _odd_out_kernel(cl_ref, sl_ref, uc_ref, us_ref, sz_ref, x_ref, gate_ref, wf_ref, bf_ref, wo_ref, g_ref, b_ref,
                    o_ref, *, alpha, whole_seqs):
    seq_len = cl_ref.shape[1]
    sub = seq_len if whole_seqs else SUB_ROWS
    for r in range(x_ref.shape[0] // sub):
        rows = slice(r * sub, (r + 1) * sub)
        if whole_seqs:
            mixed = _dot(cl_ref[...], uc_ref[rows, :]) - _dot(sl_ref[...], us_ref[rows, :])
        else:
            mixed = _dot(cl_ref[rows, :], uc_ref[...]) - _dot(sl_ref[rows, :], us_ref[...])
        y = ((_dot(mixed.astype(BF16), wf_ref[...]) + bf_ref[...]) * sz_ref[rows, :].astype(F32)).astype(BF16)
        o_ref[rows, :] = _post_norm(x_ref[rows, :], gate_ref[...], _dot(y, wo_ref[...]), g_ref[...], b_ref[...],
                                    alpha)


def _odd_out_proj(uc, us, sz, x2, mod3, cl, sl, w_fno_bf, b_fno, w_out_bf, ln_g, ln_b, layer, cond_of,
                  bsz, seq_len, tl, alpha):
    t, d = x2.shape
    whole_seqs = tl >= seq_len
    full = _resident
    if whole_seqs:
        assert tl % seq_len == 0
        grid = (t // tl, 1)
        dft = _resident((seq_len, seq_len))
        seq = pl.BlockSpec((tl, W_C), lambda b, i: (b, 0))
        wide = seq
        row = pl.BlockSpec((tl, d), lambda b, i: (b, 0))
    else:
        assert seq_len % tl == 0 and tl % SUB_ROWS == 0
        nl = seq_len // tl
        grid = (bsz, nl)
        dft = pl.BlockSpec((tl, seq_len), lambda b, i: (i, 0))
        seq = pl.BlockSpec((seq_len, W_C), lambda b, i: (b, 0))
        wide = pl.BlockSpec((tl, W_C), lambda b, i: (b * nl + i, 0))
        row = pl.BlockSpec((tl, d), lambda b, i: (b * nl + i, 0))
    return pl.pallas_call(
        functools.partial(_odd_out_kernel, alpha=alpha, whole_seqs=whole_seqs),
        grid=grid,
        in_specs=[dft, dft, seq, seq, wide, row, _mod_spec(layer, 2, cond_of), full(w_fno_bf.shape),
                  full((1, W_C)), full(w_out_bf.shape), full((1, d)), full((1, d))],
        out_specs=row,
        out_shape=jax.ShapeDtypeStruct((t, d), F32),
        compiler_params=_cparams(2),
        name="odd_out_proj",
    )(cl, sl, uc, us, sz, x2, mod3, w_fno_bf, b_fno.reshape(1, W_C), w_out_bf, ln_g.reshape(1, d),
      ln_b.reshape(1, d))


def _rope_tables(seq_len):
    rows = seq_len // GRID_W
    row = jnp.repeat(jnp.arange(rows), GRID_W).astype(F32)
    col = jnp.tile(jnp.arange(GRID_W), rows).astype(F32)
    freqs = ROPE_BASE ** (-jnp.arange(ROT_FREQS, dtype=F32) / ROT_FREQS)
    ang = jnp.concatenate([row[:, None] * freqs, col[:, None] * freqs], axis=-1)
    cos, sin = jnp.cos(ang), jnp.sin(ang)
    cos128 = jnp.tile(cos, (1, 128 // ROT_HALF))
    sin128 = jnp.tile(jnp.concatenate([-sin, sin], axis=-1), (1, 128 // DH))
    return cos128, sin128


def kernel(x_prompt, x_sample, cache_k, cache_v, state_ssm_re, state_ssm_im, c, c_ctx, w_mod, b_mod, ln_g, ln_b, w_in_e, ssm_lam_re, ssm_lam_im, ssm_log_dt, ssm_b_re, ssm_b_im, ssm_c_re, ssm_c_im, ssm_d, w_glu, b_glu, lam_q1, lam_k1, lam_q2, lam_k2, subln_g, w_out_e, w_in_o, w_fno, b_fno, w_out_o):
    depth = w_mod.shape[0]
    bp_, lp, d = x_prompt.shape
    bs_, ls, _ = x_sample.shape
    past = cache_k.shape[2]
    alpha = (2 * depth) ** 0.25
    assert bs_ + 1 <= MOD_ROWS and d == D_MODEL

    cond8 = jnp.concatenate([c_ctx[None, :], c, jnp.zeros((MOD_ROWS - 1 - bs_, d), F32)], axis=0).astype(F32)
    mod3 = _modulation(cond8, w_mod, b_mod).reshape(depth * MOD_ROWS * 3, 1, d)

    tm = ROW_TILE
    cond_p = lambda *idx: 0
    cond_s_row = lambda i: 1 + i // (ls // tm)
    cond_s_grid = lambda b, i: 1 + b
    rope_tabs = _rope_tables(ls)
    xp = x_prompt.reshape(bp_ * lp, d)
    xs = x_sample.reshape(bs_ * ls, d)
    new_k, new_v, new_sr, new_si = [], [], [], []
    zeros_h0 = jnp.zeros((bp_, 2, G_A, P_A), F32)

    for layer in range(depth):
        if layer % 2 == 0:
            e = layer // 2
            lam_init = 0.8 - 0.6 * math.exp(-0.3 * layer)
            w_in_bf = w_in_e[e].astype(BF16)
            w_glu_bf = w_glu[e].astype(BF16)
            w_out_bf = w_out_e[e].astype(BF16)
            ops = _s5_operators(ssm_lam_re[e], ssm_lam_im[e], ssm_log_dt[e], ssm_b_re[e], ssm_b_im[e],
                                ssm_c_re[e], ssm_c_im[e], ssm_d[e])
            lam_vecs = [v[e].reshape(1, DH).astype(F32) for v in (lam_q1, lam_k1, lam_q2, lam_k2)]
            subln = subln_g[e].reshape(1, 2 * DH).astype(F32)

            u, sza, q, kb, vb, szb, kf, vf = _even_in_proj(xp, mod3, w_in_bf, layer, cond_p, tm, None, lp)
            new_k.append(kf.reshape(bp_, H_B, 2, DH, lp).transpose(0, 4, 1, 2, 3))
            new_v.append(vf.reshape(bp_, lp, H_B, 2 * DH))
            ys, s_re, s_im = _s5_mix(u, ops, zeros_h0, zeros_h0, bp_, lp, 2 * S5_TILE_ELEMS // (lp * d), d // 2)
            new_sr.append(s_re)
            new_si.append(s_im)
            yb = _attention_prompt(q, kb, vb, szb, lam_vecs, subln, lam_init, bp_, lp)
            next_odd = None
            if layer + 1 < depth:
                next_odd = (w_in_o[(layer + 1) // 2].astype(BF16),) + _dft_mats(GC_C, GC_C ** -0.5)
            res = _even_out_proj(ys, sza, yb, xp, mod3, w_glu_bf, b_glu[e], w_out_bf, ln_g[layer], ln_b[layer],
                                 layer, cond_p, tm, alpha, next_odd)
            xp, fused_p = (res, None) if next_odd is None else (res[0], res[1:])

            u, sza, q, kb, vb, szb = _even_in_proj(xs, mod3, w_in_bf, layer, cond_s_row, tm, rope_tabs, ls)
            ys, _, _ = _s5_mix(u, ops, state_ssm_re[:, e], state_ssm_im[:, e], bs_, ls, bs_,
                               S5_TILE_ELEMS // (bs_ * ls))
            kc = cache_k[:, e].reshape(bs_, past, d).astype(BF16)
            vc = cache_v[:, e].reshape(bs_, past, d).astype(BF16)
            yb = _attention_sample(q, kc, kb, vc, vb, szb, lam_vecs, subln, lam_init, bs_, ls, ATTN_Q_TILE)
            res = _even_out_proj(ys, sza, yb, xs, mod3, w_glu_bf, b_glu[e], w_out_bf, ln_g[layer], ln_b[layer],
                                 layer, cond_s_row, tm, alpha, next_odd)
            xs, fused_s = (res, None) if next_odd is None else (res[0], res[1:])
        else:
            o = layer // 2
            w_fno_bf = w_fno[o].astype(BF16)
            w_out_bf = w_out_o[o].astype(BF16)
            for which in ("prompt", "sample"):
                if which == "prompt":
                    x2, cond_row, cond_grid, bsz, seq, fused = xp, cond_p, cond_p, bp_, lp, fused_p
                else:
                    x2, cond_row, cond_grid, bsz, seq, fused = xs, cond_s_row, cond_s_grid, bs_, ls, fused_s
                cl, sl = _dft_mats(seq, seq ** -0.5)
                uc, us, sz = fused
                x2 = _odd_out_proj(uc, us, sz, x2, mod3, cl, sl, w_fno_bf, b_fno[o], w_out_bf, ln_g[layer],
                                   ln_b[layer], layer, cond_grid, bsz, seq, tm, alpha)
                if which == "prompt":
                    xp = x2
                else:
                    xs = x2

    return (xp.reshape(bp_, lp, d), xs.reshape(bs_, ls, d), jnp.stack(new_k, axis=1), jnp.stack(new_v, axis=1),
            jnp.stack(new_sr, axis=1), jnp.stack(new_si, axis=1))
```

```python
import functools
import math

import jax
import jax.numpy as jnp
import numpy as np
from jax import lax
from jax.experimental import pallas as pl
from jax.experimental.pallas import tpu as pltpu

F32 = jnp.float32
BF16 = jnp.bfloat16

D_MODEL = 1024
GRID_W = 64
SSM_GROUP = 16
G_A = D_MODEL // SSM_GROUP
P_A = 64
DH = 64
H_B = D_MODEL // (2 * DH)
ROPE_BASE = 10000.0
ROT_HALF = DH // 2
ROT_FREQS = DH // 4
NG_C = 8
GC_C = 2 * D_MODEL // NG_C
W_C = 2 * D_MODEL
LN_EPS = 1e-5
LOG2_E = 1.4426950408889634
CHUNK = 16
SSM_TILE = CHUNK * SSM_GROUP
LANES = 128
PERM_TILE = CHUNK * CHUNK
SUB_ROWS = PERM_TILE
ROW_TILE = 2 * SUB_ROWS
ATTN_Q_TILE = 512
ATTN_SEQS_PER_STEP = 4
S5_GROUPS_PER_STEP = 16
S5_SCAN_ROWS = 4096
S5_TILE_ELEMS = 2048 * 1024
MOD_ROWS = 8
VMEM_LIMIT = 56 * 1024 * 1024


def _cparams(n_axes):
    return pltpu.CompilerParams(dimension_semantics=("arbitrary",) * n_axes, vmem_limit_bytes=VMEM_LIMIT)


def _ln_rows(x):
    mu = jnp.mean(x, axis=-1, keepdims=True)
    xc = x - mu
    var = jnp.mean(xc * xc, axis=-1, keepdims=True)
    return xc * lax.rsqrt(var + LN_EPS)


def _silu(z):
    return z * jax.nn.sigmoid(z)


def _dot(a, b):
    return jnp.dot(a, b, preferred_element_type=F32)


def _dot_nt(a, b):
    return lax.dot_general(a, b, (((1,), (1,)), ((), ())), preferred_element_type=F32)


def _post_norm(x, gate, out, g, b, alpha):
    return _ln_rows(alpha * x + gate * out) * g + b


def _mod_kernel(c_ref, w_ref, b_ref, o_ref):
    c = _silu(c_ref[...]).astype(BF16)
    o_ref[...] = _dot(c, w_ref[...].astype(BF16)) + b_ref[...]


def _modulation(cond8, w_mod, b_mod):
    depth, d, n3 = w_mod.shape
    tn = 1024
    return pl.pallas_call(
        _mod_kernel,
        grid=(depth, n3 // tn),
        in_specs=[
            pl.BlockSpec((MOD_ROWS, d), lambda l, j: (0, 0)),
            pl.BlockSpec((None, d, tn), lambda l, j: (l, 0, j)),
            pl.BlockSpec((None, 1, tn), lambda l, j: (l, 0, j)),
        ],
        out_specs=pl.BlockSpec((None, MOD_ROWS, tn), lambda l, j: (l, 0, j)),
        out_shape=jax.ShapeDtypeStruct((depth, MOD_ROWS, n3), F32),
        compiler_params=_cparams(2),
        name="modulation",
    )(cond8, w_mod, b_mod.reshape(depth, 1, n3))


def _resident(shape):
    return pl.BlockSpec(shape, lambda *idx: (0,) * len(shape), pipeline_mode=pl.Buffered(1))


def _mod_spec(layer, part, cond_of):
    return pl.BlockSpec((None, 1, D_MODEL), lambda *idx: ((layer * MOD_ROWS + cond_of(*idx)) * 3 + part, 0, 0))


def _rope(x, cos, sin_signed, first_half):
    blocks = []
    for hh in range(x.shape[1] // 128):
        b = x[:, hh * 128:(hh + 1) * 128]
        partner = jnp.where(first_half, pltpu.roll(b, 128 - ROT_HALF, 1), pltpu.roll(b, ROT_HALF, 1))
        blocks.append(b * cos + partner * sin_signed)
    return jnp.concatenate(blocks, axis=1)


def _chunk_transpose_matrix():
    i = lax.broadcasted_iota(jnp.int32, (PERM_TILE, PERM_TILE), 0)
    j = lax.broadcasted_iota(jnp.int32, (PERM_TILE, PERM_TILE), 1)
    return (j == (i % CHUNK) * CHUNK + i // CHUNK).astype(BF16)


def _even_in_kernel(*refs, rope):
    if rope:
        (x_ref, shift_ref, scale_ref, w_ref, perm_ref, cos_ref, sin_ref,
         u_ref, sza_ref, q_ref, kb_ref, vb_ref, szb_ref) = refs
    else:
        (x_ref, shift_ref, scale_ref, w_ref, perm_ref, u_ref, sza_ref, q_ref, kb_ref, vb_ref, szb_ref,
         kf_ref, vf_ref) = refs
    w = D_MODEL
    for r in range(x_ref.shape[0] // PERM_TILE):
        rows = slice(r * PERM_TILE, (r + 1) * PERM_TILE)
        h = (_ln_rows(x_ref[rows, :]) * (1.0 + scale_ref[...]) + shift_ref[...]).astype(BF16)

        def proj(j):
            return _dot(h, w_ref[:, j * w:(j + 1) * w])

        u_ref[rows, :] = _dot(perm_ref[...], proj(0).astype(BF16)).astype(BF16)
        sza_ref[rows, :] = _silu(proj(1)).astype(BF16)
        q = proj(2) * (DH ** -0.5 * LOG2_E)
        k = proj(3)
        v = proj(4)
        if rope:
            lane = lax.broadcasted_iota(jnp.int32, (1, 128), 1)
            first_half = (lane % DH) < ROT_HALF
            q = _rope(q, cos_ref[rows, :], sin_ref[rows, :], first_half)
            kb_ref[rows, :] = _rope(k, cos_ref[rows, :], sin_ref[rows, :], first_half).astype(BF16)
        else:
            kf_ref[r] = k.T
            vf_ref[rows, :] = v
            kb_ref[rows, :] = k.astype(BF16)
        q_ref[rows, :] = q.astype(BF16)
        vb_ref[rows, :] = v.astype(BF16)
        szb_ref[rows, :] = _silu(proj(5)).astype(BF16)


def _even_in_proj(x2, mod3, w_in_bf, layer, cond_of, tm, rope_tabs, seq_len):
    t, d = x2.shape
    n = w_in_bf.shape[1]
    assert tm % PERM_TILE == 0
    row = pl.BlockSpec((tm, d), lambda i: (i, 0))
    in_specs = [row, _mod_spec(layer, 0, cond_of), _mod_spec(layer, 1, cond_of),
                _resident((d, n)), _resident((PERM_TILE, PERM_TILE))]
    args = [x2, mod3, mod3, w_in_bf, _chunk_transpose_matrix()]
    bf = jax.ShapeDtypeStruct((t, d), BF16)
    out_shape = [bf] * 6
    out_specs = [row] * 6
    if rope_tabs is not None:
        tiles_per_seq = seq_len // tm
        tab = pl.BlockSpec((tm, 128), lambda i: (i % tiles_per_seq, 0))
        in_specs += [tab, tab]
        args += list(rope_tabs)
    else:
        assert seq_len == PERM_TILE
        out_shape += [jax.ShapeDtypeStruct((t // seq_len, d, seq_len), F32), jax.ShapeDtypeStruct((t, d), F32)]
        out_specs += [pl.BlockSpec((tm // seq_len, d, seq_len), lambda i: (i, 0, 0)), row]
    return pl.pallas_call(
        functools.partial(_even_in_kernel, rope=rope_tabs is not None),
        grid=(t // tm,),
        in_specs=in_specs,
        out_specs=out_specs,
        out_shape=out_shape,
        compiler_params=_cparams(1),
        name="even_in_proj",
    )(*args)


BUILD_GROUPS = 8


def _s5_build_kernel(q_ref, b_ref, cn_ref, ccr_ref, cci_ref, ecr_ref, eci_ref, d_ref,
                     rep_ref, til_ref, wst_ref, kt_ref, wo_ref):
    def split(x):
        hi = x.astype(BF16)
        return hi, (x - hi.astype(F32)).astype(BF16)

    def hdot(a, b):
        (ah, al), (bh, bl) = split(a), split(b)
        return _dot(ah, bh) + _dot(ah, bl) + _dot(al, bh)

    lane = lax.broadcasted_iota(jnp.int32, (SSM_GROUP, SSM_TILE), 1)
    sub = lax.broadcasted_iota(jnp.int32, (SSM_GROUP, SSM_TILE), 0)
    zeros = jnp.zeros((SSM_GROUP, SSM_TILE), F32)
    rows4 = 4 * P_A
    def expand(x_t, sel):
        tn = lambda a: lax.dot_general(a, sel, (((0,), (0,)), ((), ())), preferred_element_type=F32)
        hi, lo = split(x_t)
        return tn(hi) + tn(lo)

    for g in range(BUILD_GROUPS):
        qe = expand(q_ref[g], rep_ref[...])
        be = expand(b_ref[g], til_ref[...])
        part = lambda x, k: x[k * P_A:(k + 1) * P_A]
        st_re = [part(qe, d) * part(be, d) - part(qe, 2 + d) * part(be, 2 + d) for d in range(2)]
        st_im = [part(qe, d) * part(be, 2 + d) + part(qe, 2 + d) * part(be, d) for d in range(2)]
        wst_ref[g] = jnp.concatenate([st_re[0], st_re[1], st_im[0], st_im[1]], axis=0).astype(BF16)
        klag = [hdot(cn_ref[d, g], jnp.concatenate([st_re[d], st_im[d]], axis=0)) for d in range(2)]
        fwd_p = jnp.concatenate([klag[0], zeros], axis=1)
        bwd_p = jnp.concatenate([zeros, klag[1]], axis=1)
        ccr, cci = ccr_ref[g], cci_ref[g]
        d_lanes = jnp.broadcast_to(d_ref[g], (SSM_GROUP, SSM_TILE))
        for t in range(CHUNK):
            lo_f = SSM_GROUP * (CHUNK - 1 - t)
            lo_b = SSM_TILE - SSM_GROUP * t
            skip = jnp.where(lane == t * SSM_GROUP + sub, d_lanes, 0.0)
            rows = slice(t * SSM_GROUP, (t + 1) * SSM_GROUP)
            kt_ref[g, rows, :] = (fwd_p[:, lo_f:lo_f + SSM_TILE] + bwd_p[:, lo_b:lo_b + SSM_TILE] + skip).astype(BF16)
            er, ei = ecr_ref[g, t:t + 1, :], eci_ref[g, t:t + 1, :]
            wo_ref[g, rows, :] = jnp.concatenate([ccr * er - cci * ei, -(ccr * ei + cci * er)], axis=1).astype(BF16)


def _s5_operators(lam_re, lam_im, log_dt, b_re, b_im, c_re, c_im, d_skip):
    lr, li = lam_re.astype(F32), lam_im.astype(F32)
    dt = jnp.exp(log_dt.astype(F32))[..., None]
    mag = jnp.exp(lr * dt)
    ar = mag * jnp.cos(li * dt)
    ai = mag * jnp.sin(li * dt)
    den = lr * lr + li * li
    fr = ((ar - 1.0) * lr + ai * li) / den
    fi = (ai * lr - (ar - 1.0) * li) / den
    br, bi = b_re.astype(F32), b_im.astype(F32)
    bbr = fr[..., None] * br - fi[..., None] * bi
    bbi = fr[..., None] * bi + fi[..., None] * br
    cr, ci = c_re.astype(F32), c_im.astype(F32)
    e_pow = jnp.arange(CHUNK + 1, dtype=F32)[:, None, None, None]
    pmag = jnp.exp(e_pow * (lr * dt))
    pr = pmag * jnp.cos(e_pow * (li * dt))
    pi = pmag * jnp.sin(e_pow * (li * dt))

    def by_position(p):
        return jnp.stack([p[:CHUNK, 0][::-1], p[:CHUNK, 1]], axis=0).transpose(0, 2, 1, 3)
    qr, qi = by_position(pr), by_position(pi)

    lane_sn = np.arange(SSM_TILE)
    rep = jnp.asarray(lane_sn[None, :] // SSM_GROUP == np.arange(CHUNK)[:, None], dtype=BF16)
    til = jnp.asarray(lane_sn[None, :] % SSM_GROUP == np.arange(SSM_GROUP)[:, None], dtype=BF16)

    def by_output(p):
        return jnp.concatenate([p[1:, 0], p[1:, 1][::-1]], axis=-1).transpose(1, 0, 2)
    cat_c = lambda c: jnp.concatenate([c[0], c[1]], axis=-1)

    def stack_lanes(re, im):
        return jnp.concatenate([re[0], re[1], im[0], im[1]], axis=-1)
    q_all = stack_lanes(qr, qi)
    b_all = stack_lanes(bbr.transpose(0, 1, 3, 2), bbi.transpose(0, 1, 3, 2))
    c_neg = jnp.concatenate([cr, -ci], axis=-1)

    gb = BUILD_GROUPS
    narrow = pl.BlockSpec((gb, SSM_GROUP, 4 * P_A), lambda i: (i, 0, 0))
    expand = pl.BlockSpec((SSM_GROUP, SSM_TILE), lambda i: (0, 0))
    cspec = pl.BlockSpec((2, gb, SSM_GROUP, 2 * P_A), lambda i: (0, i, 0, 0))
    half = pl.BlockSpec((gb, SSM_GROUP, 2 * P_A), lambda i: (i, 0, 0))
    dspec = pl.BlockSpec((gb, SSM_GROUP, 1), lambda i: (i, 0, 0))
    ospec = pl.BlockSpec((gb, SSM_TILE, SSM_TILE), lambda i: (i, 0, 0))
    oshape = jax.ShapeDtypeStruct((G_A, SSM_TILE, SSM_TILE), BF16)
    wst_t, kt_t, wo_t = pl.pallas_call(
        _s5_build_kernel,
        grid=(G_A // gb,),
        in_specs=[narrow] * 2 + [cspec] + [half] * 4 + [dspec, expand, expand],
        out_specs=[ospec] * 3,
        out_shape=[oshape] * 3,
        compiler_params=_cparams(1),
        name="s5_build_operators",
    )(q_all, b_all, c_neg, cat_c(cr), cat_c(ci),
      by_output(pr), by_output(pi), d_skip.astype(F32).reshape(G_A, SSM_GROUP, 1), rep, til)

    a16r = jnp.concatenate([pr[CHUNK, 0], pr[CHUNK, 1]], axis=-1)[:, None, :]
    a16i = jnp.concatenate([pi[CHUNK, 0], pi[CHUNK, 1]], axis=-1)[:, None, :]
    return wst_t, kt_t, wo_t, a16r, a16i


def _s5_kernel(u_ref, wst_ref, kt_ref, wo_ref, a16r_ref, a16i_ref, h0r_ref, h0i_ref,
               pin_ref, pout_ref, y_ref, fr_ref, br_ref, fi_ref, bi_ref, ut, yt, s_sc, hf_sc, hb_sc,
               *, nb, nc, gpb, il):
    gb = pl.program_id(2)
    seq = nc * CHUNK
    r = nb * nc
    sw = 2 * P_A
    n_cb = u_ref.shape[1] // LANES
    pieces = [b * seq + q * PERM_TILE for b in range(nb) for q in range(seq // PERM_TILE)]

    @pl.when(gb == 0)
    def _():
        for s in range(CHUNK):
            cols = []
            for cb in range(n_cb):
                rows = jnp.concatenate([u_ref[o + s * CHUNK:o + (s + 1) * CHUNK, cb * LANES:(cb + 1) * LANES]
                                        for o in pieces], axis=0)
                cols.append(rows.astype(F32).T.astype(BF16))
            ut[:, s * r:(s + 1) * r] = _dot(jnp.concatenate(cols, axis=0), pin_ref[...]).astype(BF16)

    fwd = (lax.broadcasted_iota(jnp.int32, (1, sw), 1)) < P_A

    def groups(it, carry):
        js = [it * il + i for i in range(il)]
        grows = [pl.multiple_of((gb * gpb + j) * SSM_GROUP, SSM_GROUP) for j in js]
        ds = [jnp.concatenate([ut[pl.ds(g, SSM_GROUP), s * r:(s + 1) * r] for s in range(CHUNK)], axis=0)
              for g in grows]
        for i, j in enumerate(js):
            s_t = _dot(wst_ref[j], ds[i])
            s_sc[i, 0] = s_t[:sw].T
            s_sc[i, 1] = s_t[sw:].T
        a_r = [a16r_ref[j] for j in js]
        a_i = [a16i_ref[j] for j in js]
        re = [h0r_ref[j] for j in js]
        im = [h0i_ref[j] for j in js]
        for k in range(nc):
            rf = pl.ds(k * nb, nb)
            rb = pl.ds((nc - 1 - k) * nb, nb)
            for i in range(il):
                hf_sc[i, 0, rf, :] = re[i]
                hf_sc[i, 1, rf, :] = im[i]
                hb_sc[i, 0, rb, :] = re[i]
                hb_sc[i, 1, rb, :] = im[i]
                xr = jnp.where(fwd, s_sc[i, 0, rf, :], s_sc[i, 0, rb, :])
                xi = jnp.where(fwd, s_sc[i, 1, rf, :], s_sc[i, 1, rb, :])
                re[i], im[i] = a_r[i] * re[i] - a_i[i] * im[i] + xr, a_r[i] * im[i] + a_i[i] * re[i] + xi
        for i, j in enumerate(js):
            fr_ref[j] = re[i]
            br_ref[j] = pltpu.roll(re[i], P_A, 1)
            fi_ref[j] = im[i]
            bi_ref[j] = pltpu.roll(im[i], P_A, 1)
            hin_t = jnp.concatenate([jnp.where(fwd, hf_sc[i, 0], hb_sc[i, 0]).T,
                                     jnp.where(fwd, hf_sc[i, 1], hb_sc[i, 1]).T], axis=0)
            y_t = (_dot(kt_ref[j], ds[i]) + _dot(wo_ref[j], hin_t.astype(BF16))).astype(BF16)
            for t in range(CHUNK):
                yt[pl.ds(grows[i], SSM_GROUP), t * r:(t + 1) * r] = y_t[t * SSM_GROUP:(t + 1) * SSM_GROUP, :]
        return carry

    lax.fori_loop(0, gpb // il, groups, 0)

    @pl.when(gb == pl.num_programs(2) - 1)
    def _():
        for t in range(CHUNK):
            full = _dot(yt[:, t * r:(t + 1) * r], pout_ref[...])
            for cb in range(n_cb):
                rows = full[cb * LANES:(cb + 1) * LANES, :].T.astype(BF16)
                for idx, o in enumerate(pieces):
                    y_ref[o + t * CHUNK:o + (t + 1) * CHUNK, cb * LANES:(cb + 1) * LANES] = (
                        rows[idx * CHUNK:(idx + 1) * CHUNK, :])


def _s5_mix(u, ops, h0_re, h0_im, bsz, seq_len, nb, cw):
    wst_t, kt_t, wo_t, a16r, a16i = ops
    t, w = u.shape
    nc = seq_len // CHUNK
    r = nb * nc
    gpb = S5_GROUPS_PER_STEP
    n_row_tiles = bsz // nb
    n_col_tiles = w // cw
    gb_per_tile = cw // SSM_GROUP // gpb
    assert r % 128 == 0 and cw % (SSM_GROUP * gpb) == 0 and bsz % nb == 0

    def h0_layout(h0):
        return h0.astype(F32).transpose(2, 0, 1, 3).reshape(G_A, bsz, 2 * P_A)

    tile = pl.BlockSpec((nb * seq_len, cw), lambda i, c, g: (i, c))
    grp = lambda i, c, g: (c * gb_per_tile + g, 0, 0)
    wspec = pl.BlockSpec((gpb, SSM_TILE, SSM_TILE), grp)
    aspec = pl.BlockSpec((gpb, 1, 2 * P_A), grp)
    hspec = pl.BlockSpec((gpb, nb, 2 * P_A), lambda i, c, g: (c * gb_per_tile + g, i, 0))
    state = jax.ShapeDtypeStruct((G_A, bsz, 2 * P_A), F32)
    il = min(gpb, S5_SCAN_ROWS // r)
    scratch = [pltpu.VMEM((cw, CHUNK * r), BF16)] * 2 + [pltpu.VMEM((il, 2, r, 2 * P_A), F32)] * 3
    src = lax.broadcasted_iota(jnp.int32, (r, r), 0)
    dst = lax.broadcasted_iota(jnp.int32, (r, r), 1)
    pin = (dst == (src % nc) * nb + src // nc).astype(BF16)
    pspec = pl.BlockSpec((r, r), lambda i, c, g: (0, 0))

    y, fwd_re, bwd_re, fwd_im, bwd_im = pl.pallas_call(
        functools.partial(_s5_kernel, nb=nb, nc=nc, gpb=gpb, il=il),
        grid=(n_row_tiles, n_col_tiles, gb_per_tile),
        in_specs=[tile, wspec, wspec, wspec, aspec, aspec, hspec, hspec, pspec, pspec],
        out_specs=[tile] + [hspec] * 4,
        out_shape=[jax.ShapeDtypeStruct((t, w), BF16)] + [state] * 4,
        scratch_shapes=scratch,
        compiler_params=_cparams(3),
        name="s5_core",
    )(u, wst_t, kt_t, wo_t, a16r, a16i, h0_layout(h0_re), h0_layout(h0_im), pin, pin.T)

    def fin(f, b):
        return jnp.stack([f[:, :, :P_A], b[:, :, :P_A]], axis=0).transpose(2, 0, 1, 3)
    return y, fin(fwd_re, bwd_re), fin(fwd_im, bwd_im)


def _diff_lambda(lq1, lk1, lq2, lk2, lam_init):
    return (jnp.exp(jnp.sum(lq1[...] * lk1[...], axis=-1, keepdims=True))
            - jnp.exp(jnp.sum(lq2[...] * lk2[...], axis=-1, keepdims=True)) + lam_init)


def _attn_kernel(*refs, lam_init, cached, seq_len):
    if cached:
        q_ref, kc_ref, kn_ref, vc_ref, vn_ref, szb_ref, lq1, lk1, lq2, lk2, g_ref, o_ref = refs
    else:
        q_ref, kn_ref, vn_ref, szb_ref, lq1, lk1, lq2, lk2, g_ref, o_ref = refs
    lam = _diff_lambda(lq1, lk1, lq2, lk2, lam_init)
    lq = q_ref.shape[0] if cached else seq_len
    hw = 2 * DH
    low = lax.broadcasted_iota(jnp.int32, (1, hw), 1) < DH
    zero = jnp.zeros((), BF16)
    ones = jnp.ones((kn_ref.shape[0] if cached else seq_len, hw), BF16)
    for h in range(H_B * (q_ref.shape[0] // lq)):
        rows = slice((h // H_B) * lq, (h // H_B + 1) * lq)
        krows = slice(None) if cached else rows
        cols = slice((h % H_B) * hw, (h % H_B + 1) * hw)
        qh = q_ref[rows, cols]
        qs = jnp.concatenate([jnp.where(low, qh, zero), jnp.where(low, zero, qh)], axis=0)
        s = _dot_nt(qs, kn_ref[krows, cols])
        if cached:
            s = jnp.concatenate([_dot_nt(qs, kc_ref[:, cols]), s], axis=1)
        e = jnp.exp2(s - jnp.max(s, axis=-1, keepdims=True)).astype(BF16)
        vn = jnp.concatenate([vn_ref[krows, cols], ones], axis=1)
        if cached:
            lc = kc_ref.shape[0]
            vc = jnp.concatenate([vc_ref[:, cols], ones[:lc]], axis=1)
            oa = _dot(e[:, :lc], vc) + _dot(e[:, lc:], vn)
        else:
            oa = _dot(e, vn)
        on = oa[:, :hw] * (1.0 / oa[:, hw:])
        o = on[:lq] - lam * on[lq:]
        o = o * lax.rsqrt(jnp.mean(o * o, axis=-1, keepdims=True) + LN_EPS)
        o = o * g_ref[...] * (1.0 - lam_init)
        o_ref[rows, cols] = (o * szb_ref[rows, cols].astype(F32)).astype(BF16)


def _lam_specs(n_axes):
    zero = lambda *idx: (0, 0)
    return [pl.BlockSpec((1, DH), zero)] * 4 + [pl.BlockSpec((1, 2 * DH), zero)]


def _attention_prompt(q, k, v, szb, lam_vecs, subln, lam_init, bsz, seq_len):
    blk = pl.BlockSpec((ATTN_SEQS_PER_STEP * seq_len, D_MODEL), lambda b: (b, 0))
    return pl.pallas_call(
        functools.partial(_attn_kernel, lam_init=lam_init, cached=False, seq_len=seq_len),
        grid=(bsz // ATTN_SEQS_PER_STEP,),
        in_specs=[blk] * 4 + _lam_specs(1),
        out_specs=blk,
        out_shape=jax.ShapeDtypeStruct(q.shape, BF16),
        compiler_params=_cparams(1),
        name="diff_attention_prompt",
    )(q, k, v, szb, *lam_vecs, subln)


def _attention_sample(q, kc, kn, vc, vn, szb, lam_vecs, subln, lam_init, bsz, seq_len, tq):
    nq = seq_len // tq
    past = kc.shape[1]
    qblk = pl.BlockSpec((tq, D_MODEL), lambda b, i: (b * nq + i, 0))
    cblk = pl.BlockSpec((None, past, D_MODEL), lambda b, i: (b, 0, 0))
    nblk = pl.BlockSpec((seq_len, D_MODEL), lambda b, i: (b, 0))
    return pl.pallas_call(
        functools.partial(_attn_kernel, lam_init=lam_init, cached=True, seq_len=seq_len),
        grid=(bsz, nq),
        in_specs=[qblk, cblk, nblk, cblk, nblk, qblk] + _lam_specs(2),
        out_specs=qblk,
        out_shape=jax.ShapeDtypeStruct(q.shape, BF16),
        compiler_params=_cparams(2),
        name="diff_attention_sample",
    )(q, kc, kn, vc, vn, szb, *lam_vecs, subln)


def _even_out_kernel(ys_ref, sza_ref, yb_ref, x_ref, gate_ref, wglu_ref, bglu_ref, wout_ref, g_ref, b_ref,
                     perm_ref, *rest, alpha):
    o_ref = rest[0] if len(rest) == 1 else rest[5]
    wa = ys_ref.shape[1]
    subs = [slice(r * PERM_TILE, (r + 1) * PERM_TILE) for r in range(x_ref.shape[0] // PERM_TILE)]
    yas = []
    for rows in subs:
        ga = jax.nn.gelu(_dot(perm_ref[...], ys_ref[rows, :]))
        glu = jax.nn.sigmoid(_dot(ga.astype(BF16), wglu_ref[...]) + bglu_ref[...])
        yas.append((ga * glu * sza_ref[rows, :].astype(F32)).astype(BF16))
    for rows, ya in zip(subs, yas):
        out = _dot(ya, wout_ref[:wa, :]) + _dot(yb_ref[rows, :], wout_ref[wa:, :])
        o_ref[rows, :] = _post_norm(x_ref[rows, :], gate_ref[...], out, g_ref[...], b_ref[...], alpha)
    if len(rest) > 1:
        for r in range(x_ref.shape[0] // PERM_TILE):
            rows = slice(r * PERM_TILE, (r + 1) * PERM_TILE)
            _odd_in_rows(o_ref[rows, :], rows, *rest[:5], *rest[6:])


def _even_out_proj(ys, sza, yb, x2, mod3, w_glu_bf, b_glu, w_out_bf, ln_g, ln_b, layer, cond_of, tm, alpha,
                   next_odd=None):
    t, d = x2.shape
    assert tm % PERM_TILE == 0 and PERM_TILE == SUB_ROWS
    row = pl.BlockSpec((tm, d), lambda i: (i, 0))
    full = _resident
    in_specs = [row, row, row, row, _mod_spec(layer, 2, cond_of), full(w_glu_bf.shape), full((1, d)),
                full(w_out_bf.shape), full((1, d)), full((1, d)), full((PERM_TILE, PERM_TILE))]
    args = [ys, sza, yb, x2, mod3, w_glu_bf, b_glu.reshape(1, d), w_out_bf, ln_g.reshape(1, d), ln_b.reshape(1, d),
            _chunk_transpose_matrix()]
    out_specs, out_shape = [row], [jax.ShapeDtypeStruct((t, d), F32)]
    if next_odd is not None:
        w_in_bf, cc, sc = next_odd
        wide = pl.BlockSpec((tm, W_C), lambda i: (i, 0))
        in_specs += [_mod_spec(layer + 1, 0, cond_of), _mod_spec(layer + 1, 1, cond_of), full(w_in_bf.shape),
                     full(cc.shape), full(sc.shape)]
        args += [mod3, mod3, w_in_bf, cc, sc]
        out_specs += [wide] * 3
        out_shape += [jax.ShapeDtypeStruct((t, W_C), BF16)] * 3
    res = pl.pallas_call(
        functools.partial(_even_out_kernel, alpha=alpha),
        grid=(t // tm,),
        in_specs=in_specs,
        out_specs=out_specs,
        out_shape=out_shape,
        compiler_params=_cparams(1),
        name="even_out_proj",
    )(*args)
    return res[0] if next_odd is None else tuple(res)


def _dft_mats(n, scale):
    k = np.arange(n, dtype=np.int64)
    ang = ((k[:, None] * k[None, :]) % n).astype(np.float64) * (2.0 * math.pi / n)
    return (jnp.asarray((np.cos(ang) * scale).astype(np.float32)).astype(BF16),
            jnp.asarray((np.sin(ang) * scale).astype(np.float32)).astype(BF16))


def _odd_in_rows(x, rows, shift_ref, scale_ref, w_ref, cc_ref, sc_ref, uc_ref, us_ref, sz_ref):
    h = (_ln_rows(x) * (1.0 + scale_ref[...]) + shift_ref[...]).astype(BF16)
    u = _dot(h, w_ref[:, :W_C]).astype(BF16)
    sz_ref[rows, :] = _silu(_dot(h, w_ref[:, W_C:])).astype(BF16)
    for g in range(NG_C):
        cols = slice(g * GC_C, (g + 1) * GC_C)
        uc_ref[rows, cols] = _dot(u[:, cols], cc_ref[...]).astype(BF16)
        us_ref[rows, cols] = _dot(u[:, cols], sc_ref[...]).astype(BF16)


def _odd_out_kernel(cl_ref, sl_ref, uc_ref, us_ref, sz_ref, x_ref, gate_ref, wf_ref, bf_ref, wo_ref, g_ref, b_ref,
                    o_ref, *, alpha, whole_seqs):
    seq_len = cl_ref.shape[1]
    sub = seq_len if whole_seqs else SUB_ROWS
    for r in range(x_ref.shape[0] // sub):
        rows = slice(r * sub, (r + 1) * sub)
        if whole_seqs:
            mixed = _dot(cl_ref[...], uc_ref[rows, :]) - _dot(sl_ref[...], us_ref[rows, :])
        else:
            mixed = _dot(cl_ref[rows, :], uc_ref[...]) - _dot(sl_ref[rows, :], us_ref[...])
        y = ((_dot(mixed.astype(BF16), wf_ref[...]) + bf_ref[...]) * sz_ref[rows, :].astype(F32)).astype(BF16)
        o_ref[rows, :] = _post_norm(x_ref[rows, :], gate_ref[...], _dot(y, wo_ref[...]), g_ref[...], b_ref[...],
                                    alpha)


def _odd_out_proj(uc, us, sz, x2, mod3, cl, sl, w_fno_bf, b_fno, w_out_bf, ln_g, ln_b, layer, cond_of,
                  bsz, seq_len, tl, alpha):
    t, d = x2.shape
    whole_seqs = tl >= seq_len
    full = _resident
    if whole_seqs:
        assert tl % seq_len == 0
        grid = (t // tl, 1)
        dft = _resident((seq_len, seq_len))
        seq = pl.BlockSpec((tl, W_C), lambda b, i: (b, 0))
        wide = seq
        row = pl.BlockSpec((tl, d), lambda b, i: (b, 0))
    else:
        assert seq_len % tl == 0 and tl % SUB_ROWS == 0
        nl = seq_len // tl
        grid = (bsz, nl)
        dft = pl.BlockSpec((tl, seq_len), lambda b, i: (i, 0))
        seq = pl.BlockSpec((seq_len, W_C), lambda b, i: (b, 0))
        wide = pl.BlockSpec((tl, W_C), lambda b, i: (b * nl + i, 0))
        row = pl.BlockSpec((tl, d), lambda b, i: (b * nl + i, 0))
    return pl.pallas_call(
        functools.partial(_odd_out_kernel, alpha=alpha, whole_seqs=whole_seqs),
        grid=grid,
        in_specs=[dft, dft, seq, seq, wide, row, _mod_spec(layer, 2, cond_of), full(w_fno_bf.shape),
                  full((1, W_C)), full(w_out_bf.shape), full((1, d)), full((1, d))],
        out_specs=row,
        out_shape=jax.ShapeDtypeStruct((t, d), F32),
        compiler_params=_cparams(2),
        name="odd_out_proj",
    )(cl, sl, uc, us, sz, x2, mod3, w_fno_bf, b_fno.reshape(1, W_C), w_out_bf, ln_g.reshape(1, d),
      ln_b.reshape(1, d))


def _rope_tables(seq_len):
    rows = seq_len // GRID_W
    row = jnp.repeat(jnp.arange(rows), GRID_W).astype(F32)
    col = jnp.tile(jnp.arange(GRID_W), rows).astype(F32)
    freqs = ROPE_BASE ** (-jnp.arange(ROT_FREQS, dtype=F32) / ROT_FREQS)
    ang = jnp.concatenate([row[:, None] * freqs, col[:, None] * freqs], axis=-1)
    cos, sin = jnp.cos(ang), jnp.sin(ang)
    cos128 = jnp.tile(cos, (1, 128 // ROT_HALF))
    sin128 = jnp.tile(jnp.concatenate([-sin, sin], axis=-1), (1, 128 // DH))
    return cos128, sin128


def kernel(x_prompt, x_sample, cache_k, cache_v, state_ssm_re, state_ssm_im, c, c_ctx, w_mod, b_mod, ln_g, ln_b, w_in_e, ssm_lam_re, ssm_lam_im, ssm_log_dt, ssm_b_re, ssm_b_im, ssm_c_re, ssm_c_im, ssm_d, w_glu, b_glu, lam_q1, lam_k1, lam_q2, lam_k2, subln_g, w_out_e, w_in_o, w_fno, b_fno, w_out_o):
    depth = w_mod.shape[0]
    bp_, lp, d = x_prompt.shape
    bs_, ls, _ = x_sample.shape
    past = cache_k.shape[2]
    alpha = (2 * depth) ** 0.25
    assert bs_ + 1 <= MOD_ROWS and d == D_MODEL

    cond8 = jnp.concatenate([c_ctx[None, :], c, jnp.zeros((MOD_ROWS - 1 - bs_, d), F32)], axis=0).astype(F32)
    mod3 = _modulation(cond8, w_mod, b_mod).reshape(depth * MOD_ROWS * 3, 1, d)

    tm = ROW_TILE
    cond_p = lambda *idx: 0
    cond_s_row = lambda i: 1 + i // (ls // tm)
    cond_s_grid = lambda b, i: 1 + b
    rope_tabs = _rope_tables(ls)
    xp = x_prompt.reshape(bp_ * lp, d)
    xs = x_sample.reshape(bs_ * ls, d)
    new_k, new_v, new_sr, new_si = [], [], [], []
    zeros_h0 = jnp.zeros((bp_, 2, G_A, P_A), F32)

    for layer in range(depth):
        if layer % 2 == 0:
            e = layer // 2
            lam_init = 0.8 - 0.6 * math.exp(-0.3 * layer)
            w_in_bf = w_in_e[e].astype(BF16)
            w_glu_bf = w_glu[e].astype(BF16)
            w_out_bf = w_out_e[e].astype(BF16)
            ops = _s5_operators(ssm_lam_re[e], ssm_lam_im[e], ssm_log_dt[e], ssm_b_re[e], ssm_b_im[e],
                                ssm_c_re[e], ssm_c_im[e], ssm_d[e])
            lam_vecs = [v[e].reshape(1, DH).astype(F32) for v in (lam_q1, lam_k1, lam_q2, lam_k2)]
            subln = subln_g[e].reshape(1, 2 * DH).astype(F32)

            u, sza, q, kb, vb, szb, kf, vf = _even_in_proj(xp, mod3, w_in_bf, layer, cond_p, tm, None, lp)
            new_k.append(kf.reshape(bp_, H_B, 2, DH, lp).transpose(0, 4, 1, 2, 3))
            new_v.append(vf.reshape(bp_, lp, H_B, 2 * DH))
            ys, s_re, s_im = _s5_mix(u, ops, zeros_h0, zeros_h0, bp_, lp, 2 * S5_TILE_ELEMS // (lp * d), d // 2)
            new_sr.append(s_re)
            new_si.append(s_im)
            yb = _attention_prompt(q, kb, vb, szb, lam_vecs, subln, lam_init, bp_, lp)
            next_odd = None
            if layer + 1 < depth:
                next_odd = (w_in_o[(layer + 1) // 2].astype(BF16),) + _dft_mats(GC_C, GC_C ** -0.5)
            res = _even_out_proj(ys, sza, yb, xp, mod3, w_glu_bf, b_glu[e], w_out_bf, ln_g[layer], ln_b[layer],
                                 layer, cond_p, tm, alpha, next_odd)
            xp, fused_p = (res, None) if next_odd is None else (res[0], res[1:])

            u, sza, q, kb, vb, szb = _even_in_proj(xs, mod3, w_in_bf, layer, cond_s_row, tm, rope_tabs, ls)
            ys, _, _ = _s5_mix(u, ops, state_ssm_re[:, e], state_ssm_im[:, e], bs_, ls, bs_,
                               S5_TILE_ELEMS // (bs_ * ls))
            kc = cache_k[:, e].reshape(bs_, past, d).astype(BF16)
            vc = cache_v[:, e].reshape(bs_, past, d).astype(BF16)
            yb = _attention_sample(q, kc, kb, vc, vb, szb, lam_vecs, subln, lam_init, bs_, ls, ATTN_Q_TILE)
            res = _even_out_proj(ys, sza, yb, xs, mod3, w_glu_bf, b_glu[e], w_out_bf, ln_g[layer], ln_b[layer],
                                 layer, cond_s_row, tm, alpha, next_odd)
            xs, fused_s = (res, None) if next_odd is None else (res[0], res[1:])
        else:
            o = layer // 2
            w_fno_bf = w_fno[o].astype(BF16)
            w_out_bf = w_out_o[o].astype(BF16)
            for which in ("prompt", "sample"):
                if which == "prompt":
                    x2, cond_row, cond_grid, bsz, seq, fused = xp, cond_p, cond_p, bp_, lp, fused_p
                else:
                    x2, cond_row, cond_grid, bsz, seq, fused = xs, cond_s_row, cond_s_grid, bs_, ls, fused_s
                cl, sl = _dft_mats(seq, seq ** -0.5)
                uc, us, sz = fused
                x2 = _odd_out_proj(uc, us, sz, x2, mod3, cl, sl, w_fno_bf, b_fno[o], w_out_bf, ln_g[layer],
                                   ln_b[layer], layer, cond_grid, bsz, seq, tm, alpha)
                if which == "prompt":
                    xp = x2
                else:
                    xs = x2

    return (xp.reshape(bp_, lp, d), xs.reshape(bs_, ls, d), jnp.stack(new_k, axis=1), jnp.stack(new_v, axis=1),
            jnp.stack(new_sr, axis=1), jnp.stack(new_si, axis=1))
```
